```python
import math
import jax, jax.numpy as jnp
from jax import lax
import numpy as np

D_MODEL = 2048
BATCH = 4
SEQ = 2048
DEPTH = 4
DEC_BATCH = 8
DEC_SEQ = 1
PAST_LEN = 16384
PAGE_SIZE = 128

HEAD_DIM = 64
D_MIX = D_MODEL
D_ATTN = D_MIX // 2
D_CONV = D_MIX - D_ATTN
N_HEADS = D_ATTN // HEAD_DIM
CONV_GROUPS = D_CONV // HEAD_DIM
CONV_WIDTH = 3
D_FF = 4 * D_MODEL
DILATED_CONFIGS = ((128, 1), (512, 4), (2048, 16))
ATTN_WINDOW = max(w for w, _ in DILATED_CONFIGS)
N_BUCKETS = 32
MAX_DISTANCE = 2048
Q_BLOCK = 128
EPS = 1e-6
ATTN_SCALE = HEAD_DIM ** -0.5
D_IN = 3 * D_ATTN + 3 * D_CONV

kernel_name = "hymba_dilated_swa_shortconv_decoder_step"


def _rmsnorm(x, g):
    xf = x.astype(jnp.float32)
    y = xf * lax.rsqrt(jnp.mean(xf * xf, axis=-1, keepdims=True) + EPS)
    return (y * g.astype(jnp.float32)).astype(x.dtype)


def _t5_bucket(dist):
    n_exact = N_BUCKETS // 2
    large = n_exact + (np.log(np.maximum(dist, 1) / n_exact) / np.log(MAX_DISTANCE / n_exact)
                       * (N_BUCKETS - n_exact)).astype(np.int32)
    large = np.minimum(large, N_BUCKETS - 1)
    return np.where(dist < n_exact, dist, large).astype(np.int32)


def _branch_biases(rel_bias):
    return [rel_bias[_t5_bucket(np.arange(w // d + 1) * d)] for w, d in DILATED_CONFIGS]


def _dilated_branch_prompt(q, k, v, bias_k, dil):
    b, s, h, e = q.shape
    nw = bias_k.shape[0] - 1
    L = s // dil
    bq = math.gcd(L, Q_BLOCK)
    nblk = L // bq
    pad = ((0, 0), (nw, 0), (0, 0), (0, 0), (0, 0))
    qb = q.reshape(b, nblk, bq, dil, h, e)
    kp = jnp.pad(k.reshape(b, L, dil, h, e), pad)
    vp = jnp.pad(v.reshape(b, L, dil, h, e), pad)
    idx = np.arange(nblk)[:, None] * bq + np.arange(bq + nw)[None, :]
    kb = kp[:, idx]
    vb = vp[:, idx]
    logits = jnp.einsum('bnqrhe,bnkrhe->bnrhqk', qb, kb).astype(jnp.float32) * ATTN_SCALE
    rel = np.arange(bq)[:, None] - np.arange(bq + nw)[None, :] + nw
    valid = ((rel >= 0) & (rel <= nw))[None] & ((idx - nw) >= 0)[:, None, :]
    bias = bias_k[np.clip(rel, 0, nw)].transpose(2, 0, 1).astype(jnp.float32)
    logits = jnp.where(valid[None, :, None, None], logits + bias[None, None, None], -jnp.inf)
    m = jnp.max(logits, axis=-1)
    p = jnp.exp(logits - m[..., None])
    den = jnp.sum(p, axis=-1)
    o = jnp.einsum('bnrhqk,bnkrhe->bnqrhe', p, vb.astype(jnp.float32)).reshape(b, s, h, e)
    m = m.transpose(0, 1, 4, 2, 3).reshape(b, s, h)
    den = den.transpose(0, 1, 4, 2, 3).reshape(b, s, h)
    return o, m, den


def _dilated_branch_sample(q, kcat, vcat, bias_k, dil):
    b, t, h, e = q.shape
    wb = kcat.shape[1] - t
    nk = bias_k.shape[0]
    j = wb + np.arange(t)[:, None] - np.arange(nk)[None, :] * dil
    valid = j >= 0
    jc = np.maximum(j, 0)
    kg = kcat[:, jc]
    vg = vcat[:, jc]
    logits = jnp.einsum('bthe,btkhe->bhtk', q, kg).astype(jnp.float32) * ATTN_SCALE
    logits = logits + bias_k.T.astype(jnp.float32)[None, :, None, :]
    logits = jnp.where(valid[None, None], logits, -jnp.inf)
    m = jnp.max(logits, axis=-1)
    p = jnp.exp(logits - m[..., None])
    den = jnp.sum(p, axis=-1)
    o = jnp.einsum('bhtk,btkhe->bthe', p, vg.astype(jnp.float32))
    return o, m.transpose(0, 2, 1), den.transpose(0, 2, 1)


def _combine_branches(branches):
    ms = jnp.stack([br[1] for br in branches], axis=0)
    mmax = jnp.max(ms, axis=0)
    num = 0.0
    den = 0.0
    for o_c, m_c, d_c in branches:
        w = jnp.exp(m_c - mmax)
        num = num + w[..., None] * o_c
        den = den + w * d_c
    return num / den[..., None]


def _mixer_inputs(x, g_norm, w_in, q_g, k_g):
    b, s, _ = x.shape
    h = _rmsnorm(x, g_norm)
    proj = jnp.einsum('bsd,dc->bsc', h, w_in)
    cuts = [D_ATTN, 2 * D_ATTN, 3 * D_ATTN, 3 * D_ATTN + D_CONV, 3 * D_ATTN + 2 * D_CONV]
    q, k, v, vc, gate_b, gate_c = jnp.split(proj, cuts, axis=-1)
    q = _rmsnorm(q.reshape(b, s, N_HEADS, HEAD_DIM), q_g)
    k = _rmsnorm(k.reshape(b, s, N_HEADS, HEAD_DIM), k_g)
    v = v.reshape(b, s, N_HEADS, HEAD_DIM)
    u = gate_c * vc
    return q, k, v, u, gate_b


def _depthwise_conv(ucat, w, t):
    y = w[0] * ucat[:, 0:t]
    for i in range(1, CONV_WIDTH):
        y = y + w[i] * ucat[:, i:i + t]
    return y


def _mixer_output(o_attn, y_conv, gate_b, attn_g, conv_g, w_out):
    b, s = o_attn.shape[:2]
    oa = _rmsnorm(o_attn.reshape(b, s, D_ATTN).astype(y_conv.dtype), attn_g)
    oc = _rmsnorm(gate_b * y_conv, conv_g)
    return jnp.einsum('bsc,cd->bsd', jnp.concatenate([oa, oc], axis=-1), w_out)


def _mlp(x, g, w_up, w_down):
    h = _rmsnorm(x, g)
    a = jnp.square(jax.nn.relu(jnp.einsum('bsd,df->bsf', h, w_up)))
    return jnp.einsum('bsf,fd->bsd', a, w_down)


def setup_inputs(seed: int = 0) -> dict:
    key = jax.random.key(seed)
    ks = jax.random.split(key, 17)
    wb = min(ATTN_WINDOW, PAST_LEN)
    f32 = jnp.float32

    def nrm(k, shape, scale=1.0):
        return jax.random.normal(k, shape, f32) * scale

    return {
        "x_prompt": nrm(ks[0], (BATCH, SEQ, D_MODEL)),
        "x_sample": nrm(ks[1], (DEC_BATCH, DEC_SEQ, D_MODEL)),
        "state_attn_k": nrm(ks[2], (DEPTH, DEC_BATCH, wb, N_HEADS, HEAD_DIM)),
        "state_attn_v": nrm(ks[3], (DEPTH, DEC_BATCH, wb, N_HEADS, HEAD_DIM)),
        "state_conv": nrm(ks[4], (DEPTH, DEC_BATCH, CONV_WIDTH - 1, D_CONV)),
        "rel_bias": nrm(ks[5], (N_BUCKETS, N_HEADS), 0.5),
        "norm_mix": 1.0 + nrm(ks[6], (DEPTH, D_MODEL), 0.05),
        "w_in": nrm(ks[7], (DEPTH, D_MODEL, D_IN), D_MODEL ** -0.5),
        "q_norm": 1.0 + nrm(ks[8], (DEPTH, HEAD_DIM), 0.05),
        "k_norm": 1.0 + nrm(ks[9], (DEPTH, HEAD_DIM), 0.05),
        "conv_w": nrm(ks[10], (DEPTH, CONV_WIDTH, D_CONV), CONV_WIDTH ** -0.5),
        "attn_out_norm": 1.0 + nrm(ks[11], (DEPTH, D_ATTN), 0.05),
        "conv_out_norm": 1.0 + nrm(ks[12], (DEPTH, D_CONV), 0.05),
        "w_out": nrm(ks[13], (DEPTH, D_MIX, D_MODEL), D_MIX ** -0.5),
        "norm_mlp": 1.0 + nrm(ks[14], (DEPTH, D_MODEL), 0.05),
        "w_up": nrm(ks[15], (DEPTH, D_MODEL, D_FF), D_MODEL ** -0.5),
        "w_down": nrm(ks[16], (DEPTH, D_FF, D_MODEL), 0.8 * D_FF ** -0.5),
    }


def reference(x_prompt, x_sample, state_attn_k, state_attn_v, state_conv, rel_bias,
              norm_mix, w_in, q_norm, k_norm, conv_w, attn_out_norm, conv_out_norm,
              w_out, norm_mlp, w_up, w_down):
    biases = _branch_biases(rel_bias)
    s_p = x_prompt.shape[1]
    t_s = x_sample.shape[1]
    pw = min(ATTN_WINDOW, s_p)
    wb = state_attn_k.shape[2]
    xp, xs = x_prompt, x_sample
    kp_new, vp_new, cp_new, ks_new, vs_new, cs_new = [], [], [], [], [], []
    for l in range(DEPTH):
        q, k, v, u, gb = _mixer_inputs(xp, norm_mix[l], w_in[l], q_norm[l], k_norm[l])
        o = _combine_branches([_dilated_branch_prompt(q, k, v, bias_c, d)
                               for bias_c, (_, d) in zip(biases, DILATED_CONFIGS)])
        ucat = jnp.pad(u, ((0, 0), (CONV_WIDTH - 1, 0), (0, 0)))
        yc = _depthwise_conv(ucat, conv_w[l], s_p)
        xp = xp + _mixer_output(o, yc, gb, attn_out_norm[l], conv_out_norm[l], w_out[l])
        xp = xp + _mlp(xp, norm_mlp[l], w_up[l], w_down[l])
        kp_new.append(k[:, s_p - pw:])
        vp_new.append(v[:, s_p - pw:])
        cp_new.append(u[:, s_p - (CONV_WIDTH - 1):])

        q, k, v, u, gb = _mixer_inputs(xs, norm_mix[l], w_in[l], q_norm[l], k_norm[l])
        kcat = jnp.concatenate([state_attn_k[l].astype(k.dtype), k], axis=1)
        vcat = jnp.concatenate([state_attn_v[l].astype(v.dtype), v], axis=1)
        o = _combine_branches([_dilated_branch_sample(q, kcat, vcat, bias_c, d)
                               for bias_c, (_, d) in zip(biases, DILATED_CONFIGS)])
        ucat = jnp.concatenate([state_conv[l].astype(u.dtype), u], axis=1)
        yc = _depthwise_conv(ucat, conv_w[l], t_s)
        xs = xs + _mixer_output(o, yc, gb, attn_out_norm[l], conv_out_norm[l], w_out[l])
        xs = xs + _mlp(xs, norm_mlp[l], w_up[l], w_down[l])
        ks_new.append(kcat[:, t_s:t_s + wb])
        vs_new.append(vcat[:, t_s:t_s + wb])
        cs_new.append(ucat[:, t_s:])

    return (xp, xs, jnp.stack(kp_new), jnp.stack(vp_new), jnp.stack(cp_new),
            jnp.stack(ks_new), jnp.stack(vs_new), jnp.stack(cs_new))
```

```python
import functools

import jax
import jax.numpy as jnp
import numpy as np
from jax import lax
from jax.experimental import pallas as pl
from jax.experimental.pallas import tpu as pltpu

HEAD_DIM = 64
CONV_WIDTH = 3
DILATED_CONFIGS = ((128, 1), (512, 4), (2048, 16))
N_BUCKETS = 32
MAX_DISTANCE = 2048
EPS = 1e-6
ATTN_SCALE = HEAD_DIM ** -0.5
MASKED = -1e30

LANES = 128
Q_BLOCK = 128
HEADS_PER_BLOCK = LANES // HEAD_DIM
VMEM_LIMIT_BYTES = 56 * 1024 * 1024

F32 = jnp.float32
BF16 = jnp.bfloat16


def _compiler_params(semantics):
    return pltpu.CompilerParams(dimension_semantics=semantics,
                                vmem_limit_bytes=VMEM_LIMIT_BYTES)


def _rms_rows(x, gain):
    ms = jnp.mean(x * x, axis=-1, keepdims=True)
    return (x * lax.rsqrt(ms + EPS)) * gain


N_SEGMENTS = 6
SEG_Q, SEG_K, SEG_V, SEG_VC, SEG_B, SEG_C = range(N_SEGMENTS)


def _inproj_kernel(x_ref, g_ref, w_ref, qg_ref, kg_ref, grp_ref, o_ref, h_scr, *, tiles_per_seg):
    n = pl.program_id(1)

    @pl.when(n == 0)
    def _():
        h_scr[...] = _rms_rows(x_ref[...], g_ref[...]).astype(BF16)

    y = jnp.dot(h_scr[...], w_ref[...].astype(BF16), preferred_element_type=F32)
    seg = n // tiles_per_seg

    def head_norm(gain_ref):
        ms = jnp.dot((y * y).astype(BF16), grp_ref[...], preferred_element_type=F32)
        return (y * lax.rsqrt(ms + EPS)) * gain_ref[...]

    @pl.when(seg == SEG_Q)
    def _():
        o_ref[0] = head_norm(qg_ref)

    @pl.when(seg == SEG_K)
    def _():
        o_ref[0] = head_norm(kg_ref)

    @pl.when(seg > SEG_K)
    def _():
        o_ref[0] = y


def _inproj(x, gain, w_in, layer, q_gain, k_gain, *, tm, tn):
    m_rows, d_model = x.shape
    d_in = w_in.shape[2]
    seg_w = d_in // N_SEGMENTS
    tiles_per_seg = seg_w // tn
    grp = jnp.asarray(
        (np.arange(tn)[:, None] // HEAD_DIM == np.arange(tn)[None, :] // HEAD_DIM)
        .astype(np.float32) / HEAD_DIM, BF16)
    qg = jnp.tile(q_gain, tn // HEAD_DIM).reshape(1, tn)
    kg = jnp.tile(k_gain, tn // HEAD_DIM).reshape(1, tn)
    const = lambda m, n: (0, 0)
    return pl.pallas_call(
        functools.partial(_inproj_kernel, tiles_per_seg=tiles_per_seg),
        grid=(m_rows // tm, d_in // tn),
        in_specs=[pl.BlockSpec((tm, d_model), lambda m, n: (m, 0)),
                  pl.BlockSpec((1, d_model), const),
                  pl.BlockSpec((None, d_model, tn), lambda m, n: (layer, 0, n)),
                  pl.BlockSpec((1, tn), const),
                  pl.BlockSpec((1, tn), const),
                  pl.BlockSpec((tn, tn), const)],
        out_specs=pl.BlockSpec(
            (1, tm, tn), lambda m, n: (n // tiles_per_seg, m, n % tiles_per_seg)),
        out_shape=jax.ShapeDtypeStruct((N_SEGMENTS, m_rows, seg_w), F32),
        scratch_shapes=[pltpu.VMEM((tm, d_model), BF16)],
        compiler_params=_compiler_params(("arbitrary", "arbitrary")),
        name="inproj",
    )(x, gain.reshape(1, d_model), w_in, qg, kg, grp)


def _t5_bucket(dist):
    n_exact = N_BUCKETS // 2
    large = n_exact + (np.log(np.maximum(dist, 1) / n_exact) / np.log(MAX_DISTANCE / n_exact)
                       * (N_BUCKETS - n_exact)).astype(np.int32)
    large = np.minimum(large, N_BUCKETS - 1)
    return np.where(dist < n_exact, dist, large).astype(np.int32)


def _prompt_bias_tables(rel_bias):
    n_heads = rel_bias.shape[1]
    qi = np.arange(Q_BLOCK)[:, None]
    kj = np.arange(2 * Q_BLOCK)[None, :]
    tables = []
    for window, dil in DILATED_CONFIGS:
        nw = window // dil
        assert nw == Q_BLOCK
        rel = qi - kj + nw
        valid = (rel >= 0) & (rel <= nw)
        bias = rel_bias[_t5_bucket(np.clip(rel, 0, nw) * dil)]
        per_first = []
        for first in (False, True):
            ok = valid & (kj >= Q_BLOCK) if first else valid
            per_first.append(jnp.where(ok[:, :, None], bias, MASKED))
        t = jnp.stack(per_first)
        t = t.transpose(3, 0, 1, 2)
        t = t.reshape(n_heads // HEADS_PER_BLOCK, HEADS_PER_BLOCK, 2, Q_BLOCK, 2 * Q_BLOCK)
        t = t.transpose(0, 2, 1, 3, 4).reshape(n_heads // HEADS_PER_BLOCK, 2,
                                                HEADS_PER_BLOCK * Q_BLOCK, 2 * Q_BLOCK)
        tables.append(t)
    return jnp.stack(tables)


def _mixer_prompt_kernel(q_ref, k_ref, v_ref, vc_ref, gb_ref, gc_ref, cw_ref, bias_ref,
                         oa_ref, oc_ref, cs_ref,
                         qs_scr, kd_scr, v0_scr, v1_scr, o_scr, m_scr, d_scr, u_scr, *, seq):
    nblk_total = seq // Q_BLOCK
    lane = lax.broadcasted_iota(jnp.int32, (Q_BLOCK, LANES), 1)
    head0 = lane < HEAD_DIM

    zeros_pad = jnp.zeros((Q_BLOCK, LANES), BF16)
    kd_scr[0:Q_BLOCK, :] = zeros_pad
    v0_scr[0:Q_BLOCK, :] = zeros_pad
    v1_scr[0:Q_BLOCK, :] = zeros_pad

    for c, (window, dil) in enumerate(DILATED_CONFIGS):
        sub_len = seq // dil
        blocks_per_sub = sub_len // Q_BLOCK

        for r in range(dil):
            rows = pl.ds(r, sub_len, stride=dil) if dil > 1 else pl.ds(0, seq)
            head0_rows = lax.broadcasted_iota(jnp.int32, (sub_len, LANES), 1) < HEAD_DIM
            qv = (q_ref[0, rows, :] * ATTN_SCALE).astype(BF16)
            kv = k_ref[0, rows, :].astype(BF16)
            vv = v_ref[0, rows, :].astype(BF16)
            zero = jnp.zeros_like(qv)
            q0 = jnp.where(head0_rows, qv, zero)
            q1 = jnp.where(head0_rows, zero, qv)
            for gb in range(blocks_per_sub):
                g = r * blocks_per_sub + gb
                src = slice(gb * Q_BLOCK, (gb + 1) * Q_BLOCK)
                qs_scr[2 * g * Q_BLOCK:(2 * g + 1) * Q_BLOCK, :] = q0[src]
                qs_scr[(2 * g + 1) * Q_BLOCK:(2 * g + 2) * Q_BLOCK, :] = q1[src]
            dst = slice(Q_BLOCK + r * sub_len, Q_BLOCK + (r + 1) * sub_len)
            kd_scr[dst, :] = kv
            v0_scr[dst, :] = jnp.where(head0_rows, vv, zero)
            v1_scr[dst, :] = jnp.where(head0_rows, zero, vv)

        for g in range(nblk_total):
            r, gb = divmod(g, blocks_per_sub)
            first = 1 if gb == 0 else 0
            lhs = qs_scr[2 * g * Q_BLOCK:(2 * g + 2) * Q_BLOCK, :]
            win = slice(g * Q_BLOCK, (g + 2) * Q_BLOCK)
            logits = lax.dot_general(lhs, kd_scr[win, :], (((1,), (1,)), ((), ())),
                                     preferred_element_type=F32)
            logits = logits + bias_ref[c, 0, first]
            m = jnp.max(logits, axis=-1, keepdims=True)
            p = jnp.exp(logits - m)
            den = jnp.sum(p, axis=-1, keepdims=True)
            pb = p.astype(BF16)
            o = (jnp.dot(pb[0:Q_BLOCK], v0_scr[win, :], preferred_element_type=F32)
                 + jnp.dot(pb[Q_BLOCK:], v1_scr[win, :], preferred_element_type=F32))
            m2 = jnp.where(head0, m[0:Q_BLOCK], m[Q_BLOCK:])
            d2 = jnp.where(head0, den[0:Q_BLOCK], den[Q_BLOCK:])
            if dil > 1:
                out_rows = pl.ds(gb * Q_BLOCK * dil + r, Q_BLOCK, stride=dil)
            else:
                out_rows = pl.ds(g * Q_BLOCK, Q_BLOCK)
            o_scr[c, out_rows, :] = o
            m_scr[c, out_rows, :] = m2
            d_scr[c, out_rows, :] = d2

    n_br = len(DILATED_CONFIGS)
    mmax = m_scr[0]
    for c in range(1, n_br):
        mmax = jnp.maximum(mmax, m_scr[c])
    num = jnp.zeros((seq, LANES), F32)
    den = jnp.zeros((seq, LANES), F32)
    for c in range(n_br):
        w = jnp.exp(m_scr[c] - mmax)
        num = num + w * o_scr[c]
        den = den + w * d_scr[c]
    oa_ref[0] = num / den

    u = gc_ref[0] * vc_ref[0]
    u_scr[0:8, :] = jnp.zeros((8, LANES), F32)
    u_scr[8:8 + seq, :] = u
    y = cw_ref[0:1, :] * u_scr[pl.ds(8 - (CONV_WIDTH - 1), seq), :]
    for i in range(1, CONV_WIDTH):
        y = y + cw_ref[i:i + 1, :] * u_scr[pl.ds(8 - (CONV_WIDTH - 1) + i, seq), :]
    oc_ref[0] = gb_ref[0] * y
    cs_ref[0] = u_scr[pl.ds(8 + seq - (CONV_WIDTH - 1), CONV_WIDTH - 1), :]


def _mixer_prompt(proj, conv_w_l, bias_tables):
    _, batch, seq, width = proj.shape
    n_pairs = width // LANES
    n_br = len(DILATED_CONFIGS)
    tok = pl.BlockSpec((1, seq, LANES), lambda b, hp: (b, 0, hp))
    seg_specs = [pl.BlockSpec((None, 1, seq, LANES), lambda b, hp, s=s: (s, b, 0, hp))
                 for s in range(N_SEGMENTS)]
    return pl.pallas_call(
        functools.partial(_mixer_prompt_kernel, seq=seq),
        grid=(batch, n_pairs),
        in_specs=seg_specs + [
            pl.BlockSpec((CONV_WIDTH, LANES), lambda b, hp: (0, hp)),
            pl.BlockSpec((n_br, 1, 2, 2 * Q_BLOCK, 2 * Q_BLOCK), lambda b, hp: (0, hp, 0, 0, 0))],
        out_specs=[tok, tok,
                   pl.BlockSpec((1, CONV_WIDTH - 1, LANES), lambda b, hp: (b, 0, hp))],
        out_shape=[jax.ShapeDtypeStruct((batch, seq, width), F32),
                   jax.ShapeDtypeStruct((batch, seq, width), F32),
                   jax.ShapeDtypeStruct((batch, CONV_WIDTH - 1, width), F32)],
        scratch_shapes=[pltpu.VMEM((2 * seq, LANES), BF16),
                        pltpu.VMEM((seq + Q_BLOCK, LANES), BF16),
                        pltpu.VMEM((seq + Q_BLOCK, LANES), BF16),
                        pltpu.VMEM((seq + Q_BLOCK, LANES), BF16),
                        pltpu.VMEM((n_br, seq, LANES), F32),
                        pltpu.VMEM((n_br, seq, LANES), F32),
                        pltpu.VMEM((n_br, seq, LANES), F32),
                        pltpu.VMEM((seq + 8, LANES), F32)],
        compiler_params=_compiler_params(("arbitrary", "arbitrary")),
        name="mixer_prompt",
    )(*([proj] * N_SEGMENTS), conv_w_l, bias_tables)


def _sample_bias_tables(rel_bias):
    tables = []
    for window, dil in DILATED_CONFIGS:
        nk = window // dil
        dist = (nk - np.arange(nk)) * dil
        tables.append(rel_bias[_t5_bucket(dist)])
    return jnp.stack(tables)[..., None], rel_bias[_t5_bucket(np.zeros((1,), np.int64))][0][:, None]


def _mixer_sample_kernel(q_ref, kn_ref, vn_ref, k1_ref, k4_ref, k16_ref, v1_ref, v4_ref, v16_ref,
                         bias_ref, bias0_ref, vc_ref, gb_ref, gc_ref, cw_ref, sc_ref,
                         oa_ref, oc_ref, cs_ref):
    q = q_ref[0]
    k_new = kn_ref[0]
    v_new = vn_ref[0]
    l_new = jnp.sum(q * k_new, axis=-1, keepdims=True) * ATTN_SCALE + bias0_ref[...]
    branches = []
    for c, (kb_ref, vb_ref) in enumerate(((k1_ref, v1_ref), (k4_ref, v4_ref), (k16_ref, v16_ref))):
        nk = kb_ref.shape[2]
        kb = kb_ref[...].reshape(nk, *q.shape)
        vb = vb_ref[...].reshape(nk, *q.shape)
        logits = jnp.sum(kb * q[None], axis=-1, keepdims=True) * ATTN_SCALE + bias_ref[c]
        m = jnp.maximum(jnp.max(logits, axis=0), l_new)
        p = jnp.exp(logits - m[None])
        p_new = jnp.exp(l_new - m)
        den = jnp.sum(p, axis=0) + p_new
        o = jnp.sum(p * vb, axis=0) + p_new * v_new
        branches.append((o, m, den))
    mmax = branches[0][1]
    for _, m, _ in branches[1:]:
        mmax = jnp.maximum(mmax, m)
    num = jnp.zeros_like(q)
    den = jnp.zeros_like(mmax)
    for o, m, d in branches:
        w = jnp.exp(m - mmax)
        num = num + w * o
        den = den + w * d
    oa_ref[0] = num / den

    u = gc_ref[0] * vc_ref[0]
    state = sc_ref[0]
    y = cw_ref[0:1, :] * state[0:1]
    for i in range(1, CONV_WIDTH - 1):
        y = y + cw_ref[i:i + 1, :] * state[i:i + 1]
    y = y + cw_ref[CONV_WIDTH - 1:CONV_WIDTH, :] * u
    oc_ref[0] = gb_ref[0] * y
    cs_ref[0] = jnp.concatenate([state[1:], u], axis=0)


def _mixer_sample(layer, proj, state_k, state_v, bias_s, bias0, conv_w_l, state_conv):
    depth, batch, wb, n_heads, head_dim = state_k.shape
    width = proj.shape[-1]
    qkv = proj[SEG_Q:SEG_V + 1].reshape(3, batch, n_heads, head_dim)
    rows = proj.reshape(N_SEGMENTS, batch, 1, width)
    views, specs = [], []
    for state in (state_k, state_v):
        for window, dil in DILATED_CONFIGS:
            nk = window // dil
            views.append(state.reshape(depth, batch, wb // dil, dil, n_heads, head_dim))
            specs.append(pl.BlockSpec(
                (1, 1, nk, 1, n_heads, head_dim),
                lambda b, blk=wb // dil // nk - 1: (layer, b, blk, 0, 0, 0)))
    head_spec = pl.BlockSpec((1, n_heads, head_dim), lambda b: (b, 0, 0))
    row_spec = pl.BlockSpec((1, 1, width), lambda b: (b, 0, 0))
    qkv_specs = [pl.BlockSpec((None, 1, n_heads, head_dim), lambda b, s=s: (s, b, 0, 0))
                 for s in range(3)]
    row_specs = [pl.BlockSpec((None, 1, 1, width), lambda b, s=s: (s, b, 0, 0))
                 for s in (SEG_VC, SEG_B, SEG_C)]
    n_br = len(DILATED_CONFIGS)
    return pl.pallas_call(
        _mixer_sample_kernel,
        grid=(batch,),
        in_specs=qkv_specs + specs + [
            pl.BlockSpec((n_br, Q_BLOCK, n_heads, 1), lambda b: (0, 0, 0, 0)),
            pl.BlockSpec((n_heads, 1), lambda b: (0, 0))] + row_specs + [
            pl.BlockSpec((CONV_WIDTH, width), lambda b: (0, 0)),
            pl.BlockSpec((1, CONV_WIDTH - 1, width), lambda b: (layer * batch + b, 0, 0))],
        out_specs=[head_spec, row_spec,
                   pl.BlockSpec((1, CONV_WIDTH - 1, width), lambda b: (b, 0, 0))],
        out_shape=[jax.ShapeDtypeStruct((batch, n_heads, head_dim), F32),
                   jax.ShapeDtypeStruct((batch, 1, width), F32),
                   jax.ShapeDtypeStruct((batch, CONV_WIDTH - 1, width), F32)],
        compiler_params=_compiler_params(("arbitrary",)),
        name="mixer_sample",
    )(qkv, qkv, qkv, *views, bias_s, bias0, rows, rows, rows, conv_w_l,
      state_conv.reshape(depth * batch, CONV_WIDTH - 1, width))


def _outproj_kernel(oa_ref, oc_ref, ga_ref, gc_ref, w_ref, x_ref, o_ref, z_scr):
    n = pl.program_id(1)
    d_attn = oa_ref.shape[1]

    @pl.when(n == 0)
    def _():
        z_scr[:, 0:d_attn] = _rms_rows(oa_ref[...], ga_ref[...]).astype(BF16)
        z_scr[:, d_attn:] = _rms_rows(oc_ref[...], gc_ref[...]).astype(BF16)

    o_ref[...] = x_ref[...] + jnp.dot(z_scr[...], w_ref[...].astype(BF16),
                                      preferred_element_type=F32)


def _outproj(oa, oc, attn_gain, conv_gain, w_out, layer, x, *, tm, tn):
    m_rows, d_attn = oa.shape
    d_conv = oc.shape[1]
    d_model = x.shape[1]
    const = lambda m, n: (0, 0)
    return pl.pallas_call(
        _outproj_kernel,
        grid=(m_rows // tm, d_model // tn),
        in_specs=[pl.BlockSpec((tm, d_attn), lambda m, n: (m, 0)),
                  pl.BlockSpec((tm, d_conv), lambda m, n: (m, 0)),
                  pl.BlockSpec((1, d_attn), const),
                  pl.BlockSpec((1, d_conv), const),
                  pl.BlockSpec((None, d_attn + d_conv, tn), lambda m, n: (layer, 0, n)),
                  pl.BlockSpec((tm, tn), lambda m, n: (m, n))],
        out_specs=pl.BlockSpec((tm, tn), lambda m, n: (m, n)),
        out_shape=jax.ShapeDtypeStruct((m_rows, d_model), F32),
        scratch_shapes=[pltpu.VMEM((tm, d_attn + d_conv), BF16)],
        compiler_params=_compiler_params(("arbitrary", "arbitrary")),
        name="outproj",
    )(oa, oc, attn_gain.reshape(1, d_attn), conv_gain.reshape(1, d_conv), w_out, x)


def _mlp_kernel(x_ref, g_ref, wu_ref, wd_ref, o_ref, h_scr):
    f = pl.program_id(1)

    @pl.when(f == 0)
    def _():
        x = x_ref[...]
        h_scr[...] = _rms_rows(x, g_ref[...]).astype(BF16)
        o_ref[...] = x

    a = jnp.dot(h_scr[...], wu_ref[...].astype(BF16), preferred_element_type=F32)
    a = jnp.square(jnp.maximum(a, 0.0)).astype(BF16)
    o_ref[...] += jnp.dot(a, wd_ref[...].astype(BF16), preferred_element_type=F32)


def _mlp(x, gain, w_up, w_down, layer, *, tm, tf):
    m_rows, d_model = x.shape
    d_ff = w_up.shape[2]
    return pl.pallas_call(
        _mlp_kernel,
        grid=(m_rows // tm, d_ff // tf),
        in_specs=[pl.BlockSpec((tm, d_model), lambda m, f: (m, 0)),
                  pl.BlockSpec((1, d_model), lambda m, f: (0, 0)),
                  pl.BlockSpec((None, d_model, tf), lambda m, f: (layer, 0, f)),
                  pl.BlockSpec((None, tf, d_model), lambda m, f: (layer, f, 0))],
        out_specs=pl.BlockSpec((tm, d_model), lambda m, f: (m, 0)),
        out_shape=jax.ShapeDtypeStruct((m_rows, d_model), F32),
        scratch_shapes=[pltpu.VMEM((tm, d_model), BF16)],
        compiler_params=_compiler_params(("arbitrary", "arbitrary")),
        name="mlp",
    )(x, gain.reshape(1, d_model), w_up, w_down)


def _roll_kernel(sk_ref, sv_ref, nk_ref, nv_ref, ok_ref, ov_ref, sem):
    depth, batch, wb = sk_ref.shape[:3]

    def copies():
        for src, new, dst, s in ((sk_ref, nk_ref, ok_ref, 0), (sv_ref, nv_ref, ov_ref, 1)):
            for l in range(depth):
                yield pltpu.make_async_copy(src.at[l, :, pl.ds(1, wb - 1)],
                                            dst.at[l, :, pl.ds(0, wb - 1)], sem.at[s, l, 0])
                yield pltpu.make_async_copy(new.at[l], dst.at[l, :, pl.ds(wb - 1, 1)],
                                            sem.at[s, l, 1])

    for cp in copies():
        cp.start()
    for cp in copies():
        cp.wait()


def _roll_windows(state_k, state_v, new_k, new_v):
    depth = state_k.shape[0]
    any_spec = pl.BlockSpec(memory_space=pl.ANY)
    return pl.pallas_call(
        _roll_kernel,
        in_specs=[any_spec] * 4,
        out_specs=[any_spec] * 2,
        out_shape=[jax.ShapeDtypeStruct(state_k.shape, state_k.dtype),
                   jax.ShapeDtypeStruct(state_v.shape, state_v.dtype)],
        scratch_shapes=[pltpu.SemaphoreType.DMA((2, depth, 2))],
        name="roll_windows",
    )(state_k, state_v, new_k, new_v)


PROMPT_TILES = dict(tm=1024, tn=512, tf=256)
SAMPLE_TILES = dict(tn=512, tf=512)


def kernel(x_prompt, x_sample, state_attn_k, state_attn_v, state_conv, rel_bias, norm_mix, w_in,
           q_norm, k_norm, conv_w, attn_out_norm, conv_out_norm, w_out, norm_mlp, w_up, w_down):
    batch, seq, d_model = x_prompt.shape
    dec_batch, dec_seq, _ = x_sample.shape
    depth, _, wb, n_heads, head_dim = state_attn_k.shape
    assert dec_seq == 1 and head_dim == HEAD_DIM and wb == max(w for w, _ in DILATED_CONFIGS)
    assert seq == wb

    bias_p = _prompt_bias_tables(rel_bias)
    bias_s, bias0 = _sample_bias_tables(rel_bias)

    xp = x_prompt.reshape(batch * seq, d_model)
    xs = x_sample.reshape(dec_batch, d_model)
    pt, st = PROMPT_TILES, SAMPLE_TILES
    kp_new, vp_new, cp_new, ks_rows, vs_rows, cs_new = [], [], [], [], [], []
    for l in range(depth):
        proj = _inproj(xp, norm_mix[l], w_in, l, q_norm[l], k_norm[l], tm=pt["tm"], tn=pt["tn"])
        width = proj.shape[-1]
        oa, oc, cs = _mixer_prompt(proj.reshape(N_SEGMENTS, batch, seq, width), conv_w[l], bias_p)
        xp = _outproj(oa.reshape(batch * seq, width), oc.reshape(batch * seq, width),
                      attn_out_norm[l], conv_out_norm[l], w_out, l, xp, tm=pt["tm"], tn=pt["tn"])
        xp = _mlp(xp, norm_mlp[l], w_up, w_down, l, tm=pt["tm"], tf=pt["tf"])
        kp_new.append(proj[SEG_K].reshape(batch, seq, n_heads, head_dim))
        vp_new.append(proj[SEG_V].reshape(batch, seq, n_heads, head_dim))
        cp_new.append(cs)

        proj = _inproj(xs, norm_mix[l], w_in, l, q_norm[l], k_norm[l], tm=dec_batch, tn=st["tn"])
        oa, oc, cs = _mixer_sample(l, proj, state_attn_k, state_attn_v, bias_s, bias0, conv_w[l],
                                   state_conv)
        xs = _outproj(oa.reshape(dec_batch, width), oc.reshape(dec_batch, width),
                      attn_out_norm[l], conv_out_norm[l], w_out, l, xs, tm=dec_batch, tn=st["tn"])
        xs = _mlp(xs, norm_mlp[l], w_up, w_down, l, tm=dec_batch, tf=st["tf"])
        ks_rows.append(proj[SEG_K].reshape(dec_batch, 1, n_heads, head_dim))
        vs_rows.append(proj[SEG_V].reshape(dec_batch, 1, n_heads, head_dim))
        cs_new.append(cs)

    ks_new, vs_new = _roll_windows(state_attn_k, state_attn_v, jnp.stack(ks_rows),
                                   jnp.stack(vs_rows))
    return (xp.reshape(batch, seq, d_model), xs.reshape(dec_batch, dec_seq, d_model),
            jnp.stack(kp_new), jnp.stack(vp_new), jnp.stack(cp_new),
            ks_new, vs_new, jnp.stack(cs_new))
```

```python
import functools

import jax
import jax.numpy as jnp
import numpy as np
from jax import lax
from jax.experimental import pallas as pl
from jax.experimental.pallas import tpu as pltpu

HEAD_DIM = 64
CONV_WIDTH = 3
DILATED_CONFIGS = ((128, 1), (512, 4), (2048, 16))
N_BUCKETS = 32
MAX_DISTANCE = 2048
EPS = 1e-6
ATTN_SCALE = HEAD_DIM ** -0.5
MASKED = -1e30

LANES = 128
Q_BLOCK = 128
HEADS_PER_BLOCK = LANES // HEAD_DIM
VMEM_LIMIT_BYTES = 56 * 1024 * 1024

F32 = jnp.float32
BF16 = jnp.bfloat16


def _compiler_params(semantics):
    return pltpu.CompilerParams(dimension_semantics=semantics,
                                vmem_limit_bytes=VMEM_LIMIT_BYTES)


def _rms_rows(x, gain):
    ms = jnp.mean(x * x, axis=-1, keepdims=True)
    return (x * lax.rsqrt(ms + EPS)) * gain


N_SEGMENTS = 6
SEG_Q, SEG_K, SEG_V, SEG_VC, SEG_B, SEG_C = range(N_SEGMENTS)


def _inproj_kernel(x_ref, g_ref, w_ref, qg_ref, kg_ref, grp_ref, o_ref, h_scr, *, tiles_per_seg):
    n = pl.program_id(1)

    @pl.when(n == 0)
    def _():
        h_scr[...] = _rms_rows(x_ref[...], g_ref[...]).astype(BF16)

    y = jnp.dot(h_scr[...], w_ref[...].astype(BF16), preferred_element_type=F32)
    seg = n // tiles_per_seg

    def head_norm(gain_ref):
        ms = jnp.dot((y * y).astype(BF16), grp_ref[...], preferred_element_type=F32)
        return (y * lax.rsqrt(ms + EPS)) * gain_ref[...]

    @pl.when(seg == SEG_Q)
    def _():
        o_ref[0] = head_norm(qg_ref)

    @pl.when(seg == SEG_K)
    def _():
        o_ref[0] = head_norm(kg_ref)

    @pl.when(seg > SEG_K)
    def _():
        o_ref[0] = y


def _inproj(x, gain, w_in, layer, q_gain, k_gain, *, tm, tn):
    m_rows, d_model = x.shape
    d_in = w_in.shape[2]
    seg_w = d_in // N_SEGMENTS
    tiles_per_seg = seg_w // tn
    grp = jnp.asarray(
        (np.arange(tn)[:, None] // HEAD_DIM == np.arange(tn)[None, :] // HEAD_DIM)
        .astype(np.float32) / HEAD_DIM, BF16)
    qg = jnp.tile(q_gain, tn // HEAD_DIM).reshape(1, tn)
    kg = jnp.tile(k_gain, tn // HEAD_DIM).reshape(1, tn)
    const = lambda m, n: (0, 0)
    return pl.pallas_call(
        functools.partial(_inproj_kernel, tiles_per_seg=tiles_per_seg),
        grid=(m_rows // tm, d_in // tn),
        in_specs=[pl.BlockSpec((tm, d_model), lambda m, n: (m, 0)),
                  pl.BlockSpec((1, d_model), const),
                  pl.BlockSpec((None, d_model, tn), lambda m, n: (layer, 0, n)),
                  pl.BlockSpec((1, tn), const),
                  pl.BlockSpec((1, tn), const),
                  pl.BlockSpec((tn, tn), const)],
        out_specs=pl.BlockSpec(
            (1, tm, tn), lambda m, n: (n // tiles_per_seg, m, n % tiles_per_seg)),
        out_shape=jax.ShapeDtypeStruct((N_SEGMENTS, m_rows, seg_w), F32),
        scratch_shapes=[pltpu.VMEM((tm, d_model), BF16)],
        compiler_params=_compiler_params(("arbitrary", "arbitrary")),
        name="inproj",
    )(x, gain.reshape(1, d_model), w_in, qg, kg, grp)


def _t5_bucket(dist):
    n_exact = N_BUCKETS // 2
    large = n_exact + (np.log(np.maximum(dist, 1) / n_exact) / np.log(MAX_DISTANCE / n_exact)
                       * (N_BUCKETS - n_exact)).astype(np.int32)
    large = np.minimum(large, N_BUCKETS - 1)
    return np.where(dist < n_exact, dist, large).astype(np.int32)


def _prompt_bias_tables(rel_bias):
    n_heads = rel_bias.shape[1]
    qi = np.arange(Q_BLOCK)[:, None]
    kj = np.arange(2 * Q_BLOCK)[None, :]
    tables = []
    for window, dil in DILATED_CONFIGS:
        nw = window // dil
        assert nw == Q_BLOCK
        rel = qi - kj + nw
        valid = (rel >= 0) & (rel <= nw)
        bias = rel_bias[_t5_bucket(np.clip(rel, 0, nw) * dil)]
        per_first = []
        for first in (False, True):
            ok = valid & (kj >= Q_BLOCK) if first else valid
            per_first.append(jnp.where(ok[:, :, None], bias, MASKED))
        t = jnp.stack(per_first)
        t = t.transpose(3, 0, 1, 2)
        t = t.reshape(n_heads // HEADS_PER_BLOCK, HEADS_PER_BLOCK, 2, Q_BLOCK, 2 * Q_BLOCK)
        t = t.transpose(0, 2, 1, 3, 4).reshape(n_heads // HEADS_PER_BLOCK, 2,
                                                HEADS_PER_BLOCK * Q_BLOCK, 2 * Q_BLOCK)
        tables.append(t)
    return jnp.stack(tables)


def _mixer_prompt_kernel(q_ref, k_ref, v_ref, vc_ref, gb_ref, gc_ref, cw_ref, bias_ref,
                         oa_ref, oc_ref, cs_ref,
                         qs_scr, kd_scr, v0_scr, v1_scr, o_scr, m_scr, d_scr, u_scr, *, seq):
    nblk_total = seq // Q_BLOCK
    lane = lax.broadcasted_iota(jnp.int32, (Q_BLOCK, LANES), 1)
    head0 = lane < HEAD_DIM

    zeros_pad = jnp.zeros((Q_BLOCK, LANES), BF16)
    kd_scr[0:Q_BLOCK, :] = zeros_pad
    v0_scr[0:Q_BLOCK, :] = zeros_pad
    v1_scr[0:Q_BLOCK, :] = zeros_pad

    for c, (window, dil) in enumerate(DILATED_CONFIGS):
        sub_len = seq // dil
        blocks_per_sub = sub_len // Q_BLOCK

        for r in range(dil):
            rows = pl.ds(r, sub_len, stride=dil) if dil > 1 else pl.ds(0, seq)
            head0_rows = lax.broadcasted_iota(jnp.int32, (sub_len, LANES), 1) < HEAD_DIM
            qv = (q_ref[0, rows, :] * ATTN_SCALE).astype(BF16)
            kv = k_ref[0, rows, :].astype(BF16)
            vv = v_ref[0, rows, :].astype(BF16)
            zero = jnp.zeros_like(qv)
            q0 = jnp.where(head0_rows, qv, zero)
            q1 = jnp.where(head0_rows, zero, qv)
            for gb in range(blocks_per_sub):
                g = r * blocks_per_sub + gb
                src = slice(gb * Q_BLOCK, (gb + 1) * Q_BLOCK)
                qs_scr[2 * g * Q_BLOCK:(2 * g + 1) * Q_BLOCK, :] = q0[src]
                qs_scr[(2 * g + 1) * Q_BLOCK:(2 * g + 2) * Q_BLOCK, :] = q1[src]
            dst = slice(Q_BLOCK + r * sub_len, Q_BLOCK + (r + 1) * sub_len)
            kd_scr[dst, :] = kv
            v0_scr[dst, :] = jnp.where(head0_rows, vv, zero)
            v1_scr[dst, :] = jnp.where(head0_rows, zero, vv)

        for g in range(nblk_total):
            r, gb = divmod(g, blocks_per_sub)
            first = 1 if gb == 0 else 0
            lhs = qs_scr[2 * g * Q_BLOCK:(2 * g + 2) * Q_BLOCK, :]
            win = slice(g * Q_BLOCK, (g + 2) * Q_BLOCK)
            logits = lax.dot_general(lhs, kd_scr[win, :], (((1,), (1,)), ((), ())),
                                     preferred_element_type=F32)
            logits = logits + bias_ref[c, 0, first]
            m = jnp.max(logits, axis=-1, keepdims=True)
            p = jnp.exp(logits - m)
            den = jnp.sum(p, axis=-1, keepdims=True)
            pb = p.astype(BF16)
            o = (jnp.dot(pb[0:Q_BLOCK], v0_scr[win, :], preferred_element_type=F32)
                 + jnp.dot(pb[Q_BLOCK:], v1_scr[win, :], preferred_element_type=F32))
            m2 = jnp.where(head0, m[0:Q_BLOCK], m[Q_BLOCK:])
            d2 = jnp.where(head0, den[0:Q_BLOCK], den[Q_BLOCK:])
            if dil > 1:
                out_rows = pl.ds(gb * Q_BLOCK * dil + r, Q_BLOCK, stride=dil)
            else:
                out_rows = pl.ds(g * Q_BLOCK, Q_BLOCK)
            o_scr[c, out_rows, :] = o
            m_scr[c, out_rows, :] = m2
            d_scr[c, out_rows, :] = d2

    n_br = len(DILATED_CONFIGS)
    mmax = m_scr[0]
    for c in range(1, n_br):
        mmax = jnp.maximum(mmax, m_scr[c])
    num = jnp.zeros((seq, LANES), F32)
    den = jnp.zeros((seq, LANES), F32)
    for c in range(n_br):
        w = jnp.exp(m_scr[c] - mmax)
        num = num + w * o_scr[c]
        den = den + w * d_scr[c]
    oa_ref[0] = num / den

    u = gc_ref[0] * vc_ref[0]
    u_scr[0:8, :] = jnp.zeros((8, LANES), F32)
    u_scr[8:8 + seq, :] = u
    y = cw_ref[0:1, :] * u_scr[pl.ds(8 - (CONV_WIDTH - 1), seq), :]
    for i in range(1, CONV_WIDTH):
        y = y + cw_ref[i:i + 1, :] * u_scr[pl.ds(8 - (CONV_WIDTH - 1) + i, seq), :]
    oc_ref[0] = gb_ref[0] * y
    cs_ref[0] = u_scr[pl.ds(8 + seq - (CONV_WIDTH - 1), CONV_WIDTH - 1), :]


def _mixer_prompt(proj, conv_w_l, bias_tables):
    _, batch, seq, width = proj.shape
    n_pairs = width // LANES
    n_br = len(DILATED_CONFIGS)
    tok = pl.BlockSpec((1, seq, LANES), lambda b, hp: (b, 0, hp))
    seg_specs = [pl.BlockSpec((None, 1, seq, LANES), lambda b, hp, s=s: (s, b, 0, hp))
                 for s in range(N_SEGMENTS)]
    return pl.pallas_call(
        functools.partial(_mixer_prompt_kernel, seq=seq),
        grid=(batch, n_pairs),
        in_specs=seg_specs + [
            pl.BlockSpec((CONV_WIDTH, LANES), lambda b, hp: (0, hp)),
            pl.BlockSpec((n_br, 1, 2, 2 * Q_BLOCK, 2 * Q_BLOCK), lambda b, hp: (0, hp, 0, 0, 0))],
        out_specs=[tok, tok,
                   pl.BlockSpec((1, CONV_WIDTH - 1, LANES), lambda b, hp: (b, 0, hp))],
        out_shape=[jax.ShapeDtypeStruct((batch, seq, width), F32),
                   jax.ShapeDtypeStruct((batch, seq, width), F32),
                   jax.ShapeDtypeStruct((batch, CONV_WIDTH - 1, width), F32)],
        scratch_shapes=[pltpu.VMEM((2 * seq, LANES), BF16),
                        pltpu.VMEM((seq + Q_BLOCK, LANES), BF16),
                        pltpu.VMEM((seq + Q_BLOCK, LANES), BF16),
                        pltpu.VMEM((seq + Q_BLOCK, LANES), BF16),
                        pltpu.VMEM((n_br, seq, LANES), F32),
                        pltpu.VMEM((n_br, seq, LANES), F32),
                        pltpu.VMEM((n_br, seq, LANES), F32),
                        pltpu.VMEM((seq + 8, LANES), F32)],
        compiler_params=_compiler_params(("arbitrary", "arbitrary")),
        name="mixer_prompt",
    )(*([proj] * N_SEGMENTS), conv_w_l, bias_tables)


SAMPLE_HEADS_PER_STEP = 4


def _sample_bias_tables(rel_bias, wb):
    dist = wb - np.arange(wb)
    tables = []
    for window, dil in DILATED_CONFIGS:
        in_branch = (dist % dil == 0) & (dist // dil <= window // dil)
        bias = jnp.where(in_branch[:, None], rel_bias[_t5_bucket(dist)], MASKED)
        tables.append(bias.T[:, None, :])
    own = rel_bias[_t5_bucket(np.zeros((1,), np.int64))][0]
    return jnp.stack(tables), own[:, None, None]


def _mixer_sample_kernel(q_ref, kn_ref, vn_ref, kt_ref, vt_ref, bias_ref, bias0_ref,
                         vc_ref, gb_ref, gc_ref, cw_ref, sc_ref, *rest):
    oa_ref, oc_ref, cs_ref, okt_ref, ovt_ref = rest[-5:]
    heads, head_dim, wb = kt_ref.shape
    newest = lax.broadcasted_iota(jnp.int32, (head_dim, wb), 1) == wb - 1
    n_br = len(DILATED_CONFIGS)
    for h in range(heads):
        kt = kt_ref[h]
        vt = vt_ref[h]
        q, k_new, v_new = q_ref[h], kn_ref[h], vn_ref[h]
        s = jnp.sum(kt * q, axis=0, keepdims=True) * ATTN_SCALE
        l_new = jnp.sum(q * k_new, axis=0, keepdims=True) * ATTN_SCALE + bias0_ref[h]
        branches = []
        for c in range(n_br):
            logits = s + bias_ref[c, h]
            m = jnp.maximum(jnp.max(logits, axis=-1, keepdims=True), l_new)
            p = jnp.exp(logits - m)
            p_new = jnp.exp(l_new - m)
            den = jnp.sum(p, axis=-1, keepdims=True) + p_new
            branches.append((p, p_new, m, den))
        mmax = branches[0][2]
        for br in branches[1:]:
            mmax = jnp.maximum(mmax, br[2])
        p_all = jnp.zeros_like(s)
        p_new_all = jnp.zeros_like(mmax)
        den_all = jnp.zeros_like(mmax)
        for p, p_new, m, den in branches:
            w = jnp.exp(m - mmax)
            p_all = p_all + w * p
            p_new_all = p_new_all + w * p_new
            den_all = den_all + w * den
        num = jnp.sum(vt * p_all, axis=-1, keepdims=True) + p_new_all * v_new
        oa_ref[h] = num / den_all
        okt_ref[h] = jnp.where(newest, k_new, pltpu.roll(kt, wb - 1, axis=1))
        ovt_ref[h] = jnp.where(newest, v_new, pltpu.roll(vt, wb - 1, axis=1))

    @pl.when(pl.program_id(1) == 0)
    def _():
        u = gc_ref[0] * vc_ref[0]
        state = sc_ref[0]
        y = cw_ref[0:1, :] * state[0:1]
        for i in range(1, CONV_WIDTH - 1):
            y = y + cw_ref[i:i + 1, :] * state[i:i + 1]
        y = y + cw_ref[CONV_WIDTH - 1:CONV_WIDTH, :] * u
        oc_ref[0] = gb_ref[0] * y
        cs_ref[0] = jnp.concatenate([state[1:], u], axis=0)


def _mixer_sample(layer, proj, kt_state, vt_state, rolled, bias_s, bias0, conv_w_l, state_conv):
    depth, batch, n_heads, head_dim, wb = kt_state.shape
    width = proj.shape[-1]
    hb = SAMPLE_HEADS_PER_STEP
    n_br = len(DILATED_CONFIGS)
    qkv = proj[SEG_Q:SEG_V + 1].reshape(3, batch, n_heads, head_dim, 1)
    rows = proj.reshape(N_SEGMENTS, batch, 1, width)
    col_specs = [pl.BlockSpec((None, None, hb, head_dim, 1), lambda b, hc, s=s: (s, b, hc, 0, 0))
                 for s in range(3)]
    row_specs = [pl.BlockSpec((None, 1, 1, width), lambda b, hc, s=s: (s, b, 0, 0))
                 for s in (SEG_VC, SEG_B, SEG_C)]
    buf_spec = pl.BlockSpec((None, None, hb, head_dim, wb), lambda b, hc: (layer, b, hc, 0, 0))
    in_specs = col_specs + [
        buf_spec, buf_spec,
        pl.BlockSpec((n_br, hb, 1, wb), lambda b, hc: (0, hc, 0, 0)),
        pl.BlockSpec((hb, 1, 1), lambda b, hc: (hc, 0, 0))] + row_specs + [
        pl.BlockSpec((CONV_WIDTH, width), lambda b, hc: (0, 0)),
        pl.BlockSpec((1, CONV_WIDTH - 1, width), lambda b, hc: (layer * batch + b, 0, 0))]
    args = [qkv, qkv, qkv, kt_state, vt_state, bias_s, bias0, rows, rows, rows, conv_w_l,
            state_conv.reshape(depth * batch, CONV_WIDTH - 1, width)]
    aliases = {}
    if rolled is not None:
        aliases = {len(args): 3, len(args) + 1: 4}
        in_specs = in_specs + [pl.BlockSpec(memory_space=pl.ANY)] * 2
        args = args + list(rolled)
    oa, oc, cs, okt, ovt = pl.pallas_call(
        _mixer_sample_kernel,
        grid=(batch, n_heads // hb),
        in_specs=in_specs,
        out_specs=[pl.BlockSpec((None, hb, head_dim, 1), lambda b, hc: (b, hc, 0, 0)),
                   pl.BlockSpec((1, 1, width), lambda b, hc: (b, 0, 0)),
                   pl.BlockSpec((1, CONV_WIDTH - 1, width), lambda b, hc: (b, 0, 0)),
                   buf_spec, buf_spec],
        out_shape=[jax.ShapeDtypeStruct((batch, n_heads, head_dim, 1), F32),
                   jax.ShapeDtypeStruct((batch, 1, width), F32),
                   jax.ShapeDtypeStruct((batch, CONV_WIDTH - 1, width), F32),
                   jax.ShapeDtypeStruct(kt_state.shape, F32),
                   jax.ShapeDtypeStruct(vt_state.shape, F32)],
        input_output_aliases=aliases,
        compiler_params=_compiler_params(("arbitrary", "arbitrary")),
        name="mixer_sample",
    )(*args)
    return oa, oc, cs, (okt, ovt)


def _outproj_kernel(oa_ref, oc_ref, ga_ref, gc_ref, w_ref, x_ref, o_ref, z_scr):
    n = pl.program_id(1)
    d_attn = oa_ref.shape[1]

    @pl.when(n == 0)
    def _():
        z_scr[:, 0:d_attn] = _rms_rows(oa_ref[...], ga_ref[...]).astype(BF16)
        z_scr[:, d_attn:] = _rms_rows(oc_ref[...], gc_ref[...]).astype(BF16)

    o_ref[...] = x_ref[...] + jnp.dot(z_scr[...], w_ref[...].astype(BF16),
                                      preferred_element_type=F32)


def _outproj(oa, oc, attn_gain, conv_gain, w_out, layer, x, *, tm, tn):
    m_rows, d_attn = oa.shape
    d_conv = oc.shape[1]
    d_model = x.shape[1]
    const = lambda m, n: (0, 0)
    return pl.pallas_call(
        _outproj_kernel,
        grid=(m_rows // tm, d_model // tn),
        in_specs=[pl.BlockSpec((tm, d_attn), lambda m, n: (m, 0)),
                  pl.BlockSpec((tm, d_conv), lambda m, n: (m, 0)),
                  pl.BlockSpec((1, d_attn), const),
                  pl.BlockSpec((1, d_conv), const),
                  pl.BlockSpec((None, d_attn + d_conv, tn), lambda m, n: (layer, 0, n)),
                  pl.BlockSpec((tm, tn), lambda m, n: (m, n))],
        out_specs=pl.BlockSpec((tm, tn), lambda m, n: (m, n)),
        out_shape=jax.ShapeDtypeStruct((m_rows, d_model), F32),
        scratch_shapes=[pltpu.VMEM((tm, d_attn + d_conv), BF16)],
        compiler_params=_compiler_params(("arbitrary", "arbitrary")),
        name="outproj",
    )(oa, oc, attn_gain.reshape(1, d_attn), conv_gain.reshape(1, d_conv), w_out, x)


def _mlp_kernel(x_ref, g_ref, wu_ref, wd_ref, o_ref, h_scr):
    f = pl.program_id(1)

    @pl.when(f == 0)
    def _():
        x = x_ref[...]
        h_scr[...] = _rms_rows(x, g_ref[...]).astype(BF16)
        o_ref[...] = x

    a = jnp.dot(h_scr[...], wu_ref[...].astype(BF16), preferred_element_type=F32)
    a = jnp.square(jnp.maximum(a, 0.0)).astype(BF16)
    o_ref[...] += jnp.dot(a, wd_ref[...].astype(BF16), preferred_element_type=F32)


def _mlp(x, gain, w_up, w_down, layer, *, tm, tf):
    m_rows, d_model = x.shape
    d_ff = w_up.shape[2]
    return pl.pallas_call(
        _mlp_kernel,
        grid=(m_rows // tm, d_ff // tf),
        in_specs=[pl.BlockSpec((tm, d_model), lambda m, f: (m, 0), pipeline_mode=pl.Buffered(1)),
                  pl.BlockSpec((1, d_model), lambda m, f: (0, 0)),
                  pl.BlockSpec((None, d_model, tf), lambda m, f: (layer, 0, f)),
                  pl.BlockSpec((None, tf, d_model), lambda m, f: (layer, f, 0))],
        out_specs=pl.BlockSpec((tm, d_model), lambda m, f: (m, 0)),
        out_shape=jax.ShapeDtypeStruct((m_rows, d_model), F32),
        scratch_shapes=[pltpu.VMEM((tm, d_model), BF16)],
        compiler_params=_compiler_params(("arbitrary", "arbitrary")),
        name="mlp",
    )(x, gain.reshape(1, d_model), w_up, w_down)


PROMPT_TILES = dict(tm=1024, tn=512, tf=512)
SAMPLE_TILES = dict(tn=512, tf=512)


def kernel(x_prompt, x_sample, state_attn_k, state_attn_v, state_conv, rel_bias, norm_mix, w_in,
           q_norm, k_norm, conv_w, attn_out_norm, conv_out_norm, w_out, norm_mlp, w_up, w_down):
    batch, seq, d_model = x_prompt.shape
    dec_batch, dec_seq, _ = x_sample.shape
    depth, _, wb, n_heads, head_dim = state_attn_k.shape
    assert dec_seq == 1 and head_dim == HEAD_DIM and wb == max(w for w, _ in DILATED_CONFIGS)
    assert seq == wb

    bias_p = _prompt_bias_tables(rel_bias)
    bias_s, bias0 = _sample_bias_tables(rel_bias, wb)
    kt_state = state_attn_k.transpose(0, 1, 3, 4, 2)
    vt_state = state_attn_v.transpose(0, 1, 3, 4, 2)
    rolled = None

    xp = x_prompt.reshape(batch * seq, d_model)
    xs = x_sample.reshape(dec_batch, d_model)
    pt, st = PROMPT_TILES, SAMPLE_TILES
    kp_new, vp_new, cp_new, cs_new = [], [], [], []
    for l in range(depth):
        proj = _inproj(xp, norm_mix[l], w_in, l, q_norm[l], k_norm[l], tm=pt["tm"], tn=pt["tn"])
        width = proj.shape[-1]
        oa, oc, cs = _mixer_prompt(proj.reshape(N_SEGMENTS, batch, seq, width), conv_w[l], bias_p)
        xp = _outproj(oa.reshape(batch * seq, width), oc.reshape(batch * seq, width),
                      attn_out_norm[l], conv_out_norm[l], w_out, l, xp, tm=pt["tm"], tn=pt["tn"])
        xp = _mlp(xp, norm_mlp[l], w_up, w_down, l, tm=pt["tm"], tf=pt["tf"])
        kp_new.append(proj[SEG_K].reshape(batch, seq, n_heads, head_dim))
        vp_new.append(proj[SEG_V].reshape(batch, seq, n_heads, head_dim))
        cp_new.append(cs)

        proj = _inproj(xs, norm_mix[l], w_in, l, q_norm[l], k_norm[l], tm=dec_batch, tn=st["tn"])
        oa, oc, cs, rolled = _mixer_sample(l, proj, kt_state, vt_state, rolled, bias_s, bias0,
                                           conv_w[l], state_conv)
        xs = _outproj(oa.reshape(dec_batch, width), oc.reshape(dec_batch, width),
                      attn_out_norm[l], conv_out_norm[l], w_out, l, xs, tm=dec_batch, tn=st["tn"])
        xs = _mlp(xs, norm_mlp[l], w_up, w_down, l, tm=dec_batch, tf=st["tf"])
        cs_new.append(cs)

    ks_new, vs_new = (r.transpose(0, 1, 4, 2, 3) for r in rolled)
    return (xp.reshape(batch, seq, d_model), xs.reshape(dec_batch, dec_seq, d_model),
            jnp.stack(kp_new), jnp.stack(vp_new), jnp.stack(cp_new),
            ks_new, vs_new, jnp.stack(cs_new))
```

```python
import functools

import jax
import jax.numpy as jnp
import numpy as np
from jax import lax
from jax.experimental import pallas as pl
from jax.experimental.pallas import tpu as pltpu

HEAD_DIM = 64
CONV_WIDTH = 3
DILATED_CONFIGS = ((128, 1), (512, 4), (2048, 16))
N_BUCKETS = 32
MAX_DISTANCE = 2048
EPS = 1e-6
ATTN_SCALE = HEAD_DIM ** -0.5
MASKED = -1e30

LANES = 128
Q_BLOCK = 128
HEADS_PER_BLOCK = LANES // HEAD_DIM
VMEM_LIMIT_BYTES = 56 * 1024 * 1024

F32 = jnp.float32
BF16 = jnp.bfloat16


def _compiler_params(semantics):
    return pltpu.CompilerParams(dimension_semantics=semantics,
                                vmem_limit_bytes=VMEM_LIMIT_BYTES)


def _rms_rows(x, gain):
    ms = jnp.mean(x * x, axis=-1, keepdims=True)
    return (x * lax.rsqrt(ms + EPS)) * gain


N_SEGMENTS = 6
SEG_Q, SEG_K, SEG_V, SEG_VC, SEG_B, SEG_C = range(N_SEGMENTS)


def _inproj_kernel(x_ref, g_ref, w_ref, qg_ref, kg_ref, grp_ref, o_ref, h_scr, *, tiles_per_seg):
    n = pl.program_id(1)

    @pl.when(n == 0)
    def _():
        h_scr[...] = _rms_rows(x_ref[...], g_ref[...]).astype(BF16)

    y = jnp.dot(h_scr[...], w_ref[...].astype(BF16), preferred_element_type=F32)
    seg = n // tiles_per_seg

    def head_norm(gain_ref):
        ms = jnp.dot((y * y).astype(BF16), grp_ref[...], preferred_element_type=F32)
        return (y * lax.rsqrt(ms + EPS)) * gain_ref[...]

    @pl.when(seg == SEG_Q)
    def _():
        o_ref[0] = head_norm(qg_ref)

    @pl.when(seg == SEG_K)
    def _():
        o_ref[0] = head_norm(kg_ref)

    @pl.when(seg > SEG_K)
    def _():
        o_ref[0] = y


def _inproj(x, gain, w_in, layer, q_gain, k_gain, *, tm, tn):
    m_rows, d_model = x.shape
    d_in = w_in.shape[2]
    seg_w = d_in // N_SEGMENTS
    tiles_per_seg = seg_w // tn
    grp = jnp.asarray(
        (np.arange(tn)[:, None] // HEAD_DIM == np.arange(tn)[None, :] // HEAD_DIM)
        .astype(np.float32) / HEAD_DIM, BF16)
    qg = jnp.tile(q_gain, tn // HEAD_DIM).reshape(1, tn)
    kg = jnp.tile(k_gain, tn // HEAD_DIM).reshape(1, tn)
    const = lambda m, n: (0, 0)
    return pl.pallas_call(
        functools.partial(_inproj_kernel, tiles_per_seg=tiles_per_seg),
        grid=(m_rows // tm, d_in // tn),
        in_specs=[pl.BlockSpec((tm, d_model), lambda m, n: (m, 0)),
                  pl.BlockSpec((1, d_model), const),
                  pl.BlockSpec((None, d_model, tn), lambda m, n: (layer, 0, n)),
                  pl.BlockSpec((1, tn), const),
                  pl.BlockSpec((1, tn), const),
                  pl.BlockSpec((tn, tn), const)],
        out_specs=pl.BlockSpec(
            (1, tm, tn), lambda m, n: (n // tiles_per_seg, m, n % tiles_per_seg)),
        out_shape=jax.ShapeDtypeStruct((N_SEGMENTS, m_rows, seg_w), F32),
        scratch_shapes=[pltpu.VMEM((tm, d_model), BF16)],
        compiler_params=_compiler_params(("arbitrary", "arbitrary")),
        name="inproj",
    )(x, gain.reshape(1, d_model), w_in, qg, kg, grp)


def _t5_bucket(dist):
    n_exact = N_BUCKETS // 2
    large = n_exact + (np.log(np.maximum(dist, 1) / n_exact) / np.log(MAX_DISTANCE / n_exact)
                       * (N_BUCKETS - n_exact)).astype(np.int32)
    large = np.minimum(large, N_BUCKETS - 1)
    return np.where(dist < n_exact, dist, large).astype(np.int32)


def _bias_lookup(rel_bias, buckets):
    onehot = buckets[..., None] == np.arange(rel_bias.shape[0])
    return jnp.sum(jnp.where(onehot[..., None], rel_bias, 0.0), axis=-2)


def _prompt_bias_tables(rel_bias):
    n_heads = rel_bias.shape[1]
    qi = np.arange(Q_BLOCK)[:, None]
    kj = np.arange(2 * Q_BLOCK)[None, :]
    tables = []
    for window, dil in DILATED_CONFIGS:
        nw = window // dil
        assert nw == Q_BLOCK
        rel = qi - kj + nw
        valid = (rel >= 0) & (rel <= nw)
        bias = _bias_lookup(rel_bias, _t5_bucket(np.clip(rel, 0, nw) * dil))
        per_first = []
        for first in (False, True):
            ok = valid & (kj >= Q_BLOCK) if first else valid
            per_first.append(jnp.where(ok[:, :, None], bias, MASKED))
        t = jnp.stack(per_first)
        t = t.transpose(3, 0, 1, 2)
        t = t.reshape(n_heads // HEADS_PER_BLOCK, HEADS_PER_BLOCK, 2, Q_BLOCK, 2 * Q_BLOCK)
        t = t.transpose(0, 2, 1, 3, 4).reshape(n_heads // HEADS_PER_BLOCK, 2,
                                                HEADS_PER_BLOCK * Q_BLOCK, 2 * Q_BLOCK)
        tables.append(t)
    return jnp.stack(tables)


def _mixer_prompt_kernel(q_ref, k_ref, v_ref, vc_ref, gb_ref, gc_ref, cw_ref, bias_ref, *rest,
                         seq, n_aliased):
    (oa_ref, oc_ref, cs_ref, okt_ref, ovt_ref,
     qs_scr, kd_scr, v0_scr, v1_scr, o_scr, m_scr, d_scr, u_scr) = rest[n_aliased:]

    for src, dst in ((k_ref, okt_ref), (v_ref, ovt_ref)):
        t = src[0].T
        for h in range(HEADS_PER_BLOCK):
            dst[h] = t[h * HEAD_DIM:(h + 1) * HEAD_DIM]

    nblk_total = seq // Q_BLOCK
    lane = lax.broadcasted_iota(jnp.int32, (Q_BLOCK, LANES), 1)
    head0 = lane < HEAD_DIM

    zeros_pad = jnp.zeros((Q_BLOCK, LANES), BF16)
    kd_scr[0:Q_BLOCK, :] = zeros_pad
    v0_scr[0:Q_BLOCK, :] = zeros_pad
    v1_scr[0:Q_BLOCK, :] = zeros_pad

    for c, (window, dil) in enumerate(DILATED_CONFIGS):
        sub_len = seq // dil
        blocks_per_sub = sub_len // Q_BLOCK

        for r in range(dil):
            rows = pl.ds(r, sub_len, stride=dil) if dil > 1 else pl.ds(0, seq)
            head0_rows = lax.broadcasted_iota(jnp.int32, (sub_len, LANES), 1) < HEAD_DIM
            qv = (q_ref[0, rows, :] * ATTN_SCALE).astype(BF16)
            kv = k_ref[0, rows, :].astype(BF16)
            vv = v_ref[0, rows, :].astype(BF16)
            zero = jnp.zeros_like(qv)
            q0 = jnp.where(head0_rows, qv, zero)
            q1 = jnp.where(head0_rows, zero, qv)
            for gb in range(blocks_per_sub):
                g = r * blocks_per_sub + gb
                src = slice(gb * Q_BLOCK, (gb + 1) * Q_BLOCK)
                qs_scr[2 * g * Q_BLOCK:(2 * g + 1) * Q_BLOCK, :] = q0[src]
                qs_scr[(2 * g + 1) * Q_BLOCK:(2 * g + 2) * Q_BLOCK, :] = q1[src]
            dst = slice(Q_BLOCK + r * sub_len, Q_BLOCK + (r + 1) * sub_len)
            kd_scr[dst, :] = kv
            v0_scr[dst, :] = jnp.where(head0_rows, vv, zero)
            v1_scr[dst, :] = jnp.where(head0_rows, zero, vv)

        for g in range(nblk_total):
            r, gb = divmod(g, blocks_per_sub)
            first = 1 if gb == 0 else 0
            lhs = qs_scr[2 * g * Q_BLOCK:(2 * g + 2) * Q_BLOCK, :]
            win = slice(g * Q_BLOCK, (g + 2) * Q_BLOCK)
            logits = lax.dot_general(lhs, kd_scr[win, :], (((1,), (1,)), ((), ())),
                                     preferred_element_type=F32)
            logits = logits + bias_ref[c, 0, first]
            m = jnp.max(logits, axis=-1, keepdims=True)
            p = jnp.exp(logits - m)
            den = jnp.sum(p, axis=-1, keepdims=True)
            pb = p.astype(BF16)
            o = (jnp.dot(pb[0:Q_BLOCK], v0_scr[win, :], preferred_element_type=F32)
                 + jnp.dot(pb[Q_BLOCK:], v1_scr[win, :], preferred_element_type=F32))
            m2 = jnp.where(head0, m[0:Q_BLOCK], m[Q_BLOCK:])
            d2 = jnp.where(head0, den[0:Q_BLOCK], den[Q_BLOCK:])
            if dil > 1:
                out_rows = pl.ds(gb * Q_BLOCK * dil + r, Q_BLOCK, stride=dil)
            else:
                out_rows = pl.ds(g * Q_BLOCK, Q_BLOCK)
            o_scr[c, out_rows, :] = o
            m_scr[c, out_rows, :] = m2
            d_scr[c, out_rows, :] = d2

    n_br = len(DILATED_CONFIGS)
    mmax = m_scr[0]
    for c in range(1, n_br):
        mmax = jnp.maximum(mmax, m_scr[c])
    num = jnp.zeros((seq, LANES), F32)
    den = jnp.zeros((seq, LANES), F32)
    for c in range(n_br):
        w = jnp.exp(m_scr[c] - mmax)
        num = num + w * o_scr[c]
        den = den + w * d_scr[c]
    oa_ref[0] = num / den

    u = gc_ref[0] * vc_ref[0]
    u_scr[0:8, :] = jnp.zeros((8, LANES), F32)
    u_scr[8:8 + seq, :] = u
    y = cw_ref[0:1, :] * u_scr[pl.ds(8 - (CONV_WIDTH - 1), seq), :]
    for i in range(1, CONV_WIDTH):
        y = y + cw_ref[i:i + 1, :] * u_scr[pl.ds(8 - (CONV_WIDTH - 1) + i, seq), :]
    oc_ref[0] = gb_ref[0] * y
    cs_ref[0] = u_scr[pl.ds(8 + seq - (CONV_WIDTH - 1), CONV_WIDTH - 1), :]


def _mixer_prompt(layer, depth, proj, conv_w_l, bias_tables, new_windows):
    _, batch, seq, width = proj.shape
    n_pairs = width // LANES
    n_heads = width // HEAD_DIM
    n_br = len(DILATED_CONFIGS)
    tok = pl.BlockSpec((1, seq, LANES), lambda b, hp: (b, 0, hp))
    seg_specs = [pl.BlockSpec((None, 1, seq, LANES), lambda b, hp, s=s: (s, b, 0, hp))
                 for s in range(N_SEGMENTS)]
    win_spec = pl.BlockSpec((None, None, HEADS_PER_BLOCK, HEAD_DIM, seq),
                            lambda b, hp: (layer, b, hp, 0, 0))
    win_shape = jax.ShapeDtypeStruct((depth, batch, n_heads, HEAD_DIM, seq), F32)
    in_specs = seg_specs + [
        pl.BlockSpec((CONV_WIDTH, LANES), lambda b, hp: (0, hp)),
        pl.BlockSpec((n_br, 1, 2, 2 * Q_BLOCK, 2 * Q_BLOCK), lambda b, hp: (0, hp, 0, 0, 0))]
    args = [proj] * N_SEGMENTS + [conv_w_l, bias_tables]
    aliases = {}
    if new_windows is not None:
        aliases = {len(args): 3, len(args) + 1: 4}
        in_specs = in_specs + [pl.BlockSpec(memory_space=pl.ANY)] * 2
        args = args + list(new_windows)
    oa, oc, cs, okt, ovt = pl.pallas_call(
        functools.partial(_mixer_prompt_kernel, seq=seq, n_aliased=len(aliases)),
        grid=(batch, n_pairs),
        in_specs=in_specs,
        out_specs=[tok, tok,
                   pl.BlockSpec((1, CONV_WIDTH - 1, LANES), lambda b, hp: (b, 0, hp)),
                   win_spec, win_spec],
        out_shape=[jax.ShapeDtypeStruct((batch, seq, width), F32),
                   jax.ShapeDtypeStruct((batch, seq, width), F32),
                   jax.ShapeDtypeStruct((batch, CONV_WIDTH - 1, width), F32),
                   win_shape, win_shape],
        input_output_aliases=aliases,
        scratch_shapes=[pltpu.VMEM((2 * seq, LANES), BF16),
                        pltpu.VMEM((seq + Q_BLOCK, LANES), BF16),
                        pltpu.VMEM((seq + Q_BLOCK, LANES), BF16),
                        pltpu.VMEM((seq + Q_BLOCK, LANES), BF16),
                        pltpu.VMEM((n_br, seq, LANES), F32),
                        pltpu.VMEM((n_br, seq, LANES), F32),
                        pltpu.VMEM((n_br, seq, LANES), F32),
                        pltpu.VMEM((seq + 8, LANES), F32)],
        compiler_params=_compiler_params(("arbitrary", "arbitrary")),
        name="mixer_prompt",
    )(*args)
    return oa, oc, cs, (okt, ovt)


SAMPLE_HEADS_PER_STEP = 4


def _sample_bias_tables(rel_bias, wb):
    dist = wb - np.arange(wb)
    tables = []
    for window, dil in DILATED_CONFIGS:
        in_branch = (dist % dil == 0) & (dist // dil <= window // dil)
        bias = jnp.where(in_branch[:, None], _bias_lookup(rel_bias, _t5_bucket(dist)), MASKED)
        tables.append(bias.T[:, None, :])
    own = rel_bias[_t5_bucket(np.zeros((1,), np.int64))][0]
    return jnp.stack(tables), own[:, None, None]


def _mixer_sample_kernel(q_ref, kn_ref, vn_ref, kt_ref, vt_ref, bias_ref, bias0_ref,
                         vc_ref, gb_ref, gc_ref, cw_ref, sc_ref, *rest):
    oa_ref, oc_ref, cs_ref, okt_ref, ovt_ref = rest[-5:]
    heads, head_dim, wb = kt_ref.shape
    newest = lax.broadcasted_iota(jnp.int32, (head_dim, wb), 1) == wb - 1
    n_br = len(DILATED_CONFIGS)
    for h in range(heads):
        kt = kt_ref[h]
        vt = vt_ref[h]
        q, k_new, v_new = q_ref[h], kn_ref[h], vn_ref[h]
        s = jnp.sum(kt * q, axis=0, keepdims=True) * ATTN_SCALE
        l_new = jnp.sum(q * k_new, axis=0, keepdims=True) * ATTN_SCALE + bias0_ref[h]
        branches = []
        for c in range(n_br):
            logits = s + bias_ref[c, h]
            m = jnp.maximum(jnp.max(logits, axis=-1, keepdims=True), l_new)
            p = jnp.exp(logits - m)
            p_new = jnp.exp(l_new - m)
            den = jnp.sum(p, axis=-1, keepdims=True) + p_new
            branches.append((p, p_new, m, den))
        mmax = branches[0][2]
        for br in branches[1:]:
            mmax = jnp.maximum(mmax, br[2])
        p_all = jnp.zeros_like(s)
        p_new_all = jnp.zeros_like(mmax)
        den_all = jnp.zeros_like(mmax)
        for p, p_new, m, den in branches:
            w = jnp.exp(m - mmax)
            p_all = p_all + w * p
            p_new_all = p_new_all + w * p_new
            den_all = den_all + w * den
        num = jnp.sum(vt * p_all, axis=-1, keepdims=True) + p_new_all * v_new
        oa_ref[h] = num / den_all
        okt_ref[h] = jnp.where(newest, k_new, pltpu.roll(kt, wb - 1, axis=1))
        ovt_ref[h] = jnp.where(newest, v_new, pltpu.roll(vt, wb - 1, axis=1))

    @pl.when(pl.program_id(1) == 0)
    def _():
        u = gc_ref[0] * vc_ref[0]
        state = sc_ref[0]
        y = cw_ref[0:1, :] * state[0:1]
        for i in range(1, CONV_WIDTH - 1):
            y = y + cw_ref[i:i + 1, :] * state[i:i + 1]
        y = y + cw_ref[CONV_WIDTH - 1:CONV_WIDTH, :] * u
        oc_ref[0] = gb_ref[0] * y
        cs_ref[0] = jnp.concatenate([state[1:], u], axis=0)


def _mixer_sample(layer, proj, kt_state, vt_state, rolled, bias_s, bias0, conv_w_l, state_conv):
    depth, batch, n_heads, head_dim, wb = kt_state.shape
    width = proj.shape[-1]
    hb = SAMPLE_HEADS_PER_STEP
    n_br = len(DILATED_CONFIGS)
    qkv = proj[SEG_Q:SEG_V + 1].reshape(3, batch, n_heads, head_dim, 1)
    rows = proj.reshape(N_SEGMENTS, batch, 1, width)
    col_specs = [pl.BlockSpec((None, None, hb, head_dim, 1), lambda b, hc, s=s: (s, b, hc, 0, 0))
                 for s in range(3)]
    row_specs = [pl.BlockSpec((None, 1, 1, width), lambda b, hc, s=s: (s, b, 0, 0))
                 for s in (SEG_VC, SEG_B, SEG_C)]
    buf_spec = pl.BlockSpec((None, None, hb, head_dim, wb), lambda b, hc: (layer, b, hc, 0, 0))
    in_specs = col_specs + [
        buf_spec, buf_spec,
        pl.BlockSpec((n_br, hb, 1, wb), lambda b, hc: (0, hc, 0, 0)),
        pl.BlockSpec((hb, 1, 1), lambda b, hc: (hc, 0, 0))] + row_specs + [
        pl.BlockSpec((CONV_WIDTH, width), lambda b, hc: (0, 0)),
        pl.BlockSpec((1, CONV_WIDTH - 1, width), lambda b, hc: (layer * batch + b, 0, 0))]
    args = [qkv, qkv, qkv, kt_state, vt_state, bias_s, bias0, rows, rows, rows, conv_w_l,
            state_conv.reshape(depth * batch, CONV_WIDTH - 1, width)]
    aliases = {}
    if rolled is not None:
        aliases = {len(args): 3, len(args) + 1: 4}
        in_specs = in_specs + [pl.BlockSpec(memory_space=pl.ANY)] * 2
        args = args + list(rolled)
    oa, oc, cs, okt, ovt = pl.pallas_call(
        _mixer_sample_kernel,
        grid=(batch, n_heads // hb),
        in_specs=in_specs,
        out_specs=[pl.BlockSpec((None, hb, head_dim, 1), lambda b, hc: (b, hc, 0, 0)),
                   pl.BlockSpec((1, 1, width), lambda b, hc: (b, 0, 0)),
                   pl.BlockSpec((1, CONV_WIDTH - 1, width), lambda b, hc: (b, 0, 0)),
                   buf_spec, buf_spec],
        out_shape=[jax.ShapeDtypeStruct((batch, n_heads, head_dim, 1), F32),
                   jax.ShapeDtypeStruct((batch, 1, width), F32),
                   jax.ShapeDtypeStruct((batch, CONV_WIDTH - 1, width), F32),
                   jax.ShapeDtypeStruct(kt_state.shape, F32),
                   jax.ShapeDtypeStruct(vt_state.shape, F32)],
        input_output_aliases=aliases,
        compiler_params=_compiler_params(("arbitrary", "arbitrary")),
        name="mixer_sample",
    )(*args)
    return oa, oc, cs, (okt, ovt)


def _outproj_kernel(oa_ref, oc_ref, ga_ref, gc_ref, w_ref, x_ref, o_ref, z_scr):
    n = pl.program_id(1)
    d_attn = oa_ref.shape[1]

    @pl.when(n == 0)
    def _():
        z_scr[:, 0:d_attn] = _rms_rows(oa_ref[...], ga_ref[...]).astype(BF16)
        z_scr[:, d_attn:] = _rms_rows(oc_ref[...], gc_ref[...]).astype(BF16)

    o_ref[...] = x_ref[...] + jnp.dot(z_scr[...], w_ref[...].astype(BF16),
                                      preferred_element_type=F32)


def _outproj(oa, oc, attn_gain, conv_gain, w_out, layer, x, *, tm, tn):
    m_rows, d_attn = oa.shape
    d_conv = oc.shape[1]
    d_model = x.shape[1]
    const = lambda m, n: (0, 0)
    return pl.pallas_call(
        _outproj_kernel,
        grid=(m_rows // tm, d_model // tn),
        in_specs=[pl.BlockSpec((tm, d_attn), lambda m, n: (m, 0)),
                  pl.BlockSpec((tm, d_conv), lambda m, n: (m, 0)),
                  pl.BlockSpec((1, d_attn), const),
                  pl.BlockSpec((1, d_conv), const),
                  pl.BlockSpec((None, d_attn + d_conv, tn), lambda m, n: (layer, 0, n)),
                  pl.BlockSpec((tm, tn), lambda m, n: (m, n))],
        out_specs=pl.BlockSpec((tm, tn), lambda m, n: (m, n)),
        out_shape=jax.ShapeDtypeStruct((m_rows, d_model), F32),
        scratch_shapes=[pltpu.VMEM((tm, d_attn + d_conv), BF16)],
        compiler_params=_compiler_params(("arbitrary", "arbitrary")),
        name="outproj",
    )(oa, oc, attn_gain.reshape(1, d_attn), conv_gain.reshape(1, d_conv), w_out, x)


def _mlp_kernel(x_ref, g_ref, wu_ref, wd_ref, o_ref, h_scr):
    f = pl.program_id(1)

    @pl.when(f == 0)
    def _():
        x = x_ref[...]
        h_scr[...] = _rms_rows(x, g_ref[...]).astype(BF16)
        o_ref[...] = x

    a = jnp.dot(h_scr[...], wu_ref[...].astype(BF16), preferred_element_type=F32)
    a = jnp.square(jnp.maximum(a, 0.0)).astype(BF16)
    o_ref[...] += jnp.dot(a, wd_ref[...].astype(BF16), preferred_element_type=F32)


def _mlp(x, gain, w_up, w_down, layer, *, tm, tf):
    m_rows, d_model = x.shape
    d_ff = w_up.shape[2]
    return pl.pallas_call(
        _mlp_kernel,
        grid=(m_rows // tm, d_ff // tf),
        in_specs=[pl.BlockSpec((tm, d_model), lambda m, f: (m, 0), pipeline_mode=pl.Buffered(1)),
                  pl.BlockSpec((1, d_model), lambda m, f: (0, 0)),
                  pl.BlockSpec((None, d_model, tf), lambda m, f: (layer, 0, f)),
                  pl.BlockSpec((None, tf, d_model), lambda m, f: (layer, f, 0))],
        out_specs=pl.BlockSpec((tm, d_model), lambda m, f: (m, 0)),
        out_shape=jax.ShapeDtypeStruct((m_rows, d_model), F32),
        scratch_shapes=[pltpu.VMEM((tm, d_model), BF16)],
        compiler_params=_compiler_params(("arbitrary", "arbitrary")),
        name="mlp",
    )(x, gain.reshape(1, d_model), w_up, w_down)


PROMPT_TILES = dict(tm=1024, tn=512, tf=512)
SAMPLE_TILES = dict(tn=512, tf=512)


def kernel(x_prompt, x_sample, state_attn_k, state_attn_v, state_conv, rel_bias, norm_mix, w_in,
           q_norm, k_norm, conv_w, attn_out_norm, conv_out_norm, w_out, norm_mlp, w_up, w_down):
    batch, seq, d_model = x_prompt.shape
    dec_batch, dec_seq, _ = x_sample.shape
    depth, _, wb, n_heads, head_dim = state_attn_k.shape
    assert dec_seq == 1 and head_dim == HEAD_DIM and wb == max(w for w, _ in DILATED_CONFIGS)
    assert seq == wb

    bias_p = _prompt_bias_tables(rel_bias)
    bias_s, bias0 = _sample_bias_tables(rel_bias, wb)
    kt_state = state_attn_k.transpose(0, 1, 3, 4, 2)
    vt_state = state_attn_v.transpose(0, 1, 3, 4, 2)
    rolled = None
    new_windows = None

    xp = x_prompt.reshape(batch * seq, d_model)
    xs = x_sample.reshape(dec_batch, d_model)
    pt, st = PROMPT_TILES, SAMPLE_TILES
    cp_new, cs_new = [], []
    for l in range(depth):
        proj = _inproj(xp, norm_mix[l], w_in, l, q_norm[l], k_norm[l], tm=pt["tm"], tn=pt["tn"])
        width = proj.shape[-1]
        oa, oc, cs, new_windows = _mixer_prompt(
            l, depth, proj.reshape(N_SEGMENTS, batch, seq, width), conv_w[l], bias_p, new_windows)
        xp = _outproj(oa.reshape(batch * seq, width), oc.reshape(batch * seq, width),
                      attn_out_norm[l], conv_out_norm[l], w_out, l, xp, tm=pt["tm"], tn=pt["tn"])
        xp = _mlp(xp, norm_mlp[l], w_up, w_down, l, tm=pt["tm"], tf=pt["tf"])
        cp_new.append(cs)

        proj = _inproj(xs, norm_mix[l], w_in, l, q_norm[l], k_norm[l], tm=dec_batch, tn=st["tn"])
        oa, oc, cs, rolled = _mixer_sample(l, proj, kt_state, vt_state, rolled, bias_s, bias0,
                                           conv_w[l], state_conv)
        xs = _outproj(oa.reshape(dec_batch, width), oc.reshape(dec_batch, width),
                      attn_out_norm[l], conv_out_norm[l], w_out, l, xs, tm=dec_batch, tn=st["tn"])
        xs = _mlp(xs, norm_mlp[l], w_up, w_down, l, tm=dec_batch, tf=st["tf"])
        cs_new.append(cs)

    ks_new, vs_new = (r.transpose(0, 1, 4, 2, 3) for r in rolled)
    kp_new, vp_new = (w.transpose(0, 1, 4, 2, 3) for w in new_windows)
    return (xp.reshape(batch, seq, d_model), xs.reshape(dec_batch, dec_seq, d_model),
            kp_new, vp_new, jnp.stack(cp_new), ks_new, vs_new, jnp.stack(cs_new))
```

```python
import functools

import jax
import jax.numpy as jnp
import numpy as np
from jax import lax
from jax.experimental import pallas as pl
from jax.experimental.pallas import tpu as pltpu

HEAD_DIM = 64
CONV_WIDTH = 3
DILATED_CONFIGS = ((128, 1), (512, 4), (2048, 16))
N_BUCKETS = 32
MAX_DISTANCE = 2048
EPS = 1e-6
ATTN_SCALE = HEAD_DIM ** -0.5
MASKED = -1e30

LANES = 128
Q_BLOCK = 128
HEADS_PER_BLOCK = LANES // HEAD_DIM
VMEM_LIMIT_BYTES = 56 * 1024 * 1024

F32 = jnp.float32
BF16 = jnp.bfloat16


def _compiler_params(semantics):
    return pltpu.CompilerParams(dimension_semantics=semantics,
                                vmem_limit_bytes=VMEM_LIMIT_BYTES)


def _rms_rows(x, gain):
    ms = jnp.mean(x * x, axis=-1, keepdims=True)
    return (x * lax.rsqrt(ms + EPS)) * gain


N_SEGMENTS = 6
SEG_Q, SEG_K, SEG_V, SEG_VC, SEG_B, SEG_C = range(N_SEGMENTS)


def _bf16_weight(w_ref, copy_ref):
    w = w_ref[...]
    if w.dtype != BF16:
        w = w.astype(BF16)
    if copy_ref is not None:
        copy_ref[...] = w
    return w


def _inproj_kernel(x_ref, g_ref, w_ref, qg_ref, kg_ref, grp_ref, o_ref, *rest, tiles_per_seg):
    h_scr = rest[-1]
    wcopy_ref = rest[0] if len(rest) == 2 else None
    n = pl.program_id(1)

    @pl.when(n == 0)
    def _():
        h_scr[...] = _rms_rows(x_ref[...], g_ref[...]).astype(BF16)

    y = jnp.dot(h_scr[...], _bf16_weight(w_ref, wcopy_ref), preferred_element_type=F32)
    seg = n // tiles_per_seg

    def head_norm(gain_ref):
        ms = jnp.dot((y * y).astype(BF16), grp_ref[...], preferred_element_type=F32)
        return (y * lax.rsqrt(ms + EPS)) * gain_ref[...]

    @pl.when(seg == SEG_Q)
    def _():
        o_ref[0] = head_norm(qg_ref)

    @pl.when(seg == SEG_K)
    def _():
        o_ref[0] = head_norm(kg_ref)

    @pl.when(seg > SEG_K)
    def _():
        o_ref[0] = y


def _weight_spec(w, layer, block, index_map):
    if w.ndim == 3:
        return pl.BlockSpec((None,) + block, lambda *ids: (layer,) + index_map(*ids))
    return pl.BlockSpec(block, index_map)


def _inproj(x, gain, w_in, layer, q_gain, k_gain, *, tm, tn):
    m_rows, d_model = x.shape
    d_in = w_in.shape[-1]
    emit_copy = w_in.ndim == 3
    seg_w = d_in // N_SEGMENTS
    tiles_per_seg = seg_w // tn
    grp = jnp.asarray(
        (np.arange(tn)[:, None] // HEAD_DIM == np.arange(tn)[None, :] // HEAD_DIM)
        .astype(np.float32) / HEAD_DIM, BF16)
    qg = jnp.tile(q_gain, tn // HEAD_DIM).reshape(1, tn)
    kg = jnp.tile(k_gain, tn // HEAD_DIM).reshape(1, tn)
    const = lambda m, n: (0, 0)
    out_specs = [pl.BlockSpec(
        (1, tm, tn), lambda m, n: (n // tiles_per_seg, m, n % tiles_per_seg))]
    out_shape = [jax.ShapeDtypeStruct((N_SEGMENTS, m_rows, seg_w), F32)]
    if emit_copy:
        assert m_rows == tm, "the bf16 copy is written once per column tile"
        out_specs.append(pl.BlockSpec((d_model, tn), lambda m, n: (0, n)))
        out_shape.append(jax.ShapeDtypeStruct((d_model, d_in), BF16))
    outs = pl.pallas_call(
        functools.partial(_inproj_kernel, tiles_per_seg=tiles_per_seg),
        grid=(m_rows // tm, d_in // tn),
        in_specs=[pl.BlockSpec((tm, d_model), lambda m, n: (m, 0)),
                  pl.BlockSpec((1, d_model), const),
                  _weight_spec(w_in, layer, (d_model, tn), lambda m, n: (0, n)),
                  pl.BlockSpec((1, tn), const),
                  pl.BlockSpec((1, tn), const),
                  pl.BlockSpec((tn, tn), const)],
        out_specs=out_specs,
        out_shape=out_shape,
        scratch_shapes=[pltpu.VMEM((tm, d_model), BF16)],
        compiler_params=_compiler_params(("arbitrary", "arbitrary")),
        name="inproj",
    )(x, gain.reshape(1, d_model), w_in, qg, kg, grp)
    return outs if emit_copy else outs[0]


def _t5_bucket(dist):
    n_exact = N_BUCKETS // 2
    large = n_exact + (np.log(np.maximum(dist, 1) / n_exact) / np.log(MAX_DISTANCE / n_exact)
                       * (N_BUCKETS - n_exact)).astype(np.int32)
    large = np.minimum(large, N_BUCKETS - 1)
    return np.where(dist < n_exact, dist, large).astype(np.int32)


def _bias_lookup(rel_bias, buckets):
    onehot = buckets[..., None] == np.arange(rel_bias.shape[0])
    return jnp.sum(jnp.where(onehot[..., None], rel_bias, 0.0), axis=-2)


def _prompt_bias_tables(rel_bias):
    n_heads = rel_bias.shape[1]
    qi = np.arange(Q_BLOCK)[:, None]
    kj = np.arange(2 * Q_BLOCK)[None, :]
    tables = []
    for window, dil in DILATED_CONFIGS:
        nw = window // dil
        assert nw == Q_BLOCK
        rel = qi - kj + nw
        valid = (rel >= 0) & (rel <= nw)
        bias = _bias_lookup(rel_bias, _t5_bucket(np.clip(rel, 0, nw) * dil))
        per_first = []
        for first in (False, True):
            ok = valid & (kj >= Q_BLOCK) if first else valid
            per_first.append(jnp.where(ok[:, :, None], bias, MASKED))
        t = jnp.stack(per_first)
        t = t.transpose(3, 0, 1, 2)
        t = t.reshape(n_heads // HEADS_PER_BLOCK, HEADS_PER_BLOCK, 2, Q_BLOCK, 2 * Q_BLOCK)
        t = t.transpose(0, 2, 1, 3, 4).reshape(n_heads // HEADS_PER_BLOCK, 2,
                                                HEADS_PER_BLOCK * Q_BLOCK, 2 * Q_BLOCK)
        tables.append(t)
    return jnp.stack(tables)


def _mixer_prompt_kernel(q_ref, k_ref, v_ref, vc_ref, gb_ref, gc_ref, cw_ref, bias_ref, *rest,
                         seq, n_aliased):
    (oa_ref, oc_ref, cs_ref, okt_ref, ovt_ref,
     qs_scr, kd_scr, v0_scr, v1_scr, o_scr, m_scr, d_scr, u_scr) = rest[n_aliased:]

    for src, dst in ((k_ref, okt_ref), (v_ref, ovt_ref)):
        t = src[0].T
        for h in range(HEADS_PER_BLOCK):
            dst[h] = t[h * HEAD_DIM:(h + 1) * HEAD_DIM]

    nblk_total = seq // Q_BLOCK
    lane = lax.broadcasted_iota(jnp.int32, (Q_BLOCK, LANES), 1)
    head0 = lane < HEAD_DIM

    zeros_pad = jnp.zeros((Q_BLOCK, LANES), BF16)
    kd_scr[0:Q_BLOCK, :] = zeros_pad
    v0_scr[0:Q_BLOCK, :] = zeros_pad
    v1_scr[0:Q_BLOCK, :] = zeros_pad

    for c, (window, dil) in enumerate(DILATED_CONFIGS):
        sub_len = seq // dil
        blocks_per_sub = sub_len // Q_BLOCK

        for r in range(dil):
            rows = pl.ds(r, sub_len, stride=dil) if dil > 1 else pl.ds(0, seq)
            head0_rows = lax.broadcasted_iota(jnp.int32, (sub_len, LANES), 1) < HEAD_DIM
            qv = (q_ref[0, rows, :] * ATTN_SCALE).astype(BF16)
            kv = k_ref[0, rows, :].astype(BF16)
            vv = v_ref[0, rows, :].astype(BF16)
            zero = jnp.zeros_like(qv)
            q0 = jnp.where(head0_rows, qv, zero)
            q1 = jnp.where(head0_rows, zero, qv)
            for gb in range(blocks_per_sub):
                g = r * blocks_per_sub + gb
                src = slice(gb * Q_BLOCK, (gb + 1) * Q_BLOCK)
                qs_scr[2 * g * Q_BLOCK:(2 * g + 1) * Q_BLOCK, :] = q0[src]
                qs_scr[(2 * g + 1) * Q_BLOCK:(2 * g + 2) * Q_BLOCK, :] = q1[src]
            dst = slice(Q_BLOCK + r * sub_len, Q_BLOCK + (r + 1) * sub_len)
            kd_scr[dst, :] = kv
            v0_scr[dst, :] = jnp.where(head0_rows, vv, zero)
            v1_scr[dst, :] = jnp.where(head0_rows, zero, vv)

        for g in range(nblk_total):
            r, gb = divmod(g, blocks_per_sub)
            first = 1 if gb == 0 else 0
            lhs = qs_scr[2 * g * Q_BLOCK:(2 * g + 2) * Q_BLOCK, :]
            win = slice(g * Q_BLOCK, (g + 2) * Q_BLOCK)
            logits = lax.dot_general(lhs, kd_scr[win, :], (((1,), (1,)), ((), ())),
                                     preferred_element_type=F32)
            logits = logits + bias_ref[c, 0, first]
            m = jnp.max(logits, axis=-1, keepdims=True)
            p = jnp.exp(logits - m)
            den = jnp.sum(p, axis=-1, keepdims=True)
            pb = p.astype(BF16)
            o = (jnp.dot(pb[0:Q_BLOCK], v0_scr[win, :], preferred_element_type=F32)
                 + jnp.dot(pb[Q_BLOCK:], v1_scr[win, :], preferred_element_type=F32))
            m2 = jnp.where(head0, m[0:Q_BLOCK], m[Q_BLOCK:])
            d2 = jnp.where(head0, den[0:Q_BLOCK], den[Q_BLOCK:])
            if dil > 1:
                out_rows = pl.ds(gb * Q_BLOCK * dil + r, Q_BLOCK, stride=dil)
            else:
                out_rows = pl.ds(g * Q_BLOCK, Q_BLOCK)
            o_scr[c, out_rows, :] = o
            m_scr[c, out_rows, :] = m2
            d_scr[c, out_rows, :] = d2

    n_br = len(DILATED_CONFIGS)
    mmax = m_scr[0]
    for c in range(1, n_br):
        mmax = jnp.maximum(mmax, m_scr[c])
    num = jnp.zeros((seq, LANES), F32)
    den = jnp.zeros((seq, LANES), F32)
    for c in range(n_br):
        w = jnp.exp(m_scr[c] - mmax)
        num = num + w * o_scr[c]
        den = den + w * d_scr[c]
    oa_ref[0] = num / den

    u = gc_ref[0] * vc_ref[0]
    u_scr[0:8, :] = jnp.zeros((8, LANES), F32)
    u_scr[8:8 + seq, :] = u
    y = cw_ref[0:1, :] * u_scr[pl.ds(8 - (CONV_WIDTH - 1), seq), :]
    for i in range(1, CONV_WIDTH):
        y = y + cw_ref[i:i + 1, :] * u_scr[pl.ds(8 - (CONV_WIDTH - 1) + i, seq), :]
    oc_ref[0] = gb_ref[0] * y
    cs_ref[0] = u_scr[pl.ds(8 + seq - (CONV_WIDTH - 1), CONV_WIDTH - 1), :]


def _mixer_prompt(layer, depth, proj, conv_w_l, bias_tables, new_windows):
    _, batch, seq, width = proj.shape
    n_pairs = width // LANES
    n_heads = width // HEAD_DIM
    n_br = len(DILATED_CONFIGS)
    tok = pl.BlockSpec((1, seq, LANES), lambda b, hp: (b, 0, hp))
    seg_specs = [pl.BlockSpec((None, 1, seq, LANES), lambda b, hp, s=s: (s, b, 0, hp))
                 for s in range(N_SEGMENTS)]
    win_spec = pl.BlockSpec((None, None, HEADS_PER_BLOCK, HEAD_DIM, seq),
                            lambda b, hp: (layer, b, hp, 0, 0))
    win_shape = jax.ShapeDtypeStruct((depth, batch, n_heads, HEAD_DIM, seq), F32)
    in_specs = seg_specs + [
        pl.BlockSpec((CONV_WIDTH, LANES), lambda b, hp: (0, hp)),
        pl.BlockSpec((n_br, 1, 2, 2 * Q_BLOCK, 2 * Q_BLOCK), lambda b, hp: (0, hp, 0, 0, 0))]
    args = [proj] * N_SEGMENTS + [conv_w_l, bias_tables]
    aliases = {}
    if new_windows is not None:
        aliases = {len(args): 3, len(args) + 1: 4}
        in_specs = in_specs + [pl.BlockSpec(memory_space=pl.ANY)] * 2
        args = args + list(new_windows)
    oa, oc, cs, okt, ovt = pl.pallas_call(
        functools.partial(_mixer_prompt_kernel, seq=seq, n_aliased=len(aliases)),
        grid=(batch, n_pairs),
        in_specs=in_specs,
        out_specs=[tok, tok,
                   pl.BlockSpec((1, CONV_WIDTH - 1, LANES), lambda b, hp: (b, 0, hp)),
                   win_spec, win_spec],
        out_shape=[jax.ShapeDtypeStruct((batch, seq, width), F32),
                   jax.ShapeDtypeStruct((batch, seq, width), F32),
                   jax.ShapeDtypeStruct((batch, CONV_WIDTH - 1, width), F32),
                   win_shape, win_shape],
        input_output_aliases=aliases,
        scratch_shapes=[pltpu.VMEM((2 * seq, LANES), BF16),
                        pltpu.VMEM((seq + Q_BLOCK, LANES), BF16),
                        pltpu.VMEM((seq + Q_BLOCK, LANES), BF16),
                        pltpu.VMEM((seq + Q_BLOCK, LANES), BF16),
                        pltpu.VMEM((n_br, seq, LANES), F32),
                        pltpu.VMEM((n_br, seq, LANES), F32),
                        pltpu.VMEM((n_br, seq, LANES), F32),
                        pltpu.VMEM((seq + 8, LANES), F32)],
        compiler_params=_compiler_params(("arbitrary", "arbitrary")),
        name="mixer_prompt",
    )(*args)
    return oa, oc, cs, (okt, ovt)


SAMPLE_HEADS_PER_STEP = 4


def _sample_bias_tables(rel_bias, wb):
    dist = wb - np.arange(wb)
    tables = []
    for window, dil in DILATED_CONFIGS:
        in_branch = (dist % dil == 0) & (dist // dil <= window // dil)
        bias = jnp.where(in_branch[:, None], _bias_lookup(rel_bias, _t5_bucket(dist)), MASKED)
        tables.append(bias.T[:, None, :])
    own = rel_bias[_t5_bucket(np.zeros((1,), np.int64))][0]
    return jnp.stack(tables), own[:, None, None]


def _mixer_sample_kernel(q_ref, kn_ref, vn_ref, kt_ref, vt_ref, bias_ref, bias0_ref,
                         vc_ref, gb_ref, gc_ref, cw_ref, sc_ref, *rest):
    oa_ref, oc_ref, cs_ref, okt_ref, ovt_ref = rest[-5:]
    heads, head_dim, wb = kt_ref.shape
    newest = lax.broadcasted_iota(jnp.int32, (head_dim, wb), 1) == wb - 1
    n_br = len(DILATED_CONFIGS)
    for h in range(heads):
        kt = kt_ref[h]
        vt = vt_ref[h]
        q, k_new, v_new = q_ref[h], kn_ref[h], vn_ref[h]
        s = jnp.sum(kt * q, axis=0, keepdims=True) * ATTN_SCALE
        l_new = jnp.sum(q * k_new, axis=0, keepdims=True) * ATTN_SCALE + bias0_ref[h]
        branches = []
        for c in range(n_br):
            logits = s + bias_ref[c, h]
            m = jnp.maximum(jnp.max(logits, axis=-1, keepdims=True), l_new)
            p = jnp.exp(logits - m)
            p_new = jnp.exp(l_new - m)
            den = jnp.sum(p, axis=-1, keepdims=True) + p_new
            branches.append((p, p_new, m, den))
        mmax = branches[0][2]
        for br in branches[1:]:
            mmax = jnp.maximum(mmax, br[2])
        p_all = jnp.zeros_like(s)
        p_new_all = jnp.zeros_like(mmax)
        den_all = jnp.zeros_like(mmax)
        for p, p_new, m, den in branches:
            w = jnp.exp(m - mmax)
            p_all = p_all + w * p
            p_new_all = p_new_all + w * p_new
            den_all = den_all + w * den
        num = jnp.sum(vt * p_all, axis=-1, keepdims=True) + p_new_all * v_new
        oa_ref[h] = num / den_all
        okt_ref[h] = jnp.where(newest, k_new, pltpu.roll(kt, wb - 1, axis=1))
        ovt_ref[h] = jnp.where(newest, v_new, pltpu.roll(vt, wb - 1, axis=1))

    @pl.when(pl.program_id(1) == 0)
    def _():
        u = gc_ref[0] * vc_ref[0]
        state = sc_ref[0]
        y = cw_ref[0:1, :] * state[0:1]
        for i in range(1, CONV_WIDTH - 1):
            y = y + cw_ref[i:i + 1, :] * state[i:i + 1]
        y = y + cw_ref[CONV_WIDTH - 1:CONV_WIDTH, :] * u
        oc_ref[0] = gb_ref[0] * y
        cs_ref[0] = jnp.concatenate([state[1:], u], axis=0)


def _mixer_sample(layer, proj, kt_state, vt_state, rolled, bias_s, bias0, conv_w_l, state_conv):
    depth, batch, n_heads, head_dim, wb = kt_state.shape
    width = proj.shape[-1]
    hb = SAMPLE_HEADS_PER_STEP
    n_br = len(DILATED_CONFIGS)
    qkv = proj[SEG_Q:SEG_V + 1].reshape(3, batch, n_heads, head_dim, 1)
    rows = proj.reshape(N_SEGMENTS, batch, 1, width)
    col_specs = [pl.BlockSpec((None, None, hb, head_dim, 1), lambda b, hc, s=s: (s, b, hc, 0, 0))
                 for s in range(3)]
    row_specs = [pl.BlockSpec((None, 1, 1, width), lambda b, hc, s=s: (s, b, 0, 0))
                 for s in (SEG_VC, SEG_B, SEG_C)]
    buf_spec = pl.BlockSpec((None, None, hb, head_dim, wb), lambda b, hc: (layer, b, hc, 0, 0))
    in_specs = col_specs + [
        buf_spec, buf_spec,
        pl.BlockSpec((n_br, hb, 1, wb), lambda b, hc: (0, hc, 0, 0)),
        pl.BlockSpec((hb, 1, 1), lambda b, hc: (hc, 0, 0))] + row_specs + [
        pl.BlockSpec((CONV_WIDTH, width), lambda b, hc: (0, 0)),
        pl.BlockSpec((1, CONV_WIDTH - 1, width), lambda b, hc: (layer * batch + b, 0, 0))]
    args = [qkv, qkv, qkv, kt_state, vt_state, bias_s, bias0, rows, rows, rows, conv_w_l,
            state_conv.reshape(depth * batch, CONV_WIDTH - 1, width)]
    aliases = {}
    if rolled is not None:
        aliases = {len(args): 3, len(args) + 1: 4}
        in_specs = in_specs + [pl.BlockSpec(memory_space=pl.ANY)] * 2
        args = args + list(rolled)
    oa, oc, cs, okt, ovt = pl.pallas_call(
        _mixer_sample_kernel,
        grid=(batch, n_heads // hb),
        in_specs=in_specs,
        out_specs=[pl.BlockSpec((None, hb, head_dim, 1), lambda b, hc: (b, hc, 0, 0)),
                   pl.BlockSpec((1, 1, width), lambda b, hc: (b, 0, 0)),
                   pl.BlockSpec((1, CONV_WIDTH - 1, width), lambda b, hc: (b, 0, 0)),
                   buf_spec, buf_spec],
        out_shape=[jax.ShapeDtypeStruct((batch, n_heads, head_dim, 1), F32),
                   jax.ShapeDtypeStruct((batch, 1, width), F32),
                   jax.ShapeDtypeStruct((batch, CONV_WIDTH - 1, width), F32),
                   jax.ShapeDtypeStruct(kt_state.shape, F32),
                   jax.ShapeDtypeStruct(vt_state.shape, F32)],
        input_output_aliases=aliases,
        compiler_params=_compiler_params(("arbitrary", "arbitrary")),
        name="mixer_sample",
    )(*args)
    return oa, oc, cs, (okt, ovt)


def _outproj_kernel(oa_ref, oc_ref, ga_ref, gc_ref, w_ref, x_ref, o_ref, *rest):
    z_scr = rest[-1]
    wcopy_ref = rest[0] if len(rest) == 2 else None
    n = pl.program_id(1)
    d_attn = oa_ref.shape[1]
    tn = o_ref.shape[1]

    @pl.when(n == 0)
    def _():
        z_scr[:, 0:d_attn] = _rms_rows(oa_ref[...], ga_ref[...]).astype(BF16)
        z_scr[:, d_attn:] = _rms_rows(oc_ref[...], gc_ref[...]).astype(BF16)

    if w_ref.shape[1] == tn:
        w = _bf16_weight(w_ref, wcopy_ref)
    else:
        w = w_ref[:, pl.ds(pl.multiple_of(n * tn, tn), tn)]
    o_ref[...] = x_ref[...] + jnp.dot(z_scr[...], w, preferred_element_type=F32)


def _outproj(oa, oc, attn_gain, conv_gain, w_out, layer, x, *, tm, tn):
    m_rows, d_attn = oa.shape
    d_conv = oc.shape[1]
    d_model = x.shape[1]
    d_mix = d_attn + d_conv
    emit_copy = w_out.ndim == 3
    const = lambda m, n: (0, 0)
    out_specs = [pl.BlockSpec((tm, tn), lambda m, n: (m, n))]
    out_shape = [jax.ShapeDtypeStruct((m_rows, d_model), F32)]
    if emit_copy:
        assert m_rows == tm, "the bf16 copy is written once per column tile"
        w_spec = _weight_spec(w_out, layer, (d_mix, tn), lambda m, n: (0, n))
        out_specs.append(pl.BlockSpec((d_mix, tn), lambda m, n: (0, n)))
        out_shape.append(jax.ShapeDtypeStruct((d_mix, d_model), BF16))
    else:
        w_spec = pl.BlockSpec((d_mix, d_model), const)
    outs = pl.pallas_call(
        _outproj_kernel,
        grid=(m_rows // tm, d_model // tn),
        in_specs=[pl.BlockSpec((tm, d_attn), lambda m, n: (m, 0)),
                  pl.BlockSpec((tm, d_conv), lambda m, n: (m, 0)),
                  pl.BlockSpec((1, d_attn), const),
                  pl.BlockSpec((1, d_conv), const),
                  w_spec,
                  pl.BlockSpec((tm, tn), lambda m, n: (m, n))],
        out_specs=out_specs,
        out_shape=out_shape,
        scratch_shapes=[pltpu.VMEM((tm, d_mix), BF16)],
        compiler_params=_compiler_params(("arbitrary", "arbitrary")),
        name="outproj",
    )(oa, oc, attn_gain.reshape(1, d_attn), conv_gain.reshape(1, d_conv), w_out, x)
    return outs if emit_copy else outs[0]


def _mlp_kernel(x_ref, g_ref, wu_ref, wd_ref, o_ref, *rest):
    h_scr = rest[-1]
    wu_copy_ref, wd_copy_ref = rest[:2] if len(rest) == 3 else (None, None)
    f = pl.program_id(1)

    @pl.when(f == 0)
    def _():
        x = x_ref[...]
        h_scr[...] = _rms_rows(x, g_ref[...]).astype(BF16)
        o_ref[...] = x

    a = jnp.dot(h_scr[...], _bf16_weight(wu_ref, wu_copy_ref), preferred_element_type=F32)
    a = jnp.square(jnp.maximum(a, 0.0)).astype(BF16)
    o_ref[...] += jnp.dot(a, _bf16_weight(wd_ref, wd_copy_ref), preferred_element_type=F32)


def _mlp(x, gain, w_up, w_down, layer, *, tm, tf):
    m_rows, d_model = x.shape
    d_ff = w_up.shape[-1]
    emit_copy = w_up.ndim == 3
    out_specs = [pl.BlockSpec((tm, d_model), lambda m, f: (m, 0))]
    out_shape = [jax.ShapeDtypeStruct((m_rows, d_model), F32)]
    if emit_copy:
        assert m_rows == tm, "the bf16 copies are written once per d_ff tile"
        out_specs += [pl.BlockSpec((d_model, tf), lambda m, f: (0, f)),
                      pl.BlockSpec((tf, d_model), lambda m, f: (f, 0))]
        out_shape += [jax.ShapeDtypeStruct((d_model, d_ff), BF16),
                      jax.ShapeDtypeStruct((d_ff, d_model), BF16)]
    outs = pl.pallas_call(
        _mlp_kernel,
        grid=(m_rows // tm, d_ff // tf),
        in_specs=[pl.BlockSpec((tm, d_model), lambda m, f: (m, 0), pipeline_mode=pl.Buffered(1)),
                  pl.BlockSpec((1, d_model), lambda m, f: (0, 0)),
                  _weight_spec(w_up, layer, (d_model, tf), lambda m, f: (0, f)),
                  _weight_spec(w_down, layer, (tf, d_model), lambda m, f: (f, 0))],
        out_specs=out_specs,
        out_shape=out_shape,
        scratch_shapes=[pltpu.VMEM((tm, d_model), BF16)],
        compiler_params=_compiler_params(("arbitrary", "arbitrary")),
        name="mlp",
    )(x, gain.reshape(1, d_model), w_up, w_down)
    return outs if emit_copy else outs[0]


PROMPT_TILES = dict(tm=1024, tn=512, tf=1024)
SAMPLE_TILES = dict(tn=512, tf=512)


def kernel(x_prompt, x_sample, state_attn_k, state_attn_v, state_conv, rel_bias, norm_mix, w_in,
           q_norm, k_norm, conv_w, attn_out_norm, conv_out_norm, w_out, norm_mlp, w_up, w_down):
    batch, seq, d_model = x_prompt.shape
    dec_batch, dec_seq, _ = x_sample.shape
    depth, _, wb, n_heads, head_dim = state_attn_k.shape
    assert dec_seq == 1 and head_dim == HEAD_DIM and wb == max(w for w, _ in DILATED_CONFIGS)
    assert seq == wb

    bias_p = _prompt_bias_tables(rel_bias)
    bias_s, bias0 = _sample_bias_tables(rel_bias, wb)
    kt_state = state_attn_k.transpose(0, 1, 3, 4, 2)
    vt_state = state_attn_v.transpose(0, 1, 3, 4, 2)
    rolled = None
    new_windows = None

    xp = x_prompt.reshape(batch * seq, d_model)
    xs = x_sample.reshape(dec_batch, d_model)
    pt, st = PROMPT_TILES, SAMPLE_TILES
    cp_new, cs_new = [], []
    for l in range(depth):
        proj, w_in_l = _inproj(xs, norm_mix[l], w_in, l, q_norm[l], k_norm[l],
                               tm=dec_batch, tn=st["tn"])
        width = proj.shape[-1]
        oa, oc, cs, rolled = _mixer_sample(l, proj, kt_state, vt_state, rolled, bias_s, bias0,
                                           conv_w[l], state_conv)
        xs, w_out_l = _outproj(oa.reshape(dec_batch, width), oc.reshape(dec_batch, width),
                               attn_out_norm[l], conv_out_norm[l], w_out, l, xs,
                               tm=dec_batch, tn=st["tn"])
        xs, w_up_l, w_down_l = _mlp(xs, norm_mlp[l], w_up, w_down, l, tm=dec_batch, tf=st["tf"])
        cs_new.append(cs)

        proj = _inproj(xp, norm_mix[l], w_in_l, l, q_norm[l], k_norm[l], tm=pt["tm"], tn=pt["tn"])
        oa, oc, cs, new_windows = _mixer_prompt(
            l, depth, proj.reshape(N_SEGMENTS, batch, seq, width), conv_w[l], bias_p, new_windows)
        xp = _outproj(oa.reshape(batch * seq, width), oc.reshape(batch * seq, width),
                      attn_out_norm[l], conv_out_norm[l], w_out_l, l, xp, tm=pt["tm"], tn=pt["tn"])
        xp = _mlp(xp, norm_mlp[l], w_up_l, w_down_l, l, tm=pt["tm"], tf=pt["tf"])
        cp_new.append(cs)

    ks_new, vs_new = (r.transpose(0, 1, 4, 2, 3) for r in rolled)
    kp_new, vp_new = (w.transpose(0, 1, 4, 2, 3) for w in new_windows)
    return (xp.reshape(batch, seq, d_model), xs.reshape(dec_batch, dec_seq, d_model),
            kp_new, vp_new, jnp.stack(cp_new), ks_new, vs_new, jnp.stack(cs_new))
```

```python
import functools

import jax
import jax.numpy as jnp
import numpy as np
from jax import lax
from jax.experimental import pallas as pl
from jax.experimental.pallas import tpu as pltpu

HEAD_DIM = 64
CONV_WIDTH = 3
DILATED_CONFIGS = ((128, 1), (512, 4), (2048, 16))
N_BUCKETS = 32
MAX_DISTANCE = 2048
EPS = 1e-6
ATTN_SCALE = HEAD_DIM ** -0.5
MASKED = -1e30

LANES = 128
Q_BLOCK = 128
MERGE_ROWS = 128
HEADS_PER_BLOCK = LANES // HEAD_DIM
VMEM_LIMIT_BYTES = 56 * 1024 * 1024

F32 = jnp.float32
BF16 = jnp.bfloat16


def _compiler_params(semantics, flags=None):
    return pltpu.CompilerParams(dimension_semantics=semantics,
                                vmem_limit_bytes=VMEM_LIMIT_BYTES, flags=flags)


def _rms_rows(x, gain):
    ms = jnp.mean(x * x, axis=-1, keepdims=True)
    return (x * lax.rsqrt(ms + EPS)) * gain


N_SEGMENTS = 6
SEG_Q, SEG_K, SEG_V, SEG_VC, SEG_B, SEG_C = range(N_SEGMENTS)


def _bf16_weight(w_ref, copy_ref):
    w = w_ref[...]
    if w.dtype != BF16:
        w = w.astype(BF16)
    if copy_ref is not None:
        copy_ref[...] = w
    return w


def _inproj_kernel(x_ref, g_ref, w_ref, qg_ref, kg_ref, grp_ref, o_ref, *rest, tiles_per_seg):
    h_scr = rest[-1]
    wcopy_ref = rest[0] if len(rest) == 2 else None
    n = pl.program_id(1)

    @pl.when(n == 0)
    def _():
        h_scr[...] = _rms_rows(x_ref[...], g_ref[...]).astype(BF16)

    y = jnp.dot(h_scr[...], _bf16_weight(w_ref, wcopy_ref), preferred_element_type=F32)
    seg = n // tiles_per_seg

    def head_norm(gain_ref):
        ms = jnp.dot((y * y).astype(BF16), grp_ref[...], preferred_element_type=F32)
        return (y * lax.rsqrt(ms + EPS)) * gain_ref[...]

    @pl.when(seg == SEG_Q)
    def _():
        o_ref[0] = head_norm(qg_ref)

    @pl.when(seg == SEG_K)
    def _():
        o_ref[0] = head_norm(kg_ref)

    @pl.when(seg > SEG_K)
    def _():
        o_ref[0] = y


def _weight_spec(w, layer, block, index_map):
    if w.ndim == 3:
        return pl.BlockSpec((None,) + block, lambda *ids: (layer,) + index_map(*ids))
    return pl.BlockSpec(block, index_map)


def _inproj(x, gain, w_in, layer, q_gain, k_gain, *, tm, tn):
    m_rows, d_model = x.shape
    d_in = w_in.shape[-1]
    emit_copy = w_in.ndim == 3
    seg_w = d_in // N_SEGMENTS
    tiles_per_seg = seg_w // tn
    grp = jnp.asarray(
        (np.arange(tn)[:, None] // HEAD_DIM == np.arange(tn)[None, :] // HEAD_DIM)
        .astype(np.float32) / HEAD_DIM, BF16)
    qg = jnp.tile(q_gain, tn // HEAD_DIM).reshape(1, tn)
    kg = jnp.tile(k_gain, tn // HEAD_DIM).reshape(1, tn)
    const = lambda m, n: (0, 0)
    out_specs = [pl.BlockSpec(
        (1, tm, tn), lambda m, n: (n // tiles_per_seg, m, n % tiles_per_seg))]
    out_shape = [jax.ShapeDtypeStruct((N_SEGMENTS, m_rows, seg_w), F32)]
    if emit_copy:
        assert m_rows == tm, "the bf16 copy is written once per column tile"
        out_specs.append(pl.BlockSpec((d_model, tn), lambda m, n: (0, n)))
        out_shape.append(jax.ShapeDtypeStruct((d_model, d_in), BF16))
    outs = pl.pallas_call(
        functools.partial(_inproj_kernel, tiles_per_seg=tiles_per_seg),
        grid=(m_rows // tm, d_in // tn),
        in_specs=[pl.BlockSpec((tm, d_model), lambda m, n: (m, 0)),
                  pl.BlockSpec((1, d_model), const),
                  _weight_spec(w_in, layer, (d_model, tn), lambda m, n: (0, n)),
                  pl.BlockSpec((1, tn), const),
                  pl.BlockSpec((1, tn), const),
                  pl.BlockSpec((tn, tn), const)],
        out_specs=out_specs,
        out_shape=out_shape,
        scratch_shapes=[pltpu.VMEM((tm, d_model), BF16)],
        compiler_params=_compiler_params(("arbitrary", "arbitrary")),
        name="inproj",
    )(x, gain.reshape(1, d_model), w_in, qg, kg, grp)
    return outs if emit_copy else outs[0]


def _t5_bucket(dist):
    n_exact = N_BUCKETS // 2
    large = n_exact + (np.log(np.maximum(dist, 1) / n_exact) / np.log(MAX_DISTANCE / n_exact)
                       * (N_BUCKETS - n_exact)).astype(np.int32)
    large = np.minimum(large, N_BUCKETS - 1)
    return np.where(dist < n_exact, dist, large).astype(np.int32)


def _bias_lookup(rel_bias, buckets):
    onehot = buckets[..., None] == np.arange(rel_bias.shape[0])
    return jnp.sum(jnp.where(onehot[..., None], rel_bias, 0.0), axis=-2)


def _prompt_bias_tables(rel_bias):
    n_heads = rel_bias.shape[1]
    qi = np.arange(Q_BLOCK)[:, None]
    kj = np.arange(2 * Q_BLOCK)[None, :]
    tables = []
    for window, dil in DILATED_CONFIGS:
        nw = window // dil
        assert nw == Q_BLOCK
        rel = qi - kj + nw
        valid = (rel >= 0) & (rel <= nw)
        bias = _bias_lookup(rel_bias, _t5_bucket(np.clip(rel, 0, nw) * dil))
        t = jnp.where(valid[:, :, None], bias, MASKED).transpose(2, 0, 1)
        tables.append(t.reshape(n_heads // HEADS_PER_BLOCK, HEADS_PER_BLOCK * Q_BLOCK,
                                2 * Q_BLOCK))
    return jnp.stack(tables)


def _mixer_prompt_kernel(q_ref, k_ref, v_ref, vc_ref, gb_ref, gc_ref, cw_ref, bias_ref, *rest,
                         seq, n_aliased):
    (oa_ref, oc_ref, cs_ref, okt_ref, ovt_ref,
     qs_scr, kd_scr, v0_scr, v1_scr, o_scr, m_scr, d_scr, u_scr) = rest[n_aliased:]

    for src, dst in ((k_ref, okt_ref), (v_ref, ovt_ref)):
        t = src[0].T
        for h in range(HEADS_PER_BLOCK):
            dst[h] = t[h * HEAD_DIM:(h + 1) * HEAD_DIM]

    nblk_total = seq // Q_BLOCK
    lane = lax.broadcasted_iota(jnp.int32, (Q_BLOCK, LANES), 1)
    head0 = lane < HEAD_DIM

    for c, (window, dil) in enumerate(DILATED_CONFIGS):
        sub_len = seq // dil
        blocks_per_sub = sub_len // Q_BLOCK

        for r in range(dil):
            rows = pl.ds(r, sub_len, stride=dil) if dil > 1 else pl.ds(0, seq)
            head0_rows = lax.broadcasted_iota(jnp.int32, (sub_len, LANES), 1) < HEAD_DIM
            qv = (q_ref[0, rows, :] * ATTN_SCALE).astype(BF16)
            kv = k_ref[0, rows, :].astype(BF16)
            vv = v_ref[0, rows, :].astype(BF16)
            zero = jnp.zeros_like(qv)
            q0 = jnp.where(head0_rows, qv, zero)
            q1 = jnp.where(head0_rows, zero, qv)
            for gb in range(blocks_per_sub):
                g = r * blocks_per_sub + gb
                src = slice(gb * Q_BLOCK, (gb + 1) * Q_BLOCK)
                qs_scr[c, 2 * g * Q_BLOCK:(2 * g + 1) * Q_BLOCK, :] = q0[src]
                qs_scr[c, (2 * g + 1) * Q_BLOCK:(2 * g + 2) * Q_BLOCK, :] = q1[src]
            dst = slice(r * sub_len, (r + 1) * sub_len)
            kd_scr[c, dst, :] = kv
            v0_scr[c, dst, 0:LANES] = jnp.where(head0_rows, vv, zero)
            v1_scr[c, dst, 0:LANES] = jnp.where(head0_rows, zero, vv)
            v0_scr[c, dst, LANES:] = jnp.ones_like(vv)
            v1_scr[c, dst, LANES:] = jnp.ones_like(vv)

        for g in range(nblk_total):
            r, gb = divmod(g, blocks_per_sub)
            lhs = qs_scr[c, 2 * g * Q_BLOCK:(2 * g + 2) * Q_BLOCK, :]
            if gb == 0:
                win = slice(g * Q_BLOCK, (g + 1) * Q_BLOCK)
                bias = bias_ref[c, 0, :, Q_BLOCK:]
            else:
                win = slice((g - 1) * Q_BLOCK, (g + 1) * Q_BLOCK)
                bias = bias_ref[c, 0]
            logits = lax.dot_general(lhs, kd_scr[c, win, :], (((1,), (1,)), ((), ())),
                                     preferred_element_type=F32) + bias
            m = jnp.max(logits, axis=-1, keepdims=True)
            pb = jnp.exp(logits - m).astype(BF16)
            od0 = jnp.dot(pb[0:Q_BLOCK], v0_scr[c, win, :], preferred_element_type=F32)
            od1 = jnp.dot(pb[Q_BLOCK:], v1_scr[c, win, :], preferred_element_type=F32)
            o = od0[:, 0:LANES] + od1[:, 0:LANES]
            m2 = jnp.where(head0, m[0:Q_BLOCK], m[Q_BLOCK:])
            d2 = jnp.where(head0, od0[:, LANES:], od1[:, LANES:])
            if dil > 1:
                out_rows = pl.ds(gb * Q_BLOCK * dil + r, Q_BLOCK, stride=dil)
            else:
                out_rows = pl.ds(g * Q_BLOCK, Q_BLOCK)
            o_scr[c, out_rows, :] = o
            m_scr[c, out_rows, :] = m2
            d_scr[c, out_rows, :] = d2

    n_br = len(DILATED_CONFIGS)

    def merge_rows(i, carry):
        rows = pl.ds(pl.multiple_of(i * MERGE_ROWS, MERGE_ROWS), MERGE_ROWS)
        mmax = m_scr[0, rows, :]
        for c in range(1, n_br):
            mmax = jnp.maximum(mmax, m_scr[c, rows, :])
        num = jnp.zeros((MERGE_ROWS, LANES), F32)
        den = jnp.zeros((MERGE_ROWS, LANES), F32)
        for c in range(n_br):
            w = jnp.exp(m_scr[c, rows, :] - mmax)
            num = num + w * o_scr[c, rows, :]
            den = den + w * d_scr[c, rows, :]
        oa_ref[0, rows, :] = num / den
        return carry

    lax.fori_loop(0, seq // MERGE_ROWS, merge_rows, 0)

    u = gc_ref[0] * vc_ref[0]
    u_scr[0:8, :] = jnp.zeros((8, LANES), F32)
    u_scr[8:8 + seq, :] = u
    y = cw_ref[0:1, :] * u_scr[pl.ds(8 - (CONV_WIDTH - 1), seq), :]
    for i in range(1, CONV_WIDTH):
        y = y + cw_ref[i:i + 1, :] * u_scr[pl.ds(8 - (CONV_WIDTH - 1) + i, seq), :]
    oc_ref[0] = gb_ref[0] * y
    cs_ref[0] = u_scr[pl.ds(8 + seq - (CONV_WIDTH - 1), CONV_WIDTH - 1), :]


def _mixer_prompt(layer, depth, proj, conv_w_l, bias_tables, new_windows):
    _, batch, seq, width = proj.shape
    n_pairs = width // LANES
    n_heads = width // HEAD_DIM
    n_br = len(DILATED_CONFIGS)
    tok = pl.BlockSpec((1, seq, LANES), lambda b, hp: (b, 0, hp))
    seg_specs = [pl.BlockSpec((None, 1, seq, LANES), lambda b, hp, s=s: (s, b, 0, hp))
                 for s in range(N_SEGMENTS)]
    win_spec = pl.BlockSpec((None, None, HEADS_PER_BLOCK, HEAD_DIM, seq),
                            lambda b, hp: (layer, b, hp, 0, 0))
    win_shape = jax.ShapeDtypeStruct((depth, batch, n_heads, HEAD_DIM, seq), F32)
    in_specs = seg_specs + [
        pl.BlockSpec((CONV_WIDTH, LANES), lambda b, hp: (0, hp)),
        pl.BlockSpec((n_br, 1, 2 * Q_BLOCK, 2 * Q_BLOCK), lambda b, hp: (0, hp, 0, 0))]
    args = [proj] * N_SEGMENTS + [conv_w_l, bias_tables]
    aliases = {}
    if new_windows is not None:
        aliases = {len(args): 3, len(args) + 1: 4}
        in_specs = in_specs + [pl.BlockSpec(memory_space=pl.ANY)] * 2
        args = args + list(new_windows)
    oa, oc, cs, okt, ovt = pl.pallas_call(
        functools.partial(_mixer_prompt_kernel, seq=seq, n_aliased=len(aliases)),
        grid=(batch, n_pairs),
        in_specs=in_specs,
        out_specs=[tok, tok,
                   pl.BlockSpec((1, CONV_WIDTH - 1, LANES), lambda b, hp: (b, 0, hp)),
                   win_spec, win_spec],
        out_shape=[jax.ShapeDtypeStruct((batch, seq, width), F32),
                   jax.ShapeDtypeStruct((batch, seq, width), F32),
                   jax.ShapeDtypeStruct((batch, CONV_WIDTH - 1, width), F32),
                   win_shape, win_shape],
        input_output_aliases=aliases,
        scratch_shapes=[pltpu.VMEM((n_br, 2 * seq, LANES), BF16),
                        pltpu.VMEM((n_br, seq, LANES), BF16),
                        pltpu.VMEM((n_br, seq, 2 * LANES), BF16),
                        pltpu.VMEM((n_br, seq, 2 * LANES), BF16),
                        pltpu.VMEM((n_br, seq, LANES), F32),
                        pltpu.VMEM((n_br, seq, LANES), F32),
                        pltpu.VMEM((n_br, seq, LANES), F32),
                        pltpu.VMEM((seq + 8, LANES), F32)],
        compiler_params=_compiler_params(("arbitrary", "arbitrary")),
        name="mixer_prompt",
    )(*args)
    return oa, oc, cs, (okt, ovt)


SAMPLE_HEADS_PER_STEP = 4


def _sample_bias_tables(rel_bias, wb):
    dist = wb - np.arange(wb)
    tables = []
    for window, dil in DILATED_CONFIGS:
        in_branch = (dist % dil == 0) & (dist // dil <= window // dil)
        bias = jnp.where(in_branch[:, None], _bias_lookup(rel_bias, _t5_bucket(dist)), MASKED)
        tables.append(bias.T[:, None, :])
    own = rel_bias[_t5_bucket(np.zeros((1,), np.int64))][0]
    return jnp.stack(tables), own[:, None, None]


def _mixer_sample_kernel(q_ref, kn_ref, vn_ref, kt_ref, vt_ref, bias_ref, bias0_ref,
                         vc_ref, gb_ref, gc_ref, cw_ref, sc_ref, *rest):
    oa_ref, oc_ref, cs_ref, okt_ref, ovt_ref = rest[-5:]
    heads, head_dim, wb = kt_ref.shape
    newest = lax.broadcasted_iota(jnp.int32, (head_dim, wb), 1) == wb - 1
    n_br = len(DILATED_CONFIGS)
    for h in range(heads):
        kt = kt_ref[h]
        vt = vt_ref[h]
        q, k_new, v_new = q_ref[h], kn_ref[h], vn_ref[h]
        s = jnp.sum(kt * q, axis=0, keepdims=True) * ATTN_SCALE
        l_new = jnp.sum(q * k_new, axis=0, keepdims=True) * ATTN_SCALE + bias0_ref[h]
        branches = []
        for c in range(n_br):
            logits = s + bias_ref[c, h]
            m = jnp.maximum(jnp.max(logits, axis=-1, keepdims=True), l_new)
            p = jnp.exp(logits - m)
            p_new = jnp.exp(l_new - m)
            den = jnp.sum(p, axis=-1, keepdims=True) + p_new
            branches.append((p, p_new, m, den))
        mmax = branches[0][2]
        for br in branches[1:]:
            mmax = jnp.maximum(mmax, br[2])
        p_all = jnp.zeros_like(s)
        p_new_all = jnp.zeros_like(mmax)
        den_all = jnp.zeros_like(mmax)
        for p, p_new, m, den in branches:
            w = jnp.exp(m - mmax)
            p_all = p_all + w * p
            p_new_all = p_new_all + w * p_new
            den_all = den_all + w * den
        num = jnp.sum(vt * p_all, axis=-1, keepdims=True) + p_new_all * v_new
        oa_ref[h] = num / den_all
        okt_ref[h] = jnp.where(newest, k_new, pltpu.roll(kt, wb - 1, axis=1))
        ovt_ref[h] = jnp.where(newest, v_new, pltpu.roll(vt, wb - 1, axis=1))

    @pl.when(pl.program_id(1) == 0)
    def _():
        u = gc_ref[0] * vc_ref[0]
        state = sc_ref[0]
        y = cw_ref[0:1, :] * state[0:1]
        for i in range(1, CONV_WIDTH - 1):
            y = y + cw_ref[i:i + 1, :] * state[i:i + 1]
        y = y + cw_ref[CONV_WIDTH - 1:CONV_WIDTH, :] * u
        oc_ref[0] = gb_ref[0] * y
        cs_ref[0] = jnp.concatenate([state[1:], u], axis=0)


def _mixer_sample(layer, proj, kt_state, vt_state, rolled, bias_s, bias0, conv_w_l, state_conv):
    depth, batch, n_heads, head_dim, wb = kt_state.shape
    width = proj.shape[-1]
    hb = SAMPLE_HEADS_PER_STEP
    n_br = len(DILATED_CONFIGS)
    qkv = proj[SEG_Q:SEG_V + 1].reshape(3, batch, n_heads, head_dim, 1)
    rows = proj.reshape(N_SEGMENTS, batch, 1, width)
    col_specs = [pl.BlockSpec((None, None, hb, head_dim, 1), lambda b, hc, s=s: (s, b, hc, 0, 0))
                 for s in range(3)]
    row_specs = [pl.BlockSpec((None, 1, 1, width), lambda b, hc, s=s: (s, b, 0, 0))
                 for s in (SEG_VC, SEG_B, SEG_C)]
    buf_spec = pl.BlockSpec((None, None, hb, head_dim, wb), lambda b, hc: (layer, b, hc, 0, 0))
    in_specs = col_specs + [
        buf_spec, buf_spec,
        pl.BlockSpec((n_br, hb, 1, wb), lambda b, hc: (0, hc, 0, 0)),
        pl.BlockSpec((hb, 1, 1), lambda b, hc: (hc, 0, 0))] + row_specs + [
        pl.BlockSpec((CONV_WIDTH, width), lambda b, hc: (0, 0)),
        pl.BlockSpec((1, CONV_WIDTH - 1, width), lambda b, hc: (layer * batch + b, 0, 0))]
    args = [qkv, qkv, qkv, kt_state, vt_state, bias_s, bias0, rows, rows, rows, conv_w_l,
            state_conv.reshape(depth * batch, CONV_WIDTH - 1, width)]
    aliases = {}
    if rolled is not None:
        aliases = {len(args): 3, len(args) + 1: 4}
        in_specs = in_specs + [pl.BlockSpec(memory_space=pl.ANY)] * 2
        args = args + list(rolled)
    oa, oc, cs, okt, ovt = pl.pallas_call(
        _mixer_sample_kernel,
        grid=(batch, n_heads // hb),
        in_specs=in_specs,
        out_specs=[pl.BlockSpec((None, hb, head_dim, 1), lambda b, hc: (b, hc, 0, 0)),
                   pl.BlockSpec((1, 1, width), lambda b, hc: (b, 0, 0)),
                   pl.BlockSpec((1, CONV_WIDTH - 1, width), lambda b, hc: (b, 0, 0)),
                   buf_spec, buf_spec],
        out_shape=[jax.ShapeDtypeStruct((batch, n_heads, head_dim, 1), F32),
                   jax.ShapeDtypeStruct((batch, 1, width), F32),
                   jax.ShapeDtypeStruct((batch, CONV_WIDTH - 1, width), F32),
                   jax.ShapeDtypeStruct(kt_state.shape, F32),
                   jax.ShapeDtypeStruct(vt_state.shape, F32)],
        input_output_aliases=aliases,
        compiler_params=_compiler_params(("arbitrary", "arbitrary")),
        name="mixer_sample",
    )(*args)
    return oa, oc, cs, (okt, ovt)


def _outproj_kernel(oa_ref, oc_ref, ga_ref, gc_ref, w_ref, x_ref, o_ref, *rest):
    z_scr = rest[-1]
    wcopy_ref = rest[0] if len(rest) == 2 else None
    n = pl.program_id(1)
    d_attn = oa_ref.shape[1]
    tn = o_ref.shape[1]

    @pl.when(n == 0)
    def _():
        z_scr[:, 0:d_attn] = _rms_rows(oa_ref[...], ga_ref[...]).astype(BF16)
        z_scr[:, d_attn:] = _rms_rows(oc_ref[...], gc_ref[...]).astype(BF16)

    if w_ref.shape[1] == tn:
        w = _bf16_weight(w_ref, wcopy_ref)
    else:
        w = w_ref[:, pl.ds(pl.multiple_of(n * tn, tn), tn)]
    o_ref[...] = x_ref[...] + jnp.dot(z_scr[...], w, preferred_element_type=F32)


def _outproj(oa, oc, attn_gain, conv_gain, w_out, layer, x, *, tm, tn):
    m_rows, d_attn = oa.shape
    d_conv = oc.shape[1]
    d_model = x.shape[1]
    d_mix = d_attn + d_conv
    emit_copy = w_out.ndim == 3
    const = lambda m, n: (0, 0)
    out_specs = [pl.BlockSpec((tm, tn), lambda m, n: (m, n))]
    out_shape = [jax.ShapeDtypeStruct((m_rows, d_model), F32)]
    if emit_copy:
        assert m_rows == tm, "the bf16 copy is written once per column tile"
        w_spec = _weight_spec(w_out, layer, (d_mix, tn), lambda m, n: (0, n))
        out_specs.append(pl.BlockSpec((d_mix, tn), lambda m, n: (0, n)))
        out_shape.append(jax.ShapeDtypeStruct((d_mix, d_model), BF16))
    else:
        w_spec = pl.BlockSpec((d_mix, d_model), const)
    outs = pl.pallas_call(
        _outproj_kernel,
        grid=(m_rows // tm, d_model // tn),
        in_specs=[pl.BlockSpec((tm, d_attn), lambda m, n: (m, 0)),
                  pl.BlockSpec((tm, d_conv), lambda m, n: (m, 0)),
                  pl.BlockSpec((1, d_attn), const),
                  pl.BlockSpec((1, d_conv), const),
                  w_spec,
                  pl.BlockSpec((tm, tn), lambda m, n: (m, n))],
        out_specs=out_specs,
        out_shape=out_shape,
        scratch_shapes=[pltpu.VMEM((tm, d_mix), BF16)],
        compiler_params=_compiler_params(("arbitrary", "arbitrary")),
        name="outproj",
    )(oa, oc, attn_gain.reshape(1, d_attn), conv_gain.reshape(1, d_conv), w_out, x)
    return outs if emit_copy else outs[0]


def _mlp_kernel(x_ref, g_ref, wu_ref, wd_ref, o_ref, *rest):
    h_scr = rest[-1]
    wu_copy_ref, wd_copy_ref = rest[:2] if len(rest) == 3 else (None, None)
    f = pl.program_id(1)

    @pl.when(f == 0)
    def _():
        x = x_ref[...]
        h_scr[...] = _rms_rows(x, g_ref[...]).astype(BF16)
        o_ref[...] = x

    a = jnp.dot(h_scr[...], _bf16_weight(wu_ref, wu_copy_ref), preferred_element_type=F32)
    a = jnp.square(jnp.maximum(a, 0.0)).astype(BF16)
    o_ref[...] += jnp.dot(a, _bf16_weight(wd_ref, wd_copy_ref), preferred_element_type=F32)


def _mlp(x, gain, w_up, w_down, layer, *, tm, tf):
    m_rows, d_model = x.shape
    d_ff = w_up.shape[-1]
    emit_copy = w_up.ndim == 3
    out_specs = [pl.BlockSpec((tm, d_model), lambda m, f: (m, 0))]
    out_shape = [jax.ShapeDtypeStruct((m_rows, d_model), F32)]
    if emit_copy:
        assert m_rows == tm, "the bf16 copies are written once per d_ff tile"
        out_specs += [pl.BlockSpec((d_model, tf), lambda m, f: (0, f)),
                      pl.BlockSpec((tf, d_model), lambda m, f: (f, 0))]
        out_shape += [jax.ShapeDtypeStruct((d_model, d_ff), BF16),
                      jax.ShapeDtypeStruct((d_ff, d_model), BF16)]
    outs = pl.pallas_call(
        _mlp_kernel,
        grid=(m_rows // tm, d_ff // tf),
        in_specs=[pl.BlockSpec((tm, d_model), lambda m, f: (m, 0), pipeline_mode=pl.Buffered(1)),
                  pl.BlockSpec((1, d_model), lambda m, f: (0, 0)),
                  _weight_spec(w_up, layer, (d_model, tf), lambda m, f: (0, f)),
                  _weight_spec(w_down, layer, (tf, d_model), lambda m, f: (f, 0))],
        out_specs=out_specs,
        out_shape=out_shape,
        scratch_shapes=[pltpu.VMEM((tm, d_model), BF16)],
        compiler_params=_compiler_params(("arbitrary", "arbitrary")),
        name="mlp",
    )(x, gain.reshape(1, d_model), w_up, w_down)
    return outs if emit_copy else outs[0]


PROMPT_TILES = dict(tm=1024, tn=512, tf=1024)
SAMPLE_TILES = dict(tn=512, tf=512)


def kernel(x_prompt, x_sample, state_attn_k, state_attn_v, state_conv, rel_bias, norm_mix, w_in,
           q_norm, k_norm, conv_w, attn_out_norm, conv_out_norm, w_out, norm_mlp, w_up, w_down):
    batch, seq, d_model = x_prompt.shape
    dec_batch, dec_seq, _ = x_sample.shape
    depth, _, wb, n_heads, head_dim = state_attn_k.shape
    assert dec_seq == 1 and head_dim == HEAD_DIM and wb == max(w for w, _ in DILATED_CONFIGS)
    assert seq == wb

    bias_p = _prompt_bias_tables(rel_bias)
    bias_s, bias0 = _sample_bias_tables(rel_bias, wb)
    kt_state = state_attn_k.transpose(0, 1, 3, 4, 2)
    vt_state = state_attn_v.transpose(0, 1, 3, 4, 2)
    rolled = None
    new_windows = None

    xp = x_prompt.reshape(batch * seq, d_model)
    xs = x_sample.reshape(dec_batch, d_model)
    pt, st = PROMPT_TILES, SAMPLE_TILES
    cp_new, cs_new = [], []
    for l in range(depth):
        proj, w_in_l = _inproj(xs, norm_mix[l], w_in, l, q_norm[l], k_norm[l],
                               tm=dec_batch, tn=st["tn"])
        width = proj.shape[-1]
        oa, oc, cs, rolled = _mixer_sample(l, proj, kt_state, vt_state, rolled, bias_s, bias0,
                                           conv_w[l], state_conv)
        xs, w_out_l = _outproj(oa.reshape(dec_batch, width), oc.reshape(dec_batch, width),
                               attn_out_norm[l], conv_out_norm[l], w_out, l, xs,
                               tm=dec_batch, tn=st["tn"])
        xs, w_up_l, w_down_l = _mlp(xs, norm_mlp[l], w_up, w_down, l, tm=dec_batch, tf=st["tf"])
        cs_new.append(cs)

        proj = _inproj(xp, norm_mix[l], w_in_l, l, q_norm[l], k_norm[l], tm=pt["tm"], tn=pt["tn"])
        oa, oc, cs, new_windows = _mixer_prompt(
            l, depth, proj.reshape(N_SEGMENTS, batch, seq, width), conv_w[l], bias_p, new_windows)
        xp = _outproj(oa.reshape(batch * seq, width), oc.reshape(batch * seq, width),
                      attn_out_norm[l], conv_out_norm[l], w_out_l, l, xp, tm=pt["tm"], tn=pt["tn"])
        xp = _mlp(xp, norm_mlp[l], w_up_l, w_down_l, l, tm=pt["tm"], tf=pt["tf"])
        cp_new.append(cs)

    ks_new, vs_new = (r.transpose(0, 1, 4, 2, 3) for r in rolled)
    kp_new, vp_new = (w.transpose(0, 1, 4, 2, 3) for w in new_windows)
    return (xp.reshape(batch, seq, d_model), xs.reshape(dec_batch, dec_seq, d_model),
            kp_new, vp_new, jnp.stack(cp_new), ks_new, vs_new, jnp.stack(cs_new))
```

```python
import functools

import jax
import jax.numpy as jnp
import numpy as np
from jax import lax
from jax.experimental import pallas as pl
from jax.experimental.pallas import tpu as pltpu

HEAD_DIM = 64
CONV_WIDTH = 3
DILATED_CONFIGS = ((128, 1), (512, 4), (2048, 16))
N_BUCKETS = 32
MAX_DISTANCE = 2048
EPS = 1e-6
ATTN_SCALE = HEAD_DIM ** -0.5
MASKED = -1e30

LANES = 128
Q_BLOCK = 128
MERGE_ROWS = 128
HEADS_PER_BLOCK = LANES // HEAD_DIM
VMEM_LIMIT_BYTES = 56 * 1024 * 1024

F32 = jnp.float32
BF16 = jnp.bfloat16


def _compiler_params(semantics, flags=None):
    return pltpu.CompilerParams(dimension_semantics=semantics,
                                vmem_limit_bytes=VMEM_LIMIT_BYTES, flags=flags)


def _rms_rows(x, gain):
    ms = jnp.mean(x * x, axis=-1, keepdims=True)
    return (x * lax.rsqrt(ms + EPS)) * gain


N_SEGMENTS = 6
SEG_Q, SEG_K, SEG_V, SEG_VC, SEG_B, SEG_C = range(N_SEGMENTS)


def _bf16_weight(w_ref, copy_ref):
    w = w_ref[...]
    if w.dtype != BF16:
        w = w.astype(BF16)
    if copy_ref is not None:
        copy_ref[...] = w
    return w


def _inproj_kernel(x_ref, g_ref, w_ref, qg_ref, kg_ref, grp_ref, *rest, tiles_per_seg, emit_copy,
                   round_next):
    rest = list(rest)
    wnext_ref = rest.pop(0) if round_next else None
    o_ref = rest.pop(0)
    wcopy_ref = rest.pop(0) if emit_copy else None
    wnext_copy_ref = rest.pop(0) if round_next else None
    h_scr, = rest
    n = pl.program_id(1)

    if round_next:
        wnext_copy_ref[...] = wnext_ref[...].astype(BF16)

    @pl.when(n == 0)
    def _():
        h_scr[...] = _rms_rows(x_ref[...], g_ref[...]).astype(BF16)

    y = jnp.dot(h_scr[...], _bf16_weight(w_ref, wcopy_ref), preferred_element_type=F32)
    seg = n // tiles_per_seg

    def head_norm(gain_ref):
        ms = jnp.dot((y * y).astype(BF16), grp_ref[...], preferred_element_type=F32)
        return (y * lax.rsqrt(ms + EPS)) * gain_ref[...]

    @pl.when(seg == SEG_Q)
    def _():
        o_ref[0] = head_norm(qg_ref)

    @pl.when(seg == SEG_K)
    def _():
        o_ref[0] = head_norm(kg_ref)

    @pl.when(seg > SEG_K)
    def _():
        o_ref[0] = y


def _weight_spec(w, layer, block, index_map):
    if w.ndim == 3:
        return pl.BlockSpec((None,) + block, lambda *ids: (layer,) + index_map(*ids))
    return pl.BlockSpec(block, index_map)


def _inproj(x, gain, w_in, layer, q_gain, k_gain, *, tm, tn, w_next=None):
    m_rows, d_model = x.shape
    d_in = w_in.shape[-1]
    emit_copy = w_in.ndim == 3
    seg_w = d_in // N_SEGMENTS
    tiles_per_seg = seg_w // tn
    grp = jnp.asarray(
        (np.arange(tn)[:, None] // HEAD_DIM == np.arange(tn)[None, :] // HEAD_DIM)
        .astype(np.float32) / HEAD_DIM, BF16)
    qg = jnp.tile(q_gain, tn // HEAD_DIM).reshape(1, tn)
    kg = jnp.tile(k_gain, tn // HEAD_DIM).reshape(1, tn)
    const = lambda m, n: (0, 0)
    out_specs = [pl.BlockSpec(
        (1, tm, tn), lambda m, n: (n // tiles_per_seg, m, n % tiles_per_seg))]
    out_shape = [jax.ShapeDtypeStruct((N_SEGMENTS, m_rows, seg_w), F32)]
    if emit_copy:
        assert m_rows == tm, "the bf16 copy is written once per column tile"
        out_specs.append(pl.BlockSpec((d_model, tn), lambda m, n: (0, n)))
        out_shape.append(jax.ShapeDtypeStruct((d_model, d_in), BF16))
    in_specs = [pl.BlockSpec((tm, d_model), lambda m, n: (m, 0)),
                pl.BlockSpec((1, d_model), const),
                _weight_spec(w_in, layer, (d_model, tn), lambda m, n: (0, n)),
                pl.BlockSpec((1, tn), const),
                pl.BlockSpec((1, tn), const),
                pl.BlockSpec((tn, tn), const)]
    args = [x, gain.reshape(1, d_model), w_in, qg, kg, grp]
    if w_next is not None:
        w_stack, next_layer = w_next
        rows = d_model // (m_rows // tm)
        in_specs.append(pl.BlockSpec((None, rows, tn), lambda m, n: (next_layer, m, n)))
        args.append(w_stack)
        out_specs.append(pl.BlockSpec((rows, tn), lambda m, n: (m, n)))
        out_shape.append(jax.ShapeDtypeStruct((d_model, d_in), BF16))
    outs = pl.pallas_call(
        functools.partial(_inproj_kernel, tiles_per_seg=tiles_per_seg, emit_copy=emit_copy,
                          round_next=w_next is not None),
        grid=(m_rows // tm, d_in // tn),
        in_specs=in_specs,
        out_specs=out_specs,
        out_shape=out_shape,
        scratch_shapes=[pltpu.VMEM((tm, d_model), BF16)],
        compiler_params=_compiler_params(("arbitrary", "arbitrary")),
        name="inproj",
    )(*args)
    return outs if len(outs) > 1 else outs[0]


def _t5_bucket(dist):
    n_exact = N_BUCKETS // 2
    large = n_exact + (np.log(np.maximum(dist, 1) / n_exact) / np.log(MAX_DISTANCE / n_exact)
                       * (N_BUCKETS - n_exact)).astype(np.int32)
    large = np.minimum(large, N_BUCKETS - 1)
    return np.where(dist < n_exact, dist, large).astype(np.int32)


def _bias_lookup(rel_bias, buckets):
    onehot = buckets[..., None] == np.arange(rel_bias.shape[0])
    return jnp.sum(jnp.where(onehot[..., None], rel_bias, 0.0), axis=-2)


def _prompt_bias_tables(rel_bias):
    n_heads = rel_bias.shape[1]
    qi = np.arange(Q_BLOCK)[:, None]
    kj = np.arange(2 * Q_BLOCK)[None, :]
    tables = []
    for window, dil in DILATED_CONFIGS:
        nw = window // dil
        assert nw == Q_BLOCK
        rel = qi - kj + nw
        valid = (rel >= 0) & (rel <= nw)
        bias = _bias_lookup(rel_bias, _t5_bucket(np.clip(rel, 0, nw) * dil))
        t = jnp.where(valid[:, :, None], bias, MASKED).transpose(2, 0, 1)
        tables.append(t.reshape(n_heads // HEADS_PER_BLOCK, HEADS_PER_BLOCK * Q_BLOCK,
                                2 * Q_BLOCK))
    return jnp.stack(tables)


def _mixer_prompt_kernel(q_ref, k_ref, v_ref, vc_ref, gb_ref, gc_ref, cw_ref, bias_ref, *rest,
                         seq, n_aliased, n_round):
    w_refs = rest[:n_round]
    rest = rest[n_round + n_aliased:]
    oa_ref, oc_ref, cs_ref, okt_ref, ovt_ref = rest[:5]
    wcopy_refs = rest[5:5 + n_round]
    qs_scr, kd_scr, v0_scr, v1_scr, o_scr, m_scr, d_scr, u_scr = rest[5 + n_round:]

    for w_ref, wcopy_ref in zip(w_refs, wcopy_refs):
        wcopy_ref[...] = w_ref[...].astype(BF16)

    for src, dst in ((k_ref, okt_ref), (v_ref, ovt_ref)):
        t = src[0].T
        for h in range(HEADS_PER_BLOCK):
            dst[h] = t[h * HEAD_DIM:(h + 1) * HEAD_DIM]

    nblk_total = seq // Q_BLOCK
    lane = lax.broadcasted_iota(jnp.int32, (Q_BLOCK, LANES), 1)
    head0 = lane < HEAD_DIM

    for c, (window, dil) in enumerate(DILATED_CONFIGS):
        sub_len = seq // dil
        blocks_per_sub = sub_len // Q_BLOCK

        for r in range(dil):
            rows = pl.ds(r, sub_len, stride=dil) if dil > 1 else pl.ds(0, seq)
            head0_rows = lax.broadcasted_iota(jnp.int32, (sub_len, LANES), 1) < HEAD_DIM
            qv = (q_ref[0, rows, :] * ATTN_SCALE).astype(BF16)
            kv = k_ref[0, rows, :].astype(BF16)
            vv = v_ref[0, rows, :].astype(BF16)
            zero = jnp.zeros_like(qv)
            q0 = jnp.where(head0_rows, qv, zero)
            q1 = jnp.where(head0_rows, zero, qv)
            for gb in range(blocks_per_sub):
                g = r * blocks_per_sub + gb
                src = slice(gb * Q_BLOCK, (gb + 1) * Q_BLOCK)
                qs_scr[c, 2 * g * Q_BLOCK:(2 * g + 1) * Q_BLOCK, :] = q0[src]
                qs_scr[c, (2 * g + 1) * Q_BLOCK:(2 * g + 2) * Q_BLOCK, :] = q1[src]
            dst = slice(r * sub_len, (r + 1) * sub_len)
            kd_scr[c, dst, :] = kv
            v0_scr[c, dst, :] = jnp.where(head0_rows, vv, zero)
            v1_scr[c, dst, :] = jnp.where(head0_rows, zero, vv)

        for g in range(nblk_total):
            r, gb = divmod(g, blocks_per_sub)
            lhs = qs_scr[c, 2 * g * Q_BLOCK:(2 * g + 2) * Q_BLOCK, :]
            if gb == 0:
                win = slice(g * Q_BLOCK, (g + 1) * Q_BLOCK)
                bias = bias_ref[c, 0, :, Q_BLOCK:]
            else:
                win = slice((g - 1) * Q_BLOCK, (g + 1) * Q_BLOCK)
                bias = bias_ref[c, 0]
            logits = lax.dot_general(lhs, kd_scr[c, win, :], (((1,), (1,)), ((), ())),
                                     preferred_element_type=F32) + bias
            m = jnp.max(logits, axis=-1, keepdims=True)
            p = jnp.exp(logits - m)
            den = jnp.sum(p, axis=-1, keepdims=True)
            pb = p.astype(BF16)
            o = (jnp.dot(pb[0:Q_BLOCK], v0_scr[c, win, :], preferred_element_type=F32)
                 + jnp.dot(pb[Q_BLOCK:], v1_scr[c, win, :], preferred_element_type=F32))
            m2 = jnp.where(head0, m[0:Q_BLOCK], m[Q_BLOCK:])
            d2 = jnp.where(head0, den[0:Q_BLOCK], den[Q_BLOCK:])
            if dil > 1:
                out_rows = pl.ds(gb * Q_BLOCK * dil + r, Q_BLOCK, stride=dil)
            else:
                out_rows = pl.ds(g * Q_BLOCK, Q_BLOCK)
            o_scr[c, out_rows, :] = o
            m_scr[c, out_rows, :] = m2
            d_scr[c, out_rows, :] = d2

    n_br = len(DILATED_CONFIGS)

    def merge_rows(i, carry):
        rows = pl.ds(pl.multiple_of(i * MERGE_ROWS, MERGE_ROWS), MERGE_ROWS)
        mmax = m_scr[0, rows, :]
        for c in range(1, n_br):
            mmax = jnp.maximum(mmax, m_scr[c, rows, :])
        num = jnp.zeros((MERGE_ROWS, LANES), F32)
        den = jnp.zeros((MERGE_ROWS, LANES), F32)
        for c in range(n_br):
            w = jnp.exp(m_scr[c, rows, :] - mmax)
            num = num + w * o_scr[c, rows, :]
            den = den + w * d_scr[c, rows, :]
        oa_ref[0, rows, :] = num / den
        return carry

    lax.fori_loop(0, seq // MERGE_ROWS, merge_rows, 0)

    u = gc_ref[0] * vc_ref[0]
    u_scr[0:8, :] = jnp.zeros((8, LANES), F32)
    u_scr[8:8 + seq, :] = u
    y = cw_ref[0:1, :] * u_scr[pl.ds(8 - (CONV_WIDTH - 1), seq), :]
    for i in range(1, CONV_WIDTH):
        y = y + cw_ref[i:i + 1, :] * u_scr[pl.ds(8 - (CONV_WIDTH - 1) + i, seq), :]
    oc_ref[0] = gb_ref[0] * y
    cs_ref[0] = u_scr[pl.ds(8 + seq - (CONV_WIDTH - 1), CONV_WIDTH - 1), :]


def _mixer_prompt(layer, depth, proj, conv_w_l, bias_tables, new_windows, to_round):
    _, batch, seq, width = proj.shape
    n_pairs = width // LANES
    n_heads = width // HEAD_DIM
    n_br = len(DILATED_CONFIGS)
    tok = pl.BlockSpec((1, seq, LANES), lambda b, hp: (b, 0, hp))
    seg_specs = [pl.BlockSpec((None, 1, seq, LANES), lambda b, hp, s=s: (s, b, 0, hp))
                 for s in range(N_SEGMENTS)]
    win_spec = pl.BlockSpec((None, None, HEADS_PER_BLOCK, HEAD_DIM, seq),
                            lambda b, hp: (layer, b, hp, 0, 0))
    win_shape = jax.ShapeDtypeStruct((depth, batch, n_heads, HEAD_DIM, seq), F32)
    in_specs = seg_specs + [
        pl.BlockSpec((CONV_WIDTH, LANES), lambda b, hp: (0, hp)),
        pl.BlockSpec((n_br, 1, 2 * Q_BLOCK, 2 * Q_BLOCK), lambda b, hp: (0, hp, 0, 0))]
    args = [proj] * N_SEGMENTS + [conv_w_l, bias_tables]
    n_steps = batch * n_pairs
    copy_specs, copy_shapes = [], []
    for w, w_layer in to_round:
        _, w_rows, w_cols = w.shape
        rows = w_rows // n_steps
        in_specs.append(pl.BlockSpec((None, rows, w_cols),
                                     lambda b, hp, w_layer=w_layer: (w_layer, b * n_pairs + hp, 0)))
        args.append(w)
        copy_specs.append(pl.BlockSpec((rows, w_cols), lambda b, hp: (b * n_pairs + hp, 0)))
        copy_shapes.append(jax.ShapeDtypeStruct((w_rows, w_cols), BF16))
    aliases = {}
    if new_windows is not None:
        aliases = {len(args): 3, len(args) + 1: 4}
        in_specs = in_specs + [pl.BlockSpec(memory_space=pl.ANY)] * 2
        args = args + list(new_windows)
    oa, oc, cs, okt, ovt, *copies = pl.pallas_call(
        functools.partial(_mixer_prompt_kernel, seq=seq, n_aliased=len(aliases),
                          n_round=len(to_round)),
        grid=(batch, n_pairs),
        in_specs=in_specs,
        out_specs=[tok, tok,
                   pl.BlockSpec((1, CONV_WIDTH - 1, LANES), lambda b, hp: (b, 0, hp)),
                   win_spec, win_spec] + copy_specs,
        out_shape=[jax.ShapeDtypeStruct((batch, seq, width), F32),
                   jax.ShapeDtypeStruct((batch, seq, width), F32),
                   jax.ShapeDtypeStruct((batch, CONV_WIDTH - 1, width), F32),
                   win_shape, win_shape] + copy_shapes,
        input_output_aliases=aliases,
        scratch_shapes=[pltpu.VMEM((n_br, 2 * seq, LANES), BF16),
                        pltpu.VMEM((n_br, seq, LANES), BF16),
                        pltpu.VMEM((n_br, seq, LANES), BF16),
                        pltpu.VMEM((n_br, seq, LANES), BF16),
                        pltpu.VMEM((n_br, seq, LANES), F32),
                        pltpu.VMEM((n_br, seq, LANES), F32),
                        pltpu.VMEM((n_br, seq, LANES), F32),
                        pltpu.VMEM((seq + 8, LANES), F32)],
        compiler_params=_compiler_params(("arbitrary", "arbitrary")),
        name="mixer_prompt",
    )(*args)
    return oa, oc, cs, (okt, ovt), copies


SAMPLE_HEADS_PER_STEP = 4


def _sample_bias_tables(rel_bias, wb):
    dist = wb - np.arange(wb)
    tables = []
    for window, dil in DILATED_CONFIGS:
        in_branch = (dist % dil == 0) & (dist // dil <= window // dil)
        bias = jnp.where(in_branch[:, None], _bias_lookup(rel_bias, _t5_bucket(dist)), MASKED)
        tables.append(bias.T[:, None, :])
    own = rel_bias[_t5_bucket(np.zeros((1,), np.int64))][0]
    return jnp.stack(tables), own[:, None, None]


def _mixer_sample_kernel(q_ref, kn_ref, vn_ref, kt_ref, vt_ref, bias_ref, bias0_ref,
                         vc_ref, gb_ref, gc_ref, cw_ref, sc_ref, *rest):
    oa_ref, oc_ref, cs_ref, okt_ref, ovt_ref = rest[-5:]
    heads, head_dim, wb = kt_ref.shape
    newest = lax.broadcasted_iota(jnp.int32, (head_dim, wb), 1) == wb - 1
    n_br = len(DILATED_CONFIGS)
    for h in range(heads):
        kt = kt_ref[h]
        vt = vt_ref[h]
        q, k_new, v_new = q_ref[h], kn_ref[h], vn_ref[h]
        s = jnp.sum(kt * q, axis=0, keepdims=True) * ATTN_SCALE
        l_new = jnp.sum(q * k_new, axis=0, keepdims=True) * ATTN_SCALE + bias0_ref[h]
        branches = []
        for c in range(n_br):
            logits = s + bias_ref[c, h]
            m = jnp.maximum(jnp.max(logits, axis=-1, keepdims=True), l_new)
            p = jnp.exp(logits - m)
            p_new = jnp.exp(l_new - m)
            den = jnp.sum(p, axis=-1, keepdims=True) + p_new
            branches.append((p, p_new, m, den))
        mmax = branches[0][2]
        for br in branches[1:]:
            mmax = jnp.maximum(mmax, br[2])
        p_all = jnp.zeros_like(s)
        p_new_all = jnp.zeros_like(mmax)
        den_all = jnp.zeros_like(mmax)
        for p, p_new, m, den in branches:
            w = jnp.exp(m - mmax)
            p_all = p_all + w * p
            p_new_all = p_new_all + w * p_new
            den_all = den_all + w * den
        num = jnp.sum(vt * p_all, axis=-1, keepdims=True) + p_new_all * v_new
        oa_ref[h] = num / den_all
        okt_ref[h] = jnp.where(newest, k_new, pltpu.roll(kt, wb - 1, axis=1))
        ovt_ref[h] = jnp.where(newest, v_new, pltpu.roll(vt, wb - 1, axis=1))

    @pl.when(pl.program_id(1) == 0)
    def _():
        u = gc_ref[0] * vc_ref[0]
        state = sc_ref[0]
        y = cw_ref[0:1, :] * state[0:1]
        for i in range(1, CONV_WIDTH - 1):
            y = y + cw_ref[i:i + 1, :] * state[i:i + 1]
        y = y + cw_ref[CONV_WIDTH - 1:CONV_WIDTH, :] * u
        oc_ref[0] = gb_ref[0] * y
        cs_ref[0] = jnp.concatenate([state[1:], u], axis=0)


def _mixer_sample(layer, proj, kt_state, vt_state, rolled, bias_s, bias0, conv_w_l, state_conv):
    depth, batch, n_heads, head_dim, wb = kt_state.shape
    width = proj.shape[-1]
    hb = SAMPLE_HEADS_PER_STEP
    n_br = len(DILATED_CONFIGS)
    qkv = proj[SEG_Q:SEG_V + 1].reshape(3, batch, n_heads, head_dim, 1)
    rows = proj.reshape(N_SEGMENTS, batch, 1, width)
    col_specs = [pl.BlockSpec((None, None, hb, head_dim, 1), lambda b, hc, s=s: (s, b, hc, 0, 0))
                 for s in range(3)]
    row_specs = [pl.BlockSpec((None, 1, 1, width), lambda b, hc, s=s: (s, b, 0, 0))
                 for s in (SEG_VC, SEG_B, SEG_C)]
    buf_spec = pl.BlockSpec((None, None, hb, head_dim, wb), lambda b, hc: (layer, b, hc, 0, 0))
    in_specs = col_specs + [
        buf_spec, buf_spec,
        pl.BlockSpec((n_br, hb, 1, wb), lambda b, hc: (0, hc, 0, 0)),
        pl.BlockSpec((hb, 1, 1), lambda b, hc: (hc, 0, 0))] + row_specs + [
        pl.BlockSpec((CONV_WIDTH, width), lambda b, hc: (0, 0)),
        pl.BlockSpec((1, CONV_WIDTH - 1, width), lambda b, hc: (layer * batch + b, 0, 0))]
    args = [qkv, qkv, qkv, kt_state, vt_state, bias_s, bias0, rows, rows, rows, conv_w_l,
            state_conv.reshape(depth * batch, CONV_WIDTH - 1, width)]
    aliases = {}
    if rolled is not None:
        aliases = {len(args): 3, len(args) + 1: 4}
        in_specs = in_specs + [pl.BlockSpec(memory_space=pl.ANY)] * 2
        args = args + list(rolled)
    oa, oc, cs, okt, ovt = pl.pallas_call(
        _mixer_sample_kernel,
        grid=(batch, n_heads // hb),
        in_specs=in_specs,
        out_specs=[pl.BlockSpec((None, hb, head_dim, 1), lambda b, hc: (b, hc, 0, 0)),
                   pl.BlockSpec((1, 1, width), lambda b, hc: (b, 0, 0)),
                   pl.BlockSpec((1, CONV_WIDTH - 1, width), lambda b, hc: (b, 0, 0)),
                   buf_spec, buf_spec],
        out_shape=[jax.ShapeDtypeStruct((batch, n_heads, head_dim, 1), F32),
                   jax.ShapeDtypeStruct((batch, 1, width), F32),
                   jax.ShapeDtypeStruct((batch, CONV_WIDTH - 1, width), F32),
                   jax.ShapeDtypeStruct(kt_state.shape, F32),
                   jax.ShapeDtypeStruct(vt_state.shape, F32)],
        input_output_aliases=aliases,
        compiler_params=_compiler_params(("arbitrary", "arbitrary")),
        name="mixer_sample",
    )(*args)
    return oa, oc, cs, (okt, ovt)


def _outproj_kernel(oa_ref, oc_ref, ga_ref, gc_ref, w_ref, x_ref, o_ref, *rest):
    z_scr = rest[-1]
    wcopy_ref = rest[0] if len(rest) == 2 else None
    n = pl.program_id(1)
    d_attn = oa_ref.shape[1]
    tn = o_ref.shape[1]

    @pl.when(n == 0)
    def _():
        z_scr[:, 0:d_attn] = _rms_rows(oa_ref[...], ga_ref[...]).astype(BF16)
        z_scr[:, d_attn:] = _rms_rows(oc_ref[...], gc_ref[...]).astype(BF16)

    if w_ref.shape[1] == tn:
        w = _bf16_weight(w_ref, wcopy_ref)
    else:
        w = w_ref[:, pl.ds(pl.multiple_of(n * tn, tn), tn)]
    o_ref[...] = x_ref[...] + jnp.dot(z_scr[...], w, preferred_element_type=F32)


def _outproj(oa, oc, attn_gain, conv_gain, w_out, layer, x, *, tm, tn):
    m_rows, d_attn = oa.shape
    d_conv = oc.shape[1]
    d_model = x.shape[1]
    d_mix = d_attn + d_conv
    emit_copy = w_out.ndim == 3
    const = lambda m, n: (0, 0)
    out_specs = [pl.BlockSpec((tm, tn), lambda m, n: (m, n))]
    out_shape = [jax.ShapeDtypeStruct((m_rows, d_model), F32)]
    if emit_copy:
        assert m_rows == tm, "the bf16 copy is written once per column tile"
        w_spec = _weight_spec(w_out, layer, (d_mix, tn), lambda m, n: (0, n))
        out_specs.append(pl.BlockSpec((d_mix, tn), lambda m, n: (0, n)))
        out_shape.append(jax.ShapeDtypeStruct((d_mix, d_model), BF16))
    else:
        w_spec = pl.BlockSpec((d_mix, d_model), const)
    outs = pl.pallas_call(
        _outproj_kernel,
        grid=(m_rows // tm, d_model // tn),
        in_specs=[pl.BlockSpec((tm, d_attn), lambda m, n: (m, 0)),
                  pl.BlockSpec((tm, d_conv), lambda m, n: (m, 0)),
                  pl.BlockSpec((1, d_attn), const),
                  pl.BlockSpec((1, d_conv), const),
                  w_spec,
                  pl.BlockSpec((tm, tn), lambda m, n: (m, n))],
        out_specs=out_specs,
        out_shape=out_shape,
        scratch_shapes=[pltpu.VMEM((tm, d_mix), BF16)],
        compiler_params=_compiler_params(("arbitrary", "arbitrary")),
        name="outproj",
    )(oa, oc, attn_gain.reshape(1, d_attn), conv_gain.reshape(1, d_conv), w_out, x)
    return outs if emit_copy else outs[0]


def _mlp_kernel(x_ref, g_ref, wu_ref, wd_ref, o_ref, *rest):
    h_scr = rest[-1]
    wu_copy_ref, wd_copy_ref = rest[:2] if len(rest) == 3 else (None, None)
    f = pl.program_id(1)

    @pl.when(f == 0)
    def _():
        x = x_ref[...]
        h_scr[...] = _rms_rows(x, g_ref[...]).astype(BF16)
        o_ref[...] = x

    a = jnp.dot(h_scr[...], _bf16_weight(wu_ref, wu_copy_ref), preferred_element_type=F32)
    a = jnp.square(jnp.maximum(a, 0.0)).astype(BF16)
    o_ref[...] += jnp.dot(a, _bf16_weight(wd_ref, wd_copy_ref), preferred_element_type=F32)


def _mlp(x, gain, w_up, w_down, layer, *, tm, tf):
    m_rows, d_model = x.shape
    d_ff = w_up.shape[-1]
    emit_copy = w_up.ndim == 3
    out_specs = [pl.BlockSpec((tm, d_model), lambda m, f: (m, 0))]
    out_shape = [jax.ShapeDtypeStruct((m_rows, d_model), F32)]
    if emit_copy:
        assert m_rows == tm, "the bf16 copies are written once per d_ff tile"
        out_specs += [pl.BlockSpec((d_model, tf), lambda m, f: (0, f)),
                      pl.BlockSpec((tf, d_model), lambda m, f: (f, 0))]
        out_shape += [jax.ShapeDtypeStruct((d_model, d_ff), BF16),
                      jax.ShapeDtypeStruct((d_ff, d_model), BF16)]
    outs = pl.pallas_call(
        _mlp_kernel,
        grid=(m_rows // tm, d_ff // tf),
        in_specs=[pl.BlockSpec((tm, d_model), lambda m, f: (m, 0), pipeline_mode=pl.Buffered(1)),
                  pl.BlockSpec((1, d_model), lambda m, f: (0, 0)),
                  _weight_spec(w_up, layer, (d_model, tf), lambda m, f: (0, f)),
                  _weight_spec(w_down, layer, (tf, d_model), lambda m, f: (f, 0))],
        out_specs=out_specs,
        out_shape=out_shape,
        scratch_shapes=[pltpu.VMEM((tm, d_model), BF16)],
        compiler_params=_compiler_params(("arbitrary", "arbitrary")),
        name="mlp",
    )(x, gain.reshape(1, d_model), w_up, w_down)
    return outs if emit_copy else outs[0]


PROMPT_TILES = dict(tm=1024, tn=512, tf=1024)
SAMPLE_TILES = dict(tn=1024, tf=1024)


def kernel(x_prompt, x_sample, state_attn_k, state_attn_v, state_conv, rel_bias, norm_mix, w_in,
           q_norm, k_norm, conv_w, attn_out_norm, conv_out_norm, w_out, norm_mlp, w_up, w_down):
    batch, seq, d_model = x_prompt.shape
    dec_batch, dec_seq, _ = x_sample.shape
    depth, _, wb, n_heads, head_dim = state_attn_k.shape
    assert dec_seq == 1 and head_dim == HEAD_DIM and wb == max(w for w, _ in DILATED_CONFIGS)
    assert seq == wb

    bias_p = _prompt_bias_tables(rel_bias)
    bias_s, bias0 = _sample_bias_tables(rel_bias, wb)
    kt_state = state_attn_k.transpose(0, 1, 3, 4, 2)
    vt_state = state_attn_v.transpose(0, 1, 3, 4, 2)
    rolled = None
    new_windows = None

    xp = x_prompt.reshape(batch * seq, d_model)
    xs = x_sample.reshape(dec_batch, d_model)
    pt, st = PROMPT_TILES, SAMPLE_TILES
    cp_new, cs_new = [], []
    w_in_l = w_in
    for l in range(depth):
        proj_s = _inproj(xs, norm_mix[l], w_in_l, l, q_norm[l], k_norm[l],
                         tm=dec_batch, tn=st["tn"])
        if l == 0:
            proj_s, w_in_l = proj_s
        width = proj_s.shape[-1]
        oa_s, oc_s, cs, rolled = _mixer_sample(l, proj_s, kt_state, vt_state, rolled, bias_s, bias0,
                                               conv_w[l], state_conv)
        cs_new.append(cs)

        if l + 1 < depth:
            proj, w_in_l = _inproj(xp, norm_mix[l], w_in_l, l, q_norm[l], k_norm[l],
                                   tm=pt["tm"], tn=pt["tn"], w_next=(w_in, l + 1))
        else:
            proj = _inproj(xp, norm_mix[l], w_in_l, l, q_norm[l], k_norm[l],
                           tm=pt["tm"], tn=pt["tn"])
        oa, oc, cs, new_windows, (w_out_l, w_up_l, w_down_l) = _mixer_prompt(
            l, depth, proj.reshape(N_SEGMENTS, batch, seq, width), conv_w[l], bias_p, new_windows,
            [(w_out, l), (w_up, l), (w_down, l)])
        cp_new.append(cs)

        xs = _outproj(oa_s.reshape(dec_batch, width), oc_s.reshape(dec_batch, width),
                      attn_out_norm[l], conv_out_norm[l], w_out_l, l, xs,
                      tm=dec_batch, tn=st["tn"])
        xs = _mlp(xs, norm_mlp[l], w_up_l, w_down_l, l, tm=dec_batch, tf=st["tf"])
        xp = _outproj(oa.reshape(batch * seq, width), oc.reshape(batch * seq, width),
                      attn_out_norm[l], conv_out_norm[l], w_out_l, l, xp, tm=pt["tm"], tn=pt["tn"])
        xp = _mlp(xp, norm_mlp[l], w_up_l, w_down_l, l, tm=pt["tm"], tf=pt["tf"])

    ks_new, vs_new = (r.transpose(0, 1, 4, 2, 3) for r in rolled)
    kp_new, vp_new = (w.transpose(0, 1, 4, 2, 3) for w in new_windows)
    return (xp.reshape(batch, seq, d_model), xs.reshape(dec_batch, dec_seq, d_model),
            kp_new, vp_new, jnp.stack(cp_new), ks_new, vs_new, jnp.stack(cs_new))
```

```python
import functools

import jax
import jax.numpy as jnp
import numpy as np
from jax import lax
from jax.experimental import pallas as pl
from jax.experimental.pallas import tpu as pltpu

HEAD_DIM = 64
CONV_WIDTH = 3
DILATED_CONFIGS = ((128, 1), (512, 4), (2048, 16))
N_BUCKETS = 32
MAX_DISTANCE = 2048
EPS = 1e-6
ATTN_SCALE = HEAD_DIM ** -0.5
MASKED = -1e30

LANES = 128
MXU_WIDTH = 256
Q_BLOCK = 128
MERGE_ROWS = 128
HEADS_PER_BLOCK = LANES // HEAD_DIM
VMEM_LIMIT_BYTES = 56 * 1024 * 1024

F32 = jnp.float32
BF16 = jnp.bfloat16


def _compiler_params(semantics, flags=None):
    return pltpu.CompilerParams(dimension_semantics=semantics,
                                vmem_limit_bytes=VMEM_LIMIT_BYTES, flags=flags)


def _rms_rows(x, gain):
    ms = jnp.mean(x * x, axis=-1, keepdims=True)
    return (x * lax.rsqrt(ms + EPS)) * gain


N_SEGMENTS = 6
SEG_Q, SEG_K, SEG_V, SEG_VC, SEG_B, SEG_C = range(N_SEGMENTS)


def _bf16_weight(w_ref, copy_ref):
    w = w_ref[...]
    if w.dtype != BF16:
        w = w.astype(BF16)
    if copy_ref is not None:
        copy_ref[...] = w
    return w


def _inproj_kernel(x_ref, g_ref, w_ref, qg_ref, kg_ref, grp_ref, *rest, tiles_per_seg, emit_copy,
                   round_next):
    rest = list(rest)
    wnext_ref = rest.pop(0) if round_next else None
    o_ref = rest.pop(0)
    wcopy_ref = rest.pop(0) if emit_copy else None
    wnext_copy_ref = rest.pop(0) if round_next else None
    h_scr, = rest
    n = pl.program_id(1)

    if round_next:
        wnext_copy_ref[...] = wnext_ref[...].astype(BF16)

    @pl.when(n == 0)
    def _():
        h_scr[...] = _rms_rows(x_ref[...], g_ref[...]).astype(BF16)

    y = jnp.dot(h_scr[...], _bf16_weight(w_ref, wcopy_ref), preferred_element_type=F32)
    seg = n // tiles_per_seg

    def head_norm(gain_ref):
        chunk = grp_ref.shape[0]
        for j in range(y.shape[1] // chunk):
            cols = slice(j * chunk, (j + 1) * chunk)
            yj = y[:, cols]
            ms = jnp.dot((yj * yj).astype(BF16), grp_ref[...], preferred_element_type=F32)
            o_ref[0, :, cols] = (yj * lax.rsqrt(ms + EPS)) * gain_ref[:, cols]

    @pl.when(seg == SEG_Q)
    def _():
        head_norm(qg_ref)

    @pl.when(seg == SEG_K)
    def _():
        head_norm(kg_ref)

    @pl.when(seg > SEG_K)
    def _():
        o_ref[0] = y


def _weight_spec(w, layer, block, index_map):
    if w.ndim == 3:
        return pl.BlockSpec((None,) + block, lambda *ids: (layer,) + index_map(*ids))
    return pl.BlockSpec(block, index_map)


def _inproj(x, gain, w_in, layer, q_gain, k_gain, *, tm, tn, w_next=None):
    m_rows, d_model = x.shape
    d_in = w_in.shape[-1]
    emit_copy = w_in.ndim == 3
    seg_w = d_in // N_SEGMENTS
    tiles_per_seg = seg_w // tn
    grp = jnp.asarray(
        (np.arange(MXU_WIDTH)[:, None] // HEAD_DIM == np.arange(MXU_WIDTH)[None, :] // HEAD_DIM)
        .astype(np.float32) / HEAD_DIM, BF16)
    qg = jnp.tile(q_gain, tn // HEAD_DIM).reshape(1, tn)
    kg = jnp.tile(k_gain, tn // HEAD_DIM).reshape(1, tn)
    const = lambda m, n: (0, 0)
    out_specs = [pl.BlockSpec(
        (1, tm, tn), lambda m, n: (n // tiles_per_seg, m, n % tiles_per_seg))]
    out_shape = [jax.ShapeDtypeStruct((N_SEGMENTS, m_rows, seg_w), F32)]
    if emit_copy:
        assert m_rows == tm, "the bf16 copy is written once per column tile"
        out_specs.append(pl.BlockSpec((d_model, tn), lambda m, n: (0, n)))
        out_shape.append(jax.ShapeDtypeStruct((d_model, d_in), BF16))
    in_specs = [pl.BlockSpec((tm, d_model), lambda m, n: (m, 0)),
                pl.BlockSpec((1, d_model), const),
                _weight_spec(w_in, layer, (d_model, tn), lambda m, n: (0, n)),
                pl.BlockSpec((1, tn), const),
                pl.BlockSpec((1, tn), const),
                pl.BlockSpec((MXU_WIDTH, MXU_WIDTH), const)]
    args = [x, gain.reshape(1, d_model), w_in, qg, kg, grp]
    if w_next is not None:
        w_stack, next_layer = w_next
        rows = d_model // (m_rows // tm)
        in_specs.append(pl.BlockSpec((None, rows, tn), lambda m, n: (next_layer, m, n)))
        args.append(w_stack)
        out_specs.append(pl.BlockSpec((rows, tn), lambda m, n: (m, n)))
        out_shape.append(jax.ShapeDtypeStruct((d_model, d_in), BF16))
    outs = pl.pallas_call(
        functools.partial(_inproj_kernel, tiles_per_seg=tiles_per_seg, emit_copy=emit_copy,
                          round_next=w_next is not None),
        grid=(m_rows // tm, d_in // tn),
        in_specs=in_specs,
        out_specs=out_specs,
        out_shape=out_shape,
        scratch_shapes=[pltpu.VMEM((tm, d_model), BF16)],
        compiler_params=_compiler_params(("arbitrary", "arbitrary")),
        name="inproj",
    )(*args)
    return outs if len(outs) > 1 else outs[0]


def _t5_bucket(dist):
    n_exact = N_BUCKETS // 2
    large = n_exact + (np.log(np.maximum(dist, 1) / n_exact) / np.log(MAX_DISTANCE / n_exact)
                       * (N_BUCKETS - n_exact)).astype(np.int32)
    large = np.minimum(large, N_BUCKETS - 1)
    return np.where(dist < n_exact, dist, large).astype(np.int32)


def _bias_lookup(rel_bias, buckets):
    onehot = buckets[..., None] == np.arange(rel_bias.shape[0])
    return jnp.sum(jnp.where(onehot[..., None], rel_bias, 0.0), axis=-2)


def _prompt_bias_tables(rel_bias):
    n_heads = rel_bias.shape[1]
    tables = []
    for window, dil in DILATED_CONFIGS:
        nw = window // dil
        assert nw == Q_BLOCK
        bias_k = _bias_lookup(rel_bias, _t5_bucket(np.arange(nw + 1) * dil))
        pad = jnp.full((Q_BLOCK - 1, n_heads), MASKED, F32)
        u = jnp.concatenate([pad, bias_k[::-1], pad, pad[:1]])
        row = 3 * Q_BLOCK - 1
        skew = jnp.tile(u, (Q_BLOCK, 1))[:Q_BLOCK * row].reshape(Q_BLOCK, row, n_heads)
        t = skew[:, Q_BLOCK - 1:3 * Q_BLOCK - 1].transpose(2, 0, 1)
        tables.append(t.reshape(n_heads // HEADS_PER_BLOCK, HEADS_PER_BLOCK * Q_BLOCK,
                                2 * Q_BLOCK))
    return jnp.stack(tables)


def _mixer_prompt_kernel(q_ref, k_ref, v_ref, vc_ref, gb_ref, gc_ref, cw_ref, bias_ref, *rest,
                         seq, n_aliased, n_round):
    w_refs = rest[:n_round]
    rest = rest[n_round + n_aliased:]
    oa_ref, oc_ref, cs_ref, okt_ref, ovt_ref = rest[:5]
    wcopy_refs = rest[5:5 + n_round]
    qs_scr, kd_scr, v0_scr, v1_scr, o_scr, m_scr, d_scr, u_scr = rest[5 + n_round:]

    for w_ref, wcopy_ref in zip(w_refs, wcopy_refs):
        wcopy_ref[...] = w_ref[...].astype(BF16)

    for src, dst in ((k_ref, okt_ref), (v_ref, ovt_ref)):
        t = src[0].T
        for h in range(HEADS_PER_BLOCK):
            dst[h] = t[h * HEAD_DIM:(h + 1) * HEAD_DIM]

    nblk_total = seq // Q_BLOCK
    lane = lax.broadcasted_iota(jnp.int32, (Q_BLOCK, LANES), 1)
    head0 = lane < HEAD_DIM

    for c, (window, dil) in enumerate(DILATED_CONFIGS):
        sub_len = seq // dil
        blocks_per_sub = sub_len // Q_BLOCK

        for r in range(dil):
            rows = pl.ds(r, sub_len, stride=dil) if dil > 1 else pl.ds(0, seq)
            head0_rows = lax.broadcasted_iota(jnp.int32, (sub_len, LANES), 1) < HEAD_DIM
            qv = (q_ref[0, rows, :] * ATTN_SCALE).astype(BF16)
            kv = k_ref[0, rows, :].astype(BF16)
            vv = v_ref[0, rows, :].astype(BF16)
            zero = jnp.zeros_like(qv)
            q0 = jnp.where(head0_rows, qv, zero)
            q1 = jnp.where(head0_rows, zero, qv)
            for gb in range(blocks_per_sub):
                g = r * blocks_per_sub + gb
                src = slice(gb * Q_BLOCK, (gb + 1) * Q_BLOCK)
                qs_scr[c, 2 * g * Q_BLOCK:(2 * g + 1) * Q_BLOCK, :] = q0[src]
                qs_scr[c, (2 * g + 1) * Q_BLOCK:(2 * g + 2) * Q_BLOCK, :] = q1[src]
            dst = slice(r * sub_len, (r + 1) * sub_len)
            kd_scr[c, dst, :] = kv
            v0_scr[c, dst, :] = jnp.where(head0_rows, vv, zero)
            v1_scr[c, dst, :] = jnp.where(head0_rows, zero, vv)

        for g in range(nblk_total):
            r, gb = divmod(g, blocks_per_sub)
            lhs = qs_scr[c, 2 * g * Q_BLOCK:(2 * g + 2) * Q_BLOCK, :]
            if gb == 0:
                win = slice(g * Q_BLOCK, (g + 1) * Q_BLOCK)
                bias = bias_ref[c, 0, :, Q_BLOCK:]
            else:
                win = slice((g - 1) * Q_BLOCK, (g + 1) * Q_BLOCK)
                bias = bias_ref[c, 0]
            logits = lax.dot_general(lhs, kd_scr[c, win, :], (((1,), (1,)), ((), ())),
                                     preferred_element_type=F32) + bias
            m = jnp.max(logits, axis=-1, keepdims=True)
            p = jnp.exp(logits - m)
            den = jnp.sum(p, axis=-1, keepdims=True)
            pb = p.astype(BF16)
            o = (jnp.dot(pb[0:Q_BLOCK], v0_scr[c, win, :], preferred_element_type=F32)
                 + jnp.dot(pb[Q_BLOCK:], v1_scr[c, win, :], preferred_element_type=F32))
            m2 = jnp.where(head0, m[0:Q_BLOCK], m[Q_BLOCK:])
            d2 = jnp.where(head0, den[0:Q_BLOCK], den[Q_BLOCK:])
            if dil > 1:
                out_rows = pl.ds(gb * Q_BLOCK * dil + r, Q_BLOCK, stride=dil)
            else:
                out_rows = pl.ds(g * Q_BLOCK, Q_BLOCK)
            o_scr[c, out_rows, :] = o
            m_scr[c, out_rows, :] = m2
            d_scr[c, out_rows, :] = d2

    n_br = len(DILATED_CONFIGS)

    def merge_rows(i, carry):
        rows = pl.ds(pl.multiple_of(i * MERGE_ROWS, MERGE_ROWS), MERGE_ROWS)
        mmax = m_scr[0, rows, :]
        for c in range(1, n_br):
            mmax = jnp.maximum(mmax, m_scr[c, rows, :])
        num = jnp.zeros((MERGE_ROWS, LANES), F32)
        den = jnp.zeros((MERGE_ROWS, LANES), F32)
        for c in range(n_br):
            w = jnp.exp(m_scr[c, rows, :] - mmax)
            num = num + w * o_scr[c, rows, :]
            den = den + w * d_scr[c, rows, :]
        oa_ref[0, rows, :] = num / den
        return carry

    lax.fori_loop(0, seq // MERGE_ROWS, merge_rows, 0)

    u = gc_ref[0] * vc_ref[0]
    u_scr[0:8, :] = jnp.zeros((8, LANES), F32)
    u_scr[8:8 + seq, :] = u
    y = cw_ref[0:1, :] * u_scr[pl.ds(8 - (CONV_WIDTH - 1), seq), :]
    for i in range(1, CONV_WIDTH):
        y = y + cw_ref[i:i + 1, :] * u_scr[pl.ds(8 - (CONV_WIDTH - 1) + i, seq), :]
    oc_ref[0] = gb_ref[0] * y
    cs_ref[0] = u_scr[pl.ds(8 + seq - (CONV_WIDTH - 1), CONV_WIDTH - 1), :]


def _mixer_prompt(layer, depth, proj, conv_w_l, bias_tables, new_windows, to_round):
    _, batch, seq, width = proj.shape
    n_pairs = width // LANES
    n_heads = width // HEAD_DIM
    n_br = len(DILATED_CONFIGS)
    tok = pl.BlockSpec((1, seq, LANES), lambda b, hp: (b, 0, hp))
    seg_specs = [pl.BlockSpec((None, 1, seq, LANES), lambda b, hp, s=s: (s, b, 0, hp))
                 for s in range(N_SEGMENTS)]
    win_spec = pl.BlockSpec((None, None, HEADS_PER_BLOCK, HEAD_DIM, seq),
                            lambda b, hp: (layer, b, hp, 0, 0))
    win_shape = jax.ShapeDtypeStruct((depth, batch, n_heads, HEAD_DIM, seq), F32)
    in_specs = seg_specs + [
        pl.BlockSpec((CONV_WIDTH, LANES), lambda b, hp: (0, hp)),
        pl.BlockSpec((n_br, 1, 2 * Q_BLOCK, 2 * Q_BLOCK), lambda b, hp: (0, hp, 0, 0))]
    args = [proj] * N_SEGMENTS + [conv_w_l, bias_tables]
    n_steps = batch * n_pairs
    copy_specs, copy_shapes = [], []
    for w, w_layer in to_round:
        _, w_rows, w_cols = w.shape
        rows = w_rows // n_steps
        in_specs.append(pl.BlockSpec((None, rows, w_cols),
                                     lambda b, hp, w_layer=w_layer: (w_layer, b * n_pairs + hp, 0)))
        args.append(w)
        copy_specs.append(pl.BlockSpec((rows, w_cols), lambda b, hp: (b * n_pairs + hp, 0)))
        copy_shapes.append(jax.ShapeDtypeStruct((w_rows, w_cols), BF16))
    aliases = {}
    if new_windows is not None:
        aliases = {len(args): 3, len(args) + 1: 4}
        in_specs = in_specs + [pl.BlockSpec(memory_space=pl.ANY)] * 2
        args = args + list(new_windows)
    oa, oc, cs, okt, ovt, *copies = pl.pallas_call(
        functools.partial(_mixer_prompt_kernel, seq=seq, n_aliased=len(aliases),
                          n_round=len(to_round)),
        grid=(batch, n_pairs),
        in_specs=in_specs,
        out_specs=[tok, tok,
                   pl.BlockSpec((1, CONV_WIDTH - 1, LANES), lambda b, hp: (b, 0, hp)),
                   win_spec, win_spec] + copy_specs,
        out_shape=[jax.ShapeDtypeStruct((batch, seq, width), F32),
                   jax.ShapeDtypeStruct((batch, seq, width), F32),
                   jax.ShapeDtypeStruct((batch, CONV_WIDTH - 1, width), F32),
                   win_shape, win_shape] + copy_shapes,
        input_output_aliases=aliases,
        scratch_shapes=[pltpu.VMEM((n_br, 2 * seq, LANES), BF16),
                        pltpu.VMEM((n_br, seq, LANES), BF16),
                        pltpu.VMEM((n_br, seq, LANES), BF16),
                        pltpu.VMEM((n_br, seq, LANES), BF16),
                        pltpu.VMEM((n_br, seq, LANES), F32),
                        pltpu.VMEM((n_br, seq, LANES), F32),
                        pltpu.VMEM((n_br, seq, LANES), F32),
                        pltpu.VMEM((seq + 8, LANES), F32)],
        compiler_params=_compiler_params(("arbitrary", "arbitrary")),
        name="mixer_prompt",
    )(*args)
    return oa, oc, cs, (okt, ovt), copies


SAMPLE_HEADS_PER_STEP = 4


def _sample_bias_tables(rel_bias, wb):
    dist = wb - np.arange(wb)
    tables = []
    for window, dil in DILATED_CONFIGS:
        in_branch = (dist % dil == 0) & (dist // dil <= window // dil)
        bias = jnp.where(in_branch[:, None], _bias_lookup(rel_bias, _t5_bucket(dist)), MASKED)
        tables.append(bias.T[:, None, :])
    own = rel_bias[_t5_bucket(np.zeros((1,), np.int64))][0]
    return jnp.stack(tables), own[:, None, None]


def _mixer_sample_kernel(q_ref, kn_ref, vn_ref, kt_ref, vt_ref, bias_ref, bias0_ref,
                         vc_ref, gb_ref, gc_ref, cw_ref, sc_ref, *rest):
    oa_ref, oc_ref, cs_ref, okt_ref, ovt_ref = rest[-5:]
    heads, head_dim, wb = kt_ref.shape
    newest = lax.broadcasted_iota(jnp.int32, (head_dim, wb), 1) == wb - 1
    n_br = len(DILATED_CONFIGS)
    for h in range(heads):
        kt = kt_ref[h]
        vt = vt_ref[h]
        q, k_new, v_new = q_ref[h], kn_ref[h], vn_ref[h]
        s = jnp.sum(kt * q, axis=0, keepdims=True) * ATTN_SCALE
        l_new = jnp.sum(q * k_new, axis=0, keepdims=True) * ATTN_SCALE + bias0_ref[h]
        branches = []
        for c in range(n_br):
            logits = s + bias_ref[c, h]
            m = jnp.maximum(jnp.max(logits, axis=-1, keepdims=True), l_new)
            p = jnp.exp(logits - m)
            p_new = jnp.exp(l_new - m)
            den = jnp.sum(p, axis=-1, keepdims=True) + p_new
            branches.append((p, p_new, m, den))
        mmax = branches[0][2]
        for br in branches[1:]:
            mmax = jnp.maximum(mmax, br[2])
        p_all = jnp.zeros_like(s)
        p_new_all = jnp.zeros_like(mmax)
        den_all = jnp.zeros_like(mmax)
        for p, p_new, m, den in branches:
            w = jnp.exp(m - mmax)
            p_all = p_all + w * p
            p_new_all = p_new_all + w * p_new
            den_all = den_all + w * den
        num = jnp.sum(vt * p_all, axis=-1, keepdims=True) + p_new_all * v_new
        oa_ref[h] = num / den_all
        okt_ref[h] = jnp.where(newest, k_new, pltpu.roll(kt, wb - 1, axis=1))
        ovt_ref[h] = jnp.where(newest, v_new, pltpu.roll(vt, wb - 1, axis=1))

    @pl.when(pl.program_id(1) == 0)
    def _():
        u = gc_ref[0] * vc_ref[0]
        state = sc_ref[0]
        y = cw_ref[0:1, :] * state[0:1]
        for i in range(1, CONV_WIDTH - 1):
            y = y + cw_ref[i:i + 1, :] * state[i:i + 1]
        y = y + cw_ref[CONV_WIDTH - 1:CONV_WIDTH, :] * u
        oc_ref[0] = gb_ref[0] * y
        cs_ref[0] = jnp.concatenate([state[1:], u], axis=0)


def _mixer_sample(layer, proj, kt_state, vt_state, rolled, bias_s, bias0, conv_w_l, state_conv):
    depth, batch, n_heads, head_dim, wb = kt_state.shape
    width = proj.shape[-1]
    hb = SAMPLE_HEADS_PER_STEP
    n_br = len(DILATED_CONFIGS)
    qkv = proj[SEG_Q:SEG_V + 1].reshape(3, batch, n_heads, head_dim, 1)
    rows = proj.reshape(N_SEGMENTS, batch, 1, width)
    col_specs = [pl.BlockSpec((None, None, hb, head_dim, 1), lambda b, hc, s=s: (s, b, hc, 0, 0))
                 for s in range(3)]
    row_specs = [pl.BlockSpec((None, 1, 1, width), lambda b, hc, s=s: (s, b, 0, 0))
                 for s in (SEG_VC, SEG_B, SEG_C)]
    buf_spec = pl.BlockSpec((None, None, hb, head_dim, wb), lambda b, hc: (layer, b, hc, 0, 0))
    in_specs = col_specs + [
        buf_spec, buf_spec,
        pl.BlockSpec((n_br, hb, 1, wb), lambda b, hc: (0, hc, 0, 0)),
        pl.BlockSpec((hb, 1, 1), lambda b, hc: (hc, 0, 0))] + row_specs + [
        pl.BlockSpec((CONV_WIDTH, width), lambda b, hc: (0, 0)),
        pl.BlockSpec((1, CONV_WIDTH - 1, width), lambda b, hc: (layer * batch + b, 0, 0))]
    args = [qkv, qkv, qkv, kt_state, vt_state, bias_s, bias0, rows, rows, rows, conv_w_l,
            state_conv.reshape(depth * batch, CONV_WIDTH - 1, width)]
    aliases = {}
    if rolled is not None:
        aliases = {len(args): 3, len(args) + 1: 4}
        in_specs = in_specs + [pl.BlockSpec(memory_space=pl.ANY)] * 2
        args = args + list(rolled)
    oa, oc, cs, okt, ovt = pl.pallas_call(
        _mixer_sample_kernel,
        grid=(batch, n_heads // hb),
        in_specs=in_specs,
        out_specs=[pl.BlockSpec((None, hb, head_dim, 1), lambda b, hc: (b, hc, 0, 0)),
                   pl.BlockSpec((1, 1, width), lambda b, hc: (b, 0, 0)),
                   pl.BlockSpec((1, CONV_WIDTH - 1, width), lambda b, hc: (b, 0, 0)),
                   buf_spec, buf_spec],
        out_shape=[jax.ShapeDtypeStruct((batch, n_heads, head_dim, 1), F32),
                   jax.ShapeDtypeStruct((batch, 1, width), F32),
                   jax.ShapeDtypeStruct((batch, CONV_WIDTH - 1, width), F32),
                   jax.ShapeDtypeStruct(kt_state.shape, F32),
                   jax.ShapeDtypeStruct(vt_state.shape, F32)],
        input_output_aliases=aliases,
        compiler_params=_compiler_params(("arbitrary", "arbitrary")),
        name="mixer_sample",
    )(*args)
    return oa, oc, cs, (okt, ovt)


def _outproj_kernel(oa_ref, oc_ref, ga_ref, gc_ref, w_ref, x_ref, o_ref, *rest):
    z_scr = rest[-1]
    wcopy_ref = rest[0] if len(rest) == 2 else None
    n = pl.program_id(1)
    d_attn = oa_ref.shape[1]
    tn = o_ref.shape[1]

    @pl.when(n == 0)
    def _():
        z_scr[:, 0:d_attn] = _rms_rows(oa_ref[...], ga_ref[...]).astype(BF16)
        z_scr[:, d_attn:] = _rms_rows(oc_ref[...], gc_ref[...]).astype(BF16)

    if w_ref.shape[1] == tn:
        w = _bf16_weight(w_ref, wcopy_ref)
    else:
        w = w_ref[:, pl.ds(pl.multiple_of(n * tn, tn), tn)]
    o_ref[...] = x_ref[...] + jnp.dot(z_scr[...], w, preferred_element_type=F32)


def _outproj(oa, oc, attn_gain, conv_gain, w_out, layer, x, *, tm, tn):
    m_rows, d_attn = oa.shape
    d_conv = oc.shape[1]
    d_model = x.shape[1]
    d_mix = d_attn + d_conv
    emit_copy = w_out.ndim == 3
    const = lambda m, n: (0, 0)
    out_specs = [pl.BlockSpec((tm, tn), lambda m, n: (m, n))]
    out_shape = [jax.ShapeDtypeStruct((m_rows, d_model), F32)]
    if emit_copy:
        assert m_rows == tm, "the bf16 copy is written once per column tile"
        w_spec = _weight_spec(w_out, layer, (d_mix, tn), lambda m, n: (0, n))
        out_specs.append(pl.BlockSpec((d_mix, tn), lambda m, n: (0, n)))
        out_shape.append(jax.ShapeDtypeStruct((d_mix, d_model), BF16))
    else:
        w_spec = pl.BlockSpec((d_mix, d_model), const)
    outs = pl.pallas_call(
        _outproj_kernel,
        grid=(m_rows // tm, d_model // tn),
        in_specs=[pl.BlockSpec((tm, d_attn), lambda m, n: (m, 0)),
                  pl.BlockSpec((tm, d_conv), lambda m, n: (m, 0)),
                  pl.BlockSpec((1, d_attn), const),
                  pl.BlockSpec((1, d_conv), const),
                  w_spec,
                  pl.BlockSpec((tm, tn), lambda m, n: (m, n))],
        out_specs=out_specs,
        out_shape=out_shape,
        scratch_shapes=[pltpu.VMEM((tm, d_mix), BF16)],
        compiler_params=_compiler_params(("arbitrary", "arbitrary")),
        name="outproj",
    )(oa, oc, attn_gain.reshape(1, d_attn), conv_gain.reshape(1, d_conv), w_out, x)
    return outs if emit_copy else outs[0]


def _mlp_kernel(x_ref, g_ref, wu_ref, wd_ref, o_ref, *rest):
    h_scr = rest[-1]
    wu_copy_ref, wd_copy_ref = rest[:2] if len(rest) == 3 else (None, None)
    f = pl.program_id(1)

    @pl.when(f == 0)
    def _():
        x = x_ref[...]
        h_scr[...] = _rms_rows(x, g_ref[...]).astype(BF16)
        o_ref[...] = x

    a = jnp.dot(h_scr[...], _bf16_weight(wu_ref, wu_copy_ref), preferred_element_type=F32)
    a = jnp.square(jnp.maximum(a, 0.0)).astype(BF16)
    o_ref[...] += jnp.dot(a, _bf16_weight(wd_ref, wd_copy_ref), preferred_element_type=F32)


def _mlp(x, gain, w_up, w_down, layer, *, tm, tf):
    m_rows, d_model = x.shape
    d_ff = w_up.shape[-1]
    emit_copy = w_up.ndim == 3
    out_specs = [pl.BlockSpec((tm, d_model), lambda m, f: (m, 0))]
    out_shape = [jax.ShapeDtypeStruct((m_rows, d_model), F32)]
    if emit_copy:
        assert m_rows == tm, "the bf16 copies are written once per d_ff tile"
        out_specs += [pl.BlockSpec((d_model, tf), lambda m, f: (0, f)),
                      pl.BlockSpec((tf, d_model), lambda m, f: (f, 0))]
        out_shape += [jax.ShapeDtypeStruct((d_model, d_ff), BF16),
                      jax.ShapeDtypeStruct((d_ff, d_model), BF16)]
    outs = pl.pallas_call(
        _mlp_kernel,
        grid=(m_rows // tm, d_ff // tf),
        in_specs=[pl.BlockSpec((tm, d_model), lambda m, f: (m, 0), pipeline_mode=pl.Buffered(1)),
                  pl.BlockSpec((1, d_model), lambda m, f: (0, 0)),
                  _weight_spec(w_up, layer, (d_model, tf), lambda m, f: (0, f)),
                  _weight_spec(w_down, layer, (tf, d_model), lambda m, f: (f, 0))],
        out_specs=out_specs,
        out_shape=out_shape,
        scratch_shapes=[pltpu.VMEM((tm, d_model), BF16)],
        compiler_params=_compiler_params(("arbitrary", "arbitrary")),
        name="mlp",
    )(x, gain.reshape(1, d_model), w_up, w_down)
    return outs if emit_copy else outs[0]


PROMPT_TILES = dict(tm=1024, tn_in=1024, tn_out=512, tf=1024)
SAMPLE_TILES = dict(tn=1024, tf=1024)


def kernel(x_prompt, x_sample, state_attn_k, state_attn_v, state_conv, rel_bias, norm_mix, w_in,
           q_norm, k_norm, conv_w, attn_out_norm, conv_out_norm, w_out, norm_mlp, w_up, w_down):
    batch, seq, d_model = x_prompt.shape
    dec_batch, dec_seq, _ = x_sample.shape
    depth, _, wb, n_heads, head_dim = state_attn_k.shape
    assert dec_seq == 1 and head_dim == HEAD_DIM and wb == max(w for w, _ in DILATED_CONFIGS)
    assert seq == wb

    bias_p = _prompt_bias_tables(rel_bias)
    bias_s, bias0 = _sample_bias_tables(rel_bias, wb)
    kt_state = state_attn_k.transpose(0, 1, 3, 4, 2)
    vt_state = state_attn_v.transpose(0, 1, 3, 4, 2)
    rolled = None
    new_windows = None

    xp = x_prompt.reshape(batch * seq, d_model)
    xs = x_sample.reshape(dec_batch, d_model)
    pt, st = PROMPT_TILES, SAMPLE_TILES
    cp_new, cs_new = [], []
    w_in_l = w_in
    for l in range(depth):
        proj_s = _inproj(xs, norm_mix[l], w_in_l, l, q_norm[l], k_norm[l],
                         tm=dec_batch, tn=st["tn"])
        if l == 0:
            proj_s, w_in_l = proj_s
        width = proj_s.shape[-1]
        oa_s, oc_s, cs, rolled = _mixer_sample(l, proj_s, kt_state, vt_state, rolled, bias_s, bias0,
                                               conv_w[l], state_conv)
        cs_new.append(cs)

        if l + 1 < depth:
            proj, w_in_l = _inproj(xp, norm_mix[l], w_in_l, l, q_norm[l], k_norm[l],
                                   tm=pt["tm"], tn=pt["tn_in"], w_next=(w_in, l + 1))
        else:
            proj = _inproj(xp, norm_mix[l], w_in_l, l, q_norm[l], k_norm[l],
                           tm=pt["tm"], tn=pt["tn_in"])
        oa, oc, cs, new_windows, (w_out_l, w_up_l, w_down_l) = _mixer_prompt(
            l, depth, proj.reshape(N_SEGMENTS, batch, seq, width), conv_w[l], bias_p, new_windows,
            [(w_out, l), (w_up, l), (w_down, l)])
        cp_new.append(cs)

        xs = _outproj(oa_s.reshape(dec_batch, width), oc_s.reshape(dec_batch, width),
                      attn_out_norm[l], conv_out_norm[l], w_out_l, l, xs,
                      tm=dec_batch, tn=st["tn"])
        xs = _mlp(xs, norm_mlp[l], w_up_l, w_down_l, l, tm=dec_batch, tf=st["tf"])
        xp = _outproj(oa.reshape(batch * seq, width), oc.reshape(batch * seq, width),
                      attn_out_norm[l], conv_out_norm[l], w_out_l, l, xp, tm=pt["tm"], tn=pt["tn_out"])
        xp = _mlp(xp, norm_mlp[l], w_up_l, w_down_l, l, tm=pt["tm"], tf=pt["tf"])

    ks_new, vs_new = (r.transpose(0, 1, 4, 2, 3) for r in rolled)
    kp_new, vp_new = (w.transpose(0, 1, 4, 2, 3) for w in new_windows)
    return (xp.reshape(batch, seq, d_model), xs.reshape(dec_batch, dec_seq, d_model),
            kp_new, vp_new, jnp.stack(cp_new), ks_new, vs_new, jnp.stack(cs_new))
```

```python
import functools

import jax
import jax.numpy as jnp
import numpy as np
from jax import lax
from jax.experimental import pallas as pl
from jax.experimental.pallas import tpu as pltpu

HEAD_DIM = 64
CONV_WIDTH = 3
DILATED_CONFIGS = ((128, 1), (512, 4), (2048, 16))
N_BUCKETS = 32
MAX_DISTANCE = 2048
EPS = 1e-6
ATTN_SCALE = HEAD_DIM ** -0.5
MASKED = -1e30

LANES = 128
MXU_WIDTH = 256
Q_BLOCK = 128
MERGE_ROWS = 128
HEADS_PER_BLOCK = LANES // HEAD_DIM
VMEM_LIMIT_BYTES = 56 * 1024 * 1024

F32 = jnp.float32
BF16 = jnp.bfloat16


def _compiler_params(semantics, flags=None):
    return pltpu.CompilerParams(dimension_semantics=semantics,
                                vmem_limit_bytes=VMEM_LIMIT_BYTES, flags=flags)


def _rms_rows(x, gain):
    ms = jnp.mean(x * x, axis=-1, keepdims=True)
    return (x * lax.rsqrt(ms + EPS)) * gain


N_SEGMENTS = 6
SEG_Q, SEG_K, SEG_V, SEG_VC, SEG_B, SEG_C = range(N_SEGMENTS)


def _bf16_weight(w_ref, copy_ref):
    w = w_ref[...]
    if w.dtype != BF16:
        w = w.astype(BF16)
    if copy_ref is not None:
        copy_ref[...] = w
    return w


def _inproj_kernel(x_ref, g_ref, w_ref, qg_ref, kg_ref, grp_ref, *rest, tiles_per_seg, emit_copy,
                   round_next):
    rest = list(rest)
    wnext_ref = rest.pop(0) if round_next else None
    o_ref = rest.pop(0)
    wcopy_ref = rest.pop(0) if emit_copy else None
    wnext_copy_ref = rest.pop(0) if round_next else None
    h_scr, = rest
    n = pl.program_id(1)

    if round_next:
        wnext_copy_ref[...] = wnext_ref[...].astype(BF16)

    @pl.when(n == 0)
    def _():
        h_scr[...] = _rms_rows(x_ref[...], g_ref[...]).astype(BF16)

    y = jnp.dot(h_scr[...], _bf16_weight(w_ref, wcopy_ref), preferred_element_type=F32)
    seg = n // tiles_per_seg

    def head_norm(gain_ref):
        chunk = grp_ref.shape[0]
        for j in range(y.shape[1] // chunk):
            cols = slice(j * chunk, (j + 1) * chunk)
            yj = y[:, cols]
            ms = jnp.dot((yj * yj).astype(BF16), grp_ref[...], preferred_element_type=F32)
            o_ref[0, :, cols] = (yj * lax.rsqrt(ms + EPS)) * gain_ref[:, cols]

    @pl.when(seg == SEG_Q)
    def _():
        head_norm(qg_ref)

    @pl.when(seg == SEG_K)
    def _():
        head_norm(kg_ref)

    @pl.when(seg > SEG_K)
    def _():
        o_ref[0] = y


def _weight_spec(w, layer, block, index_map):
    if w.ndim == 3:
        return pl.BlockSpec((None,) + block, lambda *ids: (layer,) + index_map(*ids))
    return pl.BlockSpec(block, index_map)


def _inproj(x, gain, w_in, layer, q_gain, k_gain, *, tm, tn, w_next=None):
    m_rows, d_model = x.shape
    d_in = w_in.shape[-1]
    emit_copy = w_in.ndim == 3
    seg_w = d_in // N_SEGMENTS
    tiles_per_seg = seg_w // tn
    grp = jnp.asarray(
        (np.arange(MXU_WIDTH)[:, None] // HEAD_DIM == np.arange(MXU_WIDTH)[None, :] // HEAD_DIM)
        .astype(np.float32) / HEAD_DIM, BF16)
    qg = jnp.tile(q_gain, tn // HEAD_DIM).reshape(1, tn)
    kg = jnp.tile(k_gain, tn // HEAD_DIM).reshape(1, tn)
    const = lambda m, n: (0, 0)
    out_specs = [pl.BlockSpec(
        (1, tm, tn), lambda m, n: (n // tiles_per_seg, m, n % tiles_per_seg))]
    out_shape = [jax.ShapeDtypeStruct((N_SEGMENTS, m_rows, seg_w), F32)]
    if emit_copy:
        assert m_rows == tm, "the bf16 copy is written once per column tile"
        out_specs.append(pl.BlockSpec((d_model, tn), lambda m, n: (0, n)))
        out_shape.append(jax.ShapeDtypeStruct((d_model, d_in), BF16))
    in_specs = [pl.BlockSpec((tm, d_model), lambda m, n: (m, 0)),
                pl.BlockSpec((1, d_model), const),
                _weight_spec(w_in, layer, (d_model, tn), lambda m, n: (0, n)),
                pl.BlockSpec((1, tn), const),
                pl.BlockSpec((1, tn), const),
                pl.BlockSpec((MXU_WIDTH, MXU_WIDTH), const)]
    args = [x, gain.reshape(1, d_model), w_in, qg, kg, grp]
    if w_next is not None:
        w_stack, next_layer = w_next
        rows = d_model // (m_rows // tm)
        in_specs.append(pl.BlockSpec((None, rows, tn), lambda m, n: (next_layer, m, n)))
        args.append(w_stack)
        out_specs.append(pl.BlockSpec((rows, tn), lambda m, n: (m, n)))
        out_shape.append(jax.ShapeDtypeStruct((d_model, d_in), BF16))
    outs = pl.pallas_call(
        functools.partial(_inproj_kernel, tiles_per_seg=tiles_per_seg, emit_copy=emit_copy,
                          round_next=w_next is not None),
        grid=(m_rows // tm, d_in // tn),
        in_specs=in_specs,
        out_specs=out_specs,
        out_shape=out_shape,
        scratch_shapes=[pltpu.VMEM((tm, d_model), BF16)],
        compiler_params=_compiler_params(("arbitrary", "arbitrary")),
        name="inproj",
    )(*args)
    return outs if len(outs) > 1 else outs[0]


def _t5_bucket(dist):
    n_exact = N_BUCKETS // 2
    large = n_exact + (np.log(np.maximum(dist, 1) / n_exact) / np.log(MAX_DISTANCE / n_exact)
                       * (N_BUCKETS - n_exact)).astype(np.int32)
    large = np.minimum(large, N_BUCKETS - 1)
    return np.where(dist < n_exact, dist, large).astype(np.int32)


def _bias_lookup(rel_bias, buckets):
    onehot = buckets[..., None] == np.arange(rel_bias.shape[0])
    return jnp.sum(jnp.where(onehot[..., None], rel_bias, 0.0), axis=-2)


def _prompt_bias_tables(rel_bias):
    n_heads = rel_bias.shape[1]
    tables = []
    for window, dil in DILATED_CONFIGS:
        nw = window // dil
        assert nw == Q_BLOCK
        bias_k = _bias_lookup(rel_bias, _t5_bucket(np.arange(nw + 1) * dil))
        pad = jnp.full((Q_BLOCK - 1, n_heads), MASKED, F32)
        u = jnp.concatenate([pad, bias_k[::-1], pad, pad[:1]])
        row = 3 * Q_BLOCK - 1
        skew = jnp.tile(u, (Q_BLOCK, 1))[:Q_BLOCK * row].reshape(Q_BLOCK, row, n_heads)
        t = skew[:, Q_BLOCK - 1:3 * Q_BLOCK - 1].transpose(2, 0, 1)
        tables.append(t.reshape(n_heads // HEADS_PER_BLOCK, HEADS_PER_BLOCK * Q_BLOCK,
                                2 * Q_BLOCK))
    return jnp.stack(tables)


def _mixer_prompt_kernel(q_ref, k_ref, v_ref, vc_ref, gb_ref, gc_ref, cw_ref, bias_ref, *rest,
                         seq, n_aliased, n_round):
    w_refs = rest[:n_round]
    rest = rest[n_round + n_aliased:]
    oa_ref, oc_ref, cs_ref, okt_ref, ovt_ref = rest[:5]
    wcopy_refs = rest[5:5 + n_round]
    qs_scr, kd_scr, v0_scr, v1_scr, o_scr, m_scr, d_scr, u_scr = rest[5 + n_round:]

    for w_ref, wcopy_ref in zip(w_refs, wcopy_refs):
        wcopy_ref[...] = w_ref[...].astype(BF16)

    for src, dst in ((k_ref, okt_ref), (v_ref, ovt_ref)):
        t = src[0].T
        for h in range(HEADS_PER_BLOCK):
            dst[h] = t[h * HEAD_DIM:(h + 1) * HEAD_DIM]

    nblk_total = seq // Q_BLOCK
    lane = lax.broadcasted_iota(jnp.int32, (Q_BLOCK, LANES), 1)
    head0 = lane < HEAD_DIM

    for c, (window, dil) in enumerate(DILATED_CONFIGS):
        sub_len = seq // dil
        blocks_per_sub = sub_len // Q_BLOCK

        for r in range(dil):
            rows = pl.ds(r, sub_len, stride=dil) if dil > 1 else pl.ds(0, seq)
            head0_rows = lax.broadcasted_iota(jnp.int32, (sub_len, LANES), 1) < HEAD_DIM
            qv = (q_ref[0, rows, :] * ATTN_SCALE).astype(BF16)
            kv = k_ref[0, rows, :].astype(BF16)
            vv = v_ref[0, rows, :].astype(BF16)
            zero = jnp.zeros_like(qv)
            q0 = jnp.where(head0_rows, qv, zero)
            q1 = jnp.where(head0_rows, zero, qv)
            for gb in range(blocks_per_sub):
                g = r * blocks_per_sub + gb
                src = slice(gb * Q_BLOCK, (gb + 1) * Q_BLOCK)
                qs_scr[c, 2 * g * Q_BLOCK:(2 * g + 1) * Q_BLOCK, :] = q0[src]
                qs_scr[c, (2 * g + 1) * Q_BLOCK:(2 * g + 2) * Q_BLOCK, :] = q1[src]
            dst = slice(r * sub_len, (r + 1) * sub_len)
            kd_scr[c, dst, :] = kv
            v0_scr[c, dst, :] = jnp.where(head0_rows, vv, zero)
            v1_scr[c, dst, :] = jnp.where(head0_rows, zero, vv)

        for g in range(nblk_total):
            r, gb = divmod(g, blocks_per_sub)
            lhs = qs_scr[c, 2 * g * Q_BLOCK:(2 * g + 2) * Q_BLOCK, :]
            if gb == 0:
                win = slice(g * Q_BLOCK, (g + 1) * Q_BLOCK)
                bias = bias_ref[c, 0, :, Q_BLOCK:]
            else:
                win = slice((g - 1) * Q_BLOCK, (g + 1) * Q_BLOCK)
                bias = bias_ref[c, 0]
            logits = lax.dot_general(lhs, kd_scr[c, win, :], (((1,), (1,)), ((), ())),
                                     preferred_element_type=F32) + bias
            m = jnp.max(logits, axis=-1, keepdims=True)
            p = jnp.exp(logits - m)
            den = jnp.sum(p, axis=-1, keepdims=True)
            pb = p.astype(BF16)
            o = (jnp.dot(pb[0:Q_BLOCK], v0_scr[c, win, :], preferred_element_type=F32)
                 + jnp.dot(pb[Q_BLOCK:], v1_scr[c, win, :], preferred_element_type=F32))
            m2 = jnp.where(head0, m[0:Q_BLOCK], m[Q_BLOCK:])
            d2 = jnp.where(head0, den[0:Q_BLOCK], den[Q_BLOCK:])
            if dil > 1:
                out_rows = pl.ds(gb * Q_BLOCK * dil + r, Q_BLOCK, stride=dil)
            else:
                out_rows = pl.ds(g * Q_BLOCK, Q_BLOCK)
            o_scr[c, out_rows, :] = o
            m_scr[c, out_rows, :] = m2
            d_scr[c, out_rows, :] = d2

    n_br = len(DILATED_CONFIGS)

    def merge_rows(i, carry):
        rows = pl.ds(pl.multiple_of(i * MERGE_ROWS, MERGE_ROWS), MERGE_ROWS)
        mmax = m_scr[0, rows, :]
        for c in range(1, n_br):
            mmax = jnp.maximum(mmax, m_scr[c, rows, :])
        num = jnp.zeros((MERGE_ROWS, LANES), F32)
        den = jnp.zeros((MERGE_ROWS, LANES), F32)
        for c in range(n_br):
            w = jnp.exp(m_scr[c, rows, :] - mmax)
            num = num + w * o_scr[c, rows, :]
            den = den + w * d_scr[c, rows, :]
        oa_ref[0, rows, :] = num / den
        return carry

    lax.fori_loop(0, seq // MERGE_ROWS, merge_rows, 0)

    u = gc_ref[0] * vc_ref[0]
    u_scr[0:8, :] = jnp.zeros((8, LANES), F32)
    u_scr[8:8 + seq, :] = u
    y = cw_ref[0:1, :] * u_scr[pl.ds(8 - (CONV_WIDTH - 1), seq), :]
    for i in range(1, CONV_WIDTH):
        y = y + cw_ref[i:i + 1, :] * u_scr[pl.ds(8 - (CONV_WIDTH - 1) + i, seq), :]
    oc_ref[0] = gb_ref[0] * y
    cs_ref[0] = u_scr[pl.ds(8 + seq - (CONV_WIDTH - 1), CONV_WIDTH - 1), :]


def _mixer_prompt(layer, depth, proj, conv_w_l, bias_tables, new_windows, to_round):
    _, batch, seq, width = proj.shape
    n_pairs = width // LANES
    n_heads = width // HEAD_DIM
    n_br = len(DILATED_CONFIGS)
    tok = pl.BlockSpec((1, seq, LANES), lambda b, hp: (b, 0, hp))
    seg_specs = [pl.BlockSpec((None, 1, seq, LANES), lambda b, hp, s=s: (s, b, 0, hp))
                 for s in range(N_SEGMENTS)]
    win_spec = pl.BlockSpec((None, None, HEADS_PER_BLOCK, HEAD_DIM, seq),
                            lambda b, hp: (layer, b, hp, 0, 0))
    win_shape = jax.ShapeDtypeStruct((depth, batch, n_heads, HEAD_DIM, seq), F32)
    in_specs = seg_specs + [
        pl.BlockSpec((CONV_WIDTH, LANES), lambda b, hp: (0, hp)),
        pl.BlockSpec((n_br, 1, 2 * Q_BLOCK, 2 * Q_BLOCK), lambda b, hp: (0, hp, 0, 0))]
    args = [proj] * N_SEGMENTS + [conv_w_l, bias_tables]
    n_steps = batch * n_pairs
    copy_specs, copy_shapes = [], []
    for w, w_layer in to_round:
        _, w_rows, w_cols = w.shape
        rows = w_rows // n_steps
        in_specs.append(pl.BlockSpec((None, rows, w_cols),
                                     lambda b, hp, w_layer=w_layer: (w_layer, b * n_pairs + hp, 0)))
        args.append(w)
        copy_specs.append(pl.BlockSpec((rows, w_cols), lambda b, hp: (b * n_pairs + hp, 0)))
        copy_shapes.append(jax.ShapeDtypeStruct((w_rows, w_cols), BF16))
    aliases = {}
    if new_windows is not None:
        aliases = {len(args): 3, len(args) + 1: 4}
        in_specs = in_specs + [pl.BlockSpec(memory_space=pl.ANY)] * 2
        args = args + list(new_windows)
    oa, oc, cs, okt, ovt, *copies = pl.pallas_call(
        functools.partial(_mixer_prompt_kernel, seq=seq, n_aliased=len(aliases),
                          n_round=len(to_round)),
        grid=(batch, n_pairs),
        in_specs=in_specs,
        out_specs=[tok, tok,
                   pl.BlockSpec((1, CONV_WIDTH - 1, LANES), lambda b, hp: (b, 0, hp)),
                   win_spec, win_spec] + copy_specs,
        out_shape=[jax.ShapeDtypeStruct((batch, seq, width), F32),
                   jax.ShapeDtypeStruct((batch, seq, width), F32),
                   jax.ShapeDtypeStruct((batch, CONV_WIDTH - 1, width), F32),
                   win_shape, win_shape] + copy_shapes,
        input_output_aliases=aliases,
        scratch_shapes=[pltpu.VMEM((n_br, 2 * seq, LANES), BF16),
                        pltpu.VMEM((n_br, seq, LANES), BF16),
                        pltpu.VMEM((n_br, seq, LANES), BF16),
                        pltpu.VMEM((n_br, seq, LANES), BF16),
                        pltpu.VMEM((n_br, seq, LANES), F32),
                        pltpu.VMEM((n_br, seq, LANES), F32),
                        pltpu.VMEM((n_br, seq, LANES), F32),
                        pltpu.VMEM((seq + 8, LANES), F32)],
        compiler_params=_compiler_params(("arbitrary", "arbitrary")),
        name="mixer_prompt",
    )(*args)
    return oa, oc, cs, (okt, ovt), copies


SAMPLE_HEADS_PER_STEP = 8


def _sample_bias_tables(rel_bias, wb):
    dist = wb - np.arange(wb)
    tables = []
    for window, dil in DILATED_CONFIGS:
        in_branch = (dist % dil == 0) & (dist // dil <= window // dil)
        bias = jnp.where(in_branch[:, None], _bias_lookup(rel_bias, _t5_bucket(dist)), MASKED)
        tables.append(bias.T[:, None, :])
    own = rel_bias[_t5_bucket(np.zeros((1,), np.int64))][0]
    return jnp.stack(tables), own[:, None, None]


def _mixer_sample_kernel(q_ref, kn_ref, vn_ref, kt_ref, vt_ref, bias_ref, bias0_ref,
                         vc_ref, gb_ref, gc_ref, cw_ref, sc_ref, *rest):
    oa_ref, oc_ref, cs_ref, okt_ref, ovt_ref = rest[-5:]
    heads, head_dim, wb = kt_ref.shape
    newest = lax.broadcasted_iota(jnp.int32, (head_dim, wb), 1) == wb - 1
    n_br = len(DILATED_CONFIGS)
    for h in range(heads):
        kt = kt_ref[h]
        vt = vt_ref[h]
        q, k_new, v_new = q_ref[h], kn_ref[h], vn_ref[h]
        s = jnp.sum(kt * q, axis=0, keepdims=True) * ATTN_SCALE
        l_new = jnp.sum(q * k_new, axis=0, keepdims=True) * ATTN_SCALE + bias0_ref[h]
        branches = []
        for c in range(n_br):
            logits = s + bias_ref[c, h]
            m = jnp.maximum(jnp.max(logits, axis=-1, keepdims=True), l_new)
            p = jnp.exp(logits - m)
            p_new = jnp.exp(l_new - m)
            den = jnp.sum(p, axis=-1, keepdims=True) + p_new
            branches.append((p, p_new, m, den))
        mmax = branches[0][2]
        for br in branches[1:]:
            mmax = jnp.maximum(mmax, br[2])
        p_all = jnp.zeros_like(s)
        p_new_all = jnp.zeros_like(mmax)
        den_all = jnp.zeros_like(mmax)
        for p, p_new, m, den in branches:
            w = jnp.exp(m - mmax)
            p_all = p_all + w * p
            p_new_all = p_new_all + w * p_new
            den_all = den_all + w * den
        num = jnp.sum(vt * p_all, axis=-1, keepdims=True) + p_new_all * v_new
        oa_ref[h] = num / den_all
        okt_ref[h] = jnp.where(newest, k_new, pltpu.roll(kt, wb - 1, axis=1))
        ovt_ref[h] = jnp.where(newest, v_new, pltpu.roll(vt, wb - 1, axis=1))

    @pl.when(pl.program_id(1) == 0)
    def _():
        u = gc_ref[0] * vc_ref[0]
        state = sc_ref[0]
        y = cw_ref[0:1, :] * state[0:1]
        for i in range(1, CONV_WIDTH - 1):
            y = y + cw_ref[i:i + 1, :] * state[i:i + 1]
        y = y + cw_ref[CONV_WIDTH - 1:CONV_WIDTH, :] * u
        oc_ref[0] = gb_ref[0] * y
        cs_ref[0] = jnp.concatenate([state[1:], u], axis=0)


def _mixer_sample(layer, proj, kt_state, vt_state, rolled, bias_s, bias0, conv_w_l, state_conv):
    depth, batch, n_heads, head_dim, wb = kt_state.shape
    width = proj.shape[-1]
    hb = SAMPLE_HEADS_PER_STEP
    n_br = len(DILATED_CONFIGS)
    qkv = proj[SEG_Q:SEG_V + 1].reshape(3, batch, n_heads, head_dim, 1)
    rows = proj.reshape(N_SEGMENTS, batch, 1, width)
    col_specs = [pl.BlockSpec((None, None, hb, head_dim, 1), lambda b, hc, s=s: (s, b, hc, 0, 0))
                 for s in range(3)]
    row_specs = [pl.BlockSpec((None, 1, 1, width), lambda b, hc, s=s: (s, b, 0, 0))
                 for s in (SEG_VC, SEG_B, SEG_C)]
    buf_spec = pl.BlockSpec((None, None, hb, head_dim, wb), lambda b, hc: (layer, b, hc, 0, 0))
    in_specs = col_specs + [
        buf_spec, buf_spec,
        pl.BlockSpec((n_br, hb, 1, wb), lambda b, hc: (0, hc, 0, 0)),
        pl.BlockSpec((hb, 1, 1), lambda b, hc: (hc, 0, 0))] + row_specs + [
        pl.BlockSpec((CONV_WIDTH, width), lambda b, hc: (0, 0)),
        pl.BlockSpec((1, CONV_WIDTH - 1, width), lambda b, hc: (layer * batch + b, 0, 0))]
    args = [qkv, qkv, qkv, kt_state, vt_state, bias_s, bias0, rows, rows, rows, conv_w_l,
            state_conv.reshape(depth * batch, CONV_WIDTH - 1, width)]
    aliases = {}
    if rolled is not None:
        aliases = {len(args): 3, len(args) + 1: 4}
        in_specs = in_specs + [pl.BlockSpec(memory_space=pl.ANY)] * 2
        args = args + list(rolled)
    oa, oc, cs, okt, ovt = pl.pallas_call(
        _mixer_sample_kernel,
        grid=(batch, n_heads // hb),
        in_specs=in_specs,
        out_specs=[pl.BlockSpec((None, hb, head_dim, 1), lambda b, hc: (b, hc, 0, 0)),
                   pl.BlockSpec((1, 1, width), lambda b, hc: (b, 0, 0)),
                   pl.BlockSpec((1, CONV_WIDTH - 1, width), lambda b, hc: (b, 0, 0)),
                   buf_spec, buf_spec],
        out_shape=[jax.ShapeDtypeStruct((batch, n_heads, head_dim, 1), F32),
                   jax.ShapeDtypeStruct((batch, 1, width), F32),
                   jax.ShapeDtypeStruct((batch, CONV_WIDTH - 1, width), F32),
                   jax.ShapeDtypeStruct(kt_state.shape, F32),
                   jax.ShapeDtypeStruct(vt_state.shape, F32)],
        input_output_aliases=aliases,
        compiler_params=_compiler_params(("arbitrary", "arbitrary")),
        name="mixer_sample",
    )(*args)
    return oa, oc, cs, (okt, ovt)


def _outproj_kernel(oa_ref, oc_ref, ga_ref, gc_ref, w_ref, x_ref, o_ref, *rest):
    z_scr = rest[-1]
    wcopy_ref = rest[0] if len(rest) == 2 else None
    n = pl.program_id(1)
    d_attn = oa_ref.shape[1]
    tn = o_ref.shape[1]

    @pl.when(n == 0)
    def _():
        z_scr[:, 0:d_attn] = _rms_rows(oa_ref[...], ga_ref[...]).astype(BF16)
        z_scr[:, d_attn:] = _rms_rows(oc_ref[...], gc_ref[...]).astype(BF16)

    if w_ref.shape[1] == tn:
        w = _bf16_weight(w_ref, wcopy_ref)
    else:
        w = w_ref[:, pl.ds(pl.multiple_of(n * tn, tn), tn)]
    o_ref[...] = x_ref[...] + jnp.dot(z_scr[...], w, preferred_element_type=F32)


def _outproj(oa, oc, attn_gain, conv_gain, w_out, layer, x, *, tm, tn):
    m_rows, d_attn = oa.shape
    d_conv = oc.shape[1]
    d_model = x.shape[1]
    d_mix = d_attn + d_conv
    emit_copy = w_out.ndim == 3
    const = lambda m, n: (0, 0)
    out_specs = [pl.BlockSpec((tm, tn), lambda m, n: (m, n))]
    out_shape = [jax.ShapeDtypeStruct((m_rows, d_model), F32)]
    if emit_copy:
        assert m_rows == tm, "the bf16 copy is written once per column tile"
        w_spec = _weight_spec(w_out, layer, (d_mix, tn), lambda m, n: (0, n))
        out_specs.append(pl.BlockSpec((d_mix, tn), lambda m, n: (0, n)))
        out_shape.append(jax.ShapeDtypeStruct((d_mix, d_model), BF16))
    else:
        w_spec = pl.BlockSpec((d_mix, d_model), const, pipeline_mode=pl.Buffered(1))
    outs = pl.pallas_call(
        _outproj_kernel,
        grid=(m_rows // tm, d_model // tn),
        in_specs=[pl.BlockSpec((tm, d_attn), lambda m, n: (m, 0)),
                  pl.BlockSpec((tm, d_conv), lambda m, n: (m, 0)),
                  pl.BlockSpec((1, d_attn), const),
                  pl.BlockSpec((1, d_conv), const),
                  w_spec,
                  pl.BlockSpec((tm, tn), lambda m, n: (m, n))],
        out_specs=out_specs,
        out_shape=out_shape,
        scratch_shapes=[pltpu.VMEM((tm, d_mix), BF16)],
        compiler_params=_compiler_params(("arbitrary", "arbitrary")),
        name="outproj",
    )(oa, oc, attn_gain.reshape(1, d_attn), conv_gain.reshape(1, d_conv), w_out, x)
    return outs if emit_copy else outs[0]


def _mlp_kernel(x_ref, g_ref, wu_ref, wd_ref, o_ref, *rest):
    h_scr = rest[-1]
    wu_copy_ref, wd_copy_ref = rest[:2] if len(rest) == 3 else (None, None)
    f = pl.program_id(1)

    @pl.when(f == 0)
    def _():
        x = x_ref[...]
        h_scr[...] = _rms_rows(x, g_ref[...]).astype(BF16)
        o_ref[...] = x

    a = jnp.dot(h_scr[...], _bf16_weight(wu_ref, wu_copy_ref), preferred_element_type=F32)
    a = jnp.square(jnp.maximum(a, 0.0)).astype(BF16)
    o_ref[...] += jnp.dot(a, _bf16_weight(wd_ref, wd_copy_ref), preferred_element_type=F32)


def _mlp(x, gain, w_up, w_down, layer, *, tm, tf):
    m_rows, d_model = x.shape
    d_ff = w_up.shape[-1]
    emit_copy = w_up.ndim == 3
    out_specs = [pl.BlockSpec((tm, d_model), lambda m, f: (m, 0))]
    out_shape = [jax.ShapeDtypeStruct((m_rows, d_model), F32)]
    if emit_copy:
        assert m_rows == tm, "the bf16 copies are written once per d_ff tile"
        out_specs += [pl.BlockSpec((d_model, tf), lambda m, f: (0, f)),
                      pl.BlockSpec((tf, d_model), lambda m, f: (f, 0))]
        out_shape += [jax.ShapeDtypeStruct((d_model, d_ff), BF16),
                      jax.ShapeDtypeStruct((d_ff, d_model), BF16)]
    outs = pl.pallas_call(
        _mlp_kernel,
        grid=(m_rows // tm, d_ff // tf),
        in_specs=[pl.BlockSpec((tm, d_model), lambda m, f: (m, 0), pipeline_mode=pl.Buffered(1)),
                  pl.BlockSpec((1, d_model), lambda m, f: (0, 0)),
                  _weight_spec(w_up, layer, (d_model, tf), lambda m, f: (0, f)),
                  _weight_spec(w_down, layer, (tf, d_model), lambda m, f: (f, 0))],
        out_specs=out_specs,
        out_shape=out_shape,
        scratch_shapes=[pltpu.VMEM((tm, d_model), BF16)],
        compiler_params=_compiler_params(("arbitrary", "arbitrary")),
        name="mlp",
    )(x, gain.reshape(1, d_model), w_up, w_down)
    return outs if emit_copy else outs[0]


PROMPT_TILES = dict(tm=1024, tn_in=1024, tn_out=1024, tf=1024)
SAMPLE_TILES = dict(tn=1024, tf=1024)


def kernel(x_prompt, x_sample, state_attn_k, state_attn_v, state_conv, rel_bias, norm_mix, w_in,
           q_norm, k_norm, conv_w, attn_out_norm, conv_out_norm, w_out, norm_mlp, w_up, w_down):
    batch, seq, d_model = x_prompt.shape
    dec_batch, dec_seq, _ = x_sample.shape
    depth, _, wb, n_heads, head_dim = state_attn_k.shape
    assert dec_seq == 1 and head_dim == HEAD_DIM and wb == max(w for w, _ in DILATED_CONFIGS)
    assert seq == wb

    bias_p = _prompt_bias_tables(rel_bias)
    bias_s, bias0 = _sample_bias_tables(rel_bias, wb)
    kt_state = state_attn_k.transpose(0, 1, 3, 4, 2)
    vt_state = state_attn_v.transpose(0, 1, 3, 4, 2)
    rolled = None
    new_windows = None

    xp = x_prompt.reshape(batch * seq, d_model)
    xs = x_sample.reshape(dec_batch, d_model)
    pt, st = PROMPT_TILES, SAMPLE_TILES
    cp_new, cs_new = [], []
    w_in_l = w_in
    for l in range(depth):
        proj_s = _inproj(xs, norm_mix[l], w_in_l, l, q_norm[l], k_norm[l],
                         tm=dec_batch, tn=st["tn"])
        if l == 0:
            proj_s, w_in_l = proj_s
        width = proj_s.shape[-1]
        oa_s, oc_s, cs, rolled = _mixer_sample(l, proj_s, kt_state, vt_state, rolled, bias_s, bias0,
                                               conv_w[l], state_conv)
        cs_new.append(cs)

        if l + 1 < depth:
            proj, w_in_l = _inproj(xp, norm_mix[l], w_in_l, l, q_norm[l], k_norm[l],
                                   tm=pt["tm"], tn=pt["tn_in"], w_next=(w_in, l + 1))
        else:
            proj = _inproj(xp, norm_mix[l], w_in_l, l, q_norm[l], k_norm[l],
                           tm=pt["tm"], tn=pt["tn_in"])
        oa, oc, cs, new_windows, (w_out_l, w_up_l, w_down_l) = _mixer_prompt(
            l, depth, proj.reshape(N_SEGMENTS, batch, seq, width), conv_w[l], bias_p, new_windows,
            [(w_out, l), (w_up, l), (w_down, l)])
        cp_new.append(cs)

        xs = _outproj(oa_s.reshape(dec_batch, width), oc_s.reshape(dec_batch, width),
                      attn_out_norm[l], conv_out_norm[l], w_out_l, l, xs,
                      tm=dec_batch, tn=st["tn"])
        xs = _mlp(xs, norm_mlp[l], w_up_l, w_down_l, l, tm=dec_batch, tf=st["tf"])
        xp = _outproj(oa.reshape(batch * seq, width), oc.reshape(batch * seq, width),
                      attn_out_norm[l], conv_out_norm[l], w_out_l, l, xp, tm=pt["tm"], tn=pt["tn_out"])
        xp = _mlp(xp, norm_mlp[l], w_up_l, w_down_l, l, tm=pt["tm"], tf=pt["tf"])

    ks_new, vs_new = (r.transpose(0, 1, 4, 2, 3) for r in rolled)
    kp_new, vp_new = (w.transpose(0, 1, 4, 2, 3) for w in new_windows)
    return (xp.reshape(batch, seq, d_model), xs.reshape(dec_batch, dec_seq, d_model),
            kp_new, vp_new, jnp.stack(cp_new), ks_new, vs_new, jnp.stack(cs_new))
```

```python
import functools

import jax
import jax.numpy as jnp
import numpy as np
from jax import lax
from jax.experimental import pallas as pl
from jax.experimental.pallas import tpu as pltpu

HEAD_DIM = 64
CONV_WIDTH = 3
DILATED_CONFIGS = ((128, 1), (512, 4), (2048, 16))
N_BUCKETS = 32
MAX_DISTANCE = 2048
EPS = 1e-6
ATTN_SCALE = HEAD_DIM ** -0.5
MASKED = -1e30

LANES = 128
MXU_WIDTH = 256
Q_BLOCK = 128
MERGE_ROWS = 128
HEADS_PER_BLOCK = LANES // HEAD_DIM
VMEM_LIMIT_BYTES = 56 * 1024 * 1024

F32 = jnp.float32
BF16 = jnp.bfloat16


def _compiler_params(semantics, flags=None):
    return pltpu.CompilerParams(dimension_semantics=semantics,
                                vmem_limit_bytes=VMEM_LIMIT_BYTES, flags=flags)


def _rms_rows(x, gain):
    ms = jnp.mean(x * x, axis=-1, keepdims=True)
    return (x * lax.rsqrt(ms + EPS)) * gain


N_SEGMENTS = 6
SEG_Q, SEG_K, SEG_V, SEG_VC, SEG_B, SEG_C = range(N_SEGMENTS)


def _bf16_weight(w_ref, copy_ref):
    w = w_ref[...]
    if w.dtype != BF16:
        w = w.astype(BF16)
    if copy_ref is not None:
        copy_ref[...] = w
    return w


def _inproj_kernel(x_ref, g_ref, w_ref, hg_ref, grp_ref, *rest, tiles_per_seg, emit_copy,
                   round_next):
    rest = list(rest)
    wnext_ref = rest.pop(0) if round_next else None
    o_ref = rest.pop(0)
    wcopy_ref = rest.pop(0) if emit_copy else None
    wnext_copy_ref = rest.pop(0) if round_next else None
    h_scr, = rest
    n = pl.program_id(1)

    if round_next:
        wnext_copy_ref[...] = wnext_ref[...].astype(BF16)

    @pl.when(n == 0)
    def _():
        h_scr[...] = _rms_rows(x_ref[...], g_ref[...]).astype(BF16)

    o_ref[0] = jnp.dot(h_scr[...], _bf16_weight(w_ref, wcopy_ref), preferred_element_type=F32)
    seg = n // tiles_per_seg

    @pl.when(seg <= SEG_K)
    def _():
        gain = hg_ref[pl.ds(seg, 1), :]
        chunk = grp_ref.shape[0]
        for j in range(o_ref.shape[2] // chunk):
            cols = slice(j * chunk, (j + 1) * chunk)
            yj = o_ref[0, :, cols]
            ms = jnp.dot((yj * yj).astype(BF16), grp_ref[...], preferred_element_type=F32)
            o_ref[0, :, cols] = (yj * lax.rsqrt(ms + EPS)) * gain[:, cols]


def _weight_spec(w, layer, block, index_map):
    if w.ndim == 3:
        return pl.BlockSpec((None,) + block, lambda *ids: (layer,) + index_map(*ids))
    return pl.BlockSpec(block, index_map)


def _inproj(x, gain, w_in, layer, q_gain, k_gain, *, tm, tn, w_next=None):
    m_rows, d_model = x.shape
    d_in = w_in.shape[-1]
    emit_copy = w_in.ndim == 3
    seg_w = d_in // N_SEGMENTS
    tiles_per_seg = seg_w // tn
    grp = jnp.asarray(
        (np.arange(MXU_WIDTH)[:, None] // HEAD_DIM == np.arange(MXU_WIDTH)[None, :] // HEAD_DIM)
        .astype(np.float32) / HEAD_DIM, BF16)
    head_gains = jnp.stack([jnp.tile(q_gain, tn // HEAD_DIM), jnp.tile(k_gain, tn // HEAD_DIM)])
    assert (SEG_Q, SEG_K) == (0, 1)
    const = lambda m, n: (0, 0)
    out_specs = [pl.BlockSpec(
        (1, tm, tn), lambda m, n: (n // tiles_per_seg, m, n % tiles_per_seg))]
    out_shape = [jax.ShapeDtypeStruct((N_SEGMENTS, m_rows, seg_w), F32)]
    if emit_copy:
        assert m_rows == tm, "the bf16 copy is written once per column tile"
        out_specs.append(pl.BlockSpec((d_model, tn), lambda m, n: (0, n)))
        out_shape.append(jax.ShapeDtypeStruct((d_model, d_in), BF16))
    in_specs = [pl.BlockSpec((tm, d_model), lambda m, n: (m, 0)),
                pl.BlockSpec((1, d_model), const),
                _weight_spec(w_in, layer, (d_model, tn), lambda m, n: (0, n)),
                pl.BlockSpec((2, tn), const),
                pl.BlockSpec((MXU_WIDTH, MXU_WIDTH), const)]
    args = [x, gain.reshape(1, d_model), w_in, head_gains, grp]
    if w_next is not None:
        w_stack, next_layer = w_next
        rows = d_model // (m_rows // tm)
        in_specs.append(pl.BlockSpec((None, rows, tn), lambda m, n: (next_layer, m, n)))
        args.append(w_stack)
        out_specs.append(pl.BlockSpec((rows, tn), lambda m, n: (m, n)))
        out_shape.append(jax.ShapeDtypeStruct((d_model, d_in), BF16))
    outs = pl.pallas_call(
        functools.partial(_inproj_kernel, tiles_per_seg=tiles_per_seg, emit_copy=emit_copy,
                          round_next=w_next is not None),
        grid=(m_rows // tm, d_in // tn),
        in_specs=in_specs,
        out_specs=out_specs,
        out_shape=out_shape,
        scratch_shapes=[pltpu.VMEM((tm, d_model), BF16)],
        compiler_params=_compiler_params(("arbitrary", "arbitrary")),
        name="inproj",
    )(*args)
    return outs if len(outs) > 1 else outs[0]


def _t5_bucket(dist):
    n_exact = N_BUCKETS // 2
    large = n_exact + (np.log(np.maximum(dist, 1) / n_exact) / np.log(MAX_DISTANCE / n_exact)
                       * (N_BUCKETS - n_exact)).astype(np.int32)
    large = np.minimum(large, N_BUCKETS - 1)
    return np.where(dist < n_exact, dist, large).astype(np.int32)


def _bias_lookup(rel_bias, buckets):
    onehot = buckets[..., None] == np.arange(rel_bias.shape[0])
    return jnp.sum(jnp.where(onehot[..., None], rel_bias, 0.0), axis=-2)


def _prompt_bias_tables(rel_bias):
    n_heads = rel_bias.shape[1]
    tables = []
    for window, dil in DILATED_CONFIGS:
        nw = window // dil
        assert nw == Q_BLOCK
        bias_k = _bias_lookup(rel_bias, _t5_bucket(np.arange(nw + 1) * dil))
        pad = jnp.full((Q_BLOCK - 1, n_heads), MASKED, F32)
        u = jnp.concatenate([pad, bias_k[::-1], pad, pad[:1]])
        row = 3 * Q_BLOCK - 1
        skew = jnp.tile(u, (Q_BLOCK, 1))[:Q_BLOCK * row].reshape(Q_BLOCK, row, n_heads)
        t = skew[:, Q_BLOCK - 1:3 * Q_BLOCK - 1].transpose(2, 0, 1)
        tables.append(t.reshape(n_heads // HEADS_PER_BLOCK, HEADS_PER_BLOCK * Q_BLOCK,
                                2 * Q_BLOCK))
    return jnp.stack(tables)


def _mixer_prompt_kernel(q_ref, k_ref, v_ref, vc_ref, gb_ref, gc_ref, cw_ref, bias_ref, *rest,
                         seq, n_aliased, n_round):
    w_refs = rest[:n_round]
    rest = rest[n_round + n_aliased:]
    oa_ref, oc_ref, cs_ref, okt_ref, ovt_ref = rest[:5]
    wcopy_refs = rest[5:5 + n_round]
    qs_scr, kd_scr, v0_scr, v1_scr, o_scr, m_scr, d_scr, u_scr = rest[5 + n_round:]

    for w_ref, wcopy_ref in zip(w_refs, wcopy_refs):
        wcopy_ref[...] = w_ref[...].astype(BF16)

    for src, dst in ((k_ref, okt_ref), (v_ref, ovt_ref)):
        t = src[0].T
        for h in range(HEADS_PER_BLOCK):
            dst[h] = t[h * HEAD_DIM:(h + 1) * HEAD_DIM]

    nblk_total = seq // Q_BLOCK
    lane = lax.broadcasted_iota(jnp.int32, (Q_BLOCK, LANES), 1)
    head0 = lane < HEAD_DIM

    for c, (window, dil) in enumerate(DILATED_CONFIGS):
        sub_len = seq // dil
        blocks_per_sub = sub_len // Q_BLOCK

        for r in range(dil):
            rows = pl.ds(r, sub_len, stride=dil) if dil > 1 else pl.ds(0, seq)
            head0_rows = lax.broadcasted_iota(jnp.int32, (sub_len, LANES), 1) < HEAD_DIM
            qv = (q_ref[0, rows, :] * ATTN_SCALE).astype(BF16)
            kv = k_ref[0, rows, :].astype(BF16)
            vv = v_ref[0, rows, :].astype(BF16)
            zero = jnp.zeros_like(qv)
            q0 = jnp.where(head0_rows, qv, zero)
            q1 = jnp.where(head0_rows, zero, qv)
            for gb in range(blocks_per_sub):
                g = r * blocks_per_sub + gb
                src = slice(gb * Q_BLOCK, (gb + 1) * Q_BLOCK)
                qs_scr[c, 2 * g * Q_BLOCK:(2 * g + 1) * Q_BLOCK, :] = q0[src]
                qs_scr[c, (2 * g + 1) * Q_BLOCK:(2 * g + 2) * Q_BLOCK, :] = q1[src]
            dst = slice(r * sub_len, (r + 1) * sub_len)
            kd_scr[c, dst, :] = kv
            v0_scr[c, dst, :] = jnp.where(head0_rows, vv, zero)
            v1_scr[c, dst, :] = jnp.where(head0_rows, zero, vv)

        for g in range(nblk_total):
            r, gb = divmod(g, blocks_per_sub)
            lhs = qs_scr[c, 2 * g * Q_BLOCK:(2 * g + 2) * Q_BLOCK, :]
            if gb == 0:
                win = slice(g * Q_BLOCK, (g + 1) * Q_BLOCK)
                bias = bias_ref[c, 0, :, Q_BLOCK:]
            else:
                win = slice((g - 1) * Q_BLOCK, (g + 1) * Q_BLOCK)
                bias = bias_ref[c, 0]
            logits = lax.dot_general(lhs, kd_scr[c, win, :], (((1,), (1,)), ((), ())),
                                     preferred_element_type=F32) + bias
            m = jnp.max(logits, axis=-1, keepdims=True)
            p = jnp.exp(logits - m)
            den = jnp.sum(p, axis=-1, keepdims=True)
            pb = p.astype(BF16)
            o = (jnp.dot(pb[0:Q_BLOCK], v0_scr[c, win, :], preferred_element_type=F32)
                 + jnp.dot(pb[Q_BLOCK:], v1_scr[c, win, :], preferred_element_type=F32))
            m2 = jnp.where(head0, m[0:Q_BLOCK], m[Q_BLOCK:])
            d2 = jnp.where(head0, den[0:Q_BLOCK], den[Q_BLOCK:])
            if dil > 1:
                out_rows = pl.ds(gb * Q_BLOCK * dil + r, Q_BLOCK, stride=dil)
            else:
                out_rows = pl.ds(g * Q_BLOCK, Q_BLOCK)
            o_scr[c, out_rows, :] = o
            m_scr[c, out_rows, :] = m2
            d_scr[c, out_rows, :] = d2

    n_br = len(DILATED_CONFIGS)

    def merge_rows(i, carry):
        rows = pl.ds(pl.multiple_of(i * MERGE_ROWS, MERGE_ROWS), MERGE_ROWS)
        mmax = m_scr[0, rows, :]
        for c in range(1, n_br):
            mmax = jnp.maximum(mmax, m_scr[c, rows, :])
        num = jnp.zeros((MERGE_ROWS, LANES), F32)
        den = jnp.zeros((MERGE_ROWS, LANES), F32)
        for c in range(n_br):
            w = jnp.exp(m_scr[c, rows, :] - mmax)
            num = num + w * o_scr[c, rows, :]
            den = den + w * d_scr[c, rows, :]
        oa_ref[0, rows, :] = num / den
        return carry

    lax.fori_loop(0, seq // MERGE_ROWS, merge_rows, 0)

    u = gc_ref[0] * vc_ref[0]
    u_scr[0:8, :] = jnp.zeros((8, LANES), F32)
    u_scr[8:8 + seq, :] = u
    y = cw_ref[0:1, :] * u_scr[pl.ds(8 - (CONV_WIDTH - 1), seq), :]
    for i in range(1, CONV_WIDTH):
        y = y + cw_ref[i:i + 1, :] * u_scr[pl.ds(8 - (CONV_WIDTH - 1) + i, seq), :]
    oc_ref[0] = gb_ref[0] * y
    cs_ref[0] = u_scr[pl.ds(8 + seq - (CONV_WIDTH - 1), CONV_WIDTH - 1), :]


def _mixer_prompt(layer, depth, proj, conv_w_l, bias_tables, new_windows, to_round):
    _, batch, seq, width = proj.shape
    n_pairs = width // LANES
    n_heads = width // HEAD_DIM
    n_br = len(DILATED_CONFIGS)
    tok = pl.BlockSpec((1, seq, LANES), lambda b, hp: (b, 0, hp))
    seg_specs = [pl.BlockSpec((None, 1, seq, LANES), lambda b, hp, s=s: (s, b, 0, hp))
                 for s in range(N_SEGMENTS)]
    win_spec = pl.BlockSpec((None, None, HEADS_PER_BLOCK, HEAD_DIM, seq),
                            lambda b, hp: (layer, b, hp, 0, 0))
    win_shape = jax.ShapeDtypeStruct((depth, batch, n_heads, HEAD_DIM, seq), F32)
    in_specs = seg_specs + [
        pl.BlockSpec((CONV_WIDTH, LANES), lambda b, hp: (0, hp)),
        pl.BlockSpec((n_br, 1, 2 * Q_BLOCK, 2 * Q_BLOCK), lambda b, hp: (0, hp, 0, 0))]
    args = [proj] * N_SEGMENTS + [conv_w_l, bias_tables]
    n_steps = batch * n_pairs
    copy_specs, copy_shapes = [], []
    for w, w_layer in to_round:
        _, w_rows, w_cols = w.shape
        rows = w_rows // n_steps
        in_specs.append(pl.BlockSpec((None, rows, w_cols),
                                     lambda b, hp, w_layer=w_layer: (w_layer, b * n_pairs + hp, 0)))
        args.append(w)
        copy_specs.append(pl.BlockSpec((rows, w_cols), lambda b, hp: (b * n_pairs + hp, 0)))
        copy_shapes.append(jax.ShapeDtypeStruct((w_rows, w_cols), BF16))
    aliases = {}
    if new_windows is not None:
        aliases = {len(args): 3, len(args) + 1: 4}
        in_specs = in_specs + [pl.BlockSpec(memory_space=pl.ANY)] * 2
        args = args + list(new_windows)
    oa, oc, cs, okt, ovt, *copies = pl.pallas_call(
        functools.partial(_mixer_prompt_kernel, seq=seq, n_aliased=len(aliases),
                          n_round=len(to_round)),
        grid=(batch, n_pairs),
        in_specs=in_specs,
        out_specs=[tok, tok,
                   pl.BlockSpec((1, CONV_WIDTH - 1, LANES), lambda b, hp: (b, 0, hp)),
                   win_spec, win_spec] + copy_specs,
        out_shape=[jax.ShapeDtypeStruct((batch, seq, width), F32),
                   jax.ShapeDtypeStruct((batch, seq, width), F32),
                   jax.ShapeDtypeStruct((batch, CONV_WIDTH - 1, width), F32),
                   win_shape, win_shape] + copy_shapes,
        input_output_aliases=aliases,
        scratch_shapes=[pltpu.VMEM((n_br, 2 * seq, LANES), BF16),
                        pltpu.VMEM((n_br, seq, LANES), BF16),
                        pltpu.VMEM((n_br, seq, LANES), BF16),
                        pltpu.VMEM((n_br, seq, LANES), BF16),
                        pltpu.VMEM((n_br, seq, LANES), F32),
                        pltpu.VMEM((n_br, seq, LANES), F32),
                        pltpu.VMEM((n_br, seq, LANES), F32),
                        pltpu.VMEM((seq + 8, LANES), F32)],
        compiler_params=_compiler_params(("arbitrary", "arbitrary")),
        name="mixer_prompt",
    )(*args)
    return oa, oc, cs, (okt, ovt), copies


SAMPLE_HEADS_PER_STEP = 8


def _sample_bias_tables(rel_bias, wb):
    dist = wb - np.arange(wb)
    tables = []
    for window, dil in DILATED_CONFIGS:
        in_branch = (dist % dil == 0) & (dist // dil <= window // dil)
        bias = jnp.where(in_branch[:, None], _bias_lookup(rel_bias, _t5_bucket(dist)), MASKED)
        tables.append(bias.T[:, None, :])
    own = rel_bias[_t5_bucket(np.zeros((1,), np.int64))][0]
    return jnp.stack(tables), own[:, None, None]


def _mixer_sample_kernel(q_ref, kn_ref, vn_ref, kt_ref, vt_ref, bias_ref, bias0_ref,
                         vc_ref, gb_ref, gc_ref, cw_ref, sc_ref, *rest):
    oa_ref, oc_ref, cs_ref, okt_ref, ovt_ref = rest[-5:]
    heads, head_dim, wb = kt_ref.shape
    newest = lax.broadcasted_iota(jnp.int32, (head_dim, wb), 1) == wb - 1
    n_br = len(DILATED_CONFIGS)
    for h in range(heads):
        kt = kt_ref[h]
        vt = vt_ref[h]
        q, k_new, v_new = q_ref[h], kn_ref[h], vn_ref[h]
        s = jnp.sum(kt * q, axis=0, keepdims=True) * ATTN_SCALE
        l_new = jnp.sum(q * k_new, axis=0, keepdims=True) * ATTN_SCALE + bias0_ref[h]
        branches = []
        for c in range(n_br):
            logits = s + bias_ref[c, h]
            m = jnp.maximum(jnp.max(logits, axis=-1, keepdims=True), l_new)
            p = jnp.exp(logits - m)
            p_new = jnp.exp(l_new - m)
            den = jnp.sum(p, axis=-1, keepdims=True) + p_new
            branches.append((p, p_new, m, den))
        mmax = branches[0][2]
        for br in branches[1:]:
            mmax = jnp.maximum(mmax, br[2])
        p_all = jnp.zeros_like(s)
        p_new_all = jnp.zeros_like(mmax)
        den_all = jnp.zeros_like(mmax)
        for p, p_new, m, den in branches:
            w = jnp.exp(m - mmax)
            p_all = p_all + w * p
            p_new_all = p_new_all + w * p_new
            den_all = den_all + w * den
        num = jnp.sum(vt * p_all, axis=-1, keepdims=True) + p_new_all * v_new
        oa_ref[h] = num / den_all
        okt_ref[h] = jnp.where(newest, k_new, pltpu.roll(kt, wb - 1, axis=1))
        ovt_ref[h] = jnp.where(newest, v_new, pltpu.roll(vt, wb - 1, axis=1))

    @pl.when(pl.program_id(1) == 0)
    def _():
        u = gc_ref[0] * vc_ref[0]
        state = sc_ref[0]
        y = cw_ref[0:1, :] * state[0:1]
        for i in range(1, CONV_WIDTH - 1):
            y = y + cw_ref[i:i + 1, :] * state[i:i + 1]
        y = y + cw_ref[CONV_WIDTH - 1:CONV_WIDTH, :] * u
        oc_ref[0] = gb_ref[0] * y
        cs_ref[0] = jnp.concatenate([state[1:], u], axis=0)


def _mixer_sample(layer, proj, kt_state, vt_state, rolled, bias_s, bias0, conv_w_l, state_conv):
    depth, batch, n_heads, head_dim, wb = kt_state.shape
    width = proj.shape[-1]
    hb = SAMPLE_HEADS_PER_STEP
    n_br = len(DILATED_CONFIGS)
    qkv = proj[SEG_Q:SEG_V + 1].reshape(3, batch, n_heads, head_dim, 1)
    rows = proj.reshape(N_SEGMENTS, batch, 1, width)
    col_specs = [pl.BlockSpec((None, None, hb, head_dim, 1), lambda b, hc, s=s: (s, b, hc, 0, 0))
                 for s in range(3)]
    row_specs = [pl.BlockSpec((None, 1, 1, width), lambda b, hc, s=s: (s, b, 0, 0))
                 for s in (SEG_VC, SEG_B, SEG_C)]
    buf_spec = pl.BlockSpec((None, None, hb, head_dim, wb), lambda b, hc: (layer, b, hc, 0, 0))
    in_specs = col_specs + [
        buf_spec, buf_spec,
        pl.BlockSpec((n_br, hb, 1, wb), lambda b, hc: (0, hc, 0, 0)),
        pl.BlockSpec((hb, 1, 1), lambda b, hc: (hc, 0, 0))] + row_specs + [
        pl.BlockSpec((CONV_WIDTH, width), lambda b, hc: (0, 0)),
        pl.BlockSpec((1, CONV_WIDTH - 1, width), lambda b, hc: (layer * batch + b, 0, 0))]
    args = [qkv, qkv, qkv, kt_state, vt_state, bias_s, bias0, rows, rows, rows, conv_w_l,
            state_conv.reshape(depth * batch, CONV_WIDTH - 1, width)]
    aliases = {}
    if rolled is not None:
        aliases = {len(args): 3, len(args) + 1: 4}
        in_specs = in_specs + [pl.BlockSpec(memory_space=pl.ANY)] * 2
        args = args + list(rolled)
    oa, oc, cs, okt, ovt = pl.pallas_call(
        _mixer_sample_kernel,
        grid=(batch, n_heads // hb),
        in_specs=in_specs,
        out_specs=[pl.BlockSpec((None, hb, head_dim, 1), lambda b, hc: (b, hc, 0, 0)),
                   pl.BlockSpec((1, 1, width), lambda b, hc: (b, 0, 0)),
                   pl.BlockSpec((1, CONV_WIDTH - 1, width), lambda b, hc: (b, 0, 0)),
                   buf_spec, buf_spec],
        out_shape=[jax.ShapeDtypeStruct((batch, n_heads, head_dim, 1), F32),
                   jax.ShapeDtypeStruct((batch, 1, width), F32),
                   jax.ShapeDtypeStruct((batch, CONV_WIDTH - 1, width), F32),
                   jax.ShapeDtypeStruct(kt_state.shape, F32),
                   jax.ShapeDtypeStruct(vt_state.shape, F32)],
        input_output_aliases=aliases,
        compiler_params=_compiler_params(("arbitrary", "arbitrary")),
        name="mixer_sample",
    )(*args)
    return oa, oc, cs, (okt, ovt)


def _outproj_kernel(oa_ref, oc_ref, ga_ref, gc_ref, w_ref, x_ref, o_ref, *rest):
    z_scr = rest[-1]
    wcopy_ref = rest[0] if len(rest) == 2 else None
    n = pl.program_id(1)
    d_attn = oa_ref.shape[1]
    tn = o_ref.shape[1]

    @pl.when(n == 0)
    def _():
        z_scr[:, 0:d_attn] = _rms_rows(oa_ref[...], ga_ref[...]).astype(BF16)
        z_scr[:, d_attn:] = _rms_rows(oc_ref[...], gc_ref[...]).astype(BF16)

    if w_ref.shape[1] == tn:
        w = _bf16_weight(w_ref, wcopy_ref)
    else:
        w = w_ref[:, pl.ds(pl.multiple_of(n * tn, tn), tn)]
    o_ref[...] = x_ref[...] + jnp.dot(z_scr[...], w, preferred_element_type=F32)


def _outproj(oa, oc, attn_gain, conv_gain, w_out, layer, x, *, tm, tn):
    m_rows, d_attn = oa.shape
    d_conv = oc.shape[1]
    d_model = x.shape[1]
    d_mix = d_attn + d_conv
    emit_copy = w_out.ndim == 3
    const = lambda m, n: (0, 0)
    out_specs = [pl.BlockSpec((tm, tn), lambda m, n: (m, n))]
    out_shape = [jax.ShapeDtypeStruct((m_rows, d_model), F32)]
    if emit_copy:
        assert m_rows == tm, "the bf16 copy is written once per column tile"
        w_spec = _weight_spec(w_out, layer, (d_mix, tn), lambda m, n: (0, n))
        out_specs.append(pl.BlockSpec((d_mix, tn), lambda m, n: (0, n)))
        out_shape.append(jax.ShapeDtypeStruct((d_mix, d_model), BF16))
    else:
        w_spec = pl.BlockSpec((d_mix, d_model), const, pipeline_mode=pl.Buffered(1))
    outs = pl.pallas_call(
        _outproj_kernel,
        grid=(m_rows // tm, d_model // tn),
        in_specs=[pl.BlockSpec((tm, d_attn), lambda m, n: (m, 0)),
                  pl.BlockSpec((tm, d_conv), lambda m, n: (m, 0)),
                  pl.BlockSpec((1, d_attn), const),
                  pl.BlockSpec((1, d_conv), const),
                  w_spec,
                  pl.BlockSpec((tm, tn), lambda m, n: (m, n))],
        out_specs=out_specs,
        out_shape=out_shape,
        scratch_shapes=[pltpu.VMEM((tm, d_mix), BF16)],
        compiler_params=_compiler_params(("arbitrary", "arbitrary")),
        name="outproj",
    )(oa, oc, attn_gain.reshape(1, d_attn), conv_gain.reshape(1, d_conv), w_out, x)
    return outs if emit_copy else outs[0]


def _mlp_kernel(x_ref, g_ref, wu_ref, wd_ref, o_ref, *rest):
    h_scr = rest[-1]
    wu_copy_ref, wd_copy_ref = rest[:2] if len(rest) == 3 else (None, None)
    f = pl.program_id(1)

    @pl.when(f == 0)
    def _():
        x = x_ref[...]
        h_scr[...] = _rms_rows(x, g_ref[...]).astype(BF16)
        o_ref[...] = x

    a = jnp.dot(h_scr[...], _bf16_weight(wu_ref, wu_copy_ref), preferred_element_type=F32)
    a = jnp.square(jnp.maximum(a, 0.0)).astype(BF16)
    o_ref[...] += jnp.dot(a, _bf16_weight(wd_ref, wd_copy_ref), preferred_element_type=F32)


def _mlp(x, gain, w_up, w_down, layer, *, tm, tf):
    m_rows, d_model = x.shape
    d_ff = w_up.shape[-1]
    emit_copy = w_up.ndim == 3
    out_specs = [pl.BlockSpec((tm, d_model), lambda m, f: (m, 0))]
    out_shape = [jax.ShapeDtypeStruct((m_rows, d_model), F32)]
    if emit_copy:
        assert m_rows == tm, "the bf16 copies are written once per d_ff tile"
        out_specs += [pl.BlockSpec((d_model, tf), lambda m, f: (0, f)),
                      pl.BlockSpec((tf, d_model), lambda m, f: (f, 0))]
        out_shape += [jax.ShapeDtypeStruct((d_model, d_ff), BF16),
                      jax.ShapeDtypeStruct((d_ff, d_model), BF16)]
    outs = pl.pallas_call(
        _mlp_kernel,
        grid=(m_rows // tm, d_ff // tf),
        in_specs=[pl.BlockSpec((tm, d_model), lambda m, f: (m, 0), pipeline_mode=pl.Buffered(1)),
                  pl.BlockSpec((1, d_model), lambda m, f: (0, 0)),
                  _weight_spec(w_up, layer, (d_model, tf), lambda m, f: (0, f)),
                  _weight_spec(w_down, layer, (tf, d_model), lambda m, f: (f, 0))],
        out_specs=out_specs,
        out_shape=out_shape,
        scratch_shapes=[pltpu.VMEM((tm, d_model), BF16)],
        compiler_params=_compiler_params(("arbitrary", "arbitrary")),
        name="mlp",
    )(x, gain.reshape(1, d_model), w_up, w_down)
    return outs if emit_copy else outs[0]


PROMPT_TILES = dict(tm=1024, tn_in=1024, tn_out=1024, tf=1024)
SAMPLE_TILES = dict(tn=1024, tf=1024)


def kernel(x_prompt, x_sample, state_attn_k, state_attn_v, state_conv, rel_bias, norm_mix, w_in,
           q_norm, k_norm, conv_w, attn_out_norm, conv_out_norm, w_out, norm_mlp, w_up, w_down):
    batch, seq, d_model = x_prompt.shape
    dec_batch, dec_seq, _ = x_sample.shape
    depth, _, wb, n_heads, head_dim = state_attn_k.shape
    assert dec_seq == 1 and head_dim == HEAD_DIM and wb == max(w for w, _ in DILATED_CONFIGS)
    assert seq == wb

    bias_p = _prompt_bias_tables(rel_bias)
    bias_s, bias0 = _sample_bias_tables(rel_bias, wb)
    kt_state = state_attn_k.transpose(0, 1, 3, 4, 2)
    vt_state = state_attn_v.transpose(0, 1, 3, 4, 2)
    rolled = None
    new_windows = None

    xp = x_prompt.reshape(batch * seq, d_model)
    xs = x_sample.reshape(dec_batch, d_model)
    pt, st = PROMPT_TILES, SAMPLE_TILES
    cp_new, cs_new = [], []
    w_in_l = w_in
    for l in range(depth):
        proj_s = _inproj(xs, norm_mix[l], w_in_l, l, q_norm[l], k_norm[l],
                         tm=dec_batch, tn=st["tn"])
        if l == 0:
            proj_s, w_in_l = proj_s
        width = proj_s.shape[-1]
        oa_s, oc_s, cs, rolled = _mixer_sample(l, proj_s, kt_state, vt_state, rolled, bias_s, bias0,
                                               conv_w[l], state_conv)
        cs_new.append(cs)

        if l + 1 < depth:
            proj, w_in_l = _inproj(xp, norm_mix[l], w_in_l, l, q_norm[l], k_norm[l],
                                   tm=pt["tm"], tn=pt["tn_in"], w_next=(w_in, l + 1))
        else:
            proj = _inproj(xp, norm_mix[l], w_in_l, l, q_norm[l], k_norm[l],
                           tm=pt["tm"], tn=pt["tn_in"])
        oa, oc, cs, new_windows, (w_out_l, w_up_l, w_down_l) = _mixer_prompt(
            l, depth, proj.reshape(N_SEGMENTS, batch, seq, width), conv_w[l], bias_p, new_windows,
            [(w_out, l), (w_up, l), (w_down, l)])
        cp_new.append(cs)

        xs = _outproj(oa_s.reshape(dec_batch, width), oc_s.reshape(dec_batch, width),
                      attn_out_norm[l], conv_out_norm[l], w_out_l, l, xs,
                      tm=dec_batch, tn=st["tn"])
        xs = _mlp(xs, norm_mlp[l], w_up_l, w_down_l, l, tm=dec_batch, tf=st["tf"])
        xp = _outproj(oa.reshape(batch * seq, width), oc.reshape(batch * seq, width),
                      attn_out_norm[l], conv_out_norm[l], w_out_l, l, xp, tm=pt["tm"], tn=pt["tn_out"])
        xp = _mlp(xp, norm_mlp[l], w_up_l, w_down_l, l, tm=pt["tm"], tf=pt["tf"])

    ks_new, vs_new = (r.transpose(0, 1, 4, 2, 3) for r in rolled)
    kp_new, vp_new = (w.transpose(0, 1, 4, 2, 3) for w in new_windows)
    return (xp.reshape(batch, seq, d_model), xs.reshape(dec_batch, dec_seq, d_model),
            kp_new, vp_new, jnp.stack(cp_new), ks_new, vs_new, jnp.stack(cs_new))
```

```python
import functools

import jax
import jax.numpy as jnp
import numpy as np
from jax import lax
from jax.experimental import pallas as pl
from jax.experimental.pallas import tpu as pltpu

HEAD_DIM = 64
CONV_WIDTH = 3
DILATED_CONFIGS = ((128, 1), (512, 4), (2048, 16))
N_BUCKETS = 32
MAX_DISTANCE = 2048
EPS = 1e-6
ATTN_SCALE = HEAD_DIM ** -0.5
MASKED = -1e30

LANES = 128
MXU_WIDTH = 256
Q_BLOCK = 128
MERGE_ROWS = 128
HEADS_PER_BLOCK = LANES // HEAD_DIM
VMEM_LIMIT_BYTES = 56 * 1024 * 1024

F32 = jnp.float32
BF16 = jnp.bfloat16


def _compiler_params(semantics, flags=None):
    return pltpu.CompilerParams(dimension_semantics=semantics,
                                vmem_limit_bytes=VMEM_LIMIT_BYTES, flags=flags)


def _rms_rows(x, gain):
    ms = jnp.mean(x * x, axis=-1, keepdims=True)
    return (x * lax.rsqrt(ms + EPS)) * gain


N_SEGMENTS = 6
SEG_Q, SEG_K, SEG_V, SEG_VC, SEG_B, SEG_C = range(N_SEGMENTS)


def _bf16_weight(w_ref, copy_ref):
    w = w_ref[...]
    if w.dtype != BF16:
        w = w.astype(BF16)
    if copy_ref is not None:
        copy_ref[...] = w
    return w


def _inproj_kernel(x_ref, g_ref, w_ref, hg_ref, grp_ref, *rest, tiles_per_seg, emit_copy,
                   round_next):
    rest = list(rest)
    wnext_ref = rest.pop(0) if round_next else None
    o_ref = rest.pop(0)
    wcopy_ref = rest.pop(0) if emit_copy else None
    wnext_copy_ref = rest.pop(0) if round_next else None
    h_scr, = rest
    n = pl.program_id(1)

    if round_next:
        wnext_copy_ref[...] = wnext_ref[...].astype(BF16)

    @pl.when(n == 0)
    def _():
        h_scr[...] = _rms_rows(x_ref[...], g_ref[...]).astype(BF16)

    o_ref[0] = jnp.dot(h_scr[...], _bf16_weight(w_ref, wcopy_ref), preferred_element_type=F32)
    seg = n // tiles_per_seg

    @pl.when(seg <= SEG_K)
    def _():
        gain = hg_ref[pl.ds(seg, 1), :]
        chunk = grp_ref.shape[0]
        for j in range(o_ref.shape[2] // chunk):
            cols = slice(j * chunk, (j + 1) * chunk)
            yj = o_ref[0, :, cols]
            ms = jnp.dot((yj * yj).astype(BF16), grp_ref[...], preferred_element_type=F32)
            o_ref[0, :, cols] = (yj * lax.rsqrt(ms + EPS)) * gain[:, cols]


def _weight_spec(w, layer, block, index_map):
    if w.ndim == 3:
        return pl.BlockSpec((None,) + block, lambda *ids: (layer,) + index_map(*ids))
    return pl.BlockSpec(block, index_map)


def _inproj(x, gain, w_in, layer, q_gain, k_gain, *, tm, tn, w_next=None):
    m_rows, d_model = x.shape
    d_in = w_in.shape[-1]
    emit_copy = w_in.ndim == 3
    seg_w = d_in // N_SEGMENTS
    tiles_per_seg = seg_w // tn
    grp = jnp.asarray(
        (np.arange(MXU_WIDTH)[:, None] // HEAD_DIM == np.arange(MXU_WIDTH)[None, :] // HEAD_DIM)
        .astype(np.float32) / HEAD_DIM, BF16)
    head_gains = jnp.stack([jnp.tile(q_gain, tn // HEAD_DIM), jnp.tile(k_gain, tn // HEAD_DIM)])
    assert (SEG_Q, SEG_K) == (0, 1)
    const = lambda m, n: (0, 0)
    out_specs = [pl.BlockSpec(
        (1, tm, tn), lambda m, n: (n // tiles_per_seg, m, n % tiles_per_seg))]
    out_shape = [jax.ShapeDtypeStruct((N_SEGMENTS, m_rows, seg_w), F32)]
    if emit_copy:
        assert m_rows == tm, "the bf16 copy is written once per column tile"
        out_specs.append(pl.BlockSpec((d_model, tn), lambda m, n: (0, n)))
        out_shape.append(jax.ShapeDtypeStruct((d_model, d_in), BF16))
    in_specs = [pl.BlockSpec((tm, d_model), lambda m, n: (m, 0)),
                pl.BlockSpec((1, d_model), const),
                _weight_spec(w_in, layer, (d_model, tn), lambda m, n: (0, n)),
                pl.BlockSpec((2, tn), const),
                pl.BlockSpec((MXU_WIDTH, MXU_WIDTH), const)]
    args = [x, gain.reshape(1, d_model), w_in, head_gains, grp]
    if w_next is not None:
        w_stack, next_layer = w_next
        rows = d_model // (m_rows // tm)
        in_specs.append(pl.BlockSpec((None, rows, tn), lambda m, n: (next_layer, m, n)))
        args.append(w_stack)
        out_specs.append(pl.BlockSpec((rows, tn), lambda m, n: (m, n)))
        out_shape.append(jax.ShapeDtypeStruct((d_model, d_in), BF16))
    outs = pl.pallas_call(
        functools.partial(_inproj_kernel, tiles_per_seg=tiles_per_seg, emit_copy=emit_copy,
                          round_next=w_next is not None),
        grid=(m_rows // tm, d_in // tn),
        in_specs=in_specs,
        out_specs=out_specs,
        out_shape=out_shape,
        scratch_shapes=[pltpu.VMEM((tm, d_model), BF16)],
        compiler_params=_compiler_params(("arbitrary", "arbitrary")),
        name="inproj",
    )(*args)
    return outs if len(outs) > 1 else outs[0]


def _t5_bucket(dist):
    n_exact = N_BUCKETS // 2
    large = n_exact + (np.log(np.maximum(dist, 1) / n_exact) / np.log(MAX_DISTANCE / n_exact)
                       * (N_BUCKETS - n_exact)).astype(np.int32)
    large = np.minimum(large, N_BUCKETS - 1)
    return np.where(dist < n_exact, dist, large).astype(np.int32)


def _bias_lookup(rel_bias, buckets):
    onehot = buckets[..., None] == np.arange(rel_bias.shape[0])
    return jnp.sum(jnp.where(onehot[..., None], rel_bias, 0.0), axis=-2)


def _prompt_bias_tables(rel_bias):
    n_heads = rel_bias.shape[1]
    tables = []
    for window, dil in DILATED_CONFIGS:
        nw = window // dil
        assert nw == Q_BLOCK
        bias_k = _bias_lookup(rel_bias, _t5_bucket(np.arange(nw + 1) * dil))
        pad = jnp.full((Q_BLOCK - 1, n_heads), MASKED, F32)
        u = jnp.concatenate([pad, bias_k[::-1], pad, pad[:1]])
        row = 3 * Q_BLOCK - 1
        skew = jnp.tile(u, (Q_BLOCK, 1))[:Q_BLOCK * row].reshape(Q_BLOCK, row, n_heads)
        t = skew[:, Q_BLOCK - 1:3 * Q_BLOCK - 1].transpose(2, 0, 1)
        tables.append(t.reshape(n_heads // HEADS_PER_BLOCK, HEADS_PER_BLOCK * Q_BLOCK,
                                2 * Q_BLOCK))
    return jnp.stack(tables)


def _mixer_prompt_kernel(q_ref, k_ref, v_ref, vc_ref, gb_ref, gc_ref, cw_ref, bias_ref, *rest,
                         seq, n_aliased, n_round):
    w_refs = rest[:n_round]
    rest = rest[n_round + n_aliased:]
    oa_ref, oc_ref, cs_ref, okt_ref, ovt_ref = rest[:5]
    wcopy_refs = rest[5:5 + n_round]
    qs_scr, kd_scr, v0_scr, v1_scr, o_scr, m_scr, d_scr, u_scr = rest[5 + n_round:]

    for w_ref, wcopy_ref in zip(w_refs, wcopy_refs):
        wcopy_ref[...] = w_ref[...].astype(BF16)

    for src, dst in ((k_ref, okt_ref), (v_ref, ovt_ref)):
        t = src[0].T
        for h in range(HEADS_PER_BLOCK):
            dst[h] = t[h * HEAD_DIM:(h + 1) * HEAD_DIM]

    nblk_total = seq // Q_BLOCK
    lane = lax.broadcasted_iota(jnp.int32, (Q_BLOCK, LANES), 1)
    head0 = lane < HEAD_DIM

    for c, (window, dil) in enumerate(DILATED_CONFIGS):
        sub_len = seq // dil
        blocks_per_sub = sub_len // Q_BLOCK

        for r in range(dil):
            rows = pl.ds(r, sub_len, stride=dil) if dil > 1 else pl.ds(0, seq)
            head0_rows = lax.broadcasted_iota(jnp.int32, (sub_len, LANES), 1) < HEAD_DIM
            qv = (q_ref[0, rows, :] * ATTN_SCALE).astype(BF16)
            kv = k_ref[0, rows, :].astype(BF16)
            vv = v_ref[0, rows, :].astype(BF16)
            zero = jnp.zeros_like(qv)
            q0 = jnp.where(head0_rows, qv, zero)
            q1 = jnp.where(head0_rows, zero, qv)
            for gb in range(blocks_per_sub):
                g = r * blocks_per_sub + gb
                src = slice(gb * Q_BLOCK, (gb + 1) * Q_BLOCK)
                qs_scr[c, 2 * g * Q_BLOCK:(2 * g + 1) * Q_BLOCK, :] = q0[src]
                qs_scr[c, (2 * g + 1) * Q_BLOCK:(2 * g + 2) * Q_BLOCK, :] = q1[src]
            dst = slice(r * sub_len, (r + 1) * sub_len)
            kd_scr[c, dst, :] = kv
            v0_scr[c, dst, :] = jnp.where(head0_rows, vv, zero)
            v1_scr[c, dst, :] = jnp.where(head0_rows, zero, vv)

        for g in range(nblk_total):
            r, gb = divmod(g, blocks_per_sub)
            lhs = qs_scr[c, 2 * g * Q_BLOCK:(2 * g + 2) * Q_BLOCK, :]
            if gb == 0:
                win = slice(g * Q_BLOCK, (g + 1) * Q_BLOCK)
                bias = bias_ref[c, 0, :, Q_BLOCK:]
            else:
                win = slice((g - 1) * Q_BLOCK, (g + 1) * Q_BLOCK)
                bias = bias_ref[c, 0]
            logits = lax.dot_general(lhs, kd_scr[c, win, :], (((1,), (1,)), ((), ())),
                                     preferred_element_type=F32) + bias
            m = jnp.max(logits, axis=-1, keepdims=True)
            p = jnp.exp(logits - m)
            den = jnp.sum(p, axis=-1, keepdims=True)
            pb = p.astype(BF16)
            o = (jnp.dot(pb[0:Q_BLOCK], v0_scr[c, win, :], preferred_element_type=F32)
                 + jnp.dot(pb[Q_BLOCK:], v1_scr[c, win, :], preferred_element_type=F32))
            m2 = jnp.where(head0, m[0:Q_BLOCK], m[Q_BLOCK:])
            d2 = jnp.where(head0, den[0:Q_BLOCK], den[Q_BLOCK:])
            if dil > 1:
                out_rows = pl.ds(gb * Q_BLOCK * dil + r, Q_BLOCK, stride=dil)
            else:
                out_rows = pl.ds(g * Q_BLOCK, Q_BLOCK)
            o_scr[c, out_rows, :] = o
            m_scr[c, out_rows, :] = m2
            d_scr[c, out_rows, :] = d2

    n_br = len(DILATED_CONFIGS)

    def merge_rows(i, carry):
        rows = pl.ds(pl.multiple_of(i * MERGE_ROWS, MERGE_ROWS), MERGE_ROWS)
        mmax = m_scr[0, rows, :]
        for c in range(1, n_br):
            mmax = jnp.maximum(mmax, m_scr[c, rows, :])
        num = jnp.zeros((MERGE_ROWS, LANES), F32)
        den = jnp.zeros((MERGE_ROWS, LANES), F32)
        for c in range(n_br):
            w = jnp.exp(m_scr[c, rows, :] - mmax)
            num = num + w * o_scr[c, rows, :]
            den = den + w * d_scr[c, rows, :]
        oa_ref[0, rows, :] = num / den
        return carry

    lax.fori_loop(0, seq // MERGE_ROWS, merge_rows, 0)

    u = gc_ref[0] * vc_ref[0]
    u_scr[0:8, :] = jnp.zeros((8, LANES), F32)
    u_scr[8:8 + seq, :] = u
    y = cw_ref[0:1, :] * u_scr[pl.ds(8 - (CONV_WIDTH - 1), seq), :]
    for i in range(1, CONV_WIDTH):
        y = y + cw_ref[i:i + 1, :] * u_scr[pl.ds(8 - (CONV_WIDTH - 1) + i, seq), :]
    oc_ref[0] = gb_ref[0] * y
    cs_ref[0] = u_scr[pl.ds(8 + seq - (CONV_WIDTH - 1), CONV_WIDTH - 1), :]


def _mixer_prompt(layer, depth, proj, conv_w_l, bias_tables, new_windows, to_round):
    _, batch, seq, width = proj.shape
    n_pairs = width // LANES
    n_heads = width // HEAD_DIM
    n_br = len(DILATED_CONFIGS)
    tok = pl.BlockSpec((1, seq, LANES), lambda b, hp: (b, 0, hp))
    seg_specs = [pl.BlockSpec((None, 1, seq, LANES), lambda b, hp, s=s: (s, b, 0, hp))
                 for s in range(N_SEGMENTS)]
    win_spec = pl.BlockSpec((None, None, HEADS_PER_BLOCK, HEAD_DIM, seq),
                            lambda b, hp: (layer, b, hp, 0, 0))
    win_shape = jax.ShapeDtypeStruct((depth, batch, n_heads, HEAD_DIM, seq), F32)
    in_specs = seg_specs + [
        pl.BlockSpec((CONV_WIDTH, LANES), lambda b, hp: (0, hp)),
        pl.BlockSpec((n_br, 1, 2 * Q_BLOCK, 2 * Q_BLOCK), lambda b, hp: (0, hp, 0, 0))]
    args = [proj] * N_SEGMENTS + [conv_w_l, bias_tables]
    n_steps = batch * n_pairs
    copy_specs, copy_shapes = [], []
    for w, w_layer in to_round:
        _, w_rows, w_cols = w.shape
        rows = w_rows // n_steps
        in_specs.append(pl.BlockSpec((None, rows, w_cols),
                                     lambda b, hp, w_layer=w_layer: (w_layer, b * n_pairs + hp, 0)))
        args.append(w)
        copy_specs.append(pl.BlockSpec((rows, w_cols), lambda b, hp: (b * n_pairs + hp, 0)))
        copy_shapes.append(jax.ShapeDtypeStruct((w_rows, w_cols), BF16))
    aliases = {}
    if new_windows is not None:
        aliases = {len(args): 3, len(args) + 1: 4}
        in_specs = in_specs + [pl.BlockSpec(memory_space=pl.ANY)] * 2
        args = args + list(new_windows)
    oa, oc, cs, okt, ovt, *copies = pl.pallas_call(
        functools.partial(_mixer_prompt_kernel, seq=seq, n_aliased=len(aliases),
                          n_round=len(to_round)),
        grid=(batch, n_pairs),
        in_specs=in_specs,
        out_specs=[tok, tok,
                   pl.BlockSpec((1, CONV_WIDTH - 1, LANES), lambda b, hp: (b, 0, hp)),
                   win_spec, win_spec] + copy_specs,
        out_shape=[jax.ShapeDtypeStruct((batch, seq, width), F32),
                   jax.ShapeDtypeStruct((batch, seq, width), F32),
                   jax.ShapeDtypeStruct((batch, CONV_WIDTH - 1, width), F32),
                   win_shape, win_shape] + copy_shapes,
        input_output_aliases=aliases,
        scratch_shapes=[pltpu.VMEM((n_br, 2 * seq, LANES), BF16),
                        pltpu.VMEM((n_br, seq, LANES), BF16),
                        pltpu.VMEM((n_br, seq, LANES), BF16),
                        pltpu.VMEM((n_br, seq, LANES), BF16),
                        pltpu.VMEM((n_br, seq, LANES), F32),
                        pltpu.VMEM((n_br, seq, LANES), F32),
                        pltpu.VMEM((n_br, seq, LANES), F32),
                        pltpu.VMEM((seq + 8, LANES), F32)],
        compiler_params=_compiler_params(("arbitrary", "arbitrary")),
        name="mixer_prompt",
    )(*args)
    return oa, oc, cs, (okt, ovt), copies


SAMPLE_HEADS_PER_STEP = 8


def _sample_bias_tables(rel_bias, wb):
    dist = wb - np.arange(wb)
    tables = []
    for window, dil in DILATED_CONFIGS:
        in_branch = (dist % dil == 0) & (dist // dil <= window // dil)
        bias = jnp.where(in_branch[:, None], _bias_lookup(rel_bias, _t5_bucket(dist)), MASKED)
        tables.append(bias.T)
    own = rel_bias[_t5_bucket(np.zeros((1,), np.int64))][0]
    return jnp.stack(tables), own[:, None]


def _mixer_sample_kernel(q_ref, kn_ref, vn_ref, kt_ref, vt_ref, bias_ref, bias0_ref,
                         vc_ref, gb_ref, gc_ref, cw_ref, sc_ref, *rest):
    oa_ref, oc_ref, cs_ref, okt_ref, ovt_ref = rest[-5:]
    heads, head_dim, wb = kt_ref.shape
    newest = lax.broadcasted_iota(jnp.int32, (head_dim, wb), 1) == wb - 1
    n_br = len(DILATED_CONFIGS)

    s = jnp.concatenate([jnp.sum(kt_ref[h] * q_ref[h], axis=0, keepdims=True)
                         for h in range(heads)], axis=0) * ATTN_SCALE
    l_new = jnp.concatenate([jnp.sum(q_ref[h] * kn_ref[h], axis=0, keepdims=True)
                             for h in range(heads)], axis=0) * ATTN_SCALE + bias0_ref[...]
    branches = []
    for c in range(n_br):
        logits = s + bias_ref[c]
        m = jnp.maximum(jnp.max(logits, axis=-1, keepdims=True), l_new)
        p = jnp.exp(logits - m)
        p_new = jnp.exp(l_new - m)
        den = jnp.sum(p, axis=-1, keepdims=True) + p_new
        branches.append((p, p_new, m, den))
    mmax = branches[0][2]
    for br in branches[1:]:
        mmax = jnp.maximum(mmax, br[2])
    p_all = jnp.zeros_like(s)
    p_new_all = jnp.zeros_like(mmax)
    den_all = jnp.zeros_like(mmax)
    for p, p_new, m, den in branches:
        w = jnp.exp(m - mmax)
        p_all = p_all + w * p
        p_new_all = p_new_all + w * p_new
        den_all = den_all + w * den

    for h in range(heads):
        kt = kt_ref[h]
        vt = vt_ref[h]
        k_new, v_new = kn_ref[h], vn_ref[h]
        num = (jnp.sum(vt * p_all[h:h + 1, :], axis=-1, keepdims=True)
               + p_new_all[h:h + 1, :] * v_new)
        oa_ref[h] = num / den_all[h:h + 1, :]
        okt_ref[h] = jnp.where(newest, k_new, pltpu.roll(kt, wb - 1, axis=1))
        ovt_ref[h] = jnp.where(newest, v_new, pltpu.roll(vt, wb - 1, axis=1))

    @pl.when(pl.program_id(1) == 0)
    def _():
        u = gc_ref[0] * vc_ref[0]
        state = sc_ref[0]
        y = cw_ref[0:1, :] * state[0:1]
        for i in range(1, CONV_WIDTH - 1):
            y = y + cw_ref[i:i + 1, :] * state[i:i + 1]
        y = y + cw_ref[CONV_WIDTH - 1:CONV_WIDTH, :] * u
        oc_ref[0] = gb_ref[0] * y
        cs_ref[0] = jnp.concatenate([state[1:], u], axis=0)


def _mixer_sample(layer, proj, kt_state, vt_state, rolled, bias_s, bias0, conv_w_l, state_conv):
    depth, batch, n_heads, head_dim, wb = kt_state.shape
    width = proj.shape[-1]
    hb = SAMPLE_HEADS_PER_STEP
    n_br = len(DILATED_CONFIGS)
    qkv = proj[SEG_Q:SEG_V + 1].reshape(3, batch, n_heads, head_dim, 1)
    rows = proj.reshape(N_SEGMENTS, batch, 1, width)
    col_specs = [pl.BlockSpec((None, None, hb, head_dim, 1), lambda b, hc, s=s: (s, b, hc, 0, 0))
                 for s in range(3)]
    row_specs = [pl.BlockSpec((None, 1, 1, width), lambda b, hc, s=s: (s, b, 0, 0))
                 for s in (SEG_VC, SEG_B, SEG_C)]
    buf_spec = pl.BlockSpec((None, None, hb, head_dim, wb), lambda b, hc: (layer, b, hc, 0, 0))
    in_specs = col_specs + [
        buf_spec, buf_spec,
        pl.BlockSpec((n_br, hb, wb), lambda b, hc: (0, hc, 0)),
        pl.BlockSpec((hb, 1), lambda b, hc: (hc, 0))] + row_specs + [
        pl.BlockSpec((CONV_WIDTH, width), lambda b, hc: (0, 0)),
        pl.BlockSpec((1, CONV_WIDTH - 1, width), lambda b, hc: (layer * batch + b, 0, 0))]
    args = [qkv, qkv, qkv, kt_state, vt_state, bias_s, bias0, rows, rows, rows, conv_w_l,
            state_conv.reshape(depth * batch, CONV_WIDTH - 1, width)]
    aliases = {}
    if rolled is not None:
        aliases = {len(args): 3, len(args) + 1: 4}
        in_specs = in_specs + [pl.BlockSpec(memory_space=pl.ANY)] * 2
        args = args + list(rolled)
    oa, oc, cs, okt, ovt = pl.pallas_call(
        _mixer_sample_kernel,
        grid=(batch, n_heads // hb),
        in_specs=in_specs,
        out_specs=[pl.BlockSpec((None, hb, head_dim, 1), lambda b, hc: (b, hc, 0, 0)),
                   pl.BlockSpec((1, 1, width), lambda b, hc: (b, 0, 0)),
                   pl.BlockSpec((1, CONV_WIDTH - 1, width), lambda b, hc: (b, 0, 0)),
                   buf_spec, buf_spec],
        out_shape=[jax.ShapeDtypeStruct((batch, n_heads, head_dim, 1), F32),
                   jax.ShapeDtypeStruct((batch, 1, width), F32),
                   jax.ShapeDtypeStruct((batch, CONV_WIDTH - 1, width), F32),
                   jax.ShapeDtypeStruct(kt_state.shape, F32),
                   jax.ShapeDtypeStruct(vt_state.shape, F32)],
        input_output_aliases=aliases,
        compiler_params=_compiler_params(("arbitrary", "arbitrary")),
        name="mixer_sample",
    )(*args)
    return oa, oc, cs, (okt, ovt)


def _outproj_kernel(oa_ref, oc_ref, ga_ref, gc_ref, w_ref, x_ref, o_ref, *rest):
    z_scr = rest[-1]
    wcopy_ref = rest[0] if len(rest) == 2 else None
    n = pl.program_id(1)
    d_attn = oa_ref.shape[1]
    tn = o_ref.shape[1]

    @pl.when(n == 0)
    def _():
        z_scr[:, 0:d_attn] = _rms_rows(oa_ref[...], ga_ref[...]).astype(BF16)
        z_scr[:, d_attn:] = _rms_rows(oc_ref[...], gc_ref[...]).astype(BF16)

    if w_ref.shape[1] == tn:
        w = _bf16_weight(w_ref, wcopy_ref)
    else:
        w = w_ref[:, pl.ds(pl.multiple_of(n * tn, tn), tn)]
    o_ref[...] = x_ref[...] + jnp.dot(z_scr[...], w, preferred_element_type=F32)


def _outproj(oa, oc, attn_gain, conv_gain, w_out, layer, x, *, tm, tn):
    m_rows, d_attn = oa.shape
    d_conv = oc.shape[1]
    d_model = x.shape[1]
    d_mix = d_attn + d_conv
    emit_copy = w_out.ndim == 3
    const = lambda m, n: (0, 0)
    out_specs = [pl.BlockSpec((tm, tn), lambda m, n: (m, n))]
    out_shape = [jax.ShapeDtypeStruct((m_rows, d_model), F32)]
    if emit_copy:
        assert m_rows == tm, "the bf16 copy is written once per column tile"
        w_spec = _weight_spec(w_out, layer, (d_mix, tn), lambda m, n: (0, n))
        out_specs.append(pl.BlockSpec((d_mix, tn), lambda m, n: (0, n)))
        out_shape.append(jax.ShapeDtypeStruct((d_mix, d_model), BF16))
    else:
        w_spec = pl.BlockSpec((d_mix, d_model), const, pipeline_mode=pl.Buffered(1))
    outs = pl.pallas_call(
        _outproj_kernel,
        grid=(m_rows // tm, d_model // tn),
        in_specs=[pl.BlockSpec((tm, d_attn), lambda m, n: (m, 0)),
                  pl.BlockSpec((tm, d_conv), lambda m, n: (m, 0)),
                  pl.BlockSpec((1, d_attn), const),
                  pl.BlockSpec((1, d_conv), const),
                  w_spec,
                  pl.BlockSpec((tm, tn), lambda m, n: (m, n))],
        out_specs=out_specs,
        out_shape=out_shape,
        scratch_shapes=[pltpu.VMEM((tm, d_mix), BF16)],
        compiler_params=_compiler_params(("arbitrary", "arbitrary")),
        name="outproj",
    )(oa, oc, attn_gain.reshape(1, d_attn), conv_gain.reshape(1, d_conv), w_out, x)
    return outs if emit_copy else outs[0]


def _mlp_kernel(x_ref, g_ref, wu_ref, wd_ref, o_ref, *rest):
    h_scr = rest[-1]
    wu_copy_ref, wd_copy_ref = rest[:2] if len(rest) == 3 else (None, None)
    f = pl.program_id(1)

    @pl.when(f == 0)
    def _():
        x = x_ref[...]
        h_scr[...] = _rms_rows(x, g_ref[...]).astype(BF16)
        o_ref[...] = x

    a = jnp.dot(h_scr[...], _bf16_weight(wu_ref, wu_copy_ref), preferred_element_type=F32)
    a = jnp.square(jnp.maximum(a, 0.0)).astype(BF16)
    o_ref[...] += jnp.dot(a, _bf16_weight(wd_ref, wd_copy_ref), preferred_element_type=F32)


def _mlp(x, gain, w_up, w_down, layer, *, tm, tf):
    m_rows, d_model = x.shape
    d_ff = w_up.shape[-1]
    emit_copy = w_up.ndim == 3
    out_specs = [pl.BlockSpec((tm, d_model), lambda m, f: (m, 0))]
    out_shape = [jax.ShapeDtypeStruct((m_rows, d_model), F32)]
    if emit_copy:
        assert m_rows == tm, "the bf16 copies are written once per d_ff tile"
        out_specs += [pl.BlockSpec((d_model, tf), lambda m, f: (0, f)),
                      pl.BlockSpec((tf, d_model), lambda m, f: (f, 0))]
        out_shape += [jax.ShapeDtypeStruct((d_model, d_ff), BF16),
                      jax.ShapeDtypeStruct((d_ff, d_model), BF16)]
    outs = pl.pallas_call(
        _mlp_kernel,
        grid=(m_rows // tm, d_ff // tf),
        in_specs=[pl.BlockSpec((tm, d_model), lambda m, f: (m, 0), pipeline_mode=pl.Buffered(1)),
                  pl.BlockSpec((1, d_model), lambda m, f: (0, 0)),
                  _weight_spec(w_up, layer, (d_model, tf), lambda m, f: (0, f)),
                  _weight_spec(w_down, layer, (tf, d_model), lambda m, f: (f, 0))],
        out_specs=out_specs,
        out_shape=out_shape,
        scratch_shapes=[pltpu.VMEM((tm, d_model), BF16)],
        compiler_params=_compiler_params(("arbitrary", "arbitrary")),
        name="mlp",
    )(x, gain.reshape(1, d_model), w_up, w_down)
    return outs if emit_copy else outs[0]


PROMPT_TILES = dict(tm=1024, tn_in=1024, tn_out=1024, tf=1024)
SAMPLE_TILES = dict(tn=1024, tf=1024)


def kernel(x_prompt, x_sample, state_attn_k, state_attn_v, state_conv, rel_bias, norm_mix, w_in,
           q_norm, k_norm, conv_w, attn_out_norm, conv_out_norm, w_out, norm_mlp, w_up, w_down):
    batch, seq, d_model = x_prompt.shape
    dec_batch, dec_seq, _ = x_sample.shape
    depth, _, wb, n_heads, head_dim = state_attn_k.shape
    assert dec_seq == 1 and head_dim == HEAD_DIM and wb == max(w for w, _ in DILATED_CONFIGS)
    assert seq == wb

    bias_p = _prompt_bias_tables(rel_bias)
    bias_s, bias0 = _sample_bias_tables(rel_bias, wb)
    kt_state = state_attn_k.transpose(0, 1, 3, 4, 2)
    vt_state = state_attn_v.transpose(0, 1, 3, 4, 2)
    rolled = None
    new_windows = None

    xp = x_prompt.reshape(batch * seq, d_model)
    xs = x_sample.reshape(dec_batch, d_model)
    pt, st = PROMPT_TILES, SAMPLE_TILES
    cp_new, cs_new = [], []
    w_in_l = w_in
    for l in range(depth):
        proj_s = _inproj(xs, norm_mix[l], w_in_l, l, q_norm[l], k_norm[l],
                         tm=dec_batch, tn=st["tn"])
        if l == 0:
            proj_s, w_in_l = proj_s
        width = proj_s.shape[-1]
        oa_s, oc_s, cs, rolled = _mixer_sample(l, proj_s, kt_state, vt_state, rolled, bias_s, bias0,
                                               conv_w[l], state_conv)
        cs_new.append(cs)

        if l + 1 < depth:
            proj, w_in_l = _inproj(xp, norm_mix[l], w_in_l, l, q_norm[l], k_norm[l],
                                   tm=pt["tm"], tn=pt["tn_in"], w_next=(w_in, l + 1))
        else:
            proj = _inproj(xp, norm_mix[l], w_in_l, l, q_norm[l], k_norm[l],
                           tm=pt["tm"], tn=pt["tn_in"])
        oa, oc, cs, new_windows, (w_out_l, w_up_l, w_down_l) = _mixer_prompt(
            l, depth, proj.reshape(N_SEGMENTS, batch, seq, width), conv_w[l], bias_p, new_windows,
            [(w_out, l), (w_up, l), (w_down, l)])
        cp_new.append(cs)

        xs = _outproj(oa_s.reshape(dec_batch, width), oc_s.reshape(dec_batch, width),
                      attn_out_norm[l], conv_out_norm[l], w_out_l, l, xs,
                      tm=dec_batch, tn=st["tn"])
        xs = _mlp(xs, norm_mlp[l], w_up_l, w_down_l, l, tm=dec_batch, tf=st["tf"])
        xp = _outproj(oa.reshape(batch * seq, width), oc.reshape(batch * seq, width),
                      attn_out_norm[l], conv_out_norm[l], w_out_l, l, xp, tm=pt["tm"], tn=pt["tn_out"])
        xp = _mlp(xp, norm_mlp[l], w_up_l, w_down_l, l, tm=pt["tm"], tf=pt["tf"])

    ks_new, vs_new = (r.transpose(0, 1, 4, 2, 3) for r in rolled)
    kp_new, vp_new = (w.transpose(0, 1, 4, 2, 3) for w in new_windows)
    return (xp.reshape(batch, seq, d_model), xs.reshape(dec_batch, dec_seq, d_model),
            kp_new, vp_new, jnp.stack(cp_new), ks_new, vs_new, jnp.stack(cs_new))
```

```python
import functools

import jax
import jax.numpy as jnp
import numpy as np
from jax import lax
from jax.experimental import pallas as pl
from jax.experimental.pallas import tpu as pltpu

HEAD_DIM = 64
CONV_WIDTH = 3
DILATED_CONFIGS = ((128, 1), (512, 4), (2048, 16))
N_BUCKETS = 32
MAX_DISTANCE = 2048
EPS = 1e-6
ATTN_SCALE = HEAD_DIM ** -0.5
MASKED = -1e30

LANES = 128
MXU_WIDTH = 256
Q_BLOCK = 128
MERGE_ROWS = 256
HEADS_PER_BLOCK = LANES // HEAD_DIM
VMEM_LIMIT_BYTES = 56 * 1024 * 1024

F32 = jnp.float32
BF16 = jnp.bfloat16


def _compiler_params(semantics, flags=None):
    return pltpu.CompilerParams(dimension_semantics=semantics,
                                vmem_limit_bytes=VMEM_LIMIT_BYTES, flags=flags)


def _rms_rows(x, gain):
    ms = jnp.mean(x * x, axis=-1, keepdims=True)
    return (x * lax.rsqrt(ms + EPS)) * gain


N_SEGMENTS = 6
SEG_Q, SEG_K, SEG_V, SEG_VC, SEG_B, SEG_C = range(N_SEGMENTS)


def _bf16_weight(w_ref, copy_ref):
    w = w_ref[...]
    if w.dtype != BF16:
        w = w.astype(BF16)
    if copy_ref is not None:
        copy_ref[...] = w
    return w


def _inproj_kernel(x_ref, g_ref, w_ref, hg_ref, grp_ref, *rest, tiles_per_seg, emit_copy,
                   round_next):
    rest = list(rest)
    wnext_ref = rest.pop(0) if round_next else None
    o_ref = rest.pop(0)
    wcopy_ref = rest.pop(0) if emit_copy else None
    wnext_copy_ref = rest.pop(0) if round_next else None
    h_scr, = rest
    n = pl.program_id(1)

    if round_next:
        wnext_copy_ref[...] = wnext_ref[...].astype(BF16)

    @pl.when(n == 0)
    def _():
        h_scr[...] = _rms_rows(x_ref[...], g_ref[...]).astype(BF16)

    o_ref[0] = jnp.dot(h_scr[...], _bf16_weight(w_ref, wcopy_ref), preferred_element_type=F32)
    seg = n // tiles_per_seg

    @pl.when(seg <= SEG_K)
    def _():
        gain = hg_ref[pl.ds(seg, 1), :]
        chunk = grp_ref.shape[0]
        for j in range(o_ref.shape[2] // chunk):
            cols = slice(j * chunk, (j + 1) * chunk)
            yj = o_ref[0, :, cols]
            ms = jnp.dot((yj * yj).astype(BF16), grp_ref[...], preferred_element_type=F32)
            o_ref[0, :, cols] = (yj * lax.rsqrt(ms + EPS)) * gain[:, cols]


def _weight_spec(w, layer, block, index_map):
    if w.ndim == 3:
        return pl.BlockSpec((None,) + block, lambda *ids: (layer,) + index_map(*ids))
    return pl.BlockSpec(block, index_map)


def _inproj(x, gain, w_in, layer, q_gain, k_gain, *, tm, tn, w_next=None):
    m_rows, d_model = x.shape
    d_in = w_in.shape[-1]
    emit_copy = w_in.ndim == 3
    seg_w = d_in // N_SEGMENTS
    tiles_per_seg = seg_w // tn
    grp = jnp.asarray(
        (np.arange(MXU_WIDTH)[:, None] // HEAD_DIM == np.arange(MXU_WIDTH)[None, :] // HEAD_DIM)
        .astype(np.float32) / HEAD_DIM, BF16)
    head_gains = jnp.stack([jnp.tile(q_gain, tn // HEAD_DIM), jnp.tile(k_gain, tn // HEAD_DIM)])
    assert (SEG_Q, SEG_K) == (0, 1)
    const = lambda m, n: (0, 0)
    out_specs = [pl.BlockSpec(
        (1, tm, tn), lambda m, n: (n // tiles_per_seg, m, n % tiles_per_seg))]
    out_shape = [jax.ShapeDtypeStruct((N_SEGMENTS, m_rows, seg_w), F32)]
    if emit_copy:
        assert m_rows == tm, "the bf16 copy is written once per column tile"
        out_specs.append(pl.BlockSpec((d_model, tn), lambda m, n: (0, n)))
        out_shape.append(jax.ShapeDtypeStruct((d_model, d_in), BF16))
    in_specs = [pl.BlockSpec((tm, d_model), lambda m, n: (m, 0)),
                pl.BlockSpec((1, d_model), const),
                _weight_spec(w_in, layer, (d_model, tn), lambda m, n: (0, n)),
                pl.BlockSpec((2, tn), const),
                pl.BlockSpec((MXU_WIDTH, MXU_WIDTH), const)]
    args = [x, gain.reshape(1, d_model), w_in, head_gains, grp]
    if w_next is not None:
        w_stack, next_layer = w_next
        rows = d_model // (m_rows // tm)
        in_specs.append(pl.BlockSpec((None, rows, tn), lambda m, n: (next_layer, m, n)))
        args.append(w_stack)
        out_specs.append(pl.BlockSpec((rows, tn), lambda m, n: (m, n)))
        out_shape.append(jax.ShapeDtypeStruct((d_model, d_in), BF16))
    outs = pl.pallas_call(
        functools.partial(_inproj_kernel, tiles_per_seg=tiles_per_seg, emit_copy=emit_copy,
                          round_next=w_next is not None),
        grid=(m_rows // tm, d_in // tn),
        in_specs=in_specs,
        out_specs=out_specs,
        out_shape=out_shape,
        scratch_shapes=[pltpu.VMEM((tm, d_model), BF16)],
        compiler_params=_compiler_params(("arbitrary", "arbitrary")),
        name="inproj",
    )(*args)
    return outs if len(outs) > 1 else outs[0]


def _t5_bucket(dist):
    n_exact = N_BUCKETS // 2
    large = n_exact + (np.log(np.maximum(dist, 1) / n_exact) / np.log(MAX_DISTANCE / n_exact)
                       * (N_BUCKETS - n_exact)).astype(np.int32)
    large = np.minimum(large, N_BUCKETS - 1)
    return np.where(dist < n_exact, dist, large).astype(np.int32)


def _bias_lookup(rel_bias, buckets):
    onehot = buckets[..., None] == np.arange(rel_bias.shape[0])
    return jnp.sum(jnp.where(onehot[..., None], rel_bias, 0.0), axis=-2)


def _prompt_bias_tables(rel_bias):
    n_heads = rel_bias.shape[1]
    tables = []
    for window, dil in DILATED_CONFIGS:
        nw = window // dil
        assert nw == Q_BLOCK
        bias_k = _bias_lookup(rel_bias, _t5_bucket(np.arange(nw + 1) * dil))
        pad = jnp.full((Q_BLOCK - 1, n_heads), MASKED, F32)
        u = jnp.concatenate([pad, bias_k[::-1], pad, pad[:1]])
        row = 3 * Q_BLOCK - 1
        skew = jnp.tile(u, (Q_BLOCK, 1))[:Q_BLOCK * row].reshape(Q_BLOCK, row, n_heads)
        t = skew[:, Q_BLOCK - 1:3 * Q_BLOCK - 1].transpose(2, 0, 1)
        tables.append(t.reshape(n_heads // HEADS_PER_BLOCK, HEADS_PER_BLOCK * Q_BLOCK,
                                2 * Q_BLOCK))
    return jnp.stack(tables)


def _mixer_prompt_kernel(q_ref, k_ref, v_ref, vc_ref, gb_ref, gc_ref, cw_ref, bias_ref, *rest,
                         seq, n_aliased, n_round):
    w_refs = rest[:n_round]
    rest = rest[n_round + n_aliased:]
    oa_ref, oc_ref, cs_ref, okt_ref, ovt_ref = rest[:5]
    wcopy_refs = rest[5:5 + n_round]
    qs_scr, kd_scr, v0_scr, v1_scr, o_scr, m_scr, d_scr, u_scr = rest[5 + n_round:]

    for w_ref, wcopy_ref in zip(w_refs, wcopy_refs):
        wcopy_ref[...] = w_ref[...].astype(BF16)

    for src, dst in ((k_ref, okt_ref), (v_ref, ovt_ref)):
        t = src[0].T
        for h in range(HEADS_PER_BLOCK):
            dst[h] = t[h * HEAD_DIM:(h + 1) * HEAD_DIM]

    nblk_total = seq // Q_BLOCK
    lane = lax.broadcasted_iota(jnp.int32, (Q_BLOCK, LANES), 1)
    head0 = lane < HEAD_DIM

    for c, (window, dil) in enumerate(DILATED_CONFIGS):
        sub_len = seq // dil
        blocks_per_sub = sub_len // Q_BLOCK

        for r in range(dil):
            rows = pl.ds(r, sub_len, stride=dil) if dil > 1 else pl.ds(0, seq)
            head0_rows = lax.broadcasted_iota(jnp.int32, (sub_len, LANES), 1) < HEAD_DIM
            qv = (q_ref[0, rows, :] * ATTN_SCALE).astype(BF16)
            kv = k_ref[0, rows, :].astype(BF16)
            vv = v_ref[0, rows, :].astype(BF16)
            zero = jnp.zeros_like(qv)
            q0 = jnp.where(head0_rows, qv, zero)
            q1 = jnp.where(head0_rows, zero, qv)
            for gb in range(blocks_per_sub):
                g = r * blocks_per_sub + gb
                src = slice(gb * Q_BLOCK, (gb + 1) * Q_BLOCK)
                qs_scr[c, 2 * g * Q_BLOCK:(2 * g + 1) * Q_BLOCK, :] = q0[src]
                qs_scr[c, (2 * g + 1) * Q_BLOCK:(2 * g + 2) * Q_BLOCK, :] = q1[src]
            dst = slice(r * sub_len, (r + 1) * sub_len)
            kd_scr[c, dst, :] = kv
            v0_scr[c, dst, :] = jnp.where(head0_rows, vv, zero)
            v1_scr[c, dst, :] = jnp.where(head0_rows, zero, vv)

        for g in range(nblk_total):
            r, gb = divmod(g, blocks_per_sub)
            lhs = qs_scr[c, 2 * g * Q_BLOCK:(2 * g + 2) * Q_BLOCK, :]
            if gb == 0:
                win = slice(g * Q_BLOCK, (g + 1) * Q_BLOCK)
                bias = bias_ref[c, 0, :, Q_BLOCK:]
            else:
                win = slice((g - 1) * Q_BLOCK, (g + 1) * Q_BLOCK)
                bias = bias_ref[c, 0]
            logits = lax.dot_general(lhs, kd_scr[c, win, :], (((1,), (1,)), ((), ())),
                                     preferred_element_type=F32) + bias
            m = jnp.max(logits, axis=-1, keepdims=True)
            p = jnp.exp(logits - m)
            den = jnp.sum(p, axis=-1, keepdims=True)
            pb = p.astype(BF16)
            o = (jnp.dot(pb[0:Q_BLOCK], v0_scr[c, win, :], preferred_element_type=F32)
                 + jnp.dot(pb[Q_BLOCK:], v1_scr[c, win, :], preferred_element_type=F32))
            m2 = jnp.where(head0, m[0:Q_BLOCK], m[Q_BLOCK:])
            d2 = jnp.where(head0, den[0:Q_BLOCK], den[Q_BLOCK:])
            if dil > 1:
                out_rows = pl.ds(gb * Q_BLOCK * dil + r, Q_BLOCK, stride=dil)
            else:
                out_rows = pl.ds(g * Q_BLOCK, Q_BLOCK)
            o_scr[c, out_rows, :] = o
            m_scr[c, out_rows, :] = m2
            d_scr[c, out_rows, :] = d2

    n_br = len(DILATED_CONFIGS)

    def merge_rows(i, carry):
        rows = pl.ds(pl.multiple_of(i * MERGE_ROWS, MERGE_ROWS), MERGE_ROWS)
        mmax = m_scr[0, rows, :]
        for c in range(1, n_br):
            mmax = jnp.maximum(mmax, m_scr[c, rows, :])
        num = jnp.zeros((MERGE_ROWS, LANES), F32)
        den = jnp.zeros((MERGE_ROWS, LANES), F32)
        for c in range(n_br):
            w = jnp.exp(m_scr[c, rows, :] - mmax)
            num = num + w * o_scr[c, rows, :]
            den = den + w * d_scr[c, rows, :]
        oa_ref[0, rows, :] = num / den
        return carry

    lax.fori_loop(0, seq // MERGE_ROWS, merge_rows, 0)

    u = gc_ref[0] * vc_ref[0]
    u_scr[0:8, :] = jnp.zeros((8, LANES), F32)
    u_scr[8:8 + seq, :] = u
    y = cw_ref[0:1, :] * u_scr[pl.ds(8 - (CONV_WIDTH - 1), seq), :]
    for i in range(1, CONV_WIDTH):
        y = y + cw_ref[i:i + 1, :] * u_scr[pl.ds(8 - (CONV_WIDTH - 1) + i, seq), :]
    oc_ref[0] = gb_ref[0] * y
    cs_ref[0] = u_scr[pl.ds(8 + seq - (CONV_WIDTH - 1), CONV_WIDTH - 1), :]


def _mixer_prompt(layer, depth, proj, conv_w_l, bias_tables, new_windows, to_round):
    _, batch, seq, width = proj.shape
    n_pairs = width // LANES
    n_heads = width // HEAD_DIM
    n_br = len(DILATED_CONFIGS)
    tok = pl.BlockSpec((1, seq, LANES), lambda b, hp: (b, 0, hp))
    seg_specs = [pl.BlockSpec((None, 1, seq, LANES), lambda b, hp, s=s: (s, b, 0, hp))
                 for s in range(N_SEGMENTS)]
    win_spec = pl.BlockSpec((None, None, HEADS_PER_BLOCK, HEAD_DIM, seq),
                            lambda b, hp: (layer, b, hp, 0, 0))
    win_shape = jax.ShapeDtypeStruct((depth, batch, n_heads, HEAD_DIM, seq), F32)
    in_specs = seg_specs + [
        pl.BlockSpec((CONV_WIDTH, LANES), lambda b, hp: (0, hp)),
        pl.BlockSpec((n_br, 1, 2 * Q_BLOCK, 2 * Q_BLOCK), lambda b, hp: (0, hp, 0, 0))]
    args = [proj] * N_SEGMENTS + [conv_w_l, bias_tables]
    n_steps = batch * n_pairs
    copy_specs, copy_shapes = [], []
    for w, w_layer in to_round:
        _, w_rows, w_cols = w.shape
        rows = w_rows // n_steps
        in_specs.append(pl.BlockSpec((None, rows, w_cols),
                                     lambda b, hp, w_layer=w_layer: (w_layer, b * n_pairs + hp, 0)))
        args.append(w)
        copy_specs.append(pl.BlockSpec((rows, w_cols), lambda b, hp: (b * n_pairs + hp, 0)))
        copy_shapes.append(jax.ShapeDtypeStruct((w_rows, w_cols), BF16))
    aliases = {}
    if new_windows is not None:
        aliases = {len(args): 3, len(args) + 1: 4}
        in_specs = in_specs + [pl.BlockSpec(memory_space=pl.ANY)] * 2
        args = args + list(new_windows)
    oa, oc, cs, okt, ovt, *copies = pl.pallas_call(
        functools.partial(_mixer_prompt_kernel, seq=seq, n_aliased=len(aliases),
                          n_round=len(to_round)),
        grid=(batch, n_pairs),
        in_specs=in_specs,
        out_specs=[tok, tok,
                   pl.BlockSpec((1, CONV_WIDTH - 1, LANES), lambda b, hp: (b, 0, hp)),
                   win_spec, win_spec] + copy_specs,
        out_shape=[jax.ShapeDtypeStruct((batch, seq, width), F32),
                   jax.ShapeDtypeStruct((batch, seq, width), F32),
                   jax.ShapeDtypeStruct((batch, CONV_WIDTH - 1, width), F32),
                   win_shape, win_shape] + copy_shapes,
        input_output_aliases=aliases,
        scratch_shapes=[pltpu.VMEM((n_br, 2 * seq, LANES), BF16),
                        pltpu.VMEM((n_br, seq, LANES), BF16),
                        pltpu.VMEM((n_br, seq, LANES), BF16),
                        pltpu.VMEM((n_br, seq, LANES), BF16),
                        pltpu.VMEM((n_br, seq, LANES), F32),
                        pltpu.VMEM((n_br, seq, LANES), F32),
                        pltpu.VMEM((n_br, seq, LANES), F32),
                        pltpu.VMEM((seq + 8, LANES), F32)],
        compiler_params=_compiler_params(("arbitrary", "arbitrary")),
        name="mixer_prompt",
    )(*args)
    return oa, oc, cs, (okt, ovt), copies


SAMPLE_HEADS_PER_STEP = 8


def _sample_bias_tables(rel_bias, wb):
    dist = wb - np.arange(wb)
    tables = []
    for window, dil in DILATED_CONFIGS:
        in_branch = (dist % dil == 0) & (dist // dil <= window // dil)
        bias = jnp.where(in_branch[:, None], _bias_lookup(rel_bias, _t5_bucket(dist)), MASKED)
        tables.append(bias.T)
    own = rel_bias[_t5_bucket(np.zeros((1,), np.int64))][0]
    return jnp.stack(tables), own[:, None]


def _mixer_sample_kernel(q_ref, kn_ref, vn_ref, kt_ref, vt_ref, bias_ref, bias0_ref,
                         vc_ref, gb_ref, gc_ref, cw_ref, sc_ref, *rest):
    oa_ref, oc_ref, cs_ref, okt_ref, ovt_ref = rest[-5:]
    heads, head_dim, wb = kt_ref.shape
    newest = lax.broadcasted_iota(jnp.int32, (head_dim, wb), 1) == wb - 1
    n_br = len(DILATED_CONFIGS)

    s = jnp.concatenate([jnp.sum(kt_ref[h] * q_ref[h], axis=0, keepdims=True)
                         for h in range(heads)], axis=0) * ATTN_SCALE
    l_new = jnp.concatenate([jnp.sum(q_ref[h] * kn_ref[h], axis=0, keepdims=True)
                             for h in range(heads)], axis=0) * ATTN_SCALE + bias0_ref[...]
    branches = []
    for c in range(n_br):
        logits = s + bias_ref[c]
        m = jnp.maximum(jnp.max(logits, axis=-1, keepdims=True), l_new)
        p = jnp.exp(logits - m)
        p_new = jnp.exp(l_new - m)
        den = jnp.sum(p, axis=-1, keepdims=True) + p_new
        branches.append((p, p_new, m, den))
    mmax = branches[0][2]
    for br in branches[1:]:
        mmax = jnp.maximum(mmax, br[2])
    p_all = jnp.zeros_like(s)
    p_new_all = jnp.zeros_like(mmax)
    den_all = jnp.zeros_like(mmax)
    for p, p_new, m, den in branches:
        w = jnp.exp(m - mmax)
        p_all = p_all + w * p
        p_new_all = p_new_all + w * p_new
        den_all = den_all + w * den

    for h in range(heads):
        kt = kt_ref[h]
        vt = vt_ref[h]
        k_new, v_new = kn_ref[h], vn_ref[h]
        num = (jnp.sum(vt * p_all[h:h + 1, :], axis=-1, keepdims=True)
               + p_new_all[h:h + 1, :] * v_new)
        oa_ref[h] = num / den_all[h:h + 1, :]
        okt_ref[h] = jnp.where(newest, k_new, pltpu.roll(kt, wb - 1, axis=1))
        ovt_ref[h] = jnp.where(newest, v_new, pltpu.roll(vt, wb - 1, axis=1))

    @pl.when(pl.program_id(1) == 0)
    def _():
        u = gc_ref[0] * vc_ref[0]
        state = sc_ref[0]
        y = cw_ref[0:1, :] * state[0:1]
        for i in range(1, CONV_WIDTH - 1):
            y = y + cw_ref[i:i + 1, :] * state[i:i + 1]
        y = y + cw_ref[CONV_WIDTH - 1:CONV_WIDTH, :] * u
        oc_ref[0] = gb_ref[0] * y
        cs_ref[0] = jnp.concatenate([state[1:], u], axis=0)


def _mixer_sample(layer, proj, kt_state, vt_state, rolled, bias_s, bias0, conv_w_l, state_conv):
    depth, batch, n_heads, head_dim, wb = kt_state.shape
    width = proj.shape[-1]
    hb = SAMPLE_HEADS_PER_STEP
    n_br = len(DILATED_CONFIGS)
    qkv = proj[SEG_Q:SEG_V + 1].reshape(3, batch, n_heads, head_dim, 1)
    rows = proj.reshape(N_SEGMENTS, batch, 1, width)
    col_specs = [pl.BlockSpec((None, None, hb, head_dim, 1), lambda b, hc, s=s: (s, b, hc, 0, 0))
                 for s in range(3)]
    row_specs = [pl.BlockSpec((None, 1, 1, width), lambda b, hc, s=s: (s, b, 0, 0))
                 for s in (SEG_VC, SEG_B, SEG_C)]
    buf_spec = pl.BlockSpec((None, None, hb, head_dim, wb), lambda b, hc: (layer, b, hc, 0, 0))
    in_specs = col_specs + [
        buf_spec, buf_spec,
        pl.BlockSpec((n_br, hb, wb), lambda b, hc: (0, hc, 0)),
        pl.BlockSpec((hb, 1), lambda b, hc: (hc, 0))] + row_specs + [
        pl.BlockSpec((CONV_WIDTH, width), lambda b, hc: (0, 0)),
        pl.BlockSpec((1, CONV_WIDTH - 1, width), lambda b, hc: (layer * batch + b, 0, 0))]
    args = [qkv, qkv, qkv, kt_state, vt_state, bias_s, bias0, rows, rows, rows, conv_w_l,
            state_conv.reshape(depth * batch, CONV_WIDTH - 1, width)]
    aliases = {}
    if rolled is not None:
        aliases = {len(args): 3, len(args) + 1: 4}
        in_specs = in_specs + [pl.BlockSpec(memory_space=pl.ANY)] * 2
        args = args + list(rolled)
    oa, oc, cs, okt, ovt = pl.pallas_call(
        _mixer_sample_kernel,
        grid=(batch, n_heads // hb),
        in_specs=in_specs,
        out_specs=[pl.BlockSpec((None, hb, head_dim, 1), lambda b, hc: (b, hc, 0, 0)),
                   pl.BlockSpec((1, 1, width), lambda b, hc: (b, 0, 0)),
                   pl.BlockSpec((1, CONV_WIDTH - 1, width), lambda b, hc: (b, 0, 0)),
                   buf_spec, buf_spec],
        out_shape=[jax.ShapeDtypeStruct((batch, n_heads, head_dim, 1), F32),
                   jax.ShapeDtypeStruct((batch, 1, width), F32),
                   jax.ShapeDtypeStruct((batch, CONV_WIDTH - 1, width), F32),
                   jax.ShapeDtypeStruct(kt_state.shape, F32),
                   jax.ShapeDtypeStruct(vt_state.shape, F32)],
        input_output_aliases=aliases,
        compiler_params=_compiler_params(("arbitrary", "arbitrary")),
        name="mixer_sample",
    )(*args)
    return oa, oc, cs, (okt, ovt)


def _outproj_kernel(oa_ref, oc_ref, ga_ref, gc_ref, w_ref, x_ref, o_ref, *rest):
    z_scr = rest[-1]
    wcopy_ref = rest[0] if len(rest) == 2 else None
    n = pl.program_id(1)
    d_attn = oa_ref.shape[1]
    tn = o_ref.shape[1]

    @pl.when(n == 0)
    def _():
        z_scr[:, 0:d_attn] = _rms_rows(oa_ref[...], ga_ref[...]).astype(BF16)
        z_scr[:, d_attn:] = _rms_rows(oc_ref[...], gc_ref[...]).astype(BF16)

    if w_ref.shape[1] == tn:
        w = _bf16_weight(w_ref, wcopy_ref)
    else:
        w = w_ref[:, pl.ds(pl.multiple_of(n * tn, tn), tn)]
    o_ref[...] = x_ref[...] + jnp.dot(z_scr[...], w, preferred_element_type=F32)


def _outproj(oa, oc, attn_gain, conv_gain, w_out, layer, x, *, tm, tn):
    m_rows, d_attn = oa.shape
    d_conv = oc.shape[1]
    d_model = x.shape[1]
    d_mix = d_attn + d_conv
    emit_copy = w_out.ndim == 3
    const = lambda m, n: (0, 0)
    out_specs = [pl.BlockSpec((tm, tn), lambda m, n: (m, n))]
    out_shape = [jax.ShapeDtypeStruct((m_rows, d_model), F32)]
    if emit_copy:
        assert m_rows == tm, "the bf16 copy is written once per column tile"
        w_spec = _weight_spec(w_out, layer, (d_mix, tn), lambda m, n: (0, n))
        out_specs.append(pl.BlockSpec((d_mix, tn), lambda m, n: (0, n)))
        out_shape.append(jax.ShapeDtypeStruct((d_mix, d_model), BF16))
    else:
        w_spec = pl.BlockSpec((d_mix, d_model), const, pipeline_mode=pl.Buffered(1))
    outs = pl.pallas_call(
        _outproj_kernel,
        grid=(m_rows // tm, d_model // tn),
        in_specs=[pl.BlockSpec((tm, d_attn), lambda m, n: (m, 0)),
                  pl.BlockSpec((tm, d_conv), lambda m, n: (m, 0)),
                  pl.BlockSpec((1, d_attn), const),
                  pl.BlockSpec((1, d_conv), const),
                  w_spec,
                  pl.BlockSpec((tm, tn), lambda m, n: (m, n))],
        out_specs=out_specs,
        out_shape=out_shape,
        scratch_shapes=[pltpu.VMEM((tm, d_mix), BF16)],
        compiler_params=_compiler_params(("arbitrary", "arbitrary")),
        name="outproj",
    )(oa, oc, attn_gain.reshape(1, d_attn), conv_gain.reshape(1, d_conv), w_out, x)
    return outs if emit_copy else outs[0]


def _mlp_kernel(x_ref, g_ref, wu_ref, wd_ref, o_ref, *rest):
    h_scr = rest[-1]
    wu_copy_ref, wd_copy_ref = rest[:2] if len(rest) == 3 else (None, None)
    f = pl.program_id(1)

    @pl.when(f == 0)
    def _():
        x = x_ref[...]
        h_scr[...] = _rms_rows(x, g_ref[...]).astype(BF16)
        o_ref[...] = x

    a = jnp.dot(h_scr[...], _bf16_weight(wu_ref, wu_copy_ref), preferred_element_type=F32)
    a = jnp.square(jnp.maximum(a, 0.0)).astype(BF16)
    o_ref[...] += jnp.dot(a, _bf16_weight(wd_ref, wd_copy_ref), preferred_element_type=F32)


def _mlp(x, gain, w_up, w_down, layer, *, tm, tf):
    m_rows, d_model = x.shape
    d_ff = w_up.shape[-1]
    emit_copy = w_up.ndim == 3
    out_specs = [pl.BlockSpec((tm, d_model), lambda m, f: (m, 0))]
    out_shape = [jax.ShapeDtypeStruct((m_rows, d_model), F32)]
    if emit_copy:
        assert m_rows == tm, "the bf16 copies are written once per d_ff tile"
        out_specs += [pl.BlockSpec((d_model, tf), lambda m, f: (0, f)),
                      pl.BlockSpec((tf, d_model), lambda m, f: (f, 0))]
        out_shape += [jax.ShapeDtypeStruct((d_model, d_ff), BF16),
                      jax.ShapeDtypeStruct((d_ff, d_model), BF16)]
    outs = pl.pallas_call(
        _mlp_kernel,
        grid=(m_rows // tm, d_ff // tf),
        in_specs=[pl.BlockSpec((tm, d_model), lambda m, f: (m, 0), pipeline_mode=pl.Buffered(1)),
                  pl.BlockSpec((1, d_model), lambda m, f: (0, 0)),
                  _weight_spec(w_up, layer, (d_model, tf), lambda m, f: (0, f)),
                  _weight_spec(w_down, layer, (tf, d_model), lambda m, f: (f, 0))],
        out_specs=out_specs,
        out_shape=out_shape,
        scratch_shapes=[pltpu.VMEM((tm, d_model), BF16)],
        compiler_params=_compiler_params(("arbitrary", "arbitrary")),
        name="mlp",
    )(x, gain.reshape(1, d_model), w_up, w_down)
    return outs if emit_copy else outs[0]


PROMPT_TILES = dict(tm=1024, tn_in=1024, tn_out=1024, tf=1024)
SAMPLE_TILES = dict(tn=1024, tf=1024)


def kernel(x_prompt, x_sample, state_attn_k, state_attn_v, state_conv, rel_bias, norm_mix, w_in,
           q_norm, k_norm, conv_w, attn_out_norm, conv_out_norm, w_out, norm_mlp, w_up, w_down):
    batch, seq, d_model = x_prompt.shape
    dec_batch, dec_seq, _ = x_sample.shape
    depth, _, wb, n_heads, head_dim = state_attn_k.shape
    assert dec_seq == 1 and head_dim == HEAD_DIM and wb == max(w for w, _ in DILATED_CONFIGS)
    assert seq == wb

    bias_p = _prompt_bias_tables(rel_bias)
    bias_s, bias0 = _sample_bias_tables(rel_bias, wb)
    kt_state = state_attn_k.transpose(0, 1, 3, 4, 2)
    vt_state = state_attn_v.transpose(0, 1, 3, 4, 2)
    rolled = None
    new_windows = None

    xp = x_prompt.reshape(batch * seq, d_model)
    xs = x_sample.reshape(dec_batch, d_model)
    pt, st = PROMPT_TILES, SAMPLE_TILES
    cp_new, cs_new = [], []
    w_in_l = w_in
    for l in range(depth):
        proj_s = _inproj(xs, norm_mix[l], w_in_l, l, q_norm[l], k_norm[l],
                         tm=dec_batch, tn=st["tn"])
        if l == 0:
            proj_s, w_in_l = proj_s
        width = proj_s.shape[-1]
        oa_s, oc_s, cs, rolled = _mixer_sample(l, proj_s, kt_state, vt_state, rolled, bias_s, bias0,
                                               conv_w[l], state_conv)
        cs_new.append(cs)

        if l + 1 < depth:
            proj, w_in_l = _inproj(xp, norm_mix[l], w_in_l, l, q_norm[l], k_norm[l],
                                   tm=pt["tm"], tn=pt["tn_in"], w_next=(w_in, l + 1))
        else:
            proj = _inproj(xp, norm_mix[l], w_in_l, l, q_norm[l], k_norm[l],
                           tm=pt["tm"], tn=pt["tn_in"])
        oa, oc, cs, new_windows, (w_out_l, w_up_l, w_down_l) = _mixer_prompt(
            l, depth, proj.reshape(N_SEGMENTS, batch, seq, width), conv_w[l], bias_p, new_windows,
            [(w_out, l), (w_up, l), (w_down, l)])
        cp_new.append(cs)

        xs = _outproj(oa_s.reshape(dec_batch, width), oc_s.reshape(dec_batch, width),
                      attn_out_norm[l], conv_out_norm[l], w_out_l, l, xs,
                      tm=dec_batch, tn=st["tn"])
        xs = _mlp(xs, norm_mlp[l], w_up_l, w_down_l, l, tm=dec_batch, tf=st["tf"])
        xp = _outproj(oa.reshape(batch * seq, width), oc.reshape(batch * seq, width),
                      attn_out_norm[l], conv_out_norm[l], w_out_l, l, xp, tm=pt["tm"], tn=pt["tn_out"])
        xp = _mlp(xp, norm_mlp[l], w_up_l, w_down_l, l, tm=pt["tm"], tf=pt["tf"])

    ks_new, vs_new = (r.transpose(0, 1, 4, 2, 3) for r in rolled)
    kp_new, vp_new = (w.transpose(0, 1, 4, 2, 3) for w in new_windows)
    return (xp.reshape(batch, seq, d_model), xs.reshape(dec_batch, dec_seq, d_model),
            kp_new, vp_new, jnp.stack(cp_new), ks_new, vs_new, jnp.stack(cs_new))
```

```python
import functools

import jax
import jax.numpy as jnp
import numpy as np
from jax import lax
from jax.experimental import pallas as pl
from jax.experimental.pallas import tpu as pltpu

HEAD_DIM = 64
CONV_WIDTH = 3
DILATED_CONFIGS = ((128, 1), (512, 4), (2048, 16))
N_BUCKETS = 32
MAX_DISTANCE = 2048
EPS = 1e-6
ATTN_SCALE = HEAD_DIM ** -0.5
MASKED = -1e30

LANES = 128
MXU_WIDTH = 256
Q_BLOCK = 128
MERGE_ROWS = 256
HEADS_PER_BLOCK = LANES // HEAD_DIM
VMEM_LIMIT_BYTES = 56 * 1024 * 1024

F32 = jnp.float32
BF16 = jnp.bfloat16


def _compiler_params(semantics, flags=None):
    return pltpu.CompilerParams(dimension_semantics=semantics,
                                vmem_limit_bytes=VMEM_LIMIT_BYTES, flags=flags)


def _rms_rows(x, gain):
    ms = jnp.mean(x * x, axis=-1, keepdims=True)
    return (x * lax.rsqrt(ms + EPS)) * gain


N_SEGMENTS = 6
SEG_Q, SEG_K, SEG_V, SEG_VC, SEG_B, SEG_C = range(N_SEGMENTS)


def _bf16_weight(w_ref, copy_ref):
    w = w_ref[...]
    if w.dtype != BF16:
        w = w.astype(BF16)
    if copy_ref is not None:
        copy_ref[...] = w
    return w


def _inproj_kernel(x_ref, g_ref, w_ref, hg_ref, grp_ref, *rest, tiles_per_seg, emit_copy,
                   round_next):
    rest = list(rest)
    wnext_ref = rest.pop(0) if round_next else None
    o_ref = rest.pop(0)
    wcopy_ref = rest.pop(0) if emit_copy else None
    wnext_copy_ref = rest.pop(0) if round_next else None
    h_scr, = rest
    n = pl.program_id(1)

    if round_next:
        wnext_copy_ref[...] = wnext_ref[...].astype(BF16)

    @pl.when(n == 0)
    def _():
        h_scr[...] = _rms_rows(x_ref[...], g_ref[...]).astype(BF16)

    o_ref[0] = jnp.dot(h_scr[...], _bf16_weight(w_ref, wcopy_ref), preferred_element_type=F32)
    seg = n // tiles_per_seg

    @pl.when(seg <= SEG_K)
    def _():
        gain = hg_ref[pl.ds(seg, 1), :]
        chunk = grp_ref.shape[0]
        for j in range(o_ref.shape[2] // chunk):
            cols = slice(j * chunk, (j + 1) * chunk)
            yj = o_ref[0, :, cols]
            ms = jnp.dot((yj * yj).astype(BF16), grp_ref[...], preferred_element_type=F32)
            o_ref[0, :, cols] = (yj * lax.rsqrt(ms + EPS)) * gain[:, cols]


def _weight_spec(w, layer, block, index_map):
    if w.ndim == 3:
        return pl.BlockSpec((None,) + block, lambda *ids: (layer,) + index_map(*ids))
    return pl.BlockSpec(block, index_map)


def _inproj(x, gain, w_in, layer, q_gain, k_gain, *, tm, tn, w_next=None):
    m_rows, d_model = x.shape
    d_in = w_in.shape[-1]
    emit_copy = w_in.ndim == 3
    seg_w = d_in // N_SEGMENTS
    tiles_per_seg = seg_w // tn
    grp = jnp.asarray(
        (np.arange(MXU_WIDTH)[:, None] // HEAD_DIM == np.arange(MXU_WIDTH)[None, :] // HEAD_DIM)
        .astype(np.float32) / HEAD_DIM, BF16)
    head_gains = jnp.stack([jnp.tile(q_gain, tn // HEAD_DIM), jnp.tile(k_gain, tn // HEAD_DIM)])
    assert (SEG_Q, SEG_K) == (0, 1)
    const = lambda m, n: (0, 0)
    out_specs = [pl.BlockSpec(
        (1, tm, tn), lambda m, n: (n // tiles_per_seg, m, n % tiles_per_seg))]
    out_shape = [jax.ShapeDtypeStruct((N_SEGMENTS, m_rows, seg_w), F32)]
    if emit_copy:
        assert m_rows == tm, "the bf16 copy is written once per column tile"
        out_specs.append(pl.BlockSpec((d_model, tn), lambda m, n: (0, n)))
        out_shape.append(jax.ShapeDtypeStruct((d_model, d_in), BF16))
    in_specs = [pl.BlockSpec((tm, d_model), lambda m, n: (m, 0)),
                pl.BlockSpec((1, d_model), const),
                _weight_spec(w_in, layer, (d_model, tn), lambda m, n: (0, n)),
                pl.BlockSpec((2, tn), const),
                pl.BlockSpec((MXU_WIDTH, MXU_WIDTH), const)]
    args = [x, gain.reshape(1, d_model), w_in, head_gains, grp]
    if w_next is not None:
        w_stack, next_layer = w_next
        rows = d_model // (m_rows // tm)
        in_specs.append(pl.BlockSpec((None, rows, tn), lambda m, n: (next_layer, m, n)))
        args.append(w_stack)
        out_specs.append(pl.BlockSpec((rows, tn), lambda m, n: (m, n)))
        out_shape.append(jax.ShapeDtypeStruct((d_model, d_in), BF16))
    outs = pl.pallas_call(
        functools.partial(_inproj_kernel, tiles_per_seg=tiles_per_seg, emit_copy=emit_copy,
                          round_next=w_next is not None),
        grid=(m_rows // tm, d_in // tn),
        in_specs=in_specs,
        out_specs=out_specs,
        out_shape=out_shape,
        scratch_shapes=[pltpu.VMEM((tm, d_model), BF16)],
        compiler_params=_compiler_params(("arbitrary", "arbitrary")),
        name="inproj",
    )(*args)
    return outs if len(outs) > 1 else outs[0]


def _t5_bucket(dist):
    n_exact = N_BUCKETS // 2
    large = n_exact + (np.log(np.maximum(dist, 1) / n_exact) / np.log(MAX_DISTANCE / n_exact)
                       * (N_BUCKETS - n_exact)).astype(np.int32)
    large = np.minimum(large, N_BUCKETS - 1)
    return np.where(dist < n_exact, dist, large).astype(np.int32)


def _bias_lookup(rel_bias, buckets):
    onehot = buckets[..., None] == np.arange(rel_bias.shape[0])
    return jnp.sum(jnp.where(onehot[..., None], rel_bias, 0.0), axis=-2)


def _prompt_bias_tables(rel_bias):
    n_heads = rel_bias.shape[1]
    tables = []
    for window, dil in DILATED_CONFIGS:
        nw = window // dil
        assert nw == Q_BLOCK
        bias_k = _bias_lookup(rel_bias, _t5_bucket(np.arange(nw + 1) * dil))
        pad = jnp.full((Q_BLOCK - 1, n_heads), MASKED, F32)
        u = jnp.concatenate([pad, bias_k[::-1], pad, pad[:1]])
        row = 3 * Q_BLOCK - 1
        skew = jnp.tile(u, (Q_BLOCK, 1))[:Q_BLOCK * row].reshape(Q_BLOCK, row, n_heads)
        t = skew[:, Q_BLOCK - 1:3 * Q_BLOCK - 1].transpose(2, 0, 1)
        tables.append(t.reshape(n_heads // HEADS_PER_BLOCK, HEADS_PER_BLOCK * Q_BLOCK,
                                2 * Q_BLOCK))
    return jnp.stack(tables)


def _mixer_prompt_kernel(qkv_ref, conv_ref, cw_ref, bias_ref, *rest, seq, n_aliased, n_round):
    q_ref, k_ref, v_ref = (qkv_ref.at[i] for i in range(3))
    vc_ref, gb_ref, gc_ref = (conv_ref.at[i] for i in range(3))
    w_refs = rest[:n_round]
    rest = rest[n_round + n_aliased:]
    oa_ref, oc_ref, cs_ref, okt_ref, ovt_ref = rest[:5]
    wcopy_refs = rest[5:5 + n_round]
    qs_scr, kd_scr, v0_scr, v1_scr, o_scr, m_scr, d_scr, u_scr = rest[5 + n_round:]

    for w_ref, wcopy_ref in zip(w_refs, wcopy_refs):
        wcopy_ref[...] = w_ref[...].astype(BF16)

    for src, dst in ((k_ref, okt_ref), (v_ref, ovt_ref)):
        t = src[0].T
        for h in range(HEADS_PER_BLOCK):
            dst[h] = t[h * HEAD_DIM:(h + 1) * HEAD_DIM]

    nblk_total = seq // Q_BLOCK
    lane = lax.broadcasted_iota(jnp.int32, (Q_BLOCK, LANES), 1)
    head0 = lane < HEAD_DIM

    for c, (window, dil) in enumerate(DILATED_CONFIGS):
        sub_len = seq // dil
        blocks_per_sub = sub_len // Q_BLOCK

        for r in range(dil):
            rows = pl.ds(r, sub_len, stride=dil) if dil > 1 else pl.ds(0, seq)
            head0_rows = lax.broadcasted_iota(jnp.int32, (sub_len, LANES), 1) < HEAD_DIM
            qv = (q_ref[0, rows, :] * ATTN_SCALE).astype(BF16)
            kv = k_ref[0, rows, :].astype(BF16)
            vv = v_ref[0, rows, :].astype(BF16)
            zero = jnp.zeros_like(qv)
            q0 = jnp.where(head0_rows, qv, zero)
            q1 = jnp.where(head0_rows, zero, qv)
            for gb in range(blocks_per_sub):
                g = r * blocks_per_sub + gb
                src = slice(gb * Q_BLOCK, (gb + 1) * Q_BLOCK)
                qs_scr[c, 2 * g * Q_BLOCK:(2 * g + 1) * Q_BLOCK, :] = q0[src]
                qs_scr[c, (2 * g + 1) * Q_BLOCK:(2 * g + 2) * Q_BLOCK, :] = q1[src]
            dst = slice(r * sub_len, (r + 1) * sub_len)
            kd_scr[c, dst, :] = kv
            v0_scr[c, dst, :] = jnp.where(head0_rows, vv, zero)
            v1_scr[c, dst, :] = jnp.where(head0_rows, zero, vv)

        for g in range(nblk_total):
            r, gb = divmod(g, blocks_per_sub)
            lhs = qs_scr[c, 2 * g * Q_BLOCK:(2 * g + 2) * Q_BLOCK, :]
            if gb == 0:
                win = slice(g * Q_BLOCK, (g + 1) * Q_BLOCK)
                bias = bias_ref[c, 0, :, Q_BLOCK:]
            else:
                win = slice((g - 1) * Q_BLOCK, (g + 1) * Q_BLOCK)
                bias = bias_ref[c, 0]
            logits = lax.dot_general(lhs, kd_scr[c, win, :], (((1,), (1,)), ((), ())),
                                     preferred_element_type=F32) + bias
            m = jnp.max(logits, axis=-1, keepdims=True)
            p = jnp.exp(logits - m)
            den = jnp.sum(p, axis=-1, keepdims=True)
            pb = p.astype(BF16)
            o = (jnp.dot(pb[0:Q_BLOCK], v0_scr[c, win, :], preferred_element_type=F32)
                 + jnp.dot(pb[Q_BLOCK:], v1_scr[c, win, :], preferred_element_type=F32))
            m2 = jnp.where(head0, m[0:Q_BLOCK], m[Q_BLOCK:])
            d2 = jnp.where(head0, den[0:Q_BLOCK], den[Q_BLOCK:])
            if dil > 1:
                out_rows = pl.ds(gb * Q_BLOCK * dil + r, Q_BLOCK, stride=dil)
            else:
                out_rows = pl.ds(g * Q_BLOCK, Q_BLOCK)
            o_scr[c, out_rows, :] = o
            m_scr[c, out_rows, :] = m2
            d_scr[c, out_rows, :] = d2

    n_br = len(DILATED_CONFIGS)

    def merge_rows(i, carry):
        rows = pl.ds(pl.multiple_of(i * MERGE_ROWS, MERGE_ROWS), MERGE_ROWS)
        mmax = m_scr[0, rows, :]
        for c in range(1, n_br):
            mmax = jnp.maximum(mmax, m_scr[c, rows, :])
        num = jnp.zeros((MERGE_ROWS, LANES), F32)
        den = jnp.zeros((MERGE_ROWS, LANES), F32)
        for c in range(n_br):
            w = jnp.exp(m_scr[c, rows, :] - mmax)
            num = num + w * o_scr[c, rows, :]
            den = den + w * d_scr[c, rows, :]
        oa_ref[0, rows, :] = num / den
        return carry

    lax.fori_loop(0, seq // MERGE_ROWS, merge_rows, 0)

    u = gc_ref[0] * vc_ref[0]
    u_scr[0:8, :] = jnp.zeros((8, LANES), F32)
    u_scr[8:8 + seq, :] = u
    y = cw_ref[0:1, :] * u_scr[pl.ds(8 - (CONV_WIDTH - 1), seq), :]
    for i in range(1, CONV_WIDTH):
        y = y + cw_ref[i:i + 1, :] * u_scr[pl.ds(8 - (CONV_WIDTH - 1) + i, seq), :]
    oc_ref[0] = gb_ref[0] * y
    cs_ref[0] = u_scr[pl.ds(8 + seq - (CONV_WIDTH - 1), CONV_WIDTH - 1), :]


def _mixer_prompt(layer, depth, proj, conv_w_l, bias_tables, new_windows, to_round):
    _, batch, seq, width = proj.shape
    n_pairs = width // LANES
    n_heads = width // HEAD_DIM
    n_br = len(DILATED_CONFIGS)
    tok = pl.BlockSpec((1, seq, LANES), lambda b, hp: (b, 0, hp))
    assert (SEG_Q, SEG_K, SEG_V, SEG_VC, SEG_B, SEG_C) == tuple(range(6))
    seg_specs = [pl.BlockSpec((3, 1, seq, LANES), lambda b, hp, s=s: (s, b, 0, hp))
                 for s in range(N_SEGMENTS // 3)]
    win_spec = pl.BlockSpec((None, None, HEADS_PER_BLOCK, HEAD_DIM, seq),
                            lambda b, hp: (layer, b, hp, 0, 0))
    win_shape = jax.ShapeDtypeStruct((depth, batch, n_heads, HEAD_DIM, seq), F32)
    in_specs = seg_specs + [
        pl.BlockSpec((CONV_WIDTH, LANES), lambda b, hp: (0, hp)),
        pl.BlockSpec((n_br, 1, 2 * Q_BLOCK, 2 * Q_BLOCK), lambda b, hp: (0, hp, 0, 0))]
    args = [proj] * (N_SEGMENTS // 3) + [conv_w_l, bias_tables]
    n_steps = batch * n_pairs
    copy_specs, copy_shapes = [], []
    for w, w_layer in to_round:
        _, w_rows, w_cols = w.shape
        rows = w_rows // n_steps
        in_specs.append(pl.BlockSpec((None, rows, w_cols),
                                     lambda b, hp, w_layer=w_layer: (w_layer, b * n_pairs + hp, 0)))
        args.append(w)
        copy_specs.append(pl.BlockSpec((rows, w_cols), lambda b, hp: (b * n_pairs + hp, 0)))
        copy_shapes.append(jax.ShapeDtypeStruct((w_rows, w_cols), BF16))
    aliases = {}
    if new_windows is not None:
        aliases = {len(args): 3, len(args) + 1: 4}
        in_specs = in_specs + [pl.BlockSpec(memory_space=pl.ANY)] * 2
        args = args + list(new_windows)
    oa, oc, cs, okt, ovt, *copies = pl.pallas_call(
        functools.partial(_mixer_prompt_kernel, seq=seq, n_aliased=len(aliases),
                          n_round=len(to_round)),
        grid=(batch, n_pairs),
        in_specs=in_specs,
        out_specs=[tok, tok,
                   pl.BlockSpec((1, CONV_WIDTH - 1, LANES), lambda b, hp: (b, 0, hp)),
                   win_spec, win_spec] + copy_specs,
        out_shape=[jax.ShapeDtypeStruct((batch, seq, width), F32),
                   jax.ShapeDtypeStruct((batch, seq, width), F32),
                   jax.ShapeDtypeStruct((batch, CONV_WIDTH - 1, width), F32),
                   win_shape, win_shape] + copy_shapes,
        input_output_aliases=aliases,
        scratch_shapes=[pltpu.VMEM((n_br, 2 * seq, LANES), BF16),
                        pltpu.VMEM((n_br, seq, LANES), BF16),
                        pltpu.VMEM((n_br, seq, LANES), BF16),
                        pltpu.VMEM((n_br, seq, LANES), BF16),
                        pltpu.VMEM((n_br, seq, LANES), F32),
                        pltpu.VMEM((n_br, seq, LANES), F32),
                        pltpu.VMEM((n_br, seq, LANES), F32),
                        pltpu.VMEM((seq + 8, LANES), F32)],
        compiler_params=_compiler_params(("arbitrary", "arbitrary")),
        name="mixer_prompt",
    )(*args)
    return oa, oc, cs, (okt, ovt), copies


SAMPLE_HEADS_PER_STEP = 8


def _sample_bias_tables(rel_bias, wb):
    dist = wb - np.arange(wb)
    tables = []
    for window, dil in DILATED_CONFIGS:
        in_branch = (dist % dil == 0) & (dist // dil <= window // dil)
        bias = jnp.where(in_branch[:, None], _bias_lookup(rel_bias, _t5_bucket(dist)), MASKED)
        tables.append(bias.T)
    own = rel_bias[_t5_bucket(np.zeros((1,), np.int64))][0]
    return jnp.stack(tables), own[:, None]


def _mixer_sample_kernel(q_ref, kn_ref, vn_ref, kt_ref, vt_ref, bias_ref, bias0_ref,
                         vc_ref, gb_ref, gc_ref, cw_ref, sc_ref, *rest):
    oa_ref, oc_ref, cs_ref, okt_ref, ovt_ref = rest[-5:]
    heads, head_dim, wb = kt_ref.shape
    newest = lax.broadcasted_iota(jnp.int32, (head_dim, wb), 1) == wb - 1
    n_br = len(DILATED_CONFIGS)

    s = jnp.concatenate([jnp.sum(kt_ref[h] * q_ref[h], axis=0, keepdims=True)
                         for h in range(heads)], axis=0) * ATTN_SCALE
    l_new = jnp.concatenate([jnp.sum(q_ref[h] * kn_ref[h], axis=0, keepdims=True)
                             for h in range(heads)], axis=0) * ATTN_SCALE + bias0_ref[...]
    branches = []
    for c in range(n_br):
        logits = s + bias_ref[c]
        m = jnp.maximum(jnp.max(logits, axis=-1, keepdims=True), l_new)
        p = jnp.exp(logits - m)
        p_new = jnp.exp(l_new - m)
        den = jnp.sum(p, axis=-1, keepdims=True) + p_new
        branches.append((p, p_new, m, den))
    mmax = branches[0][2]
    for br in branches[1:]:
        mmax = jnp.maximum(mmax, br[2])
    p_all = jnp.zeros_like(s)
    p_new_all = jnp.zeros_like(mmax)
    den_all = jnp.zeros_like(mmax)
    for p, p_new, m, den in branches:
        w = jnp.exp(m - mmax)
        p_all = p_all + w * p
        p_new_all = p_new_all + w * p_new
        den_all = den_all + w * den

    for h in range(heads):
        kt = kt_ref[h]
        vt = vt_ref[h]
        k_new, v_new = kn_ref[h], vn_ref[h]
        num = (jnp.sum(vt * p_all[h:h + 1, :], axis=-1, keepdims=True)
               + p_new_all[h:h + 1, :] * v_new)
        oa_ref[h] = num / den_all[h:h + 1, :]
        okt_ref[h] = jnp.where(newest, k_new, pltpu.roll(kt, wb - 1, axis=1))
        ovt_ref[h] = jnp.where(newest, v_new, pltpu.roll(vt, wb - 1, axis=1))

    @pl.when(pl.program_id(1) == 0)
    def _():
        u = gc_ref[0] * vc_ref[0]
        state = sc_ref[0]
        y = cw_ref[0:1, :] * state[0:1]
        for i in range(1, CONV_WIDTH - 1):
            y = y + cw_ref[i:i + 1, :] * state[i:i + 1]
        y = y + cw_ref[CONV_WIDTH - 1:CONV_WIDTH, :] * u
        oc_ref[0] = gb_ref[0] * y
        cs_ref[0] = jnp.concatenate([state[1:], u], axis=0)


def _mixer_sample(layer, proj, kt_state, vt_state, rolled, bias_s, bias0, conv_w_l, state_conv):
    depth, batch, n_heads, head_dim, wb = kt_state.shape
    width = proj.shape[-1]
    hb = SAMPLE_HEADS_PER_STEP
    n_br = len(DILATED_CONFIGS)
    qkv = proj[SEG_Q:SEG_V + 1].reshape(3, batch, n_heads, head_dim, 1)
    rows = proj.reshape(N_SEGMENTS, batch, 1, width)
    col_specs = [pl.BlockSpec((None, None, hb, head_dim, 1), lambda b, hc, s=s: (s, b, hc, 0, 0))
                 for s in range(3)]
    row_specs = [pl.BlockSpec((None, 1, 1, width), lambda b, hc, s=s: (s, b, 0, 0))
                 for s in (SEG_VC, SEG_B, SEG_C)]
    buf_spec = pl.BlockSpec((None, None, hb, head_dim, wb), lambda b, hc: (layer, b, hc, 0, 0))
    in_specs = col_specs + [
        buf_spec, buf_spec,
        pl.BlockSpec((n_br, hb, wb), lambda b, hc: (0, hc, 0)),
        pl.BlockSpec((hb, 1), lambda b, hc: (hc, 0))] + row_specs + [
        pl.BlockSpec((CONV_WIDTH, width), lambda b, hc: (0, 0)),
        pl.BlockSpec((1, CONV_WIDTH - 1, width), lambda b, hc: (layer * batch + b, 0, 0))]
    args = [qkv, qkv, qkv, kt_state, vt_state, bias_s, bias0, rows, rows, rows, conv_w_l,
            state_conv.reshape(depth * batch, CONV_WIDTH - 1, width)]
    aliases = {}
    if rolled is not None:
        aliases = {len(args): 3, len(args) + 1: 4}
        in_specs = in_specs + [pl.BlockSpec(memory_space=pl.ANY)] * 2
        args = args + list(rolled)
    oa, oc, cs, okt, ovt = pl.pallas_call(
        _mixer_sample_kernel,
        grid=(batch, n_heads // hb),
        in_specs=in_specs,
        out_specs=[pl.BlockSpec((None, hb, head_dim, 1), lambda b, hc: (b, hc, 0, 0)),
                   pl.BlockSpec((1, 1, width), lambda b, hc: (b, 0, 0)),
                   pl.BlockSpec((1, CONV_WIDTH - 1, width), lambda b, hc: (b, 0, 0)),
                   buf_spec, buf_spec],
        out_shape=[jax.ShapeDtypeStruct((batch, n_heads, head_dim, 1), F32),
                   jax.ShapeDtypeStruct((batch, 1, width), F32),
                   jax.ShapeDtypeStruct((batch, CONV_WIDTH - 1, width), F32),
                   jax.ShapeDtypeStruct(kt_state.shape, F32),
                   jax.ShapeDtypeStruct(vt_state.shape, F32)],
        input_output_aliases=aliases,
        compiler_params=_compiler_params(("arbitrary", "arbitrary")),
        name="mixer_sample",
    )(*args)
    return oa, oc, cs, (okt, ovt)


def _outproj_kernel(oa_ref, oc_ref, ga_ref, gc_ref, w_ref, x_ref, o_ref, *rest):
    z_scr = rest[-1]
    wcopy_ref = rest[0] if len(rest) == 2 else None
    n = pl.program_id(1)
    d_attn = oa_ref.shape[1]
    tn = o_ref.shape[1]

    @pl.when(n == 0)
    def _():
        z_scr[:, 0:d_attn] = _rms_rows(oa_ref[...], ga_ref[...]).astype(BF16)
        z_scr[:, d_attn:] = _rms_rows(oc_ref[...], gc_ref[...]).astype(BF16)

    if w_ref.shape[1] == tn:
        w = _bf16_weight(w_ref, wcopy_ref)
    else:
        w = w_ref[:, pl.ds(pl.multiple_of(n * tn, tn), tn)]
    o_ref[...] = x_ref[...] + jnp.dot(z_scr[...], w, preferred_element_type=F32)


def _outproj(oa, oc, attn_gain, conv_gain, w_out, layer, x, *, tm, tn):
    m_rows, d_attn = oa.shape
    d_conv = oc.shape[1]
    d_model = x.shape[1]
    d_mix = d_attn + d_conv
    emit_copy = w_out.ndim == 3
    const = lambda m, n: (0, 0)
    out_specs = [pl.BlockSpec((tm, tn), lambda m, n: (m, n))]
    out_shape = [jax.ShapeDtypeStruct((m_rows, d_model), F32)]
    if emit_copy:
        assert m_rows == tm, "the bf16 copy is written once per column tile"
        w_spec = _weight_spec(w_out, layer, (d_mix, tn), lambda m, n: (0, n))
        out_specs.append(pl.BlockSpec((d_mix, tn), lambda m, n: (0, n)))
        out_shape.append(jax.ShapeDtypeStruct((d_mix, d_model), BF16))
    else:
        w_spec = pl.BlockSpec((d_mix, d_model), const, pipeline_mode=pl.Buffered(1))
    outs = pl.pallas_call(
        _outproj_kernel,
        grid=(m_rows // tm, d_model // tn),
        in_specs=[pl.BlockSpec((tm, d_attn), lambda m, n: (m, 0)),
                  pl.BlockSpec((tm, d_conv), lambda m, n: (m, 0)),
                  pl.BlockSpec((1, d_attn), const),
                  pl.BlockSpec((1, d_conv), const),
                  w_spec,
                  pl.BlockSpec((tm, tn), lambda m, n: (m, n))],
        out_specs=out_specs,
        out_shape=out_shape,
        scratch_shapes=[pltpu.VMEM((tm, d_mix), BF16)],
        compiler_params=_compiler_params(("arbitrary", "arbitrary")),
        name="outproj",
    )(oa, oc, attn_gain.reshape(1, d_attn), conv_gain.reshape(1, d_conv), w_out, x)
    return outs if emit_copy else outs[0]


def _mlp_kernel(x_ref, g_ref, wu_ref, wd_ref, o_ref, *rest):
    h_scr = rest[-1]
    wu_copy_ref, wd_copy_ref = rest[:2] if len(rest) == 3 else (None, None)
    f = pl.program_id(1)

    @pl.when(f == 0)
    def _():
        x = x_ref[...]
        h_scr[...] = _rms_rows(x, g_ref[...]).astype(BF16)
        o_ref[...] = x

    a = jnp.dot(h_scr[...], _bf16_weight(wu_ref, wu_copy_ref), preferred_element_type=F32)
    a = jnp.square(jnp.maximum(a, 0.0)).astype(BF16)
    o_ref[...] += jnp.dot(a, _bf16_weight(wd_ref, wd_copy_ref), preferred_element_type=F32)


def _mlp(x, gain, w_up, w_down, layer, *, tm, tf):
    m_rows, d_model = x.shape
    d_ff = w_up.shape[-1]
    emit_copy = w_up.ndim == 3
    out_specs = [pl.BlockSpec((tm, d_model), lambda m, f: (m, 0))]
    out_shape = [jax.ShapeDtypeStruct((m_rows, d_model), F32)]
    if emit_copy:
        assert m_rows == tm, "the bf16 copies are written once per d_ff tile"
        out_specs += [pl.BlockSpec((d_model, tf), lambda m, f: (0, f)),
                      pl.BlockSpec((tf, d_model), lambda m, f: (f, 0))]
        out_shape += [jax.ShapeDtypeStruct((d_model, d_ff), BF16),
                      jax.ShapeDtypeStruct((d_ff, d_model), BF16)]
    outs = pl.pallas_call(
        _mlp_kernel,
        grid=(m_rows // tm, d_ff // tf),
        in_specs=[pl.BlockSpec((tm, d_model), lambda m, f: (m, 0), pipeline_mode=pl.Buffered(1)),
                  pl.BlockSpec((1, d_model), lambda m, f: (0, 0)),
                  _weight_spec(w_up, layer, (d_model, tf), lambda m, f: (0, f)),
                  _weight_spec(w_down, layer, (tf, d_model), lambda m, f: (f, 0))],
        out_specs=out_specs,
        out_shape=out_shape,
        scratch_shapes=[pltpu.VMEM((tm, d_model), BF16)],
        compiler_params=_compiler_params(("arbitrary", "arbitrary")),
        name="mlp",
    )(x, gain.reshape(1, d_model), w_up, w_down)
    return outs if emit_copy else outs[0]


PROMPT_TILES = dict(tm=1024, tn_in=1024, tn_out=1024, tf=1024)
SAMPLE_TILES = dict(tn=1024, tf=1024)


def kernel(x_prompt, x_sample, state_attn_k, state_attn_v, state_conv, rel_bias, norm_mix, w_in,
           q_norm, k_norm, conv_w, attn_out_norm, conv_out_norm, w_out, norm_mlp, w_up, w_down):
    batch, seq, d_model = x_prompt.shape
    dec_batch, dec_seq, _ = x_sample.shape
    depth, _, wb, n_heads, head_dim = state_attn_k.shape
    assert dec_seq == 1 and head_dim == HEAD_DIM and wb == max(w for w, _ in DILATED_CONFIGS)
    assert seq == wb

    bias_p = _prompt_bias_tables(rel_bias)
    bias_s, bias0 = _sample_bias_tables(rel_bias, wb)
    kt_state = state_attn_k.transpose(0, 1, 3, 4, 2)
    vt_state = state_attn_v.transpose(0, 1, 3, 4, 2)
    rolled = None
    new_windows = None

    xp = x_prompt.reshape(batch * seq, d_model)
    xs = x_sample.reshape(dec_batch, d_model)
    pt, st = PROMPT_TILES, SAMPLE_TILES
    cp_new, cs_new = [], []
    w_in_l = w_in
    for l in range(depth):
        proj_s = _inproj(xs, norm_mix[l], w_in_l, l, q_norm[l], k_norm[l],
                         tm=dec_batch, tn=st["tn"])
        if l == 0:
            proj_s, w_in_l = proj_s
        width = proj_s.shape[-1]
        oa_s, oc_s, cs, rolled = _mixer_sample(l, proj_s, kt_state, vt_state, rolled, bias_s, bias0,
                                               conv_w[l], state_conv)
        cs_new.append(cs)

        if l + 1 < depth:
            proj, w_in_l = _inproj(xp, norm_mix[l], w_in_l, l, q_norm[l], k_norm[l],
                                   tm=pt["tm"], tn=pt["tn_in"], w_next=(w_in, l + 1))
        else:
            proj = _inproj(xp, norm_mix[l], w_in_l, l, q_norm[l], k_norm[l],
                           tm=pt["tm"], tn=pt["tn_in"])
        oa, oc, cs, new_windows, (w_out_l, w_up_l, w_down_l) = _mixer_prompt(
            l, depth, proj.reshape(N_SEGMENTS, batch, seq, width), conv_w[l], bias_p, new_windows,
            [(w_out, l), (w_up, l), (w_down, l)])
        cp_new.append(cs)

        xs = _outproj(oa_s.reshape(dec_batch, width), oc_s.reshape(dec_batch, width),
                      attn_out_norm[l], conv_out_norm[l], w_out_l, l, xs,
                      tm=dec_batch, tn=st["tn"])
        xs = _mlp(xs, norm_mlp[l], w_up_l, w_down_l, l, tm=dec_batch, tf=st["tf"])
        xp = _outproj(oa.reshape(batch * seq, width), oc.reshape(batch * seq, width),
                      attn_out_norm[l], conv_out_norm[l], w_out_l, l, xp, tm=pt["tm"], tn=pt["tn_out"])
        xp = _mlp(xp, norm_mlp[l], w_up_l, w_down_l, l, tm=pt["tm"], tf=pt["tf"])

    ks_new, vs_new = (r.transpose(0, 1, 4, 2, 3) for r in rolled)
    kp_new, vp_new = (w.transpose(0, 1, 4, 2, 3) for w in new_windows)
    return (xp.reshape(batch, seq, d_model), xs.reshape(dec_batch, dec_seq, d_model),
            kp_new, vp_new, jnp.stack(cp_new), ks_new, vs_new, jnp.stack(cs_new))
```

```python
import functools

import jax
import jax.numpy as jnp
import numpy as np
from jax import lax
from jax.experimental import pallas as pl
from jax.experimental.pallas import tpu as pltpu

HEAD_DIM = 64
CONV_WIDTH = 3
DILATED_CONFIGS = ((128, 1), (512, 4), (2048, 16))
N_BUCKETS = 32
MAX_DISTANCE = 2048
EPS = 1e-6
ATTN_SCALE = HEAD_DIM ** -0.5
MASKED = -1e30

LANES = 128
MXU_WIDTH = 256
Q_BLOCK = 128
MERGE_ROWS = 256
HEADS_PER_BLOCK = LANES // HEAD_DIM
VMEM_LIMIT_BYTES = 56 * 1024 * 1024

F32 = jnp.float32
BF16 = jnp.bfloat16


def _compiler_params(semantics, flags=None):
    return pltpu.CompilerParams(dimension_semantics=semantics,
                                vmem_limit_bytes=VMEM_LIMIT_BYTES, flags=flags)


def _rms_rows(x, gain):
    ms = jnp.mean(x * x, axis=-1, keepdims=True)
    return (x * lax.rsqrt(ms + EPS)) * gain


N_SEGMENTS = 6
SEG_Q, SEG_K, SEG_V, SEG_VC, SEG_B, SEG_C = range(N_SEGMENTS)


def _bf16_weight(w_ref, copy_ref):
    w = w_ref[...]
    if w.dtype != BF16:
        w = w.astype(BF16)
    if copy_ref is not None:
        copy_ref[...] = w
    return w


def _inproj_kernel(x_ref, g_ref, w_ref, hg_ref, grp_ref, *rest, tiles_per_seg, emit_copy,
                   round_next):
    rest = list(rest)
    wnext_ref = rest.pop(0) if round_next else None
    o_ref = rest.pop(0)
    wcopy_ref = rest.pop(0) if emit_copy else None
    wnext_copy_ref = rest.pop(0) if round_next else None
    h_scr, = rest
    n = pl.program_id(1)

    if round_next:
        wnext_copy_ref[...] = wnext_ref[...].astype(BF16)

    @pl.when(n == 0)
    def _():
        h_scr[...] = _rms_rows(x_ref[...], g_ref[...]).astype(BF16)

    o_ref[0] = jnp.dot(h_scr[...], _bf16_weight(w_ref, wcopy_ref), preferred_element_type=F32)
    seg = n // tiles_per_seg

    @pl.when(seg <= SEG_K)
    def _():
        gain = hg_ref[pl.ds(seg, 1), :]
        chunk = grp_ref.shape[0]
        for j in range(o_ref.shape[2] // chunk):
            cols = slice(j * chunk, (j + 1) * chunk)
            yj = o_ref[0, :, cols]
            ms = jnp.dot((yj * yj).astype(BF16), grp_ref[...], preferred_element_type=F32)
            o_ref[0, :, cols] = (yj * lax.rsqrt(ms + EPS)) * gain[:, cols]


def _weight_spec(w, layer, block, index_map):
    if w.ndim == 3:
        return pl.BlockSpec((None,) + block, lambda *ids: (layer,) + index_map(*ids))
    return pl.BlockSpec(block, index_map)


def _inproj(x, gain, w_in, layer, q_gain, k_gain, *, tm, tn, w_next=None):
    m_rows, d_model = x.shape
    d_in = w_in.shape[-1]
    emit_copy = w_in.ndim == 3
    seg_w = d_in // N_SEGMENTS
    tiles_per_seg = seg_w // tn
    grp = jnp.asarray(
        (np.arange(MXU_WIDTH)[:, None] // HEAD_DIM == np.arange(MXU_WIDTH)[None, :] // HEAD_DIM)
        .astype(np.float32) / HEAD_DIM, BF16)
    head_gains = jnp.stack([jnp.tile(q_gain, tn // HEAD_DIM), jnp.tile(k_gain, tn // HEAD_DIM)])
    assert (SEG_Q, SEG_K) == (0, 1)
    const = lambda m, n: (0, 0)
    out_specs = [pl.BlockSpec(
        (1, tm, tn), lambda m, n: (n // tiles_per_seg, m, n % tiles_per_seg))]
    out_shape = [jax.ShapeDtypeStruct((N_SEGMENTS, m_rows, seg_w), F32)]
    if emit_copy:
        assert m_rows == tm, "the bf16 copy is written once per column tile"
        out_specs.append(pl.BlockSpec((d_model, tn), lambda m, n: (0, n)))
        out_shape.append(jax.ShapeDtypeStruct((d_model, d_in), BF16))
    in_specs = [pl.BlockSpec((tm, d_model), lambda m, n: (m, 0)),
                pl.BlockSpec((1, d_model), const),
                _weight_spec(w_in, layer, (d_model, tn), lambda m, n: (0, n)),
                pl.BlockSpec((2, tn), const),
                pl.BlockSpec((MXU_WIDTH, MXU_WIDTH), const)]
    args = [x, gain.reshape(1, d_model), w_in, head_gains, grp]
    if w_next is not None:
        w_stack, next_layer = w_next
        rows = d_model // (m_rows // tm)
        in_specs.append(pl.BlockSpec((None, rows, tn), lambda m, n: (next_layer, m, n)))
        args.append(w_stack)
        out_specs.append(pl.BlockSpec((rows, tn), lambda m, n: (m, n)))
        out_shape.append(jax.ShapeDtypeStruct((d_model, d_in), BF16))
    outs = pl.pallas_call(
        functools.partial(_inproj_kernel, tiles_per_seg=tiles_per_seg, emit_copy=emit_copy,
                          round_next=w_next is not None),
        grid=(m_rows // tm, d_in // tn),
        in_specs=in_specs,
        out_specs=out_specs,
        out_shape=out_shape,
        scratch_shapes=[pltpu.VMEM((tm, d_model), BF16)],
        compiler_params=_compiler_params(("arbitrary", "arbitrary")),
        name="inproj",
    )(*args)
    return outs if len(outs) > 1 else outs[0]


def _t5_bucket(dist):
    n_exact = N_BUCKETS // 2
    large = n_exact + (np.log(np.maximum(dist, 1) / n_exact) / np.log(MAX_DISTANCE / n_exact)
                       * (N_BUCKETS - n_exact)).astype(np.int32)
    large = np.minimum(large, N_BUCKETS - 1)
    return np.where(dist < n_exact, dist, large).astype(np.int32)


def _bias_lookup(rel_bias, buckets):
    onehot = buckets[..., None] == np.arange(rel_bias.shape[0])
    return jnp.sum(jnp.where(onehot[..., None], rel_bias, 0.0), axis=-2)


def _prompt_bias_tables(rel_bias):
    n_heads = rel_bias.shape[1]
    tables = []
    for window, dil in DILATED_CONFIGS:
        nw = window // dil
        assert nw == Q_BLOCK
        bias_k = _bias_lookup(rel_bias, _t5_bucket(np.arange(nw + 1) * dil))
        pad = jnp.full((Q_BLOCK - 1, n_heads), MASKED, F32)
        u = jnp.concatenate([pad, bias_k[::-1], pad, pad[:1]])
        row = 3 * Q_BLOCK - 1
        skew = jnp.tile(u, (Q_BLOCK, 1))[:Q_BLOCK * row].reshape(Q_BLOCK, row, n_heads)
        t = skew[:, Q_BLOCK - 1:3 * Q_BLOCK - 1].transpose(2, 0, 1)
        tables.append(t.reshape(n_heads // HEADS_PER_BLOCK, HEADS_PER_BLOCK * Q_BLOCK,
                                2 * Q_BLOCK))
    return jnp.stack(tables)


def _mixer_prompt_kernel(qkv_ref, conv_ref, cw_ref, bias_ref, *rest, seq, n_aliased, n_round):
    q_ref, k_ref, v_ref = (qkv_ref.at[i] for i in range(3))
    vc_ref, gb_ref, gc_ref = (conv_ref.at[i] for i in range(3))
    w_refs = rest[:n_round]
    rest = rest[n_round + n_aliased:]
    oa_ref, oc_ref, cs_ref, okt_ref, ovt_ref = rest[:5]
    wcopy_refs = rest[5:5 + n_round]
    qs_scr, kd_scr, v0_scr, v1_scr, o_scr, m_scr, d_scr, u_scr = rest[5 + n_round:]

    for w_ref, wcopy_ref in zip(w_refs, wcopy_refs):
        wcopy_ref[...] = w_ref[...].astype(BF16)

    for src, dst in ((k_ref, okt_ref), (v_ref, ovt_ref)):
        t = src[0].T
        for h in range(HEADS_PER_BLOCK):
            dst[h] = t[h * HEAD_DIM:(h + 1) * HEAD_DIM]

    nblk_total = seq // Q_BLOCK
    lane = lax.broadcasted_iota(jnp.int32, (Q_BLOCK, LANES), 1)
    head0 = lane < HEAD_DIM

    for c, (window, dil) in enumerate(DILATED_CONFIGS):
        sub_len = seq // dil
        blocks_per_sub = sub_len // Q_BLOCK

        for r in range(dil):
            rows = pl.ds(r, sub_len, stride=dil) if dil > 1 else pl.ds(0, seq)
            head0_rows = lax.broadcasted_iota(jnp.int32, (sub_len, LANES), 1) < HEAD_DIM
            qv = (q_ref[0, rows, :] * ATTN_SCALE).astype(BF16)
            kv = k_ref[0, rows, :].astype(BF16)
            vv = v_ref[0, rows, :].astype(BF16)
            zero = jnp.zeros_like(qv)
            q0 = jnp.where(head0_rows, qv, zero)
            q1 = jnp.where(head0_rows, zero, qv)
            for gb in range(blocks_per_sub):
                g = r * blocks_per_sub + gb
                src = slice(gb * Q_BLOCK, (gb + 1) * Q_BLOCK)
                qs_scr[c, 2 * g * Q_BLOCK:(2 * g + 1) * Q_BLOCK, :] = q0[src]
                qs_scr[c, (2 * g + 1) * Q_BLOCK:(2 * g + 2) * Q_BLOCK, :] = q1[src]
            dst = slice(r * sub_len, (r + 1) * sub_len)
            kd_scr[c, dst, :] = kv
            v0_scr[c, dst, :] = jnp.where(head0_rows, vv, zero)
            v1_scr[c, dst, :] = jnp.where(head0_rows, zero, vv)

        for g in range(nblk_total):
            r, gb = divmod(g, blocks_per_sub)
            lhs = qs_scr[c, 2 * g * Q_BLOCK:(2 * g + 2) * Q_BLOCK, :]
            if gb == 0:
                win = slice(g * Q_BLOCK, (g + 1) * Q_BLOCK)
                bias = bias_ref[c, 0, :, Q_BLOCK:]
            else:
                win = slice((g - 1) * Q_BLOCK, (g + 1) * Q_BLOCK)
                bias = bias_ref[c, 0]
            logits = lax.dot_general(lhs, kd_scr[c, win, :], (((1,), (1,)), ((), ())),
                                     preferred_element_type=F32) + bias
            m = jnp.max(logits, axis=-1, keepdims=True)
            p = jnp.exp(logits - m)
            den = jnp.sum(p, axis=-1, keepdims=True)
            pb = p.astype(BF16)
            o = (jnp.dot(pb[0:Q_BLOCK], v0_scr[c, win, :], preferred_element_type=F32)
                 + jnp.dot(pb[Q_BLOCK:], v1_scr[c, win, :], preferred_element_type=F32))
            m2 = jnp.where(head0, m[0:Q_BLOCK], m[Q_BLOCK:])
            d2 = jnp.where(head0, den[0:Q_BLOCK], den[Q_BLOCK:])
            if dil > 1:
                out_rows = pl.ds(gb * Q_BLOCK * dil + r, Q_BLOCK, stride=dil)
            else:
                out_rows = pl.ds(g * Q_BLOCK, Q_BLOCK)
            o_scr[c, out_rows, :] = o
            m_scr[c, out_rows, :] = m2
            d_scr[c, out_rows, :] = d2

    n_br = len(DILATED_CONFIGS)

    def merge_rows(i, carry):
        rows = pl.ds(pl.multiple_of(i * MERGE_ROWS, MERGE_ROWS), MERGE_ROWS)
        mmax = m_scr[0, rows, :]
        for c in range(1, n_br):
            mmax = jnp.maximum(mmax, m_scr[c, rows, :])
        num = jnp.zeros((MERGE_ROWS, LANES), F32)
        den = jnp.zeros((MERGE_ROWS, LANES), F32)
        for c in range(n_br):
            w = jnp.exp(m_scr[c, rows, :] - mmax)
            num = num + w * o_scr[c, rows, :]
            den = den + w * d_scr[c, rows, :]
        oa_ref[0, rows, :] = num / den
        return carry

    lax.fori_loop(0, seq // MERGE_ROWS, merge_rows, 0)

    u = gc_ref[0] * vc_ref[0]
    u_scr[0:8, :] = jnp.zeros((8, LANES), F32)
    u_scr[8:8 + seq, :] = u
    y = cw_ref[0:1, :] * u_scr[pl.ds(8 - (CONV_WIDTH - 1), seq), :]
    for i in range(1, CONV_WIDTH):
        y = y + cw_ref[i:i + 1, :] * u_scr[pl.ds(8 - (CONV_WIDTH - 1) + i, seq), :]
    oc_ref[0] = gb_ref[0] * y
    cs_ref[0] = u_scr[pl.ds(8 + seq - (CONV_WIDTH - 1), CONV_WIDTH - 1), :]


def _mixer_prompt(layer, depth, proj, conv_w_l, bias_tables, new_windows, to_round):
    _, batch, seq, width = proj.shape
    n_pairs = width // LANES
    n_heads = width // HEAD_DIM
    n_br = len(DILATED_CONFIGS)
    tok = pl.BlockSpec((1, seq, LANES), lambda b, hp: (b, 0, hp))
    assert (SEG_Q, SEG_K, SEG_V, SEG_VC, SEG_B, SEG_C) == tuple(range(6))
    seg_specs = [pl.BlockSpec((3, 1, seq, LANES), lambda b, hp, s=s: (s, b, 0, hp))
                 for s in range(N_SEGMENTS // 3)]
    win_spec = pl.BlockSpec((None, None, HEADS_PER_BLOCK, HEAD_DIM, seq),
                            lambda b, hp: (layer, b, hp, 0, 0))
    win_shape = jax.ShapeDtypeStruct((depth, batch, n_heads, HEAD_DIM, seq), F32)
    in_specs = seg_specs + [
        pl.BlockSpec((CONV_WIDTH, LANES), lambda b, hp: (0, hp)),
        pl.BlockSpec((n_br, 1, 2 * Q_BLOCK, 2 * Q_BLOCK), lambda b, hp: (0, hp, 0, 0))]
    args = [proj] * (N_SEGMENTS // 3) + [conv_w_l, bias_tables]
    n_steps = batch * n_pairs
    copy_specs, copy_shapes = [], []
    for w, w_layer in to_round:
        _, w_rows, w_cols = w.shape
        rows = w_rows // n_steps
        in_specs.append(pl.BlockSpec((None, rows, w_cols),
                                     lambda b, hp, w_layer=w_layer: (w_layer, b * n_pairs + hp, 0)))
        args.append(w)
        copy_specs.append(pl.BlockSpec((rows, w_cols), lambda b, hp: (b * n_pairs + hp, 0)))
        copy_shapes.append(jax.ShapeDtypeStruct((w_rows, w_cols), BF16))
    aliases = {}
    if new_windows is not None:
        aliases = {len(args): 3, len(args) + 1: 4}
        in_specs = in_specs + [pl.BlockSpec(memory_space=pl.ANY)] * 2
        args = args + list(new_windows)
    oa, oc, cs, okt, ovt, *copies = pl.pallas_call(
        functools.partial(_mixer_prompt_kernel, seq=seq, n_aliased=len(aliases),
                          n_round=len(to_round)),
        grid=(batch, n_pairs),
        in_specs=in_specs,
        out_specs=[tok, tok,
                   pl.BlockSpec((1, CONV_WIDTH - 1, LANES), lambda b, hp: (b, 0, hp)),
                   win_spec, win_spec] + copy_specs,
        out_shape=[jax.ShapeDtypeStruct((batch, seq, width), F32),
                   jax.ShapeDtypeStruct((batch, seq, width), F32),
                   jax.ShapeDtypeStruct((batch, CONV_WIDTH - 1, width), F32),
                   win_shape, win_shape] + copy_shapes,
        input_output_aliases=aliases,
        scratch_shapes=[pltpu.VMEM((n_br, 2 * seq, LANES), BF16),
                        pltpu.VMEM((n_br, seq, LANES), BF16),
                        pltpu.VMEM((n_br, seq, LANES), BF16),
                        pltpu.VMEM((n_br, seq, LANES), BF16),
                        pltpu.VMEM((n_br, seq, LANES), F32),
                        pltpu.VMEM((n_br, seq, LANES), F32),
                        pltpu.VMEM((n_br, seq, LANES), F32),
                        pltpu.VMEM((seq + 8, LANES), F32)],
        compiler_params=_compiler_params(("arbitrary", "arbitrary")),
        name="mixer_prompt",
    )(*args)
    return oa, oc, cs, (okt, ovt), copies


SAMPLE_HEADS_PER_STEP = 8


def _sample_bias_tables(rel_bias, wb):
    dist = wb - np.arange(wb)
    tables = []
    for window, dil in DILATED_CONFIGS:
        in_branch = (dist % dil == 0) & (dist // dil <= window // dil)
        bias = jnp.where(in_branch[:, None], _bias_lookup(rel_bias, _t5_bucket(dist)), MASKED)
        tables.append(bias.T)
    own = rel_bias[_t5_bucket(np.zeros((1,), np.int64))][0]
    return jnp.stack(tables), own[:, None]


def _mixer_sample_kernel(q_ref, kn_ref, vn_ref, kt_ref, vt_ref, bias_ref, bias0_ref,
                         vc_ref, gb_ref, gc_ref, cw_ref, sc_ref, *rest):
    oa_ref, oc_ref, cs_ref, okt_ref, ovt_ref = rest[-5:]
    heads, head_dim, wb = kt_ref.shape
    newest = lax.broadcasted_iota(jnp.int32, (head_dim, wb), 1) == wb - 1
    n_br = len(DILATED_CONFIGS)

    s = jnp.concatenate([jnp.sum(kt_ref[h] * q_ref[h], axis=0, keepdims=True)
                         for h in range(heads)], axis=0) * ATTN_SCALE
    l_new = jnp.concatenate([jnp.sum(q_ref[h] * kn_ref[h], axis=0, keepdims=True)
                             for h in range(heads)], axis=0) * ATTN_SCALE + bias0_ref[...]
    branches = []
    for c in range(n_br):
        logits = s + bias_ref[c]
        m = jnp.maximum(jnp.max(logits, axis=-1, keepdims=True), l_new)
        p = jnp.exp(logits - m)
        p_new = jnp.exp(l_new - m)
        den = jnp.sum(p, axis=-1, keepdims=True) + p_new
        branches.append((p, p_new, m, den))
    mmax = branches[0][2]
    for br in branches[1:]:
        mmax = jnp.maximum(mmax, br[2])
    p_all = jnp.zeros_like(s)
    p_new_all = jnp.zeros_like(mmax)
    den_all = jnp.zeros_like(mmax)
    for p, p_new, m, den in branches:
        w = jnp.exp(m - mmax)
        p_all = p_all + w * p
        p_new_all = p_new_all + w * p_new
        den_all = den_all + w * den

    for h in range(heads):
        kt = kt_ref[h]
        vt = vt_ref[h]
        k_new, v_new = kn_ref[h], vn_ref[h]
        num = (jnp.sum(vt * p_all[h:h + 1, :], axis=-1, keepdims=True)
               + p_new_all[h:h + 1, :] * v_new)
        oa_ref[h] = num / den_all[h:h + 1, :]
        okt_ref[h] = jnp.where(newest, k_new, pltpu.roll(kt, wb - 1, axis=1))
        ovt_ref[h] = jnp.where(newest, v_new, pltpu.roll(vt, wb - 1, axis=1))

    @pl.when(pl.program_id(1) == 0)
    def _():
        u = gc_ref[0] * vc_ref[0]
        state = sc_ref[0]
        y = cw_ref[0:1, :] * state[0:1]
        for i in range(1, CONV_WIDTH - 1):
            y = y + cw_ref[i:i + 1, :] * state[i:i + 1]
        y = y + cw_ref[CONV_WIDTH - 1:CONV_WIDTH, :] * u
        oc_ref[0] = gb_ref[0] * y
        cs_ref[0] = jnp.concatenate([state[1:], u], axis=0)


def _mixer_sample(layer, proj, kt_state, vt_state, rolled, bias_s, bias0, conv_w_l, state_conv):
    depth, batch, n_heads, head_dim, wb = kt_state.shape
    width = proj.shape[-1]
    hb = SAMPLE_HEADS_PER_STEP
    n_br = len(DILATED_CONFIGS)
    qkv = proj[SEG_Q:SEG_V + 1].reshape(3, batch, n_heads, head_dim, 1)
    rows = proj.reshape(N_SEGMENTS, batch, 1, width)
    col_specs = [pl.BlockSpec((None, None, hb, head_dim, 1), lambda b, hc, s=s: (s, b, hc, 0, 0))
                 for s in range(3)]
    row_specs = [pl.BlockSpec((None, 1, 1, width), lambda b, hc, s=s: (s, b, 0, 0))
                 for s in (SEG_VC, SEG_B, SEG_C)]
    buf_spec = pl.BlockSpec((None, None, hb, head_dim, wb), lambda b, hc: (layer, b, hc, 0, 0))
    in_specs = col_specs + [
        buf_spec, buf_spec,
        pl.BlockSpec((n_br, hb, wb), lambda b, hc: (0, hc, 0)),
        pl.BlockSpec((hb, 1), lambda b, hc: (hc, 0))] + row_specs + [
        pl.BlockSpec((CONV_WIDTH, width), lambda b, hc: (0, 0)),
        pl.BlockSpec((1, CONV_WIDTH - 1, width), lambda b, hc: (layer * batch + b, 0, 0))]
    args = [qkv, qkv, qkv, kt_state, vt_state, bias_s, bias0, rows, rows, rows, conv_w_l,
            state_conv.reshape(depth * batch, CONV_WIDTH - 1, width)]
    aliases = {}
    if rolled is not None:
        aliases = {len(args): 3, len(args) + 1: 4}
        in_specs = in_specs + [pl.BlockSpec(memory_space=pl.ANY)] * 2
        args = args + list(rolled)
    oa, oc, cs, okt, ovt = pl.pallas_call(
        _mixer_sample_kernel,
        grid=(batch, n_heads // hb),
        in_specs=in_specs,
        out_specs=[pl.BlockSpec((None, hb, head_dim, 1), lambda b, hc: (b, hc, 0, 0)),
                   pl.BlockSpec((1, 1, width), lambda b, hc: (b, 0, 0)),
                   pl.BlockSpec((1, CONV_WIDTH - 1, width), lambda b, hc: (b, 0, 0)),
                   buf_spec, buf_spec],
        out_shape=[jax.ShapeDtypeStruct((batch, n_heads, head_dim, 1), F32),
                   jax.ShapeDtypeStruct((batch, 1, width), F32),
                   jax.ShapeDtypeStruct((batch, CONV_WIDTH - 1, width), F32),
                   jax.ShapeDtypeStruct(kt_state.shape, F32),
                   jax.ShapeDtypeStruct(vt_state.shape, F32)],
        input_output_aliases=aliases,
        compiler_params=_compiler_params(("arbitrary", "arbitrary")),
        name="mixer_sample",
    )(*args)
    return oa, oc, cs, (okt, ovt)


def _outproj_kernel(oa_ref, oc_ref, ga_ref, gc_ref, w_ref, x_ref, o_ref, *rest):
    z_scr = rest[-1]
    wcopy_ref = rest[0] if len(rest) == 2 else None
    n = pl.program_id(1)
    d_attn = oa_ref.shape[1]
    tn = o_ref.shape[1]

    @pl.when(n == 0)
    def _():
        z_scr[:, 0:d_attn] = _rms_rows(oa_ref[...], ga_ref[...]).astype(BF16)
        z_scr[:, d_attn:] = _rms_rows(oc_ref[...], gc_ref[...]).astype(BF16)

    if w_ref.shape[1] == tn:
        w = _bf16_weight(w_ref, wcopy_ref)
    else:
        w = w_ref[:, pl.ds(pl.multiple_of(n * tn, tn), tn)]
    o_ref[...] = x_ref[...] + jnp.dot(z_scr[...], w, preferred_element_type=F32)


def _outproj(oa, oc, attn_gain, conv_gain, w_out, layer, x, *, tm, tn):
    m_rows, d_attn = oa.shape
    d_conv = oc.shape[1]
    d_model = x.shape[1]
    d_mix = d_attn + d_conv
    emit_copy = w_out.ndim == 3
    const = lambda m, n: (0, 0)
    out_specs = [pl.BlockSpec((tm, tn), lambda m, n: (m, n))]
    out_shape = [jax.ShapeDtypeStruct((m_rows, d_model), F32)]
    if emit_copy:
        assert m_rows == tm, "the bf16 copy is written once per column tile"
        w_spec = _weight_spec(w_out, layer, (d_mix, tn), lambda m, n: (0, n))
        out_specs.append(pl.BlockSpec((d_mix, tn), lambda m, n: (0, n)))
        out_shape.append(jax.ShapeDtypeStruct((d_mix, d_model), BF16))
    else:
        w_spec = pl.BlockSpec((d_mix, d_model), const, pipeline_mode=pl.Buffered(1))
    outs = pl.pallas_call(
        _outproj_kernel,
        grid=(m_rows // tm, d_model // tn),
        in_specs=[pl.BlockSpec((tm, d_attn), lambda m, n: (m, 0)),
                  pl.BlockSpec((tm, d_conv), lambda m, n: (m, 0)),
                  pl.BlockSpec((1, d_attn), const),
                  pl.BlockSpec((1, d_conv), const),
                  w_spec,
                  pl.BlockSpec((tm, tn), lambda m, n: (m, n))],
        out_specs=out_specs,
        out_shape=out_shape,
        scratch_shapes=[pltpu.VMEM((tm, d_mix), BF16)],
        compiler_params=_compiler_params(("arbitrary", "arbitrary")),
        name="outproj",
    )(oa, oc, attn_gain.reshape(1, d_attn), conv_gain.reshape(1, d_conv), w_out, x)
    return outs if emit_copy else outs[0]


def _mlp_kernel(x_hbm, g_ref, wu_ref, wd_ref, o_ref, x_buf, h_scr, sem, *, tm):
    m = pl.program_id(0)
    f = pl.program_id(1)

    def x_copy(block):
        rows = pl.ds(pl.multiple_of(block * tm, tm), tm)
        return pltpu.make_async_copy(x_hbm.at[rows], x_buf, sem)

    @pl.when(f == 0)
    def _():
        @pl.when(m == 0)
        def _():
            x_copy(0).start()

        x_copy(m).wait()
        x = x_buf[...]
        h_scr[...] = _rms_rows(x, g_ref[...]).astype(BF16)
        o_ref[...] = x

    @pl.when((f == 1) & (m + 1 < pl.num_programs(0)))
    def _():
        x_copy(m + 1).start()

    a = jnp.dot(h_scr[...], wu_ref[...], preferred_element_type=F32)
    a = jnp.square(jnp.maximum(a, 0.0)).astype(BF16)
    o_ref[...] += jnp.dot(a, wd_ref[...], preferred_element_type=F32)


def _mlp(x, gain, w_up, w_down, *, tm, tf):
    m_rows, d_model = x.shape
    d_ff = w_up.shape[1]
    assert d_ff // tf >= 2, "the next row block's x is fetched on a block's second step"
    return pl.pallas_call(
        functools.partial(_mlp_kernel, tm=tm),
        grid=(m_rows // tm, d_ff // tf),
        in_specs=[pl.BlockSpec(memory_space=pl.ANY),
                  pl.BlockSpec((1, d_model), lambda m, f: (0, 0)),
                  pl.BlockSpec((d_model, tf), lambda m, f: (0, f)),
                  pl.BlockSpec((tf, d_model), lambda m, f: (f, 0))],
        out_specs=pl.BlockSpec((tm, d_model), lambda m, f: (m, 0)),
        out_shape=jax.ShapeDtypeStruct((m_rows, d_model), F32),
        scratch_shapes=[pltpu.VMEM((tm, d_model), F32),
                        pltpu.VMEM((tm, d_model), BF16),
                        pltpu.SemaphoreType.DMA(())],
        compiler_params=_compiler_params(("arbitrary", "arbitrary")),
        name="mlp",
    )(x, gain.reshape(1, d_model), w_up, w_down)


PROMPT_TILES = dict(tm=1024, tn_in=1024, tn_out=1024, tf=1024)
SAMPLE_TILES = dict(tn=1024, tf=1024)


def kernel(x_prompt, x_sample, state_attn_k, state_attn_v, state_conv, rel_bias, norm_mix, w_in,
           q_norm, k_norm, conv_w, attn_out_norm, conv_out_norm, w_out, norm_mlp, w_up, w_down):
    batch, seq, d_model = x_prompt.shape
    dec_batch, dec_seq, _ = x_sample.shape
    depth, _, wb, n_heads, head_dim = state_attn_k.shape
    assert dec_seq == 1 and head_dim == HEAD_DIM and wb == max(w for w, _ in DILATED_CONFIGS)
    assert seq == wb

    bias_p = _prompt_bias_tables(rel_bias)
    bias_s, bias0 = _sample_bias_tables(rel_bias, wb)
    kt_state = state_attn_k.transpose(0, 1, 3, 4, 2)
    vt_state = state_attn_v.transpose(0, 1, 3, 4, 2)
    rolled = None
    new_windows = None

    xp = x_prompt.reshape(batch * seq, d_model)
    xs = x_sample.reshape(dec_batch, d_model)
    pt, st = PROMPT_TILES, SAMPLE_TILES
    cp_new, cs_new = [], []
    w_in_l = w_in
    for l in range(depth):
        proj_s = _inproj(xs, norm_mix[l], w_in_l, l, q_norm[l], k_norm[l],
                         tm=dec_batch, tn=st["tn"])
        if l == 0:
            proj_s, w_in_l = proj_s
        width = proj_s.shape[-1]
        oa_s, oc_s, cs, rolled = _mixer_sample(l, proj_s, kt_state, vt_state, rolled, bias_s, bias0,
                                               conv_w[l], state_conv)
        cs_new.append(cs)

        if l + 1 < depth:
            proj, w_in_l = _inproj(xp, norm_mix[l], w_in_l, l, q_norm[l], k_norm[l],
                                   tm=pt["tm"], tn=pt["tn_in"], w_next=(w_in, l + 1))
        else:
            proj = _inproj(xp, norm_mix[l], w_in_l, l, q_norm[l], k_norm[l],
                           tm=pt["tm"], tn=pt["tn_in"])
        oa, oc, cs, new_windows, (w_out_l, w_up_l, w_down_l) = _mixer_prompt(
            l, depth, proj.reshape(N_SEGMENTS, batch, seq, width), conv_w[l], bias_p, new_windows,
            [(w_out, l), (w_up, l), (w_down, l)])
        cp_new.append(cs)

        xs = _outproj(oa_s.reshape(dec_batch, width), oc_s.reshape(dec_batch, width),
                      attn_out_norm[l], conv_out_norm[l], w_out_l, l, xs,
                      tm=dec_batch, tn=st["tn"])
        xs = _mlp(xs, norm_mlp[l], w_up_l, w_down_l, tm=dec_batch, tf=st["tf"])
        xp = _outproj(oa.reshape(batch * seq, width), oc.reshape(batch * seq, width),
                      attn_out_norm[l], conv_out_norm[l], w_out_l, l, xp, tm=pt["tm"], tn=pt["tn_out"])
        xp = _mlp(xp, norm_mlp[l], w_up_l, w_down_l, tm=pt["tm"], tf=pt["tf"])

    ks_new, vs_new = (r.transpose(0, 1, 4, 2, 3) for r in rolled)
    kp_new, vp_new = (w.transpose(0, 1, 4, 2, 3) for w in new_windows)
    return (xp.reshape(batch, seq, d_model), xs.reshape(dec_batch, dec_seq, d_model),
            kp_new, vp_new, jnp.stack(cp_new), ks_new, vs_new, jnp.stack(cs_new))
```

```python
import functools

import jax
import jax.numpy as jnp
import numpy as np
from jax import lax
from jax.experimental import pallas as pl
from jax.experimental.pallas import tpu as pltpu

HEAD_DIM = 64
CONV_WIDTH = 3
DILATED_CONFIGS = ((128, 1), (512, 4), (2048, 16))
N_BUCKETS = 32
MAX_DISTANCE = 2048
EPS = 1e-6
ATTN_SCALE = HEAD_DIM ** -0.5
MASKED = -1e30

LANES = 128
MXU_WIDTH = 256
Q_BLOCK = 128
MERGE_ROWS = 256
HEADS_PER_BLOCK = LANES // HEAD_DIM
VMEM_LIMIT_BYTES = 56 * 1024 * 1024

F32 = jnp.float32
BF16 = jnp.bfloat16


def _compiler_params(semantics, flags=None):
    return pltpu.CompilerParams(dimension_semantics=semantics,
                                vmem_limit_bytes=VMEM_LIMIT_BYTES, flags=flags)


def _rms_rows(x, gain):
    ms = jnp.mean(x * x, axis=-1, keepdims=True)
    return (x * lax.rsqrt(ms + EPS)) * gain


N_SEGMENTS = 6
SEG_Q, SEG_K, SEG_V, SEG_VC, SEG_B, SEG_C = range(N_SEGMENTS)


def _bf16_weight(w_ref, copy_ref):
    w = w_ref[...]
    if w.dtype != BF16:
        w = w.astype(BF16)
    if copy_ref is not None:
        copy_ref[...] = w
    return w


def _inproj_kernel(x_ref, g_ref, w_ref, hg_ref, grp_ref, *rest, emit_copy, round_next):
    rest = list(rest)
    wnext_ref = rest.pop(0) if round_next else None
    o_ref = rest.pop(0)
    wcopy_ref = rest.pop(0) if emit_copy else None
    wnext_copy_ref = rest.pop(0) if round_next else None
    h_scr, = rest
    n = pl.program_id(1)

    if round_next:
        wnext_copy_ref[...] = wnext_ref[...].astype(BF16)

    @pl.when(n == 0)
    def _():
        h_scr[...] = _rms_rows(x_ref[...], g_ref[...]).astype(BF16)

    y = jnp.dot(h_scr[...], _bf16_weight(w_ref, wcopy_ref), preferred_element_type=F32)
    n_slabs = o_ref.shape[1]
    for p in range(n_slabs):
        o_ref[0, p] = y[:, p * LANES:(p + 1) * LANES]
    seg = n

    @pl.when(seg <= SEG_K)
    def _():
        gain = hg_ref[pl.ds(seg, 1), :]
        per_chunk = grp_ref.shape[0] // LANES
        for j in range(n_slabs // per_chunk):
            slabs = range(j * per_chunk, (j + 1) * per_chunk)
            yj = jnp.concatenate([o_ref[0, p] for p in slabs], axis=1)
            ms = jnp.dot((yj * yj).astype(BF16), grp_ref[...], preferred_element_type=F32)
            yj = (yj * lax.rsqrt(ms + EPS)) * gain[:, j * per_chunk * LANES:(j + 1) * per_chunk * LANES]
            for i, p in enumerate(slabs):
                o_ref[0, p] = yj[:, i * LANES:(i + 1) * LANES]


def _weight_spec(w, layer, block, index_map):
    if w.ndim == 3:
        return pl.BlockSpec((None,) + block, lambda *ids: (layer,) + index_map(*ids))
    return pl.BlockSpec(block, index_map)


def _inproj(x, gain, w_in, layer, q_gain, k_gain, *, tm, tn, w_next=None):
    m_rows, d_model = x.shape
    d_in = w_in.shape[-1]
    emit_copy = w_in.ndim == 3
    seg_w = d_in // N_SEGMENTS
    assert tn == seg_w, "one grid step per projection segment"
    n_pairs = seg_w // LANES
    grp = jnp.asarray(
        (np.arange(MXU_WIDTH)[:, None] // HEAD_DIM == np.arange(MXU_WIDTH)[None, :] // HEAD_DIM)
        .astype(np.float32) / HEAD_DIM, BF16)
    head_gains = jnp.stack([jnp.tile(q_gain, tn // HEAD_DIM), jnp.tile(k_gain, tn // HEAD_DIM)])
    assert (SEG_Q, SEG_K) == (0, 1)
    const = lambda m, n: (0, 0)
    out_specs = [pl.BlockSpec((1, n_pairs, tm, LANES), lambda m, n: (n, 0, m, 0))]
    out_shape = [jax.ShapeDtypeStruct((N_SEGMENTS, n_pairs, m_rows, LANES), F32)]
    if emit_copy:
        assert m_rows == tm, "the bf16 copy is written once per column tile"
        out_specs.append(pl.BlockSpec((d_model, tn), lambda m, n: (0, n)))
        out_shape.append(jax.ShapeDtypeStruct((d_model, d_in), BF16))
    in_specs = [pl.BlockSpec((tm, d_model), lambda m, n: (m, 0)),
                pl.BlockSpec((1, d_model), const),
                _weight_spec(w_in, layer, (d_model, tn), lambda m, n: (0, n)),
                pl.BlockSpec((2, tn), const),
                pl.BlockSpec((MXU_WIDTH, MXU_WIDTH), const)]
    args = [x, gain.reshape(1, d_model), w_in, head_gains, grp]
    if w_next is not None:
        w_stack, next_layer = w_next
        rows = d_model // (m_rows // tm)
        in_specs.append(pl.BlockSpec((None, rows, tn), lambda m, n: (next_layer, m, n)))
        args.append(w_stack)
        out_specs.append(pl.BlockSpec((rows, tn), lambda m, n: (m, n)))
        out_shape.append(jax.ShapeDtypeStruct((d_model, d_in), BF16))
    outs = pl.pallas_call(
        functools.partial(_inproj_kernel, emit_copy=emit_copy, round_next=w_next is not None),
        grid=(m_rows // tm, d_in // tn),
        in_specs=in_specs,
        out_specs=out_specs,
        out_shape=out_shape,
        scratch_shapes=[pltpu.VMEM((tm, d_model), BF16)],
        compiler_params=_compiler_params(("arbitrary", "arbitrary")),
        name="inproj",
    )(*args)
    return outs if len(outs) > 1 else outs[0]


def _t5_bucket(dist):
    n_exact = N_BUCKETS // 2
    large = n_exact + (np.log(np.maximum(dist, 1) / n_exact) / np.log(MAX_DISTANCE / n_exact)
                       * (N_BUCKETS - n_exact)).astype(np.int32)
    large = np.minimum(large, N_BUCKETS - 1)
    return np.where(dist < n_exact, dist, large).astype(np.int32)


def _bias_lookup(rel_bias, buckets):
    onehot = buckets[..., None] == np.arange(rel_bias.shape[0])
    return jnp.sum(jnp.where(onehot[..., None], rel_bias, 0.0), axis=-2)


def _prompt_bias_tables(rel_bias):
    n_heads = rel_bias.shape[1]
    tables = []
    for window, dil in DILATED_CONFIGS:
        nw = window // dil
        assert nw == Q_BLOCK
        bias_k = _bias_lookup(rel_bias, _t5_bucket(np.arange(nw + 1) * dil))
        pad = jnp.full((Q_BLOCK - 1, n_heads), MASKED, F32)
        u = jnp.concatenate([pad, bias_k[::-1], pad, pad[:1]])
        row = 3 * Q_BLOCK - 1
        skew = jnp.tile(u, (Q_BLOCK, 1))[:Q_BLOCK * row].reshape(Q_BLOCK, row, n_heads)
        t = skew[:, Q_BLOCK - 1:3 * Q_BLOCK - 1].transpose(2, 0, 1)
        tables.append(t.reshape(n_heads // HEADS_PER_BLOCK, HEADS_PER_BLOCK * Q_BLOCK,
                                2 * Q_BLOCK))
    return jnp.stack(tables)


def _mixer_prompt_kernel(qkv_ref, conv_ref, cw_ref, bias_ref, *rest, seq, n_aliased, n_round):
    q_ref, k_ref, v_ref = (qkv_ref.at[i] for i in range(3))
    vc_ref, gb_ref, gc_ref = (conv_ref.at[i] for i in range(3))
    w_refs = rest[:n_round]
    rest = rest[n_round + n_aliased:]
    oa_ref, oc_ref, cs_ref, okt_ref, ovt_ref = rest[:5]
    wcopy_refs = rest[5:5 + n_round]
    qs_scr, kd_scr, v0_scr, v1_scr, o_scr, m_scr, d_scr, u_scr = rest[5 + n_round:]

    for w_ref, wcopy_ref in zip(w_refs, wcopy_refs):
        wcopy_ref[...] = w_ref[...].astype(BF16)

    for src, dst in ((k_ref, okt_ref), (v_ref, ovt_ref)):
        t = src[...].T
        for h in range(HEADS_PER_BLOCK):
            dst[h] = t[h * HEAD_DIM:(h + 1) * HEAD_DIM]

    nblk_total = seq // Q_BLOCK
    lane = lax.broadcasted_iota(jnp.int32, (Q_BLOCK, LANES), 1)
    head0 = lane < HEAD_DIM

    for c, (window, dil) in enumerate(DILATED_CONFIGS):
        sub_len = seq // dil
        blocks_per_sub = sub_len // Q_BLOCK

        for r in range(dil):
            rows = pl.ds(r, sub_len, stride=dil) if dil > 1 else pl.ds(0, seq)
            head0_rows = lax.broadcasted_iota(jnp.int32, (sub_len, LANES), 1) < HEAD_DIM
            qv = (q_ref[rows, :] * ATTN_SCALE).astype(BF16)
            kv = k_ref[rows, :].astype(BF16)
            vv = v_ref[rows, :].astype(BF16)
            zero = jnp.zeros_like(qv)
            q0 = jnp.where(head0_rows, qv, zero)
            q1 = jnp.where(head0_rows, zero, qv)
            for gb in range(blocks_per_sub):
                g = r * blocks_per_sub + gb
                src = slice(gb * Q_BLOCK, (gb + 1) * Q_BLOCK)
                qs_scr[c, 2 * g * Q_BLOCK:(2 * g + 1) * Q_BLOCK, :] = q0[src]
                qs_scr[c, (2 * g + 1) * Q_BLOCK:(2 * g + 2) * Q_BLOCK, :] = q1[src]
            dst = slice(r * sub_len, (r + 1) * sub_len)
            kd_scr[c, dst, :] = kv
            v0_scr[c, dst, :] = jnp.where(head0_rows, vv, zero)
            v1_scr[c, dst, :] = jnp.where(head0_rows, zero, vv)

        for g in range(nblk_total):
            r, gb = divmod(g, blocks_per_sub)
            lhs = qs_scr[c, 2 * g * Q_BLOCK:(2 * g + 2) * Q_BLOCK, :]
            if gb == 0:
                win = slice(g * Q_BLOCK, (g + 1) * Q_BLOCK)
                bias = bias_ref[c, 0, :, Q_BLOCK:]
            else:
                win = slice((g - 1) * Q_BLOCK, (g + 1) * Q_BLOCK)
                bias = bias_ref[c, 0]
            logits = lax.dot_general(lhs, kd_scr[c, win, :], (((1,), (1,)), ((), ())),
                                     preferred_element_type=F32) + bias
            m = jnp.max(logits, axis=-1, keepdims=True)
            p = jnp.exp(logits - m)
            den = jnp.sum(p, axis=-1, keepdims=True)
            pb = p.astype(BF16)
            o = (jnp.dot(pb[0:Q_BLOCK], v0_scr[c, win, :], preferred_element_type=F32)
                 + jnp.dot(pb[Q_BLOCK:], v1_scr[c, win, :], preferred_element_type=F32))
            m2 = jnp.where(head0, m[0:Q_BLOCK], m[Q_BLOCK:])
            d2 = jnp.where(head0, den[0:Q_BLOCK], den[Q_BLOCK:])
            if dil > 1:
                out_rows = pl.ds(gb * Q_BLOCK * dil + r, Q_BLOCK, stride=dil)
            else:
                out_rows = pl.ds(g * Q_BLOCK, Q_BLOCK)
            o_scr[c, out_rows, :] = o
            m_scr[c, out_rows, :] = m2
            d_scr[c, out_rows, :] = d2

    n_br = len(DILATED_CONFIGS)

    def merge_rows(i, carry):
        rows = pl.ds(pl.multiple_of(i * MERGE_ROWS, MERGE_ROWS), MERGE_ROWS)
        mmax = m_scr[0, rows, :]
        for c in range(1, n_br):
            mmax = jnp.maximum(mmax, m_scr[c, rows, :])
        num = jnp.zeros((MERGE_ROWS, LANES), F32)
        den = jnp.zeros((MERGE_ROWS, LANES), F32)
        for c in range(n_br):
            w = jnp.exp(m_scr[c, rows, :] - mmax)
            num = num + w * o_scr[c, rows, :]
            den = den + w * d_scr[c, rows, :]
        oa_ref[rows, :] = num / den
        return carry

    lax.fori_loop(0, seq // MERGE_ROWS, merge_rows, 0)

    u = gc_ref[...] * vc_ref[...]
    u_scr[0:8, :] = jnp.zeros((8, LANES), F32)
    u_scr[8:8 + seq, :] = u
    y = cw_ref[0:1, :] * u_scr[pl.ds(8 - (CONV_WIDTH - 1), seq), :]
    for i in range(1, CONV_WIDTH):
        y = y + cw_ref[i:i + 1, :] * u_scr[pl.ds(8 - (CONV_WIDTH - 1) + i, seq), :]
    oc_ref[...] = gb_ref[...] * y
    cs_ref[0] = u_scr[pl.ds(8 + seq - (CONV_WIDTH - 1), CONV_WIDTH - 1), :]


def _mixer_prompt(layer, depth, proj, conv_w_l, bias_tables, new_windows, to_round):
    _, n_pairs, batch, seq, _ = proj.shape
    width = n_pairs * LANES
    n_heads = width // HEAD_DIM
    n_br = len(DILATED_CONFIGS)
    tok = pl.BlockSpec((None, None, seq, LANES), lambda b, hp: (hp, b, 0, 0))
    assert (SEG_Q, SEG_K, SEG_V, SEG_VC, SEG_B, SEG_C) == tuple(range(6))
    seg_specs = [pl.BlockSpec((3, None, None, seq, LANES), lambda b, hp, s=s: (s, hp, b, 0, 0))
                 for s in range(N_SEGMENTS // 3)]
    win_spec = pl.BlockSpec((None, None, HEADS_PER_BLOCK, HEAD_DIM, seq),
                            lambda b, hp: (layer, b, hp, 0, 0))
    win_shape = jax.ShapeDtypeStruct((depth, batch, n_heads, HEAD_DIM, seq), F32)
    in_specs = seg_specs + [
        pl.BlockSpec((CONV_WIDTH, LANES), lambda b, hp: (0, hp)),
        pl.BlockSpec((n_br, 1, 2 * Q_BLOCK, 2 * Q_BLOCK), lambda b, hp: (0, hp, 0, 0))]
    args = [proj] * (N_SEGMENTS // 3) + [conv_w_l, bias_tables]
    n_steps = batch * n_pairs
    copy_specs, copy_shapes = [], []
    for w, w_layer in to_round:
        _, w_rows, w_cols = w.shape
        rows = w_rows // n_steps
        in_specs.append(pl.BlockSpec((None, rows, w_cols),
                                     lambda b, hp, w_layer=w_layer: (w_layer, b * n_pairs + hp, 0)))
        args.append(w)
        copy_specs.append(pl.BlockSpec((rows, w_cols), lambda b, hp: (b * n_pairs + hp, 0)))
        copy_shapes.append(jax.ShapeDtypeStruct((w_rows, w_cols), BF16))
    aliases = {}
    if new_windows is not None:
        aliases = {len(args): 3, len(args) + 1: 4}
        in_specs = in_specs + [pl.BlockSpec(memory_space=pl.ANY)] * 2
        args = args + list(new_windows)
    oa, oc, cs, okt, ovt, *copies = pl.pallas_call(
        functools.partial(_mixer_prompt_kernel, seq=seq, n_aliased=len(aliases),
                          n_round=len(to_round)),
        grid=(batch, n_pairs),
        in_specs=in_specs,
        out_specs=[tok, tok,
                   pl.BlockSpec((1, CONV_WIDTH - 1, LANES), lambda b, hp: (b, 0, hp)),
                   win_spec, win_spec] + copy_specs,
        out_shape=[jax.ShapeDtypeStruct((n_pairs, batch, seq, LANES), F32),
                   jax.ShapeDtypeStruct((n_pairs, batch, seq, LANES), F32),
                   jax.ShapeDtypeStruct((batch, CONV_WIDTH - 1, width), F32),
                   win_shape, win_shape] + copy_shapes,
        input_output_aliases=aliases,
        scratch_shapes=[pltpu.VMEM((n_br, 2 * seq, LANES), BF16),
                        pltpu.VMEM((n_br, seq, LANES), BF16),
                        pltpu.VMEM((n_br, seq, LANES), BF16),
                        pltpu.VMEM((n_br, seq, LANES), BF16),
                        pltpu.VMEM((n_br, seq, LANES), F32),
                        pltpu.VMEM((n_br, seq, LANES), F32),
                        pltpu.VMEM((n_br, seq, LANES), F32),
                        pltpu.VMEM((seq + 8, LANES), F32)],
        compiler_params=_compiler_params(("arbitrary", "arbitrary")),
        name="mixer_prompt",
    )(*args)
    return oa, oc, cs, (okt, ovt), copies


SAMPLE_HEADS_PER_STEP = 8


def _sample_bias_tables(rel_bias, wb):
    dist = wb - np.arange(wb)
    tables = []
    for window, dil in DILATED_CONFIGS:
        in_branch = (dist % dil == 0) & (dist // dil <= window // dil)
        bias = jnp.where(in_branch[:, None], _bias_lookup(rel_bias, _t5_bucket(dist)), MASKED)
        tables.append(bias.T)
    own = rel_bias[_t5_bucket(np.zeros((1,), np.int64))][0]
    return jnp.stack(tables), own[:, None]


def _mixer_sample_kernel(q_ref, kn_ref, vn_ref, kt_ref, vt_ref, bias_ref, bias0_ref,
                         vc_ref, gb_ref, gc_ref, cw_ref, sc_ref, *rest):
    oa_ref, oc_ref, cs_ref, okt_ref, ovt_ref = rest[-5:]
    heads, head_dim, wb = kt_ref.shape
    newest = lax.broadcasted_iota(jnp.int32, (head_dim, wb), 1) == wb - 1
    n_br = len(DILATED_CONFIGS)

    s = jnp.concatenate([jnp.sum(kt_ref[h] * q_ref[h], axis=0, keepdims=True)
                         for h in range(heads)], axis=0) * ATTN_SCALE
    l_new = jnp.concatenate([jnp.sum(q_ref[h] * kn_ref[h], axis=0, keepdims=True)
                             for h in range(heads)], axis=0) * ATTN_SCALE + bias0_ref[...]
    branches = []
    for c in range(n_br):
        logits = s + bias_ref[c]
        m = jnp.maximum(jnp.max(logits, axis=-1, keepdims=True), l_new)
        p = jnp.exp(logits - m)
        p_new = jnp.exp(l_new - m)
        den = jnp.sum(p, axis=-1, keepdims=True) + p_new
        branches.append((p, p_new, m, den))
    mmax = branches[0][2]
    for br in branches[1:]:
        mmax = jnp.maximum(mmax, br[2])
    p_all = jnp.zeros_like(s)
    p_new_all = jnp.zeros_like(mmax)
    den_all = jnp.zeros_like(mmax)
    for p, p_new, m, den in branches:
        w = jnp.exp(m - mmax)
        p_all = p_all + w * p
        p_new_all = p_new_all + w * p_new
        den_all = den_all + w * den

    for h in range(heads):
        kt = kt_ref[h]
        vt = vt_ref[h]
        k_new, v_new = kn_ref[h], vn_ref[h]
        num = (jnp.sum(vt * p_all[h:h + 1, :], axis=-1, keepdims=True)
               + p_new_all[h:h + 1, :] * v_new)
        oa_ref[h] = num / den_all[h:h + 1, :]
        okt_ref[h] = jnp.where(newest, k_new, pltpu.roll(kt, wb - 1, axis=1))
        ovt_ref[h] = jnp.where(newest, v_new, pltpu.roll(vt, wb - 1, axis=1))

    @pl.when(pl.program_id(1) == 0)
    def _():
        u = gc_ref[0] * vc_ref[0]
        state = sc_ref[0]
        y = cw_ref[0:1, :] * state[0:1]
        for i in range(1, CONV_WIDTH - 1):
            y = y + cw_ref[i:i + 1, :] * state[i:i + 1]
        y = y + cw_ref[CONV_WIDTH - 1:CONV_WIDTH, :] * u
        oc_ref[0] = gb_ref[0] * y
        cs_ref[0] = jnp.concatenate([state[1:], u], axis=0)


def _mixer_sample(layer, proj, kt_state, vt_state, rolled, bias_s, bias0, conv_w_l, state_conv):
    depth, batch, n_heads, head_dim, wb = kt_state.shape
    proj = proj.transpose(0, 2, 1, 3).reshape(N_SEGMENTS, batch, -1)
    width = proj.shape[-1]
    hb = SAMPLE_HEADS_PER_STEP
    n_br = len(DILATED_CONFIGS)
    qkv = proj[SEG_Q:SEG_V + 1].reshape(3, batch, n_heads, head_dim, 1)
    rows = proj.reshape(N_SEGMENTS, batch, 1, width)
    col_specs = [pl.BlockSpec((None, None, hb, head_dim, 1), lambda b, hc, s=s: (s, b, hc, 0, 0))
                 for s in range(3)]
    row_specs = [pl.BlockSpec((None, 1, 1, width), lambda b, hc, s=s: (s, b, 0, 0))
                 for s in (SEG_VC, SEG_B, SEG_C)]
    buf_spec = pl.BlockSpec((None, None, hb, head_dim, wb), lambda b, hc: (layer, b, hc, 0, 0))
    in_specs = col_specs + [
        buf_spec, buf_spec,
        pl.BlockSpec((n_br, hb, wb), lambda b, hc: (0, hc, 0)),
        pl.BlockSpec((hb, 1), lambda b, hc: (hc, 0))] + row_specs + [
        pl.BlockSpec((CONV_WIDTH, width), lambda b, hc: (0, 0)),
        pl.BlockSpec((1, CONV_WIDTH - 1, width), lambda b, hc: (layer * batch + b, 0, 0))]
    args = [qkv, qkv, qkv, kt_state, vt_state, bias_s, bias0, rows, rows, rows, conv_w_l,
            state_conv.reshape(depth * batch, CONV_WIDTH - 1, width)]
    aliases = {}
    if rolled is not None:
        aliases = {len(args): 3, len(args) + 1: 4}
        in_specs = in_specs + [pl.BlockSpec(memory_space=pl.ANY)] * 2
        args = args + list(rolled)
    oa, oc, cs, okt, ovt = pl.pallas_call(
        _mixer_sample_kernel,
        grid=(batch, n_heads // hb),
        in_specs=in_specs,
        out_specs=[pl.BlockSpec((None, hb, head_dim, 1), lambda b, hc: (b, hc, 0, 0)),
                   pl.BlockSpec((1, 1, width), lambda b, hc: (b, 0, 0)),
                   pl.BlockSpec((1, CONV_WIDTH - 1, width), lambda b, hc: (b, 0, 0)),
                   buf_spec, buf_spec],
        out_shape=[jax.ShapeDtypeStruct((batch, n_heads, head_dim, 1), F32),
                   jax.ShapeDtypeStruct((batch, 1, width), F32),
                   jax.ShapeDtypeStruct((batch, CONV_WIDTH - 1, width), F32),
                   jax.ShapeDtypeStruct(kt_state.shape, F32),
                   jax.ShapeDtypeStruct(vt_state.shape, F32)],
        input_output_aliases=aliases,
        compiler_params=_compiler_params(("arbitrary", "arbitrary")),
        name="mixer_sample",
    )(*args)
    return oa, oc, cs, (okt, ovt)


def _outproj_kernel(oa_ref, oc_ref, ga_ref, gc_ref, w_ref, x_ref, o_ref, *rest):
    z_scr = rest[-1]
    wcopy_ref = rest[0] if len(rest) == 2 else None
    n = pl.program_id(1)
    tn = o_ref.shape[1]

    def norm_into(src_ref, gain_ref, col0):
        n_slabs = src_ref.shape[0]
        ss = sum(jnp.sum(src_ref[p] * src_ref[p], axis=-1, keepdims=True) for p in range(n_slabs))
        r = lax.rsqrt(ss / (n_slabs * LANES) + EPS)
        for p in range(n_slabs):
            cols = slice(p * LANES, (p + 1) * LANES)
            z_scr[:, col0 + p * LANES:col0 + (p + 1) * LANES] = (
                (src_ref[p] * r) * gain_ref[:, cols]).astype(BF16)

    @pl.when(n == 0)
    def _():
        norm_into(oa_ref, ga_ref, 0)
        norm_into(oc_ref, gc_ref, oa_ref.shape[0] * LANES)

    if w_ref.shape[1] == tn:
        w = _bf16_weight(w_ref, wcopy_ref)
    else:
        w = w_ref[:, pl.ds(pl.multiple_of(n * tn, tn), tn)]
    o_ref[...] = x_ref[...] + jnp.dot(z_scr[...], w, preferred_element_type=F32)


def _outproj(oa, oc, attn_gain, conv_gain, w_out, layer, x, *, tm, tn):
    _, m_rows, _ = oa.shape
    d_attn = oa.shape[0] * LANES
    d_conv = oc.shape[0] * LANES
    d_model = x.shape[1]
    d_mix = d_attn + d_conv
    emit_copy = w_out.ndim == 3
    const = lambda m, n: (0, 0)
    out_specs = [pl.BlockSpec((tm, tn), lambda m, n: (m, n))]
    out_shape = [jax.ShapeDtypeStruct((m_rows, d_model), F32)]
    if emit_copy:
        assert m_rows == tm, "the bf16 copy is written once per column tile"
        w_spec = _weight_spec(w_out, layer, (d_mix, tn), lambda m, n: (0, n))
        out_specs.append(pl.BlockSpec((d_mix, tn), lambda m, n: (0, n)))
        out_shape.append(jax.ShapeDtypeStruct((d_mix, d_model), BF16))
    else:
        w_spec = pl.BlockSpec((d_mix, d_model), const, pipeline_mode=pl.Buffered(1))
    outs = pl.pallas_call(
        _outproj_kernel,
        grid=(m_rows // tm, d_model // tn),
        in_specs=[pl.BlockSpec((d_attn // LANES, tm, LANES), lambda m, n: (0, m, 0)),
                  pl.BlockSpec((d_conv // LANES, tm, LANES), lambda m, n: (0, m, 0)),
                  pl.BlockSpec((1, d_attn), const),
                  pl.BlockSpec((1, d_conv), const),
                  w_spec,
                  pl.BlockSpec((tm, tn), lambda m, n: (m, n))],
        out_specs=out_specs,
        out_shape=out_shape,
        scratch_shapes=[pltpu.VMEM((tm, d_mix), BF16)],
        compiler_params=_compiler_params(("arbitrary", "arbitrary")),
        name="outproj",
    )(oa, oc, attn_gain.reshape(1, d_attn), conv_gain.reshape(1, d_conv), w_out, x)
    return outs if emit_copy else outs[0]


def _mlp_kernel(x_hbm, g_ref, wu_ref, wd_ref, o_ref, x_buf, h_scr, sem, *, tm):
    m = pl.program_id(0)
    f = pl.program_id(1)

    def x_copy(block):
        rows = pl.ds(pl.multiple_of(block * tm, tm), tm)
        return pltpu.make_async_copy(x_hbm.at[rows], x_buf, sem)

    @pl.when(f == 0)
    def _():
        @pl.when(m == 0)
        def _():
            x_copy(0).start()

        x_copy(m).wait()
        x = x_buf[...]
        h_scr[...] = _rms_rows(x, g_ref[...]).astype(BF16)
        o_ref[...] = x

    @pl.when((f == 1) & (m + 1 < pl.num_programs(0)))
    def _():
        x_copy(m + 1).start()

    a = jnp.dot(h_scr[...], wu_ref[...], preferred_element_type=F32)
    a = jnp.square(jnp.maximum(a, 0.0)).astype(BF16)
    o_ref[...] += jnp.dot(a, wd_ref[...], preferred_element_type=F32)


def _mlp(x, gain, w_up, w_down, *, tm, tf):
    m_rows, d_model = x.shape
    d_ff = w_up.shape[1]
    assert d_ff // tf >= 2, "the next row block's x is fetched on a block's second step"
    return pl.pallas_call(
        functools.partial(_mlp_kernel, tm=tm),
        grid=(m_rows // tm, d_ff // tf),
        in_specs=[pl.BlockSpec(memory_space=pl.ANY),
                  pl.BlockSpec((1, d_model), lambda m, f: (0, 0)),
                  pl.BlockSpec((d_model, tf), lambda m, f: (0, f)),
                  pl.BlockSpec((tf, d_model), lambda m, f: (f, 0))],
        out_specs=pl.BlockSpec((tm, d_model), lambda m, f: (m, 0)),
        out_shape=jax.ShapeDtypeStruct((m_rows, d_model), F32),
        scratch_shapes=[pltpu.VMEM((tm, d_model), F32),
                        pltpu.VMEM((tm, d_model), BF16),
                        pltpu.SemaphoreType.DMA(())],
        compiler_params=_compiler_params(("arbitrary", "arbitrary")),
        name="mlp",
    )(x, gain.reshape(1, d_model), w_up, w_down)


PROMPT_TILES = dict(tm=1024, tn_in=1024, tn_out=1024, tf=1024)
SAMPLE_TILES = dict(tn=1024, tf=1024)


def kernel(x_prompt, x_sample, state_attn_k, state_attn_v, state_conv, rel_bias, norm_mix, w_in,
           q_norm, k_norm, conv_w, attn_out_norm, conv_out_norm, w_out, norm_mlp, w_up, w_down):
    batch, seq, d_model = x_prompt.shape
    dec_batch, dec_seq, _ = x_sample.shape
    depth, _, wb, n_heads, head_dim = state_attn_k.shape
    assert dec_seq == 1 and head_dim == HEAD_DIM and wb == max(w for w, _ in DILATED_CONFIGS)
    assert seq == wb

    bias_p = _prompt_bias_tables(rel_bias)
    bias_s, bias0 = _sample_bias_tables(rel_bias, wb)
    kt_state = state_attn_k.transpose(0, 1, 3, 4, 2)
    vt_state = state_attn_v.transpose(0, 1, 3, 4, 2)
    rolled = None
    new_windows = None

    xp = x_prompt.reshape(batch * seq, d_model)
    xs = x_sample.reshape(dec_batch, d_model)
    pt, st = PROMPT_TILES, SAMPLE_TILES
    cp_new, cs_new = [], []
    w_in_l = w_in
    for l in range(depth):
        proj_s = _inproj(xs, norm_mix[l], w_in_l, l, q_norm[l], k_norm[l],
                         tm=dec_batch, tn=st["tn"])
        if l == 0:
            proj_s, w_in_l = proj_s
        n_pairs = proj_s.shape[1]
        oa_s, oc_s, cs, rolled = _mixer_sample(l, proj_s, kt_state, vt_state, rolled, bias_s, bias0,
                                               conv_w[l], state_conv)
        cs_new.append(cs)

        if l + 1 < depth:
            proj, w_in_l = _inproj(xp, norm_mix[l], w_in_l, l, q_norm[l], k_norm[l],
                                   tm=pt["tm"], tn=pt["tn_in"], w_next=(w_in, l + 1))
        else:
            proj = _inproj(xp, norm_mix[l], w_in_l, l, q_norm[l], k_norm[l],
                           tm=pt["tm"], tn=pt["tn_in"])
        oa, oc, cs, new_windows, (w_out_l, w_up_l, w_down_l) = _mixer_prompt(
            l, depth, proj.reshape(N_SEGMENTS, n_pairs, batch, seq, LANES), conv_w[l], bias_p,
            new_windows,
            [(w_out, l), (w_up, l), (w_down, l)])
        cp_new.append(cs)

        slabs = lambda a: a.reshape(dec_batch, n_pairs, LANES).transpose(1, 0, 2)
        xs = _outproj(slabs(oa_s), slabs(oc_s),
                      attn_out_norm[l], conv_out_norm[l], w_out_l, l, xs,
                      tm=dec_batch, tn=st["tn"])
        xs = _mlp(xs, norm_mlp[l], w_up_l, w_down_l, tm=dec_batch, tf=st["tf"])
        xp = _outproj(oa.reshape(n_pairs, batch * seq, LANES), oc.reshape(n_pairs, batch * seq, LANES),
                      attn_out_norm[l], conv_out_norm[l], w_out_l, l, xp, tm=pt["tm"], tn=pt["tn_out"])
        xp = _mlp(xp, norm_mlp[l], w_up_l, w_down_l, tm=pt["tm"], tf=pt["tf"])

    ks_new, vs_new = (r.transpose(0, 1, 4, 2, 3) for r in rolled)
    kp_new, vp_new = (w.transpose(0, 1, 4, 2, 3) for w in new_windows)
    return (xp.reshape(batch, seq, d_model), xs.reshape(dec_batch, dec_seq, d_model),
            kp_new, vp_new, jnp.stack(cp_new), ks_new, vs_new, jnp.stack(cs_new))
```

```python
import functools

import jax
import jax.numpy as jnp
import numpy as np
from jax import lax
from jax.experimental import pallas as pl
from jax.experimental.pallas import tpu as pltpu

HEAD_DIM = 64
CONV_WIDTH = 3
DILATED_CONFIGS = ((128, 1), (512, 4), (2048, 16))
N_BUCKETS = 32
MAX_DISTANCE = 2048
EPS = 1e-6
ATTN_SCALE = HEAD_DIM ** -0.5
MASKED = -1e30

LANES = 128
MXU_WIDTH = 256
Q_BLOCK = 128
MERGE_ROWS = 256
HEADS_PER_BLOCK = LANES // HEAD_DIM
VMEM_LIMIT_BYTES = 56 * 1024 * 1024

F32 = jnp.float32
BF16 = jnp.bfloat16


def _compiler_params(semantics, flags=None):
    return pltpu.CompilerParams(dimension_semantics=semantics,
                                vmem_limit_bytes=VMEM_LIMIT_BYTES, flags=flags)


def _rms_rows(x, gain):
    ms = jnp.mean(x * x, axis=-1, keepdims=True)
    return (x * lax.rsqrt(ms + EPS)) * gain


N_SEGMENTS = 6
SEG_Q, SEG_K, SEG_V, SEG_VC, SEG_B, SEG_C = range(N_SEGMENTS)


def _bf16_weight(w_ref, copy_ref):
    w = w_ref[...]
    if w.dtype != BF16:
        w = w.astype(BF16)
    if copy_ref is not None:
        copy_ref[...] = w
    return w


def _inproj_kernel(x_ref, g_ref, w_ref, hg_ref, grp_ref, *rest, tiles_per_seg, emit_copy,
                   round_next):
    rest = list(rest)
    wnext_ref = rest.pop(0) if round_next else None
    o_ref = rest.pop(0)
    wcopy_ref = rest.pop(0) if emit_copy else None
    wnext_copy_ref = rest.pop(0) if round_next else None
    h_scr, = rest
    n = pl.program_id(1)

    if round_next:
        wnext_copy_ref[...] = wnext_ref[...].astype(BF16)

    @pl.when(n == 0)
    def _():
        h_scr[...] = _rms_rows(x_ref[...], g_ref[...]).astype(BF16)

    o_ref[0] = jnp.dot(h_scr[...], _bf16_weight(w_ref, wcopy_ref), preferred_element_type=F32)
    seg = n // tiles_per_seg

    @pl.when(seg <= SEG_K)
    def _():
        gain = hg_ref[pl.ds(seg, 1), :]
        chunk = grp_ref.shape[0]
        for j in range(o_ref.shape[2] // chunk):
            cols = slice(j * chunk, (j + 1) * chunk)
            yj = o_ref[0, :, cols]
            ms = jnp.dot((yj * yj).astype(BF16), grp_ref[...], preferred_element_type=F32)
            o_ref[0, :, cols] = (yj * lax.rsqrt(ms + EPS)) * gain[:, cols]


def _weight_spec(w, layer, block, index_map):
    if w.ndim == 3:
        return pl.BlockSpec((None,) + block, lambda *ids: (layer,) + index_map(*ids))
    return pl.BlockSpec(block, index_map)


def _inproj(x, gain, w_in, layer, q_gain, k_gain, *, tm, tn, w_next=None):
    m_rows, d_model = x.shape
    d_in = w_in.shape[-1]
    emit_copy = w_in.ndim == 3
    seg_w = d_in // N_SEGMENTS
    tiles_per_seg = seg_w // tn
    grp = jnp.asarray(
        (np.arange(MXU_WIDTH)[:, None] // HEAD_DIM == np.arange(MXU_WIDTH)[None, :] // HEAD_DIM)
        .astype(np.float32) / HEAD_DIM, BF16)
    head_gains = jnp.stack([jnp.tile(q_gain, tn // HEAD_DIM), jnp.tile(k_gain, tn // HEAD_DIM)])
    assert (SEG_Q, SEG_K) == (0, 1)
    const = lambda m, n: (0, 0)
    out_specs = [pl.BlockSpec(
        (1, tm, tn), lambda m, n: (n // tiles_per_seg, m, n % tiles_per_seg))]
    out_shape = [jax.ShapeDtypeStruct((N_SEGMENTS, m_rows, seg_w), F32)]
    if emit_copy:
        assert m_rows == tm, "the bf16 copy is written once per column tile"
        out_specs.append(pl.BlockSpec((d_model, tn), lambda m, n: (0, n)))
        out_shape.append(jax.ShapeDtypeStruct((d_model, d_in), BF16))
    in_specs = [pl.BlockSpec((tm, d_model), lambda m, n: (m, 0)),
                pl.BlockSpec((1, d_model), const),
                _weight_spec(w_in, layer, (d_model, tn), lambda m, n: (0, n)),
                pl.BlockSpec((2, tn), const),
                pl.BlockSpec((MXU_WIDTH, MXU_WIDTH), const)]
    args = [x, gain.reshape(1, d_model), w_in, head_gains, grp]
    if w_next is not None:
        w_stack, next_layer = w_next
        rows = d_model // (m_rows // tm)
        in_specs.append(pl.BlockSpec((None, rows, tn), lambda m, n: (next_layer, m, n)))
        args.append(w_stack)
        out_specs.append(pl.BlockSpec((rows, tn), lambda m, n: (m, n)))
        out_shape.append(jax.ShapeDtypeStruct((d_model, d_in), BF16))
    outs = pl.pallas_call(
        functools.partial(_inproj_kernel, tiles_per_seg=tiles_per_seg, emit_copy=emit_copy,
                          round_next=w_next is not None),
        grid=(m_rows // tm, d_in // tn),
        in_specs=in_specs,
        out_specs=out_specs,
        out_shape=out_shape,
        scratch_shapes=[pltpu.VMEM((tm, d_model), BF16)],
        compiler_params=_compiler_params(("arbitrary", "arbitrary")),
        name="inproj",
    )(*args)
    return outs if len(outs) > 1 else outs[0]


def _t5_bucket(dist):
    n_exact = N_BUCKETS // 2
    large = n_exact + (np.log(np.maximum(dist, 1) / n_exact) / np.log(MAX_DISTANCE / n_exact)
                       * (N_BUCKETS - n_exact)).astype(np.int32)
    large = np.minimum(large, N_BUCKETS - 1)
    return np.where(dist < n_exact, dist, large).astype(np.int32)


def _bias_lookup(rel_bias, buckets):
    onehot = buckets[..., None] == np.arange(rel_bias.shape[0])
    return jnp.sum(jnp.where(onehot[..., None], rel_bias, 0.0), axis=-2)


def _prompt_bias_tables(rel_bias):
    n_heads = rel_bias.shape[1]
    tables = []
    for window, dil in DILATED_CONFIGS:
        nw = window // dil
        assert nw == Q_BLOCK
        bias_k = _bias_lookup(rel_bias, _t5_bucket(np.arange(nw + 1) * dil))
        pad = jnp.full((Q_BLOCK - 1, n_heads), MASKED, F32)
        u = jnp.concatenate([pad, bias_k[::-1], pad, pad[:1]])
        row = 3 * Q_BLOCK - 1
        skew = jnp.tile(u, (Q_BLOCK, 1))[:Q_BLOCK * row].reshape(Q_BLOCK, row, n_heads)
        t = skew[:, Q_BLOCK - 1:3 * Q_BLOCK - 1].transpose(2, 0, 1)
        tables.append(t.reshape(n_heads // HEADS_PER_BLOCK, HEADS_PER_BLOCK * Q_BLOCK,
                                2 * Q_BLOCK))
    return jnp.stack(tables)


def _mixer_prompt_kernel(qkv_ref, conv_ref, cw_ref, bias_ref, *rest, seq, n_aliased, n_round):
    q_ref, k_ref, v_ref = (qkv_ref.at[i] for i in range(3))
    vc_ref, gb_ref, gc_ref = (conv_ref.at[i] for i in range(3))
    w_refs = rest[:n_round]
    rest = rest[n_round + n_aliased:]
    oa_ref, oc_ref, cs_ref, okt_ref, ovt_ref = rest[:5]
    wcopy_refs = rest[5:5 + n_round]
    qs_scr, kd_scr, v0_scr, v1_scr, o_scr, m_scr, d_scr, u_scr = rest[5 + n_round:]

    for w_ref, wcopy_ref in zip(w_refs, wcopy_refs):
        wcopy_ref[...] = w_ref[...].astype(BF16)

    for src, dst in ((k_ref, okt_ref), (v_ref, ovt_ref)):
        t = src[0].T
        for h in range(HEADS_PER_BLOCK):
            dst[h] = t[h * HEAD_DIM:(h + 1) * HEAD_DIM]

    nblk_total = seq // Q_BLOCK
    lane = lax.broadcasted_iota(jnp.int32, (Q_BLOCK, LANES), 1)
    head0 = lane < HEAD_DIM

    for c, (window, dil) in enumerate(DILATED_CONFIGS):
        sub_len = seq // dil
        blocks_per_sub = sub_len // Q_BLOCK

        for r in range(dil):
            rows = pl.ds(r, sub_len, stride=dil) if dil > 1 else pl.ds(0, seq)
            head0_rows = lax.broadcasted_iota(jnp.int32, (sub_len, LANES), 1) < HEAD_DIM
            qv = (q_ref[0, rows, :] * ATTN_SCALE).astype(BF16)
            kv = k_ref[0, rows, :].astype(BF16)
            vv = v_ref[0, rows, :].astype(BF16)
            zero = jnp.zeros_like(qv)
            q0 = jnp.where(head0_rows, qv, zero)
            q1 = jnp.where(head0_rows, zero, qv)
            for gb in range(blocks_per_sub):
                g = r * blocks_per_sub + gb
                src = slice(gb * Q_BLOCK, (gb + 1) * Q_BLOCK)
                qs_scr[c, 2 * g * Q_BLOCK:(2 * g + 1) * Q_BLOCK, :] = q0[src]
                qs_scr[c, (2 * g + 1) * Q_BLOCK:(2 * g + 2) * Q_BLOCK, :] = q1[src]
            dst = slice(r * sub_len, (r + 1) * sub_len)
            kd_scr[c, dst, :] = kv
            v0_scr[c, dst, :] = jnp.where(head0_rows, vv, zero)
            v1_scr[c, dst, :] = jnp.where(head0_rows, zero, vv)

        for g in range(nblk_total):
            r, gb = divmod(g, blocks_per_sub)
            lhs = qs_scr[c, 2 * g * Q_BLOCK:(2 * g + 2) * Q_BLOCK, :]
            if gb == 0:
                win = slice(g * Q_BLOCK, (g + 1) * Q_BLOCK)
                bias = bias_ref[c, 0, :, Q_BLOCK:]
            else:
                win = slice((g - 1) * Q_BLOCK, (g + 1) * Q_BLOCK)
                bias = bias_ref[c, 0]
            logits = lax.dot_general(lhs, kd_scr[c, win, :], (((1,), (1,)), ((), ())),
                                     preferred_element_type=F32) + bias
            m = jnp.max(logits, axis=-1, keepdims=True)
            p = jnp.exp(logits - m)
            den = jnp.sum(p, axis=-1, keepdims=True)
            pb = p.astype(BF16)
            o = (jnp.dot(pb[0:Q_BLOCK], v0_scr[c, win, :], preferred_element_type=F32)
                 + jnp.dot(pb[Q_BLOCK:], v1_scr[c, win, :], preferred_element_type=F32))
            m2 = jnp.where(head0, m[0:Q_BLOCK], m[Q_BLOCK:])
            d2 = jnp.where(head0, den[0:Q_BLOCK], den[Q_BLOCK:])
            if dil > 1:
                out_rows = pl.ds(gb * Q_BLOCK * dil + r, Q_BLOCK, stride=dil)
            else:
                out_rows = pl.ds(g * Q_BLOCK, Q_BLOCK)
            o_scr[c, out_rows, :] = o
            m_scr[c, out_rows, :] = m2
            d_scr[c, out_rows, :] = d2

    n_br = len(DILATED_CONFIGS)

    def merge_rows(i, carry):
        rows = pl.ds(pl.multiple_of(i * MERGE_ROWS, MERGE_ROWS), MERGE_ROWS)
        mmax = m_scr[0, rows, :]
        for c in range(1, n_br):
            mmax = jnp.maximum(mmax, m_scr[c, rows, :])
        num = jnp.zeros((MERGE_ROWS, LANES), F32)
        den = jnp.zeros((MERGE_ROWS, LANES), F32)
        for c in range(n_br):
            w = jnp.exp(m_scr[c, rows, :] - mmax)
            num = num + w * o_scr[c, rows, :]
            den = den + w * d_scr[c, rows, :]
        oa_ref[0, rows, :] = num / den
        return carry

    lax.fori_loop(0, seq // MERGE_ROWS, merge_rows, 0)

    u = gc_ref[0] * vc_ref[0]
    u_scr[0:8, :] = jnp.zeros((8, LANES), F32)
    u_scr[8:8 + seq, :] = u
    y = cw_ref[0:1, :] * u_scr[pl.ds(8 - (CONV_WIDTH - 1), seq), :]
    for i in range(1, CONV_WIDTH):
        y = y + cw_ref[i:i + 1, :] * u_scr[pl.ds(8 - (CONV_WIDTH - 1) + i, seq), :]
    oc_ref[0] = gb_ref[0] * y
    cs_ref[0] = u_scr[pl.ds(8 + seq - (CONV_WIDTH - 1), CONV_WIDTH - 1), :]


def _mixer_prompt(layer, depth, proj, conv_w_l, bias_tables, new_windows, to_round):
    _, batch, seq, width = proj.shape
    n_pairs = width // LANES
    n_heads = width // HEAD_DIM
    n_br = len(DILATED_CONFIGS)
    tok = pl.BlockSpec((1, seq, LANES), lambda b, hp: (b, 0, hp))
    assert (SEG_Q, SEG_K, SEG_V, SEG_VC, SEG_B, SEG_C) == tuple(range(6))
    seg_specs = [pl.BlockSpec((3, 1, seq, LANES), lambda b, hp, s=s: (s, b, 0, hp))
                 for s in range(N_SEGMENTS // 3)]
    win_spec = pl.BlockSpec((None, None, HEADS_PER_BLOCK, HEAD_DIM, seq),
                            lambda b, hp: (layer, b, hp, 0, 0))
    win_shape = jax.ShapeDtypeStruct((depth, batch, n_heads, HEAD_DIM, seq), F32)
    in_specs = seg_specs + [
        pl.BlockSpec((CONV_WIDTH, LANES), lambda b, hp: (0, hp)),
        pl.BlockSpec((n_br, 1, 2 * Q_BLOCK, 2 * Q_BLOCK), lambda b, hp: (0, hp, 0, 0))]
    args = [proj] * (N_SEGMENTS // 3) + [conv_w_l, bias_tables]
    n_steps = batch * n_pairs
    copy_specs, copy_shapes = [], []
    for w, w_layer in to_round:
        _, w_rows, w_cols = w.shape
        rows = w_rows // n_steps
        in_specs.append(pl.BlockSpec((None, rows, w_cols),
                                     lambda b, hp, w_layer=w_layer: (w_layer, b * n_pairs + hp, 0)))
        args.append(w)
        copy_specs.append(pl.BlockSpec((rows, w_cols), lambda b, hp: (b * n_pairs + hp, 0)))
        copy_shapes.append(jax.ShapeDtypeStruct((w_rows, w_cols), BF16))
    aliases = {}
    if new_windows is not None:
        aliases = {len(args): 3, len(args) + 1: 4}
        in_specs = in_specs + [pl.BlockSpec(memory_space=pl.ANY)] * 2
        args = args + list(new_windows)
    oa, oc, cs, okt, ovt, *copies = pl.pallas_call(
        functools.partial(_mixer_prompt_kernel, seq=seq, n_aliased=len(aliases),
                          n_round=len(to_round)),
        grid=(batch, n_pairs),
        in_specs=in_specs,
        out_specs=[tok, tok,
                   pl.BlockSpec((1, CONV_WIDTH - 1, LANES), lambda b, hp: (b, 0, hp)),
                   win_spec, win_spec] + copy_specs,
        out_shape=[jax.ShapeDtypeStruct((batch, seq, width), F32),
                   jax.ShapeDtypeStruct((batch, seq, width), F32),
                   jax.ShapeDtypeStruct((batch, CONV_WIDTH - 1, width), F32),
                   win_shape, win_shape] + copy_shapes,
        input_output_aliases=aliases,
        scratch_shapes=[pltpu.VMEM((n_br, 2 * seq, LANES), BF16),
                        pltpu.VMEM((n_br, seq, LANES), BF16),
                        pltpu.VMEM((n_br, seq, LANES), BF16),
                        pltpu.VMEM((n_br, seq, LANES), BF16),
                        pltpu.VMEM((n_br, seq, LANES), F32),
                        pltpu.VMEM((n_br, seq, LANES), F32),
                        pltpu.VMEM((n_br, seq, LANES), F32),
                        pltpu.VMEM((seq + 8, LANES), F32)],
        compiler_params=_compiler_params(("arbitrary", "arbitrary")),
        name="mixer_prompt",
    )(*args)
    return oa, oc, cs, (okt, ovt), copies


SAMPLE_HEADS_PER_STEP = 8


def _sample_bias_tables(rel_bias, wb):
    dist = wb - np.arange(wb)
    tables = []
    for window, dil in DILATED_CONFIGS:
        in_branch = (dist % dil == 0) & (dist // dil <= window // dil)
        bias = jnp.where(in_branch[:, None], _bias_lookup(rel_bias, _t5_bucket(dist)), MASKED)
        tables.append(bias.T)
    own = rel_bias[_t5_bucket(np.zeros((1,), np.int64))][0]
    return jnp.stack(tables), own[:, None]


def _mixer_sample_kernel(q_ref, kn_ref, vn_ref, kt_ref, vt_ref, bias_ref, bias0_ref,
                         vc_ref, gb_ref, gc_ref, cw_ref, sc_ref, *rest):
    oa_ref, oc_ref, cs_ref, okt_ref, ovt_ref = rest[-5:]
    heads, head_dim, wb = kt_ref.shape
    newest = lax.broadcasted_iota(jnp.int32, (head_dim, wb), 1) == wb - 1
    n_br = len(DILATED_CONFIGS)

    s = jnp.concatenate([jnp.sum(kt_ref[h] * q_ref[h], axis=0, keepdims=True)
                         for h in range(heads)], axis=0) * ATTN_SCALE
    l_new = jnp.concatenate([jnp.sum(q_ref[h] * kn_ref[h], axis=0, keepdims=True)
                             for h in range(heads)], axis=0) * ATTN_SCALE + bias0_ref[...]
    branches = []
    for c in range(n_br):
        logits = s + bias_ref[c]
        m = jnp.maximum(jnp.max(logits, axis=-1, keepdims=True), l_new)
        p = jnp.exp(logits - m)
        p_new = jnp.exp(l_new - m)
        den = jnp.sum(p, axis=-1, keepdims=True) + p_new
        branches.append((p, p_new, m, den))
    mmax = branches[0][2]
    for br in branches[1:]:
        mmax = jnp.maximum(mmax, br[2])
    p_all = jnp.zeros_like(s)
    p_new_all = jnp.zeros_like(mmax)
    den_all = jnp.zeros_like(mmax)
    for p, p_new, m, den in branches:
        w = jnp.exp(m - mmax)
        p_all = p_all + w * p
        p_new_all = p_new_all + w * p_new
        den_all = den_all + w * den

    for h in range(heads):
        kt = kt_ref[h]
        vt = vt_ref[h]
        k_new, v_new = kn_ref[h], vn_ref[h]
        num = (jnp.sum(vt * p_all[h:h + 1, :], axis=-1, keepdims=True)
               + p_new_all[h:h + 1, :] * v_new)
        oa_ref[h] = num / den_all[h:h + 1, :]
        okt_ref[h] = jnp.where(newest, k_new, pltpu.roll(kt, wb - 1, axis=1))
        ovt_ref[h] = jnp.where(newest, v_new, pltpu.roll(vt, wb - 1, axis=1))

    @pl.when(pl.program_id(1) == 0)
    def _():
        u = gc_ref[0] * vc_ref[0]
        state = sc_ref[0]
        y = cw_ref[0:1, :] * state[0:1]
        for i in range(1, CONV_WIDTH - 1):
            y = y + cw_ref[i:i + 1, :] * state[i:i + 1]
        y = y + cw_ref[CONV_WIDTH - 1:CONV_WIDTH, :] * u
        oc_ref[0] = gb_ref[0] * y
        cs_ref[0] = jnp.concatenate([state[1:], u], axis=0)


def _mixer_sample(layer, proj, kt_state, vt_state, rolled, bias_s, bias0, conv_w_l, state_conv):
    depth, batch, n_heads, head_dim, wb = kt_state.shape
    width = proj.shape[-1]
    hb = SAMPLE_HEADS_PER_STEP
    n_br = len(DILATED_CONFIGS)
    qkv = proj[SEG_Q:SEG_V + 1].reshape(3, batch, n_heads, head_dim, 1)
    rows = proj.reshape(N_SEGMENTS, batch, 1, width)
    col_specs = [pl.BlockSpec((None, None, hb, head_dim, 1), lambda b, hc, s=s: (s, b, hc, 0, 0))
                 for s in range(3)]
    row_specs = [pl.BlockSpec((None, 1, 1, width), lambda b, hc, s=s: (s, b, 0, 0))
                 for s in (SEG_VC, SEG_B, SEG_C)]
    buf_spec = pl.BlockSpec((None, None, hb, head_dim, wb), lambda b, hc: (layer, b, hc, 0, 0))
    in_specs = col_specs + [
        buf_spec, buf_spec,
        pl.BlockSpec((n_br, hb, wb), lambda b, hc: (0, hc, 0)),
        pl.BlockSpec((hb, 1), lambda b, hc: (hc, 0))] + row_specs + [
        pl.BlockSpec((CONV_WIDTH, width), lambda b, hc: (0, 0)),
        pl.BlockSpec((1, CONV_WIDTH - 1, width), lambda b, hc: (layer * batch + b, 0, 0))]
    args = [qkv, qkv, qkv, kt_state, vt_state, bias_s, bias0, rows, rows, rows, conv_w_l,
            state_conv.reshape(depth * batch, CONV_WIDTH - 1, width)]
    aliases = {}
    if rolled is not None:
        aliases = {len(args): 3, len(args) + 1: 4}
        in_specs = in_specs + [pl.BlockSpec(memory_space=pl.ANY)] * 2
        args = args + list(rolled)
    oa, oc, cs, okt, ovt = pl.pallas_call(
        _mixer_sample_kernel,
        grid=(batch, n_heads // hb),
        in_specs=in_specs,
        out_specs=[pl.BlockSpec((None, hb, head_dim, 1), lambda b, hc: (b, hc, 0, 0)),
                   pl.BlockSpec((1, 1, width), lambda b, hc: (b, 0, 0)),
                   pl.BlockSpec((1, CONV_WIDTH - 1, width), lambda b, hc: (b, 0, 0)),
                   buf_spec, buf_spec],
        out_shape=[jax.ShapeDtypeStruct((batch, n_heads, head_dim, 1), F32),
                   jax.ShapeDtypeStruct((batch, 1, width), F32),
                   jax.ShapeDtypeStruct((batch, CONV_WIDTH - 1, width), F32),
                   jax.ShapeDtypeStruct(kt_state.shape, F32),
                   jax.ShapeDtypeStruct(vt_state.shape, F32)],
        input_output_aliases=aliases,
        compiler_params=_compiler_params(("arbitrary", "arbitrary")),
        name="mixer_sample",
    )(*args)
    return oa, oc, cs, (okt, ovt)


def _outproj_kernel(oa_ref, oc_ref, ga_ref, gc_ref, w_ref, x_ref, o_ref, *rest):
    z_scr = rest[-1]
    wcopy_ref = rest[0] if len(rest) == 2 else None
    n = pl.program_id(1)
    d_attn = oa_ref.shape[1]
    tn = o_ref.shape[1]

    @pl.when(n == 0)
    def _():
        z_scr[:, 0:d_attn] = _rms_rows(oa_ref[...], ga_ref[...]).astype(BF16)
        z_scr[:, d_attn:] = _rms_rows(oc_ref[...], gc_ref[...]).astype(BF16)

    if w_ref.shape[1] == tn:
        w = _bf16_weight(w_ref, wcopy_ref)
    else:
        w = w_ref[:, pl.ds(pl.multiple_of(n * tn, tn), tn)]
    o_ref[...] = x_ref[...] + jnp.dot(z_scr[...], w, preferred_element_type=F32)


def _outproj(oa, oc, attn_gain, conv_gain, w_out, layer, x, *, tm, tn):
    m_rows, d_attn = oa.shape
    d_conv = oc.shape[1]
    d_model = x.shape[1]
    d_mix = d_attn + d_conv
    emit_copy = w_out.ndim == 3
    const = lambda m, n: (0, 0)
    out_specs = [pl.BlockSpec((tm, tn), lambda m, n: (m, n))]
    out_shape = [jax.ShapeDtypeStruct((m_rows, d_model), F32)]
    if emit_copy:
        assert m_rows == tm, "the bf16 copy is written once per column tile"
        w_spec = _weight_spec(w_out, layer, (d_mix, tn), lambda m, n: (0, n))
        out_specs.append(pl.BlockSpec((d_mix, tn), lambda m, n: (0, n)))
        out_shape.append(jax.ShapeDtypeStruct((d_mix, d_model), BF16))
    else:
        w_spec = pl.BlockSpec((d_mix, d_model), const, pipeline_mode=pl.Buffered(1))
    outs = pl.pallas_call(
        _outproj_kernel,
        grid=(m_rows // tm, d_model // tn),
        in_specs=[pl.BlockSpec((tm, d_attn), lambda m, n: (m, 0)),
                  pl.BlockSpec((tm, d_conv), lambda m, n: (m, 0)),
                  pl.BlockSpec((1, d_attn), const),
                  pl.BlockSpec((1, d_conv), const),
                  w_spec,
                  pl.BlockSpec((tm, tn), lambda m, n: (m, n))],
        out_specs=out_specs,
        out_shape=out_shape,
        scratch_shapes=[pltpu.VMEM((tm, d_mix), BF16)],
        compiler_params=_compiler_params(("arbitrary", "arbitrary")),
        name="outproj",
    )(oa, oc, attn_gain.reshape(1, d_attn), conv_gain.reshape(1, d_conv), w_out, x)
    return outs if emit_copy else outs[0]


def _mlp_kernel(x_hbm, g_ref, wu_ref, wd_ref, o_ref, x_buf, h_scr, sem, *, tm):
    m = pl.program_id(0)
    f = pl.program_id(1)

    def x_copy(block):
        rows = pl.ds(pl.multiple_of(block * tm, tm), tm)
        return pltpu.make_async_copy(x_hbm.at[rows], x_buf, sem)

    @pl.when(f == 0)
    def _():
        @pl.when(m == 0)
        def _():
            x_copy(0).start()

        x_copy(m).wait()
        x = x_buf[...]
        h_scr[...] = _rms_rows(x, g_ref[...]).astype(BF16)
        o_ref[...] = x

    @pl.when((f == 1) & (m + 1 < pl.num_programs(0)))
    def _():
        x_copy(m + 1).start()

    a = jnp.dot(h_scr[...], wu_ref[...], preferred_element_type=F32)
    a = jnp.square(jnp.maximum(a, 0.0)).astype(BF16)
    o_ref[...] += jnp.dot(a, wd_ref[...], preferred_element_type=F32)


def _mlp(x, gain, w_up, w_down, *, tm, tf):
    m_rows, d_model = x.shape
    d_ff = w_up.shape[1]
    assert d_ff // tf >= 2, "the next row block's x is fetched on a block's second step"
    return pl.pallas_call(
        functools.partial(_mlp_kernel, tm=tm),
        grid=(m_rows // tm, d_ff // tf),
        in_specs=[pl.BlockSpec(memory_space=pl.ANY),
                  pl.BlockSpec((1, d_model), lambda m, f: (0, 0)),
                  pl.BlockSpec((d_model, tf), lambda m, f: (0, f)),
                  pl.BlockSpec((tf, d_model), lambda m, f: (f, 0))],
        out_specs=pl.BlockSpec((tm, d_model), lambda m, f: (m, 0)),
        out_shape=jax.ShapeDtypeStruct((m_rows, d_model), F32),
        scratch_shapes=[pltpu.VMEM((tm, d_model), F32),
                        pltpu.VMEM((tm, d_model), BF16),
                        pltpu.SemaphoreType.DMA(())],
        compiler_params=_compiler_params(("arbitrary", "arbitrary")),
        name="mlp",
    )(x, gain.reshape(1, d_model), w_up, w_down)


PROMPT_TILES = dict(tm=1024, tn_in=1024, tn_out=1024, tf=1024)
SAMPLE_TILES = dict(tn=1024, tf=1024)


def kernel(x_prompt, x_sample, state_attn_k, state_attn_v, state_conv, rel_bias, norm_mix, w_in,
           q_norm, k_norm, conv_w, attn_out_norm, conv_out_norm, w_out, norm_mlp, w_up, w_down):
    batch, seq, d_model = x_prompt.shape
    dec_batch, dec_seq, _ = x_sample.shape
    depth, _, wb, n_heads, head_dim = state_attn_k.shape
    assert dec_seq == 1 and head_dim == HEAD_DIM and wb == max(w for w, _ in DILATED_CONFIGS)
    assert seq == wb

    bias_p = _prompt_bias_tables(rel_bias)
    bias_s, bias0 = _sample_bias_tables(rel_bias, wb)
    kt_state = state_attn_k.transpose(0, 1, 3, 4, 2)
    vt_state = state_attn_v.transpose(0, 1, 3, 4, 2)
    rolled = None
    new_windows = None

    xp = x_prompt.reshape(batch * seq, d_model)
    xs = x_sample.reshape(dec_batch, d_model)
    pt, st = PROMPT_TILES, SAMPLE_TILES
    cp_new, cs_new = [], []
    w_in_l = w_in
    for l in range(depth):
        proj_s, w_in_l = _inproj(xs, norm_mix[l], w_in, l, q_norm[l], k_norm[l],
                                 tm=dec_batch, tn=st["tn"])
        width = proj_s.shape[-1]
        oa_s, oc_s, cs, rolled = _mixer_sample(l, proj_s, kt_state, vt_state, rolled, bias_s, bias0,
                                               conv_w[l], state_conv)
        cs_new.append(cs)

        proj = _inproj(xp, norm_mix[l], w_in_l, l, q_norm[l], k_norm[l],
                       tm=pt["tm"], tn=pt["tn_in"])
        oa, oc, cs, new_windows, (w_out_l, w_up_l, w_down_l) = _mixer_prompt(
            l, depth, proj.reshape(N_SEGMENTS, batch, seq, width), conv_w[l], bias_p, new_windows,
            [(w_out, l), (w_up, l), (w_down, l)])
        cp_new.append(cs)

        xs = _outproj(oa_s.reshape(dec_batch, width), oc_s.reshape(dec_batch, width),
                      attn_out_norm[l], conv_out_norm[l], w_out_l, l, xs,
                      tm=dec_batch, tn=st["tn"])
        xs = _mlp(xs, norm_mlp[l], w_up_l, w_down_l, tm=dec_batch, tf=st["tf"])
        xp = _outproj(oa.reshape(batch * seq, width), oc.reshape(batch * seq, width),
                      attn_out_norm[l], conv_out_norm[l], w_out_l, l, xp, tm=pt["tm"], tn=pt["tn_out"])
        xp = _mlp(xp, norm_mlp[l], w_up_l, w_down_l, tm=pt["tm"], tf=pt["tf"])

    ks_new, vs_new = (r.transpose(0, 1, 4, 2, 3) for r in rolled)
    kp_new, vp_new = (w.transpose(0, 1, 4, 2, 3) for w in new_windows)
    return (xp.reshape(batch, seq, d_model), xs.reshape(dec_batch, dec_seq, d_model),
            kp_new, vp_new, jnp.stack(cp_new), ks_new, vs_new, jnp.stack(cs_new))
```

```python
import functools

import jax
import jax.numpy as jnp
import numpy as np
from jax import lax
from jax.experimental import pallas as pl
from jax.experimental.pallas import tpu as pltpu

HEAD_DIM = 64
CONV_WIDTH = 3
DILATED_CONFIGS = ((128, 1), (512, 4), (2048, 16))
N_BUCKETS = 32
MAX_DISTANCE = 2048
EPS = 1e-6
ATTN_SCALE = HEAD_DIM ** -0.5
MASKED = -1e30

LANES = 128
MXU_WIDTH = 256
Q_BLOCK = 128
MERGE_ROWS = 256
HEADS_PER_BLOCK = LANES // HEAD_DIM
VMEM_LIMIT_BYTES = 56 * 1024 * 1024

F32 = jnp.float32
BF16 = jnp.bfloat16


def _compiler_params(semantics, flags=None):
    return pltpu.CompilerParams(dimension_semantics=semantics,
                                vmem_limit_bytes=VMEM_LIMIT_BYTES, flags=flags)


def _rms_rows(x, gain):
    ms = jnp.mean(x * x, axis=-1, keepdims=True)
    return (x * lax.rsqrt(ms + EPS)) * gain


N_SEGMENTS = 6
SEG_Q, SEG_K, SEG_V, SEG_VC, SEG_B, SEG_C = range(N_SEGMENTS)


def _bf16_weight(w_ref, copy_ref):
    w = w_ref[...]
    if w.dtype != BF16:
        w = w.astype(BF16)
    if copy_ref is not None:
        copy_ref[...] = w
    return w


def _inproj_kernel(x_ref, g_ref, w_ref, hg_ref, grp_ref, *rest, tiles_per_seg, emit_copy,
                   round_next):
    rest = list(rest)
    wnext_ref = rest.pop(0) if round_next else None
    o_ref = rest.pop(0)
    wcopy_ref = rest.pop(0) if emit_copy else None
    wnext_copy_ref = rest.pop(0) if round_next else None
    h_scr, = rest
    n = pl.program_id(1)

    if round_next:
        wnext_copy_ref[...] = wnext_ref[...].astype(BF16)

    @pl.when(n == 0)
    def _():
        h_scr[...] = _rms_rows(x_ref[...], g_ref[...]).astype(BF16)

    o_ref[0] = jnp.dot(h_scr[...], _bf16_weight(w_ref, wcopy_ref), preferred_element_type=F32)
    seg = n // tiles_per_seg

    @pl.when(seg <= SEG_K)
    def _():
        gain = hg_ref[pl.ds(seg, 1), :]
        chunk = grp_ref.shape[0]
        for j in range(o_ref.shape[2] // chunk):
            cols = slice(j * chunk, (j + 1) * chunk)
            yj = o_ref[0, :, cols]
            ms = jnp.dot((yj * yj).astype(BF16), grp_ref[...], preferred_element_type=F32)
            o_ref[0, :, cols] = (yj * lax.rsqrt(ms + EPS)) * gain[:, cols]


def _weight_spec(w, layer, block, index_map):
    if w.ndim == 3:
        return pl.BlockSpec((None,) + block, lambda *ids: (layer,) + index_map(*ids))
    return pl.BlockSpec(block, index_map)


def _inproj(x, gain, w_in, layer, q_gain, k_gain, *, tm, tn, w_next=None):
    m_rows, d_model = x.shape
    d_in = w_in.shape[-1]
    emit_copy = w_in.ndim == 3
    seg_w = d_in // N_SEGMENTS
    tiles_per_seg = seg_w // tn
    grp = jnp.asarray(
        (np.arange(MXU_WIDTH)[:, None] // HEAD_DIM == np.arange(MXU_WIDTH)[None, :] // HEAD_DIM)
        .astype(np.float32) / HEAD_DIM, BF16)
    head_gains = jnp.stack([jnp.tile(q_gain, tn // HEAD_DIM), jnp.tile(k_gain, tn // HEAD_DIM)])
    assert (SEG_Q, SEG_K) == (0, 1)
    const = lambda m, n: (0, 0)
    out_specs = [pl.BlockSpec(
        (1, tm, tn), lambda m, n: (n // tiles_per_seg, m, n % tiles_per_seg))]
    out_shape = [jax.ShapeDtypeStruct((N_SEGMENTS, m_rows, seg_w), F32)]
    if emit_copy:
        assert m_rows == tm, "the bf16 copy is written once per column tile"
        out_specs.append(pl.BlockSpec((d_model, tn), lambda m, n: (0, n)))
        out_shape.append(jax.ShapeDtypeStruct((d_model, d_in), BF16))
    in_specs = [pl.BlockSpec((tm, d_model), lambda m, n: (m, 0)),
                pl.BlockSpec((1, d_model), const),
                _weight_spec(w_in, layer, (d_model, tn), lambda m, n: (0, n)),
                pl.BlockSpec((2, tn), const),
                pl.BlockSpec((MXU_WIDTH, MXU_WIDTH), const)]
    args = [x, gain.reshape(1, d_model), w_in, head_gains, grp]
    if w_next is not None:
        w_stack, next_layer = w_next
        rows = d_model // (m_rows // tm)
        in_specs.append(pl.BlockSpec((None, rows, tn), lambda m, n: (next_layer, m, n)))
        args.append(w_stack)
        out_specs.append(pl.BlockSpec((rows, tn), lambda m, n: (m, n)))
        out_shape.append(jax.ShapeDtypeStruct((d_model, d_in), BF16))
    outs = pl.pallas_call(
        functools.partial(_inproj_kernel, tiles_per_seg=tiles_per_seg, emit_copy=emit_copy,
                          round_next=w_next is not None),
        grid=(m_rows // tm, d_in // tn),
        in_specs=in_specs,
        out_specs=out_specs,
        out_shape=out_shape,
        scratch_shapes=[pltpu.VMEM((tm, d_model), BF16)],
        compiler_params=_compiler_params(("arbitrary", "arbitrary")),
        name="inproj",
    )(*args)
    return outs if len(outs) > 1 else outs[0]


def _t5_bucket(dist):
    n_exact = N_BUCKETS // 2
    large = n_exact + (np.log(np.maximum(dist, 1) / n_exact) / np.log(MAX_DISTANCE / n_exact)
                       * (N_BUCKETS - n_exact)).astype(np.int32)
    large = np.minimum(large, N_BUCKETS - 1)
    return np.where(dist < n_exact, dist, large).astype(np.int32)


def _bias_lookup(rel_bias, buckets):
    onehot = buckets[..., None] == np.arange(rel_bias.shape[0])
    return jnp.sum(jnp.where(onehot[..., None], rel_bias, 0.0), axis=-2)


def _prompt_bias_tables(rel_bias):
    n_heads = rel_bias.shape[1]
    tables = []
    for window, dil in DILATED_CONFIGS:
        nw = window // dil
        assert nw == Q_BLOCK
        bias_k = _bias_lookup(rel_bias, _t5_bucket(np.arange(nw + 1) * dil))
        pad = jnp.full((Q_BLOCK - 1, n_heads), MASKED, F32)
        u = jnp.concatenate([pad, bias_k[::-1], pad, pad[:1]])
        row = 3 * Q_BLOCK - 1
        skew = jnp.tile(u, (Q_BLOCK, 1))[:Q_BLOCK * row].reshape(Q_BLOCK, row, n_heads)
        t = skew[:, Q_BLOCK - 1:3 * Q_BLOCK - 1].transpose(2, 0, 1)
        tables.append(t.reshape(n_heads // HEADS_PER_BLOCK, HEADS_PER_BLOCK * Q_BLOCK,
                                2 * Q_BLOCK))
    return jnp.stack(tables)


def _mixer_prompt_kernel(qkv_ref, conv_ref, cw_ref, bias_ref, *rest, seq, n_aliased, n_round):
    q_ref, k_ref, v_ref = (qkv_ref.at[i] for i in range(3))
    vc_ref, gb_ref, gc_ref = (conv_ref.at[i] for i in range(3))
    w_refs = rest[:n_round]
    rest = rest[n_round + n_aliased:]
    oa_ref, oc_ref, cs_ref, okt_ref, ovt_ref = rest[:5]
    wcopy_refs = rest[5:5 + n_round]
    qs_scr, kd_scr, v0_scr, v1_scr, o_scr, m_scr, d_scr, u_scr = rest[5 + n_round:]

    for w_ref, wcopy_ref in zip(w_refs, wcopy_refs):
        wcopy_ref[...] = w_ref[...].astype(BF16)

    for src, dst in ((k_ref, okt_ref), (v_ref, ovt_ref)):
        t = src[0].T
        for h in range(HEADS_PER_BLOCK):
            dst[h] = t[h * HEAD_DIM:(h + 1) * HEAD_DIM]

    nblk_total = seq // Q_BLOCK
    lane = lax.broadcasted_iota(jnp.int32, (Q_BLOCK, LANES), 1)
    head0 = lane < HEAD_DIM

    for c, (window, dil) in enumerate(DILATED_CONFIGS):
        sub_len = seq // dil
        blocks_per_sub = sub_len // Q_BLOCK

        for r in range(dil):
            rows = pl.ds(r, sub_len, stride=dil) if dil > 1 else pl.ds(0, seq)
            head0_rows = lax.broadcasted_iota(jnp.int32, (sub_len, LANES), 1) < HEAD_DIM
            qv = (q_ref[0, rows, :] * ATTN_SCALE).astype(BF16)
            kv = k_ref[0, rows, :].astype(BF16)
            vv = v_ref[0, rows, :].astype(BF16)
            zero = jnp.zeros_like(qv)
            q0 = jnp.where(head0_rows, qv, zero)
            q1 = jnp.where(head0_rows, zero, qv)
            for gb in range(blocks_per_sub):
                g = r * blocks_per_sub + gb
                src = slice(gb * Q_BLOCK, (gb + 1) * Q_BLOCK)
                qs_scr[c, 2 * g * Q_BLOCK:(2 * g + 1) * Q_BLOCK, :] = q0[src]
                qs_scr[c, (2 * g + 1) * Q_BLOCK:(2 * g + 2) * Q_BLOCK, :] = q1[src]
            dst = slice(r * sub_len, (r + 1) * sub_len)
            kd_scr[c, dst, :] = kv
            v0_scr[c, dst, :] = jnp.where(head0_rows, vv, zero)
            v1_scr[c, dst, :] = jnp.where(head0_rows, zero, vv)

        for g in range(nblk_total):
            r, gb = divmod(g, blocks_per_sub)
            lhs = qs_scr[c, 2 * g * Q_BLOCK:(2 * g + 2) * Q_BLOCK, :]
            if gb == 0:
                win = slice(g * Q_BLOCK, (g + 1) * Q_BLOCK)
                bias = bias_ref[c, 0, :, Q_BLOCK:]
            else:
                win = slice((g - 1) * Q_BLOCK, (g + 1) * Q_BLOCK)
                bias = bias_ref[c, 0]
            logits = lax.dot_general(lhs, kd_scr[c, win, :], (((1,), (1,)), ((), ())),
                                     preferred_element_type=F32) + bias
            m = jnp.max(logits, axis=-1, keepdims=True)
            p = jnp.exp(logits - m)
            den = jnp.sum(p, axis=-1, keepdims=True)
            pb = p.astype(BF16)
            o = (jnp.dot(pb[0:Q_BLOCK], v0_scr[c, win, :], preferred_element_type=F32)
                 + jnp.dot(pb[Q_BLOCK:], v1_scr[c, win, :], preferred_element_type=F32))
            m2 = jnp.where(head0, m[0:Q_BLOCK], m[Q_BLOCK:])
            d2 = jnp.where(head0, den[0:Q_BLOCK], den[Q_BLOCK:])
            if dil > 1:
                out_rows = pl.ds(gb * Q_BLOCK * dil + r, Q_BLOCK, stride=dil)
            else:
                out_rows = pl.ds(g * Q_BLOCK, Q_BLOCK)
            o_scr[c, out_rows, :] = o
            m_scr[c, out_rows, :] = m2
            d_scr[c, out_rows, :] = d2

    n_br = len(DILATED_CONFIGS)

    def merge_rows(i, carry):
        rows = pl.ds(pl.multiple_of(i * MERGE_ROWS, MERGE_ROWS), MERGE_ROWS)
        mmax = m_scr[0, rows, :]
        for c in range(1, n_br):
            mmax = jnp.maximum(mmax, m_scr[c, rows, :])
        num = jnp.zeros((MERGE_ROWS, LANES), F32)
        den = jnp.zeros((MERGE_ROWS, LANES), F32)
        for c in range(n_br):
            w = jnp.exp(m_scr[c, rows, :] - mmax)
            num = num + w * o_scr[c, rows, :]
            den = den + w * d_scr[c, rows, :]
        oa_ref[0, rows, :] = num / den
        return carry

    lax.fori_loop(0, seq // MERGE_ROWS, merge_rows, 0)

    u = gc_ref[0] * vc_ref[0]
    u_scr[0:8, :] = jnp.zeros((8, LANES), F32)
    u_scr[8:8 + seq, :] = u
    y = cw_ref[0:1, :] * u_scr[pl.ds(8 - (CONV_WIDTH - 1), seq), :]
    for i in range(1, CONV_WIDTH):
        y = y + cw_ref[i:i + 1, :] * u_scr[pl.ds(8 - (CONV_WIDTH - 1) + i, seq), :]
    oc_ref[0] = gb_ref[0] * y
    cs_ref[0] = u_scr[pl.ds(8 + seq - (CONV_WIDTH - 1), CONV_WIDTH - 1), :]


def _mixer_prompt(layer, depth, proj, conv_w_l, bias_tables, new_windows, to_round):
    _, batch, seq, width = proj.shape
    n_pairs = width // LANES
    n_heads = width // HEAD_DIM
    n_br = len(DILATED_CONFIGS)
    tok = pl.BlockSpec((1, seq, LANES), lambda b, hp: (b, 0, hp))
    assert (SEG_Q, SEG_K, SEG_V, SEG_VC, SEG_B, SEG_C) == tuple(range(6))
    seg_specs = [pl.BlockSpec((3, 1, seq, LANES), lambda b, hp, s=s: (s, b, 0, hp))
                 for s in range(N_SEGMENTS // 3)]
    win_spec = pl.BlockSpec((None, None, HEADS_PER_BLOCK, HEAD_DIM, seq),
                            lambda b, hp: (layer, b, hp, 0, 0))
    win_shape = jax.ShapeDtypeStruct((depth, batch, n_heads, HEAD_DIM, seq), F32)
    in_specs = seg_specs + [
        pl.BlockSpec((CONV_WIDTH, LANES), lambda b, hp: (0, hp)),
        pl.BlockSpec((n_br, 1, 2 * Q_BLOCK, 2 * Q_BLOCK), lambda b, hp: (0, hp, 0, 0))]
    args = [proj] * (N_SEGMENTS // 3) + [conv_w_l, bias_tables]
    n_steps = batch * n_pairs
    copy_specs, copy_shapes = [], []
    for w, w_layer in to_round:
        _, w_rows, w_cols = w.shape
        rows = w_rows // n_steps
        in_specs.append(pl.BlockSpec((None, rows, w_cols),
                                     lambda b, hp, w_layer=w_layer: (w_layer, b * n_pairs + hp, 0)))
        args.append(w)
        copy_specs.append(pl.BlockSpec((rows, w_cols), lambda b, hp: (b * n_pairs + hp, 0)))
        copy_shapes.append(jax.ShapeDtypeStruct((w_rows, w_cols), BF16))
    aliases = {}
    if new_windows is not None:
        aliases = {len(args): 3, len(args) + 1: 4}
        in_specs = in_specs + [pl.BlockSpec(memory_space=pl.ANY)] * 2
        args = args + list(new_windows)
    oa, oc, cs, okt, ovt, *copies = pl.pallas_call(
        functools.partial(_mixer_prompt_kernel, seq=seq, n_aliased=len(aliases),
                          n_round=len(to_round)),
        grid=(batch, n_pairs),
        in_specs=in_specs,
        out_specs=[tok, tok,
                   pl.BlockSpec((1, CONV_WIDTH - 1, LANES), lambda b, hp: (b, 0, hp)),
                   win_spec, win_spec] + copy_specs,
        out_shape=[jax.ShapeDtypeStruct((batch, seq, width), F32),
                   jax.ShapeDtypeStruct((batch, seq, width), F32),
                   jax.ShapeDtypeStruct((batch, CONV_WIDTH - 1, width), F32),
                   win_shape, win_shape] + copy_shapes,
        input_output_aliases=aliases,
        scratch_shapes=[pltpu.VMEM((n_br, 2 * seq, LANES), BF16),
                        pltpu.VMEM((n_br, seq, LANES), BF16),
                        pltpu.VMEM((n_br, seq, LANES), BF16),
                        pltpu.VMEM((n_br, seq, LANES), BF16),
                        pltpu.VMEM((n_br, seq, LANES), F32),
                        pltpu.VMEM((n_br, seq, LANES), F32),
                        pltpu.VMEM((n_br, seq, LANES), F32),
                        pltpu.VMEM((seq + 8, LANES), F32)],
        compiler_params=_compiler_params(("arbitrary", "arbitrary")),
        name="mixer_prompt",
    )(*args)
    return oa, oc, cs, (okt, ovt), copies


SAMPLE_HEADS_PER_STEP = 8


def _sample_bias_tables(rel_bias, wb):
    dist = wb - np.arange(wb)
    tables = []
    for window, dil in DILATED_CONFIGS:
        in_branch = (dist % dil == 0) & (dist // dil <= window // dil)
        bias = jnp.where(in_branch[:, None], _bias_lookup(rel_bias, _t5_bucket(dist)), MASKED)
        tables.append(bias.T)
    own = rel_bias[_t5_bucket(np.zeros((1,), np.int64))][0]
    return jnp.stack(tables), own[:, None]


def _mixer_sample_kernel(q_ref, kn_ref, vn_ref, kt_ref, vt_ref, bias_ref, bias0_ref,
                         vc_ref, gb_ref, gc_ref, cw_ref, sc_ref, *rest):
    oa_ref, oc_ref, cs_ref, okt_ref, ovt_ref = rest[-5:]
    heads, head_dim, wb = kt_ref.shape
    newest = lax.broadcasted_iota(jnp.int32, (head_dim, wb), 1) == wb - 1
    n_br = len(DILATED_CONFIGS)

    s = jnp.concatenate([jnp.sum(kt_ref[h] * q_ref[h], axis=0, keepdims=True)
                         for h in range(heads)], axis=0) * ATTN_SCALE
    l_new = jnp.concatenate([jnp.sum(q_ref[h] * kn_ref[h], axis=0, keepdims=True)
                             for h in range(heads)], axis=0) * ATTN_SCALE + bias0_ref[...]
    branches = []
    for c in range(n_br):
        logits = s + bias_ref[c]
        m = jnp.maximum(jnp.max(logits, axis=-1, keepdims=True), l_new)
        p = jnp.exp(logits - m)
        p_new = jnp.exp(l_new - m)
        den = jnp.sum(p, axis=-1, keepdims=True) + p_new
        branches.append((p, p_new, m, den))
    mmax = branches[0][2]
    for br in branches[1:]:
        mmax = jnp.maximum(mmax, br[2])
    p_all = jnp.zeros_like(s)
    p_new_all = jnp.zeros_like(mmax)
    den_all = jnp.zeros_like(mmax)
    for p, p_new, m, den in branches:
        w = jnp.exp(m - mmax)
        p_all = p_all + w * p
        p_new_all = p_new_all + w * p_new
        den_all = den_all + w * den

    for h in range(heads):
        kt = kt_ref[h]
        vt = vt_ref[h]
        k_new, v_new = kn_ref[h], vn_ref[h]
        num = (jnp.sum(vt * p_all[h:h + 1, :], axis=-1, keepdims=True)
               + p_new_all[h:h + 1, :] * v_new)
        oa_ref[h] = num / den_all[h:h + 1, :]
        okt_ref[h] = jnp.where(newest, k_new, pltpu.roll(kt, wb - 1, axis=1))
        ovt_ref[h] = jnp.where(newest, v_new, pltpu.roll(vt, wb - 1, axis=1))

    @pl.when(pl.program_id(1) == 0)
    def _():
        u = gc_ref[0] * vc_ref[0]
        state = sc_ref[0]
        y = cw_ref[0:1, :] * state[0:1]
        for i in range(1, CONV_WIDTH - 1):
            y = y + cw_ref[i:i + 1, :] * state[i:i + 1]
        y = y + cw_ref[CONV_WIDTH - 1:CONV_WIDTH, :] * u
        oc_ref[0] = gb_ref[0] * y
        cs_ref[0] = jnp.concatenate([state[1:], u], axis=0)


def _mixer_sample(layer, proj, kt_state, vt_state, rolled, bias_s, bias0, conv_w_l, state_conv):
    depth, batch, n_heads, head_dim, wb = kt_state.shape
    width = proj.shape[-1]
    hb = SAMPLE_HEADS_PER_STEP
    n_br = len(DILATED_CONFIGS)
    qkv = proj[SEG_Q:SEG_V + 1].reshape(3, batch, n_heads, head_dim, 1)
    rows = proj.reshape(N_SEGMENTS, batch, 1, width)
    col_specs = [pl.BlockSpec((None, None, hb, head_dim, 1), lambda b, hc, s=s: (s, b, hc, 0, 0))
                 for s in range(3)]
    row_specs = [pl.BlockSpec((None, 1, 1, width), lambda b, hc, s=s: (s, b, 0, 0))
                 for s in (SEG_VC, SEG_B, SEG_C)]
    buf_spec = pl.BlockSpec((None, None, hb, head_dim, wb), lambda b, hc: (layer, b, hc, 0, 0))
    in_specs = col_specs + [
        buf_spec, buf_spec,
        pl.BlockSpec((n_br, hb, wb), lambda b, hc: (0, hc, 0)),
        pl.BlockSpec((hb, 1), lambda b, hc: (hc, 0))] + row_specs + [
        pl.BlockSpec((CONV_WIDTH, width), lambda b, hc: (0, 0)),
        pl.BlockSpec((1, CONV_WIDTH - 1, width), lambda b, hc: (layer * batch + b, 0, 0))]
    args = [qkv, qkv, qkv, kt_state, vt_state, bias_s, bias0, rows, rows, rows, conv_w_l,
            state_conv.reshape(depth * batch, CONV_WIDTH - 1, width)]
    aliases = {}
    if rolled is not None:
        aliases = {len(args): 3, len(args) + 1: 4}
        in_specs = in_specs + [pl.BlockSpec(memory_space=pl.ANY)] * 2
        args = args + list(rolled)
    oa, oc, cs, okt, ovt = pl.pallas_call(
        _mixer_sample_kernel,
        grid=(batch, n_heads // hb),
        in_specs=in_specs,
        out_specs=[pl.BlockSpec((None, hb, head_dim, 1), lambda b, hc: (b, hc, 0, 0)),
                   pl.BlockSpec((1, 1, width), lambda b, hc: (b, 0, 0)),
                   pl.BlockSpec((1, CONV_WIDTH - 1, width), lambda b, hc: (b, 0, 0)),
                   buf_spec, buf_spec],
        out_shape=[jax.ShapeDtypeStruct((batch, n_heads, head_dim, 1), F32),
                   jax.ShapeDtypeStruct((batch, 1, width), F32),
                   jax.ShapeDtypeStruct((batch, CONV_WIDTH - 1, width), F32),
                   jax.ShapeDtypeStruct(kt_state.shape, F32),
                   jax.ShapeDtypeStruct(vt_state.shape, F32)],
        input_output_aliases=aliases,
        compiler_params=_compiler_params(("arbitrary", "arbitrary")),
        name="mixer_sample",
    )(*args)
    return oa, oc, cs, (okt, ovt)


def _outproj_kernel(oa_ref, oc_ref, ga_ref, gc_ref, w_ref, x_ref, o_ref, z_scr):
    n = pl.program_id(1)
    d_attn = oa_ref.shape[1]
    tn = o_ref.shape[1]

    @pl.when(n == 0)
    def _():
        z_scr[:, 0:d_attn] = _rms_rows(oa_ref[...], ga_ref[...]).astype(BF16)
        z_scr[:, d_attn:] = _rms_rows(oc_ref[...], gc_ref[...]).astype(BF16)

    w = w_ref[:, pl.ds(pl.multiple_of(n * tn, tn), tn)]
    o_ref[...] = x_ref[...] + jnp.dot(z_scr[...], w, preferred_element_type=F32)


def _outproj(oa, oc, attn_gain, conv_gain, w_out, x, *, tm, tn):
    m_rows, d_attn = oa.shape
    d_conv = oc.shape[1]
    d_model = x.shape[1]
    d_mix = d_attn + d_conv
    const = lambda m, n: (0, 0)
    return pl.pallas_call(
        _outproj_kernel,
        grid=(m_rows // tm, d_model // tn),
        in_specs=[pl.BlockSpec((tm, d_attn), lambda m, n: (m, 0)),
                  pl.BlockSpec((tm, d_conv), lambda m, n: (m, 0)),
                  pl.BlockSpec((1, d_attn), const),
                  pl.BlockSpec((1, d_conv), const),
                  pl.BlockSpec((d_mix, d_model), const, pipeline_mode=pl.Buffered(1)),
                  pl.BlockSpec((tm, tn), lambda m, n: (m, n))],
        out_specs=pl.BlockSpec((tm, tn), lambda m, n: (m, n)),
        out_shape=jax.ShapeDtypeStruct((m_rows, d_model), F32),
        scratch_shapes=[pltpu.VMEM((tm, d_mix), BF16)],
        compiler_params=_compiler_params(("arbitrary", "arbitrary")),
        name="outproj",
    )(oa, oc, attn_gain.reshape(1, d_attn), conv_gain.reshape(1, d_conv), w_out, x)


def _mlp_kernel(x_hbm, g_ref, wu_ref, wd_ref, o_ref, x_buf, h_scr, sem, *, tm):
    m = pl.program_id(0)
    f = pl.program_id(1)

    def x_copy(block):
        rows = pl.ds(pl.multiple_of(block * tm, tm), tm)
        return pltpu.make_async_copy(x_hbm.at[rows], x_buf, sem)

    @pl.when(f == 0)
    def _():
        @pl.when(m == 0)
        def _():
            x_copy(0).start()

        x_copy(m).wait()
        x = x_buf[...]
        h_scr[...] = _rms_rows(x, g_ref[...]).astype(BF16)
        o_ref[...] = x

    @pl.when((f == 1) & (m + 1 < pl.num_programs(0)))
    def _():
        x_copy(m + 1).start()

    a = jnp.dot(h_scr[...], wu_ref[...], preferred_element_type=F32)
    a = jnp.square(jnp.maximum(a, 0.0)).astype(BF16)
    o_ref[...] += jnp.dot(a, wd_ref[...], preferred_element_type=F32)


def _mlp(x, gain, w_up, w_down, *, tm, tf):
    m_rows, d_model = x.shape
    d_ff = w_up.shape[1]
    assert d_ff // tf >= 2, "the next row block's x is fetched on a block's second step"
    return pl.pallas_call(
        functools.partial(_mlp_kernel, tm=tm),
        grid=(m_rows // tm, d_ff // tf),
        in_specs=[pl.BlockSpec(memory_space=pl.ANY),
                  pl.BlockSpec((1, d_model), lambda m, f: (0, 0)),
                  pl.BlockSpec((d_model, tf), lambda m, f: (0, f)),
                  pl.BlockSpec((tf, d_model), lambda m, f: (f, 0))],
        out_specs=pl.BlockSpec((tm, d_model), lambda m, f: (m, 0)),
        out_shape=jax.ShapeDtypeStruct((m_rows, d_model), F32),
        scratch_shapes=[pltpu.VMEM((tm, d_model), F32),
                        pltpu.VMEM((tm, d_model), BF16),
                        pltpu.SemaphoreType.DMA(())],
        compiler_params=_compiler_params(("arbitrary", "arbitrary")),
        name="mlp",
    )(x, gain.reshape(1, d_model), w_up, w_down)


PROMPT_TILES = dict(tm=1024, tn_in=1024, tn_out=1024, tf=1024)
SAMPLE_TILES = dict(tn=1024, tf=1024)


def kernel(x_prompt, x_sample, state_attn_k, state_attn_v, state_conv, rel_bias, norm_mix, w_in,
           q_norm, k_norm, conv_w, attn_out_norm, conv_out_norm, w_out, norm_mlp, w_up, w_down):
    batch, seq, d_model = x_prompt.shape
    dec_batch, dec_seq, _ = x_sample.shape
    depth, _, wb, n_heads, head_dim = state_attn_k.shape
    assert dec_seq == 1 and head_dim == HEAD_DIM and wb == max(w for w, _ in DILATED_CONFIGS)
    assert seq == wb

    bias_p = _prompt_bias_tables(rel_bias)
    bias_s, bias0 = _sample_bias_tables(rel_bias, wb)
    kt_state = state_attn_k.transpose(0, 1, 3, 4, 2)
    vt_state = state_attn_v.transpose(0, 1, 3, 4, 2)
    rolled = None
    new_windows = None

    xp = x_prompt.reshape(batch * seq, d_model)
    xs = x_sample.reshape(dec_batch, d_model)
    pt, st = PROMPT_TILES, SAMPLE_TILES
    cp_new, cs_new = [], []
    w_in_l = w_in
    for l in range(depth):
        proj_s = _inproj(xs, norm_mix[l], w_in_l, l, q_norm[l], k_norm[l],
                         tm=dec_batch, tn=st["tn"])
        if l == 0:
            proj_s, w_in_l = proj_s
        width = proj_s.shape[-1]
        oa_s, oc_s, cs, rolled = _mixer_sample(l, proj_s, kt_state, vt_state, rolled, bias_s, bias0,
                                               conv_w[l], state_conv)
        cs_new.append(cs)

        if l + 1 < depth:
            proj, w_in_l = _inproj(xp, norm_mix[l], w_in_l, l, q_norm[l], k_norm[l],
                                   tm=pt["tm"], tn=pt["tn_in"], w_next=(w_in, l + 1))
        else:
            proj = _inproj(xp, norm_mix[l], w_in_l, l, q_norm[l], k_norm[l],
                           tm=pt["tm"], tn=pt["tn_in"])
        oa, oc, cs, new_windows, (w_out_l, w_up_l, w_down_l) = _mixer_prompt(
            l, depth, proj.reshape(N_SEGMENTS, batch, seq, width), conv_w[l], bias_p, new_windows,
            [(w_out, l), (w_up, l), (w_down, l)])
        cp_new.append(cs)

        xs = _outproj(oa_s.reshape(dec_batch, width), oc_s.reshape(dec_batch, width),
                      attn_out_norm[l], conv_out_norm[l], w_out_l, xs, tm=dec_batch, tn=st["tn"])
        xs = _mlp(xs, norm_mlp[l], w_up_l, w_down_l, tm=dec_batch, tf=st["tf"])
        xp = _outproj(oa.reshape(batch * seq, width), oc.reshape(batch * seq, width),
                      attn_out_norm[l], conv_out_norm[l], w_out_l, xp, tm=pt["tm"], tn=pt["tn_out"])
        xp = _mlp(xp, norm_mlp[l], w_up_l, w_down_l, tm=pt["tm"], tf=pt["tf"])

    ks_new, vs_new = (r.transpose(0, 1, 4, 2, 3) for r in rolled)
    kp_new, vp_new = (w.transpose(0, 1, 4, 2, 3) for w in new_windows)
    return (xp.reshape(batch, seq, d_model), xs.reshape(dec_batch, dec_seq, d_model),
            kp_new, vp_new, jnp.stack(cp_new), ks_new, vs_new, jnp.stack(cs_new))
```

```python
import functools

import jax
import jax.numpy as jnp
import numpy as np
from jax import lax
from jax.experimental import pallas as pl
from jax.experimental.pallas import tpu as pltpu

HEAD_DIM = 64
CONV_WIDTH = 3
DILATED_CONFIGS = ((128, 1), (512, 4), (2048, 16))
N_BUCKETS = 32
MAX_DISTANCE = 2048
EPS = 1e-6
ATTN_SCALE = HEAD_DIM ** -0.5
MASKED = -1e30

LANES = 128
MXU_WIDTH = 256
Q_BLOCK = 128
MERGE_ROWS = 256
HEADS_PER_BLOCK = LANES // HEAD_DIM
VMEM_LIMIT_BYTES = 56 * 1024 * 1024

F32 = jnp.float32
BF16 = jnp.bfloat16


def _compiler_params(semantics, flags=None):
    return pltpu.CompilerParams(dimension_semantics=semantics,
                                vmem_limit_bytes=VMEM_LIMIT_BYTES, flags=flags)


def _rms_rows(x, gain):
    ms = jnp.mean(x * x, axis=-1, keepdims=True)
    return (x * lax.rsqrt(ms + EPS)) * gain


N_SEGMENTS = 6
SEG_Q, SEG_K, SEG_V, SEG_VC, SEG_B, SEG_C = range(N_SEGMENTS)


def _bf16_weight(w_ref, copy_ref):
    w = w_ref[...]
    if w.dtype != BF16:
        w = w.astype(BF16)
    if copy_ref is not None:
        copy_ref[...] = w
    return w


def _inproj_kernel(x_hbm, g_ref, w_ref, hg_ref, grp_ref, *rest, tm, tiles_per_seg, emit_copy,
                   round_next):
    rest = list(rest)
    wnext_ref = rest.pop(0) if round_next else None
    o_ref = rest.pop(0)
    wcopy_ref = rest.pop(0) if emit_copy else None
    wnext_copy_ref = rest.pop(0) if round_next else None
    x_buf, h_scr, sem = rest
    m = pl.program_id(0)
    n = pl.program_id(1)

    def x_copy(block):
        rows = pl.ds(pl.multiple_of(block * tm, tm), tm)
        return pltpu.make_async_copy(x_hbm.at[rows], x_buf, sem)

    if round_next:
        wnext_copy_ref[...] = wnext_ref[...].astype(BF16)

    @pl.when(n == 0)
    def _():
        @pl.when(m == 0)
        def _():
            x_copy(0).start()

        x_copy(m).wait()
        h_scr[...] = _rms_rows(x_buf[...], g_ref[...]).astype(BF16)

    @pl.when((n == 1) & (m + 1 < pl.num_programs(0)))
    def _():
        x_copy(m + 1).start()

    o_ref[0] = jnp.dot(h_scr[...], _bf16_weight(w_ref, wcopy_ref), preferred_element_type=F32)
    seg = n // tiles_per_seg

    @pl.when(seg <= SEG_K)
    def _():
        gain = hg_ref[pl.ds(seg, 1), :]
        chunk = grp_ref.shape[0]
        for j in range(o_ref.shape[2] // chunk):
            cols = slice(j * chunk, (j + 1) * chunk)
            yj = o_ref[0, :, cols]
            ms = jnp.dot((yj * yj).astype(BF16), grp_ref[...], preferred_element_type=F32)
            o_ref[0, :, cols] = (yj * lax.rsqrt(ms + EPS)) * gain[:, cols]


def _weight_spec(w, layer, block, index_map):
    if w.ndim == 3:
        return pl.BlockSpec((None,) + block, lambda *ids: (layer,) + index_map(*ids))
    return pl.BlockSpec(block, index_map)


def _inproj(x, gain, w_in, layer, q_gain, k_gain, *, tm, tn, w_next=None):
    m_rows, d_model = x.shape
    d_in = w_in.shape[-1]
    emit_copy = w_in.ndim == 3
    seg_w = d_in // N_SEGMENTS
    tiles_per_seg = seg_w // tn
    grp = jnp.asarray(
        (np.arange(MXU_WIDTH)[:, None] // HEAD_DIM == np.arange(MXU_WIDTH)[None, :] // HEAD_DIM)
        .astype(np.float32) / HEAD_DIM, BF16)
    head_gains = jnp.stack([jnp.tile(q_gain, tn // HEAD_DIM), jnp.tile(k_gain, tn // HEAD_DIM)])
    assert (SEG_Q, SEG_K) == (0, 1)
    const = lambda m, n: (0, 0)
    out_specs = [pl.BlockSpec(
        (1, tm, tn), lambda m, n: (n // tiles_per_seg, m, n % tiles_per_seg))]
    out_shape = [jax.ShapeDtypeStruct((N_SEGMENTS, m_rows, seg_w), F32)]
    if emit_copy:
        assert m_rows == tm, "the bf16 copy is written once per column tile"
        out_specs.append(pl.BlockSpec((d_model, tn), lambda m, n: (0, n)))
        out_shape.append(jax.ShapeDtypeStruct((d_model, d_in), BF16))
    assert d_in // tn >= 2, "the next row block's x is fetched on a block's second step"
    in_specs = [pl.BlockSpec(memory_space=pl.ANY),
                pl.BlockSpec((1, d_model), const),
                _weight_spec(w_in, layer, (d_model, tn), lambda m, n: (0, n)),
                pl.BlockSpec((2, tn), const),
                pl.BlockSpec((MXU_WIDTH, MXU_WIDTH), const)]
    args = [x, gain.reshape(1, d_model), w_in, head_gains, grp]
    if w_next is not None:
        w_stack, next_layer = w_next
        rows = d_model // (m_rows // tm)
        in_specs.append(pl.BlockSpec((None, rows, tn), lambda m, n: (next_layer, m, n)))
        args.append(w_stack)
        out_specs.append(pl.BlockSpec((rows, tn), lambda m, n: (m, n)))
        out_shape.append(jax.ShapeDtypeStruct((d_model, d_in), BF16))
    outs = pl.pallas_call(
        functools.partial(_inproj_kernel, tm=tm, tiles_per_seg=tiles_per_seg,
                          emit_copy=emit_copy, round_next=w_next is not None),
        grid=(m_rows // tm, d_in // tn),
        in_specs=in_specs,
        out_specs=out_specs,
        out_shape=out_shape,
        scratch_shapes=[pltpu.VMEM((tm, d_model), F32),
                        pltpu.VMEM((tm, d_model), BF16),
                        pltpu.SemaphoreType.DMA(())],
        compiler_params=_compiler_params(("arbitrary", "arbitrary")),
        name="inproj",
    )(*args)
    return outs if len(outs) > 1 else outs[0]


def _t5_bucket(dist):
    n_exact = N_BUCKETS // 2
    large = n_exact + (np.log(np.maximum(dist, 1) / n_exact) / np.log(MAX_DISTANCE / n_exact)
                       * (N_BUCKETS - n_exact)).astype(np.int32)
    large = np.minimum(large, N_BUCKETS - 1)
    return np.where(dist < n_exact, dist, large).astype(np.int32)


def _bias_lookup(rel_bias, buckets):
    onehot = buckets[..., None] == np.arange(rel_bias.shape[0])
    return jnp.sum(jnp.where(onehot[..., None], rel_bias, 0.0), axis=-2)


def _prompt_bias_tables(rel_bias):
    n_heads = rel_bias.shape[1]
    tables = []
    for window, dil in DILATED_CONFIGS:
        nw = window // dil
        assert nw == Q_BLOCK
        bias_k = _bias_lookup(rel_bias, _t5_bucket(np.arange(nw + 1) * dil))
        pad = jnp.full((Q_BLOCK - 1, n_heads), MASKED, F32)
        u = jnp.concatenate([pad, bias_k[::-1], pad, pad[:1]])
        row = 3 * Q_BLOCK - 1
        skew = jnp.tile(u, (Q_BLOCK, 1))[:Q_BLOCK * row].reshape(Q_BLOCK, row, n_heads)
        t = skew[:, Q_BLOCK - 1:3 * Q_BLOCK - 1].transpose(2, 0, 1)
        tables.append(t.reshape(n_heads // HEADS_PER_BLOCK, HEADS_PER_BLOCK * Q_BLOCK,
                                2 * Q_BLOCK))
    return jnp.stack(tables)


def _mixer_prompt_kernel(qkv_ref, conv_ref, cw_ref, bias_ref, *rest, seq, n_aliased, n_round):
    q_ref, k_ref, v_ref = (qkv_ref.at[i] for i in range(3))
    vc_ref, gb_ref, gc_ref = (conv_ref.at[i] for i in range(3))
    w_refs = rest[:n_round]
    rest = rest[n_round + n_aliased:]
    oa_ref, oc_ref, cs_ref, okt_ref, ovt_ref = rest[:5]
    wcopy_refs = rest[5:5 + n_round]
    qs_scr, kd_scr, v0_scr, v1_scr, o_scr, m_scr, d_scr, u_scr = rest[5 + n_round:]

    for w_ref, wcopy_ref in zip(w_refs, wcopy_refs):
        wcopy_ref[...] = w_ref[...].astype(BF16)

    for src, dst in ((k_ref, okt_ref), (v_ref, ovt_ref)):
        t = src[0].T
        for h in range(HEADS_PER_BLOCK):
            dst[h] = t[h * HEAD_DIM:(h + 1) * HEAD_DIM]

    nblk_total = seq // Q_BLOCK
    lane = lax.broadcasted_iota(jnp.int32, (Q_BLOCK, LANES), 1)
    head0 = lane < HEAD_DIM

    for c, (window, dil) in enumerate(DILATED_CONFIGS):
        sub_len = seq // dil
        blocks_per_sub = sub_len // Q_BLOCK

        for r in range(dil):
            rows = pl.ds(r, sub_len, stride=dil) if dil > 1 else pl.ds(0, seq)
            head0_rows = lax.broadcasted_iota(jnp.int32, (sub_len, LANES), 1) < HEAD_DIM
            qv = (q_ref[0, rows, :] * ATTN_SCALE).astype(BF16)
            kv = k_ref[0, rows, :].astype(BF16)
            vv = v_ref[0, rows, :].astype(BF16)
            zero = jnp.zeros_like(qv)
            q0 = jnp.where(head0_rows, qv, zero)
            q1 = jnp.where(head0_rows, zero, qv)
            for gb in range(blocks_per_sub):
                g = r * blocks_per_sub + gb
                src = slice(gb * Q_BLOCK, (gb + 1) * Q_BLOCK)
                qs_scr[c, 2 * g * Q_BLOCK:(2 * g + 1) * Q_BLOCK, :] = q0[src]
                qs_scr[c, (2 * g + 1) * Q_BLOCK:(2 * g + 2) * Q_BLOCK, :] = q1[src]
            dst = slice(r * sub_len, (r + 1) * sub_len)
            kd_scr[c, dst, :] = kv
            v0_scr[c, dst, :] = jnp.where(head0_rows, vv, zero)
            v1_scr[c, dst, :] = jnp.where(head0_rows, zero, vv)

        for g in range(nblk_total):
            r, gb = divmod(g, blocks_per_sub)
            lhs = qs_scr[c, 2 * g * Q_BLOCK:(2 * g + 2) * Q_BLOCK, :]
            if gb == 0:
                win = slice(g * Q_BLOCK, (g + 1) * Q_BLOCK)
                bias = bias_ref[c, 0, :, Q_BLOCK:]
            else:
                win = slice((g - 1) * Q_BLOCK, (g + 1) * Q_BLOCK)
                bias = bias_ref[c, 0]
            logits = lax.dot_general(lhs, kd_scr[c, win, :], (((1,), (1,)), ((), ())),
                                     preferred_element_type=F32) + bias
            m = jnp.max(logits, axis=-1, keepdims=True)
            p = jnp.exp(logits - m)
            den = jnp.sum(p, axis=-1, keepdims=True)
            pb = p.astype(BF16)
            o = (jnp.dot(pb[0:Q_BLOCK], v0_scr[c, win, :], preferred_element_type=F32)
                 + jnp.dot(pb[Q_BLOCK:], v1_scr[c, win, :], preferred_element_type=F32))
            m2 = jnp.where(head0, m[0:Q_BLOCK], m[Q_BLOCK:])
            d2 = jnp.where(head0, den[0:Q_BLOCK], den[Q_BLOCK:])
            if dil > 1:
                out_rows = pl.ds(gb * Q_BLOCK * dil + r, Q_BLOCK, stride=dil)
            else:
                out_rows = pl.ds(g * Q_BLOCK, Q_BLOCK)
            o_scr[c, out_rows, :] = o
            m_scr[c, out_rows, :] = m2
            d_scr[c, out_rows, :] = d2

    n_br = len(DILATED_CONFIGS)

    def merge_rows(i, carry):
        rows = pl.ds(pl.multiple_of(i * MERGE_ROWS, MERGE_ROWS), MERGE_ROWS)
        mmax = m_scr[0, rows, :]
        for c in range(1, n_br):
            mmax = jnp.maximum(mmax, m_scr[c, rows, :])
        num = jnp.zeros((MERGE_ROWS, LANES), F32)
        den = jnp.zeros((MERGE_ROWS, LANES), F32)
        for c in range(n_br):
            w = jnp.exp(m_scr[c, rows, :] - mmax)
            num = num + w * o_scr[c, rows, :]
            den = den + w * d_scr[c, rows, :]
        oa_ref[0, rows, :] = num / den
        return carry

    lax.fori_loop(0, seq // MERGE_ROWS, merge_rows, 0)

    u = gc_ref[0] * vc_ref[0]
    u_scr[0:8, :] = jnp.zeros((8, LANES), F32)
    u_scr[8:8 + seq, :] = u
    y = cw_ref[0:1, :] * u_scr[pl.ds(8 - (CONV_WIDTH - 1), seq), :]
    for i in range(1, CONV_WIDTH):
        y = y + cw_ref[i:i + 1, :] * u_scr[pl.ds(8 - (CONV_WIDTH - 1) + i, seq), :]
    oc_ref[0] = gb_ref[0] * y
    cs_ref[0] = u_scr[pl.ds(8 + seq - (CONV_WIDTH - 1), CONV_WIDTH - 1), :]


def _mixer_prompt(layer, depth, proj, conv_w_l, bias_tables, new_windows, to_round):
    _, batch, seq, width = proj.shape
    n_pairs = width // LANES
    n_heads = width // HEAD_DIM
    n_br = len(DILATED_CONFIGS)
    tok = pl.BlockSpec((1, seq, LANES), lambda b, hp: (b, 0, hp))
    assert (SEG_Q, SEG_K, SEG_V, SEG_VC, SEG_B, SEG_C) == tuple(range(6))
    seg_specs = [pl.BlockSpec((3, 1, seq, LANES), lambda b, hp, s=s: (s, b, 0, hp))
                 for s in range(N_SEGMENTS // 3)]
    win_spec = pl.BlockSpec((None, None, HEADS_PER_BLOCK, HEAD_DIM, seq),
                            lambda b, hp: (layer, b, hp, 0, 0))
    win_shape = jax.ShapeDtypeStruct((depth, batch, n_heads, HEAD_DIM, seq), F32)
    in_specs = seg_specs + [
        pl.BlockSpec((CONV_WIDTH, LANES), lambda b, hp: (0, hp)),
        pl.BlockSpec((n_br, 1, 2 * Q_BLOCK, 2 * Q_BLOCK), lambda b, hp: (0, hp, 0, 0))]
    args = [proj] * (N_SEGMENTS // 3) + [conv_w_l, bias_tables]
    n_steps = batch * n_pairs
    copy_specs, copy_shapes = [], []
    for w, w_layer in to_round:
        _, w_rows, w_cols = w.shape
        rows = w_rows // n_steps
        in_specs.append(pl.BlockSpec((None, rows, w_cols),
                                     lambda b, hp, w_layer=w_layer: (w_layer, b * n_pairs + hp, 0)))
        args.append(w)
        copy_specs.append(pl.BlockSpec((rows, w_cols), lambda b, hp: (b * n_pairs + hp, 0)))
        copy_shapes.append(jax.ShapeDtypeStruct((w_rows, w_cols), BF16))
    aliases = {}
    if new_windows is not None:
        aliases = {len(args): 3, len(args) + 1: 4}
        in_specs = in_specs + [pl.BlockSpec(memory_space=pl.ANY)] * 2
        args = args + list(new_windows)
    oa, oc, cs, okt, ovt, *copies = pl.pallas_call(
        functools.partial(_mixer_prompt_kernel, seq=seq, n_aliased=len(aliases),
                          n_round=len(to_round)),
        grid=(batch, n_pairs),
        in_specs=in_specs,
        out_specs=[tok, tok,
                   pl.BlockSpec((1, CONV_WIDTH - 1, LANES), lambda b, hp: (b, 0, hp)),
                   win_spec, win_spec] + copy_specs,
        out_shape=[jax.ShapeDtypeStruct((batch, seq, width), F32),
                   jax.ShapeDtypeStruct((batch, seq, width), F32),
                   jax.ShapeDtypeStruct((batch, CONV_WIDTH - 1, width), F32),
                   win_shape, win_shape] + copy_shapes,
        input_output_aliases=aliases,
        scratch_shapes=[pltpu.VMEM((n_br, 2 * seq, LANES), BF16),
                        pltpu.VMEM((n_br, seq, LANES), BF16),
                        pltpu.VMEM((n_br, seq, LANES), BF16),
                        pltpu.VMEM((n_br, seq, LANES), BF16),
                        pltpu.VMEM((n_br, seq, LANES), F32),
                        pltpu.VMEM((n_br, seq, LANES), F32),
                        pltpu.VMEM((n_br, seq, LANES), F32),
                        pltpu.VMEM((seq + 8, LANES), F32)],
        compiler_params=_compiler_params(("arbitrary", "arbitrary")),
        name="mixer_prompt",
    )(*args)
    return oa, oc, cs, (okt, ovt), copies


SAMPLE_HEADS_PER_STEP = 8


def _sample_bias_tables(rel_bias, wb):
    dist = wb - np.arange(wb)
    tables = []
    for window, dil in DILATED_CONFIGS:
        in_branch = (dist % dil == 0) & (dist // dil <= window // dil)
        bias = jnp.where(in_branch[:, None], _bias_lookup(rel_bias, _t5_bucket(dist)), MASKED)
        tables.append(bias.T)
    own = rel_bias[_t5_bucket(np.zeros((1,), np.int64))][0]
    return jnp.stack(tables), own[:, None]


def _mixer_sample_kernel(q_ref, kn_ref, vn_ref, kt_ref, vt_ref, bias_ref, bias0_ref,
                         vc_ref, gb_ref, gc_ref, cw_ref, sc_ref, *rest):
    oa_ref, oc_ref, cs_ref, okt_ref, ovt_ref = rest[-5:]
    heads, head_dim, wb = kt_ref.shape
    newest = lax.broadcasted_iota(jnp.int32, (head_dim, wb), 1) == wb - 1
    n_br = len(DILATED_CONFIGS)

    s = jnp.concatenate([jnp.sum(kt_ref[h] * q_ref[h], axis=0, keepdims=True)
                         for h in range(heads)], axis=0) * ATTN_SCALE
    l_new = jnp.concatenate([jnp.sum(q_ref[h] * kn_ref[h], axis=0, keepdims=True)
                             for h in range(heads)], axis=0) * ATTN_SCALE + bias0_ref[...]
    branches = []
    for c in range(n_br):
        logits = s + bias_ref[c]
        m = jnp.maximum(jnp.max(logits, axis=-1, keepdims=True), l_new)
        p = jnp.exp(logits - m)
        p_new = jnp.exp(l_new - m)
        den = jnp.sum(p, axis=-1, keepdims=True) + p_new
        branches.append((p, p_new, m, den))
    mmax = branches[0][2]
    for br in branches[1:]:
        mmax = jnp.maximum(mmax, br[2])
    p_all = jnp.zeros_like(s)
    p_new_all = jnp.zeros_like(mmax)
    den_all = jnp.zeros_like(mmax)
    for p, p_new, m, den in branches:
        w = jnp.exp(m - mmax)
        p_all = p_all + w * p
        p_new_all = p_new_all + w * p_new
        den_all = den_all + w * den

    for h in range(heads):
        kt = kt_ref[h]
        vt = vt_ref[h]
        k_new, v_new = kn_ref[h], vn_ref[h]
        num = (jnp.sum(vt * p_all[h:h + 1, :], axis=-1, keepdims=True)
               + p_new_all[h:h + 1, :] * v_new)
        oa_ref[h] = num / den_all[h:h + 1, :]
        okt_ref[h] = jnp.where(newest, k_new, pltpu.roll(kt, wb - 1, axis=1))
        ovt_ref[h] = jnp.where(newest, v_new, pltpu.roll(vt, wb - 1, axis=1))

    @pl.when(pl.program_id(1) == 0)
    def _():
        u = gc_ref[0] * vc_ref[0]
        state = sc_ref[0]
        y = cw_ref[0:1, :] * state[0:1]
        for i in range(1, CONV_WIDTH - 1):
            y = y + cw_ref[i:i + 1, :] * state[i:i + 1]
        y = y + cw_ref[CONV_WIDTH - 1:CONV_WIDTH, :] * u
        oc_ref[0] = gb_ref[0] * y
        cs_ref[0] = jnp.concatenate([state[1:], u], axis=0)


def _mixer_sample(layer, proj, kt_state, vt_state, rolled, bias_s, bias0, conv_w_l, state_conv):
    depth, batch, n_heads, head_dim, wb = kt_state.shape
    width = proj.shape[-1]
    hb = SAMPLE_HEADS_PER_STEP
    n_br = len(DILATED_CONFIGS)
    qkv = proj[SEG_Q:SEG_V + 1].reshape(3, batch, n_heads, head_dim, 1)
    rows = proj.reshape(N_SEGMENTS, batch, 1, width)
    col_specs = [pl.BlockSpec((None, None, hb, head_dim, 1), lambda b, hc, s=s: (s, b, hc, 0, 0))
                 for s in range(3)]
    row_specs = [pl.BlockSpec((None, 1, 1, width), lambda b, hc, s=s: (s, b, 0, 0))
                 for s in (SEG_VC, SEG_B, SEG_C)]
    buf_spec = pl.BlockSpec((None, None, hb, head_dim, wb), lambda b, hc: (layer, b, hc, 0, 0))
    in_specs = col_specs + [
        buf_spec, buf_spec,
        pl.BlockSpec((n_br, hb, wb), lambda b, hc: (0, hc, 0)),
        pl.BlockSpec((hb, 1), lambda b, hc: (hc, 0))] + row_specs + [
        pl.BlockSpec((CONV_WIDTH, width), lambda b, hc: (0, 0)),
        pl.BlockSpec((1, CONV_WIDTH - 1, width), lambda b, hc: (layer * batch + b, 0, 0))]
    args = [qkv, qkv, qkv, kt_state, vt_state, bias_s, bias0, rows, rows, rows, conv_w_l,
            state_conv.reshape(depth * batch, CONV_WIDTH - 1, width)]
    aliases = {}
    if rolled is not None:
        aliases = {len(args): 3, len(args) + 1: 4}
        in_specs = in_specs + [pl.BlockSpec(memory_space=pl.ANY)] * 2
        args = args + list(rolled)
    oa, oc, cs, okt, ovt = pl.pallas_call(
        _mixer_sample_kernel,
        grid=(batch, n_heads // hb),
        in_specs=in_specs,
        out_specs=[pl.BlockSpec((None, hb, head_dim, 1), lambda b, hc: (b, hc, 0, 0)),
                   pl.BlockSpec((1, 1, width), lambda b, hc: (b, 0, 0)),
                   pl.BlockSpec((1, CONV_WIDTH - 1, width), lambda b, hc: (b, 0, 0)),
                   buf_spec, buf_spec],
        out_shape=[jax.ShapeDtypeStruct((batch, n_heads, head_dim, 1), F32),
                   jax.ShapeDtypeStruct((batch, 1, width), F32),
                   jax.ShapeDtypeStruct((batch, CONV_WIDTH - 1, width), F32),
                   jax.ShapeDtypeStruct(kt_state.shape, F32),
                   jax.ShapeDtypeStruct(vt_state.shape, F32)],
        input_output_aliases=aliases,
        compiler_params=_compiler_params(("arbitrary", "arbitrary")),
        name="mixer_sample",
    )(*args)
    return oa, oc, cs, (okt, ovt)


def _outproj_kernel(oa_ref, oc_ref, ga_ref, gc_ref, w_ref, x_ref, o_ref, z_scr):
    n = pl.program_id(1)
    d_attn = oa_ref.shape[1]
    tn = o_ref.shape[1]

    @pl.when(n == 0)
    def _():
        z_scr[:, 0:d_attn] = _rms_rows(oa_ref[...], ga_ref[...]).astype(BF16)
        z_scr[:, d_attn:] = _rms_rows(oc_ref[...], gc_ref[...]).astype(BF16)

    w = w_ref[:, pl.ds(pl.multiple_of(n * tn, tn), tn)]
    o_ref[...] = x_ref[...] + jnp.dot(z_scr[...], w, preferred_element_type=F32)


def _outproj(oa, oc, attn_gain, conv_gain, w_out, x, *, tm, tn):
    m_rows, d_attn = oa.shape
    d_conv = oc.shape[1]
    d_model = x.shape[1]
    d_mix = d_attn + d_conv
    const = lambda m, n: (0, 0)
    return pl.pallas_call(
        _outproj_kernel,
        grid=(m_rows // tm, d_model // tn),
        in_specs=[pl.BlockSpec((tm, d_attn), lambda m, n: (m, 0)),
                  pl.BlockSpec((tm, d_conv), lambda m, n: (m, 0)),
                  pl.BlockSpec((1, d_attn), const),
                  pl.BlockSpec((1, d_conv), const),
                  pl.BlockSpec((d_mix, d_model), const, pipeline_mode=pl.Buffered(1)),
                  pl.BlockSpec((tm, tn), lambda m, n: (m, n))],
        out_specs=pl.BlockSpec((tm, tn), lambda m, n: (m, n)),
        out_shape=jax.ShapeDtypeStruct((m_rows, d_model), F32),
        scratch_shapes=[pltpu.VMEM((tm, d_mix), BF16)],
        compiler_params=_compiler_params(("arbitrary", "arbitrary")),
        name="outproj",
    )(oa, oc, attn_gain.reshape(1, d_attn), conv_gain.reshape(1, d_conv), w_out, x)


def _mlp_kernel(x_hbm, g_ref, wu_ref, wd_ref, o_ref, x_buf, h_scr, sem, *, tm):
    m = pl.program_id(0)
    f = pl.program_id(1)

    def x_copy(block):
        rows = pl.ds(pl.multiple_of(block * tm, tm), tm)
        return pltpu.make_async_copy(x_hbm.at[rows], x_buf, sem)

    @pl.when(f == 0)
    def _():
        @pl.when(m == 0)
        def _():
            x_copy(0).start()

        x_copy(m).wait()
        x = x_buf[...]
        h_scr[...] = _rms_rows(x, g_ref[...]).astype(BF16)
        o_ref[...] = x

    @pl.when((f == 1) & (m + 1 < pl.num_programs(0)))
    def _():
        x_copy(m + 1).start()

    a = jnp.dot(h_scr[...], wu_ref[...], preferred_element_type=F32)
    a = jnp.square(jnp.maximum(a, 0.0)).astype(BF16)
    o_ref[...] += jnp.dot(a, wd_ref[...], preferred_element_type=F32)


def _mlp(x, gain, w_up, w_down, *, tm, tf):
    m_rows, d_model = x.shape
    d_ff = w_up.shape[1]
    assert d_ff // tf >= 2, "the next row block's x is fetched on a block's second step"
    return pl.pallas_call(
        functools.partial(_mlp_kernel, tm=tm),
        grid=(m_rows // tm, d_ff // tf),
        in_specs=[pl.BlockSpec(memory_space=pl.ANY),
                  pl.BlockSpec((1, d_model), lambda m, f: (0, 0)),
                  pl.BlockSpec((d_model, tf), lambda m, f: (0, f)),
                  pl.BlockSpec((tf, d_model), lambda m, f: (f, 0))],
        out_specs=pl.BlockSpec((tm, d_model), lambda m, f: (m, 0)),
        out_shape=jax.ShapeDtypeStruct((m_rows, d_model), F32),
        scratch_shapes=[pltpu.VMEM((tm, d_model), F32),
                        pltpu.VMEM((tm, d_model), BF16),
                        pltpu.SemaphoreType.DMA(())],
        compiler_params=_compiler_params(("arbitrary", "arbitrary")),
        name="mlp",
    )(x, gain.reshape(1, d_model), w_up, w_down)


PROMPT_TILES = dict(tm=1024, tn_in=1024, tn_out=1024, tf=1024)
SAMPLE_TILES = dict(tn=1024, tf=1024)


def kernel(x_prompt, x_sample, state_attn_k, state_attn_v, state_conv, rel_bias, norm_mix, w_in,
           q_norm, k_norm, conv_w, attn_out_norm, conv_out_norm, w_out, norm_mlp, w_up, w_down):
    batch, seq, d_model = x_prompt.shape
    dec_batch, dec_seq, _ = x_sample.shape
    depth, _, wb, n_heads, head_dim = state_attn_k.shape
    assert dec_seq == 1 and head_dim == HEAD_DIM and wb == max(w for w, _ in DILATED_CONFIGS)
    assert seq == wb

    bias_p = _prompt_bias_tables(rel_bias)
    bias_s, bias0 = _sample_bias_tables(rel_bias, wb)
    kt_state = state_attn_k.transpose(0, 1, 3, 4, 2)
    vt_state = state_attn_v.transpose(0, 1, 3, 4, 2)
    rolled = None
    new_windows = None

    xp = x_prompt.reshape(batch * seq, d_model)
    xs = x_sample.reshape(dec_batch, d_model)
    pt, st = PROMPT_TILES, SAMPLE_TILES
    cp_new, cs_new = [], []
    w_in_l = w_in
    for l in range(depth):
        proj_s = _inproj(xs, norm_mix[l], w_in_l, l, q_norm[l], k_norm[l],
                         tm=dec_batch, tn=st["tn"])
        if l == 0:
            proj_s, w_in_l = proj_s
        width = proj_s.shape[-1]
        oa_s, oc_s, cs, rolled = _mixer_sample(l, proj_s, kt_state, vt_state, rolled, bias_s, bias0,
                                               conv_w[l], state_conv)
        cs_new.append(cs)

        if l + 1 < depth:
            proj, w_in_l = _inproj(xp, norm_mix[l], w_in_l, l, q_norm[l], k_norm[l],
                                   tm=pt["tm"], tn=pt["tn_in"], w_next=(w_in, l + 1))
        else:
            proj = _inproj(xp, norm_mix[l], w_in_l, l, q_norm[l], k_norm[l],
                           tm=pt["tm"], tn=pt["tn_in"])
        oa, oc, cs, new_windows, (w_out_l, w_up_l, w_down_l) = _mixer_prompt(
            l, depth, proj.reshape(N_SEGMENTS, batch, seq, width), conv_w[l], bias_p, new_windows,
            [(w_out, l), (w_up, l), (w_down, l)])
        cp_new.append(cs)

        xs = _outproj(oa_s.reshape(dec_batch, width), oc_s.reshape(dec_batch, width),
                      attn_out_norm[l], conv_out_norm[l], w_out_l, xs, tm=dec_batch, tn=st["tn"])
        xs = _mlp(xs, norm_mlp[l], w_up_l, w_down_l, tm=dec_batch, tf=st["tf"])
        xp = _outproj(oa.reshape(batch * seq, width), oc.reshape(batch * seq, width),
                      attn_out_norm[l], conv_out_norm[l], w_out_l, xp, tm=pt["tm"], tn=pt["tn_out"])
        xp = _mlp(xp, norm_mlp[l], w_up_l, w_down_l, tm=pt["tm"], tf=pt["tf"])

    ks_new, vs_new = (r.transpose(0, 1, 4, 2, 3) for r in rolled)
    kp_new, vp_new = (w.transpose(0, 1, 4, 2, 3) for w in new_windows)
    return (xp.reshape(batch, seq, d_model), xs.reshape(dec_batch, dec_seq, d_model),
            kp_new, vp_new, jnp.stack(cp_new), ks_new, vs_new, jnp.stack(cs_new))
```

```python
import functools

import jax
import jax.numpy as jnp
import numpy as np
from jax import lax
from jax.experimental import pallas as pl
from jax.experimental.pallas import tpu as pltpu

HEAD_DIM = 64
CONV_WIDTH = 3
DILATED_CONFIGS = ((128, 1), (512, 4), (2048, 16))
N_BUCKETS = 32
MAX_DISTANCE = 2048
EPS = 1e-6
ATTN_SCALE = HEAD_DIM ** -0.5
MASKED = -1e30

LANES = 128
MXU_WIDTH = 256
Q_BLOCK = 128
MERGE_ROWS = 256
HEADS_PER_BLOCK = LANES // HEAD_DIM
VMEM_LIMIT_BYTES = 56 * 1024 * 1024

F32 = jnp.float32
BF16 = jnp.bfloat16


def _compiler_params(semantics, flags=None):
    return pltpu.CompilerParams(dimension_semantics=semantics,
                                vmem_limit_bytes=VMEM_LIMIT_BYTES, flags=flags)


def _rms_rows(x, gain):
    ms = jnp.mean(x * x, axis=-1, keepdims=True)
    return (x * lax.rsqrt(ms + EPS)) * gain


N_SEGMENTS = 6
SEG_Q, SEG_K, SEG_V, SEG_VC, SEG_B, SEG_C = range(N_SEGMENTS)


def _bf16_weight(w_ref, copy_ref):
    w = w_ref[...]
    if w.dtype != BF16:
        w = w.astype(BF16)
    if copy_ref is not None:
        copy_ref[...] = w
    return w


def _inproj_kernel(x_hbm, g_ref, w_ref, hg_ref, grp_ref, *rest, tm, emit_copy, round_next):
    rest = list(rest)
    wnext_ref = rest.pop(0) if round_next else None
    o_ref = rest.pop(0)
    wcopy_ref = rest.pop(0) if emit_copy else None
    wnext_copy_ref = rest.pop(0) if round_next else None
    x_buf, h_scr, sem = rest
    m = pl.program_id(0)
    n = pl.program_id(1)

    def x_copy(block):
        rows = pl.ds(pl.multiple_of(block * tm, tm), tm)
        return pltpu.make_async_copy(x_hbm.at[rows], x_buf, sem)

    if round_next:
        wnext_copy_ref[...] = wnext_ref[...].astype(BF16)

    @pl.when(n == 0)
    def _():
        @pl.when(m == 0)
        def _():
            x_copy(0).start()

        x_copy(m).wait()
        h_scr[...] = _rms_rows(x_buf[...], g_ref[...]).astype(BF16)

    @pl.when((n == 1) & (m + 1 < pl.num_programs(0)))
    def _():
        x_copy(m + 1).start()

    segs, _, seg_w = o_ref.shape
    w = _bf16_weight(w_ref, wcopy_ref)
    for s in range(segs):
        o_ref[s] = jnp.dot(h_scr[...], w[:, s * seg_w:(s + 1) * seg_w], preferred_element_type=F32)

    for s in range(segs):
        seg = n * segs + s

        @pl.when(seg <= SEG_K)
        def _(s=s, seg=seg):
            gain = hg_ref[pl.ds(seg, 1), :]
            chunk = grp_ref.shape[0]
            for j in range(seg_w // chunk):
                cols = slice(j * chunk, (j + 1) * chunk)
                yj = o_ref[s, :, cols]
                ms = jnp.dot((yj * yj).astype(BF16), grp_ref[...], preferred_element_type=F32)
                o_ref[s, :, cols] = (yj * lax.rsqrt(ms + EPS)) * gain[:, cols]


def _weight_spec(w, layer, block, index_map):
    if w.ndim == 3:
        return pl.BlockSpec((None,) + block, lambda *ids: (layer,) + index_map(*ids))
    return pl.BlockSpec(block, index_map)


def _inproj(x, gain, w_in, layer, q_gain, k_gain, *, tm, tn, w_next=None):
    m_rows, d_model = x.shape
    d_in = w_in.shape[-1]
    emit_copy = w_in.ndim == 3
    seg_w = d_in // N_SEGMENTS
    segs = tn // seg_w
    assert segs * seg_w == tn and N_SEGMENTS % segs == 0
    grp = jnp.asarray(
        (np.arange(MXU_WIDTH)[:, None] // HEAD_DIM == np.arange(MXU_WIDTH)[None, :] // HEAD_DIM)
        .astype(np.float32) / HEAD_DIM, BF16)
    head_gains = jnp.stack([jnp.tile(q_gain, seg_w // HEAD_DIM),
                            jnp.tile(k_gain, seg_w // HEAD_DIM)])
    assert (SEG_Q, SEG_K) == (0, 1)
    const = lambda m, n: (0, 0)
    out_specs = [pl.BlockSpec((segs, tm, seg_w), lambda m, n: (n, m, 0))]
    out_shape = [jax.ShapeDtypeStruct((N_SEGMENTS, m_rows, seg_w), F32)]
    if emit_copy:
        assert m_rows == tm, "the bf16 copy is written once per column tile"
        out_specs.append(pl.BlockSpec((d_model, tn), lambda m, n: (0, n)))
        out_shape.append(jax.ShapeDtypeStruct((d_model, d_in), BF16))
    assert d_in // tn >= 2, "the next row block's x is fetched on a block's second step"
    in_specs = [pl.BlockSpec(memory_space=pl.ANY),
                pl.BlockSpec((1, d_model), const),
                _weight_spec(w_in, layer, (d_model, tn), lambda m, n: (0, n)),
                pl.BlockSpec((2, seg_w), const),
                pl.BlockSpec((MXU_WIDTH, MXU_WIDTH), const)]
    args = [x, gain.reshape(1, d_model), w_in, head_gains, grp]
    if w_next is not None:
        w_stack, next_layer = w_next
        rows = d_model // (m_rows // tm)
        in_specs.append(pl.BlockSpec((None, rows, tn), lambda m, n: (next_layer, m, n)))
        args.append(w_stack)
        out_specs.append(pl.BlockSpec((rows, tn), lambda m, n: (m, n)))
        out_shape.append(jax.ShapeDtypeStruct((d_model, d_in), BF16))
    outs = pl.pallas_call(
        functools.partial(_inproj_kernel, tm=tm, emit_copy=emit_copy,
                          round_next=w_next is not None),
        grid=(m_rows // tm, d_in // tn),
        in_specs=in_specs,
        out_specs=out_specs,
        out_shape=out_shape,
        scratch_shapes=[pltpu.VMEM((tm, d_model), F32),
                        pltpu.VMEM((tm, d_model), BF16),
                        pltpu.SemaphoreType.DMA(())],
        compiler_params=_compiler_params(("arbitrary", "arbitrary")),
        name="inproj",
    )(*args)
    return outs if len(outs) > 1 else outs[0]


def _t5_bucket(dist):
    n_exact = N_BUCKETS // 2
    large = n_exact + (np.log(np.maximum(dist, 1) / n_exact) / np.log(MAX_DISTANCE / n_exact)
                       * (N_BUCKETS - n_exact)).astype(np.int32)
    large = np.minimum(large, N_BUCKETS - 1)
    return np.where(dist < n_exact, dist, large).astype(np.int32)


def _bias_lookup(rel_bias, buckets):
    onehot = buckets[..., None] == np.arange(rel_bias.shape[0])
    return jnp.sum(jnp.where(onehot[..., None], rel_bias, 0.0), axis=-2)


def _prompt_bias_tables(rel_bias):
    n_heads = rel_bias.shape[1]
    tables = []
    for window, dil in DILATED_CONFIGS:
        nw = window // dil
        assert nw == Q_BLOCK
        bias_k = _bias_lookup(rel_bias, _t5_bucket(np.arange(nw + 1) * dil))
        pad = jnp.full((Q_BLOCK - 1, n_heads), MASKED, F32)
        u = jnp.concatenate([pad, bias_k[::-1], pad, pad[:1]])
        row = 3 * Q_BLOCK - 1
        skew = jnp.tile(u, (Q_BLOCK, 1))[:Q_BLOCK * row].reshape(Q_BLOCK, row, n_heads)
        t = skew[:, Q_BLOCK - 1:3 * Q_BLOCK - 1].transpose(2, 0, 1)
        tables.append(t.reshape(n_heads // HEADS_PER_BLOCK, HEADS_PER_BLOCK * Q_BLOCK,
                                2 * Q_BLOCK))
    return jnp.stack(tables)


def _mixer_prompt_kernel(qkv_ref, conv_ref, cw_ref, bias_ref, *rest, seq, n_aliased, n_round):
    q_ref, k_ref, v_ref = (qkv_ref.at[i] for i in range(3))
    vc_ref, gb_ref, gc_ref = (conv_ref.at[i] for i in range(3))
    w_refs = rest[:n_round]
    rest = rest[n_round + n_aliased:]
    oa_ref, oc_ref, cs_ref, okt_ref, ovt_ref = rest[:5]
    wcopy_refs = rest[5:5 + n_round]
    qs_scr, kd_scr, v0_scr, v1_scr, o_scr, m_scr, d_scr, u_scr = rest[5 + n_round:]

    for w_ref, wcopy_ref in zip(w_refs, wcopy_refs):
        wcopy_ref[...] = w_ref[...].astype(BF16)

    for src, dst in ((k_ref, okt_ref), (v_ref, ovt_ref)):
        t = src[0].T
        for h in range(HEADS_PER_BLOCK):
            dst[h] = t[h * HEAD_DIM:(h + 1) * HEAD_DIM]

    nblk_total = seq // Q_BLOCK
    lane = lax.broadcasted_iota(jnp.int32, (Q_BLOCK, LANES), 1)
    head0 = lane < HEAD_DIM

    for c, (window, dil) in enumerate(DILATED_CONFIGS):
        sub_len = seq // dil
        blocks_per_sub = sub_len // Q_BLOCK

        for r in range(dil):
            rows = pl.ds(r, sub_len, stride=dil) if dil > 1 else pl.ds(0, seq)
            head0_rows = lax.broadcasted_iota(jnp.int32, (sub_len, LANES), 1) < HEAD_DIM
            qv = (q_ref[0, rows, :] * ATTN_SCALE).astype(BF16)
            kv = k_ref[0, rows, :].astype(BF16)
            vv = v_ref[0, rows, :].astype(BF16)
            zero = jnp.zeros_like(qv)
            q0 = jnp.where(head0_rows, qv, zero)
            q1 = jnp.where(head0_rows, zero, qv)
            for gb in range(blocks_per_sub):
                g = r * blocks_per_sub + gb
                src = slice(gb * Q_BLOCK, (gb + 1) * Q_BLOCK)
                qs_scr[c, 2 * g * Q_BLOCK:(2 * g + 1) * Q_BLOCK, :] = q0[src]
                qs_scr[c, (2 * g + 1) * Q_BLOCK:(2 * g + 2) * Q_BLOCK, :] = q1[src]
            dst = slice(r * sub_len, (r + 1) * sub_len)
            kd_scr[c, dst, :] = kv
            v0_scr[c, dst, :] = jnp.where(head0_rows, vv, zero)
            v1_scr[c, dst, :] = jnp.where(head0_rows, zero, vv)

        for g in range(nblk_total):
            r, gb = divmod(g, blocks_per_sub)
            lhs = qs_scr[c, 2 * g * Q_BLOCK:(2 * g + 2) * Q_BLOCK, :]
            if gb == 0:
                win = slice(g * Q_BLOCK, (g + 1) * Q_BLOCK)
                bias = bias_ref[c, 0, :, Q_BLOCK:]
            else:
                win = slice((g - 1) * Q_BLOCK, (g + 1) * Q_BLOCK)
                bias = bias_ref[c, 0]
            logits = lax.dot_general(lhs, kd_scr[c, win, :], (((1,), (1,)), ((), ())),
                                     preferred_element_type=F32) + bias
            m = jnp.max(logits, axis=-1, keepdims=True)
            p = jnp.exp(logits - m)
            den = jnp.sum(p, axis=-1, keepdims=True)
            pb = p.astype(BF16)
            o = (jnp.dot(pb[0:Q_BLOCK], v0_scr[c, win, :], preferred_element_type=F32)
                 + jnp.dot(pb[Q_BLOCK:], v1_scr[c, win, :], preferred_element_type=F32))
            m2 = jnp.where(head0, m[0:Q_BLOCK], m[Q_BLOCK:])
            d2 = jnp.where(head0, den[0:Q_BLOCK], den[Q_BLOCK:])
            if dil > 1:
                out_rows = pl.ds(gb * Q_BLOCK * dil + r, Q_BLOCK, stride=dil)
            else:
                out_rows = pl.ds(g * Q_BLOCK, Q_BLOCK)
            o_scr[c, out_rows, :] = o
            m_scr[c, out_rows, :] = m2
            d_scr[c, out_rows, :] = d2

    n_br = len(DILATED_CONFIGS)

    def merge_rows(i, carry):
        rows = pl.ds(pl.multiple_of(i * MERGE_ROWS, MERGE_ROWS), MERGE_ROWS)
        mmax = m_scr[0, rows, :]
        for c in range(1, n_br):
            mmax = jnp.maximum(mmax, m_scr[c, rows, :])
        num = jnp.zeros((MERGE_ROWS, LANES), F32)
        den = jnp.zeros((MERGE_ROWS, LANES), F32)
        for c in range(n_br):
            w = jnp.exp(m_scr[c, rows, :] - mmax)
            num = num + w * o_scr[c, rows, :]
            den = den + w * d_scr[c, rows, :]
        oa_ref[0, rows, :] = num / den
        return carry

    lax.fori_loop(0, seq // MERGE_ROWS, merge_rows, 0)

    u = gc_ref[0] * vc_ref[0]
    u_scr[0:8, :] = jnp.zeros((8, LANES), F32)
    u_scr[8:8 + seq, :] = u
    y = cw_ref[0:1, :] * u_scr[pl.ds(8 - (CONV_WIDTH - 1), seq), :]
    for i in range(1, CONV_WIDTH):
        y = y + cw_ref[i:i + 1, :] * u_scr[pl.ds(8 - (CONV_WIDTH - 1) + i, seq), :]
    oc_ref[0] = gb_ref[0] * y
    cs_ref[0] = u_scr[pl.ds(8 + seq - (CONV_WIDTH - 1), CONV_WIDTH - 1), :]


def _mixer_prompt(layer, depth, proj, conv_w_l, bias_tables, new_windows, to_round):
    _, batch, seq, width = proj.shape
    n_pairs = width // LANES
    n_heads = width // HEAD_DIM
    n_br = len(DILATED_CONFIGS)
    tok = pl.BlockSpec((1, seq, LANES), lambda b, hp: (b, 0, hp))
    assert (SEG_Q, SEG_K, SEG_V, SEG_VC, SEG_B, SEG_C) == tuple(range(6))
    seg_specs = [pl.BlockSpec((3, 1, seq, LANES), lambda b, hp, s=s: (s, b, 0, hp))
                 for s in range(N_SEGMENTS // 3)]
    win_spec = pl.BlockSpec((None, None, HEADS_PER_BLOCK, HEAD_DIM, seq),
                            lambda b, hp: (layer, b, hp, 0, 0))
    win_shape = jax.ShapeDtypeStruct((depth, batch, n_heads, HEAD_DIM, seq), F32)
    in_specs = seg_specs + [
        pl.BlockSpec((CONV_WIDTH, LANES), lambda b, hp: (0, hp)),
        pl.BlockSpec((n_br, 1, 2 * Q_BLOCK, 2 * Q_BLOCK), lambda b, hp: (0, hp, 0, 0))]
    args = [proj] * (N_SEGMENTS // 3) + [conv_w_l, bias_tables]
    n_steps = batch * n_pairs
    copy_specs, copy_shapes = [], []
    for w, w_layer in to_round:
        _, w_rows, w_cols = w.shape
        rows = w_rows // n_steps
        in_specs.append(pl.BlockSpec((None, rows, w_cols),
                                     lambda b, hp, w_layer=w_layer: (w_layer, b * n_pairs + hp, 0)))
        args.append(w)
        copy_specs.append(pl.BlockSpec((rows, w_cols), lambda b, hp: (b * n_pairs + hp, 0)))
        copy_shapes.append(jax.ShapeDtypeStruct((w_rows, w_cols), BF16))
    aliases = {}
    if new_windows is not None:
        aliases = {len(args): 3, len(args) + 1: 4}
        in_specs = in_specs + [pl.BlockSpec(memory_space=pl.ANY)] * 2
        args = args + list(new_windows)
    oa, oc, cs, okt, ovt, *copies = pl.pallas_call(
        functools.partial(_mixer_prompt_kernel, seq=seq, n_aliased=len(aliases),
                          n_round=len(to_round)),
        grid=(batch, n_pairs),
        in_specs=in_specs,
        out_specs=[tok, tok,
                   pl.BlockSpec((1, CONV_WIDTH - 1, LANES), lambda b, hp: (b, 0, hp)),
                   win_spec, win_spec] + copy_specs,
        out_shape=[jax.ShapeDtypeStruct((batch, seq, width), F32),
                   jax.ShapeDtypeStruct((batch, seq, width), F32),
                   jax.ShapeDtypeStruct((batch, CONV_WIDTH - 1, width), F32),
                   win_shape, win_shape] + copy_shapes,
        input_output_aliases=aliases,
        scratch_shapes=[pltpu.VMEM((n_br, 2 * seq, LANES), BF16),
                        pltpu.VMEM((n_br, seq, LANES), BF16),
                        pltpu.VMEM((n_br, seq, LANES), BF16),
                        pltpu.VMEM((n_br, seq, LANES), BF16),
                        pltpu.VMEM((n_br, seq, LANES), F32),
                        pltpu.VMEM((n_br, seq, LANES), F32),
                        pltpu.VMEM((n_br, seq, LANES), F32),
                        pltpu.VMEM((seq + 8, LANES), F32)],
        compiler_params=_compiler_params(("arbitrary", "arbitrary")),
        name="mixer_prompt",
    )(*args)
    return oa, oc, cs, (okt, ovt), copies


SAMPLE_HEADS_PER_STEP = 8


def _sample_bias_tables(rel_bias, wb):
    dist = wb - np.arange(wb)
    tables = []
    for window, dil in DILATED_CONFIGS:
        in_branch = (dist % dil == 0) & (dist // dil <= window // dil)
        bias = jnp.where(in_branch[:, None], _bias_lookup(rel_bias, _t5_bucket(dist)), MASKED)
        tables.append(bias.T)
    own = rel_bias[_t5_bucket(np.zeros((1,), np.int64))][0]
    return jnp.stack(tables), own[:, None]


def _mixer_sample_kernel(q_ref, kn_ref, vn_ref, kt_ref, vt_ref, bias_ref, bias0_ref,
                         vc_ref, gb_ref, gc_ref, cw_ref, sc_ref, *rest):
    oa_ref, oc_ref, cs_ref, okt_ref, ovt_ref = rest[-5:]
    heads, head_dim, wb = kt_ref.shape
    newest = lax.broadcasted_iota(jnp.int32, (head_dim, wb), 1) == wb - 1
    n_br = len(DILATED_CONFIGS)

    s = jnp.concatenate([jnp.sum(kt_ref[h] * q_ref[h], axis=0, keepdims=True)
                         for h in range(heads)], axis=0) * ATTN_SCALE
    l_new = jnp.concatenate([jnp.sum(q_ref[h] * kn_ref[h], axis=0, keepdims=True)
                             for h in range(heads)], axis=0) * ATTN_SCALE + bias0_ref[...]
    branches = []
    for c in range(n_br):
        logits = s + bias_ref[c]
        m = jnp.maximum(jnp.max(logits, axis=-1, keepdims=True), l_new)
        p = jnp.exp(logits - m)
        p_new = jnp.exp(l_new - m)
        den = jnp.sum(p, axis=-1, keepdims=True) + p_new
        branches.append((p, p_new, m, den))
    mmax = branches[0][2]
    for br in branches[1:]:
        mmax = jnp.maximum(mmax, br[2])
    p_all = jnp.zeros_like(s)
    p_new_all = jnp.zeros_like(mmax)
    den_all = jnp.zeros_like(mmax)
    for p, p_new, m, den in branches:
        w = jnp.exp(m - mmax)
        p_all = p_all + w * p
        p_new_all = p_new_all + w * p_new
        den_all = den_all + w * den

    for h in range(heads):
        kt = kt_ref[h]
        vt = vt_ref[h]
        k_new, v_new = kn_ref[h], vn_ref[h]
        num = (jnp.sum(vt * p_all[h:h + 1, :], axis=-1, keepdims=True)
               + p_new_all[h:h + 1, :] * v_new)
        oa_ref[h] = num / den_all[h:h + 1, :]
        okt_ref[h] = jnp.where(newest, k_new, pltpu.roll(kt, wb - 1, axis=1))
        ovt_ref[h] = jnp.where(newest, v_new, pltpu.roll(vt, wb - 1, axis=1))

    @pl.when(pl.program_id(1) == 0)
    def _():
        u = gc_ref[0] * vc_ref[0]
        state = sc_ref[0]
        y = cw_ref[0:1, :] * state[0:1]
        for i in range(1, CONV_WIDTH - 1):
            y = y + cw_ref[i:i + 1, :] * state[i:i + 1]
        y = y + cw_ref[CONV_WIDTH - 1:CONV_WIDTH, :] * u
        oc_ref[0] = gb_ref[0] * y
        cs_ref[0] = jnp.concatenate([state[1:], u], axis=0)


def _mixer_sample(layer, proj, kt_state, vt_state, rolled, bias_s, bias0, conv_w_l, state_conv):
    depth, batch, n_heads, head_dim, wb = kt_state.shape
    width = proj.shape[-1]
    hb = SAMPLE_HEADS_PER_STEP
    n_br = len(DILATED_CONFIGS)
    qkv = proj[SEG_Q:SEG_V + 1].reshape(3, batch, n_heads, head_dim, 1)
    rows = proj.reshape(N_SEGMENTS, batch, 1, width)
    col_specs = [pl.BlockSpec((None, None, hb, head_dim, 1), lambda b, hc, s=s: (s, b, hc, 0, 0))
                 for s in range(3)]
    row_specs = [pl.BlockSpec((None, 1, 1, width), lambda b, hc, s=s: (s, b, 0, 0))
                 for s in (SEG_VC, SEG_B, SEG_C)]
    buf_spec = pl.BlockSpec((None, None, hb, head_dim, wb), lambda b, hc: (layer, b, hc, 0, 0))
    in_specs = col_specs + [
        buf_spec, buf_spec,
        pl.BlockSpec((n_br, hb, wb), lambda b, hc: (0, hc, 0)),
        pl.BlockSpec((hb, 1), lambda b, hc: (hc, 0))] + row_specs + [
        pl.BlockSpec((CONV_WIDTH, width), lambda b, hc: (0, 0)),
        pl.BlockSpec((1, CONV_WIDTH - 1, width), lambda b, hc: (layer * batch + b, 0, 0))]
    args = [qkv, qkv, qkv, kt_state, vt_state, bias_s, bias0, rows, rows, rows, conv_w_l,
            state_conv.reshape(depth * batch, CONV_WIDTH - 1, width)]
    aliases = {}
    if rolled is not None:
        aliases = {len(args): 3, len(args) + 1: 4}
        in_specs = in_specs + [pl.BlockSpec(memory_space=pl.ANY)] * 2
        args = args + list(rolled)
    oa, oc, cs, okt, ovt = pl.pallas_call(
        _mixer_sample_kernel,
        grid=(batch, n_heads // hb),
        in_specs=in_specs,
        out_specs=[pl.BlockSpec((None, hb, head_dim, 1), lambda b, hc: (b, hc, 0, 0)),
                   pl.BlockSpec((1, 1, width), lambda b, hc: (b, 0, 0)),
                   pl.BlockSpec((1, CONV_WIDTH - 1, width), lambda b, hc: (b, 0, 0)),
                   buf_spec, buf_spec],
        out_shape=[jax.ShapeDtypeStruct((batch, n_heads, head_dim, 1), F32),
                   jax.ShapeDtypeStruct((batch, 1, width), F32),
                   jax.ShapeDtypeStruct((batch, CONV_WIDTH - 1, width), F32),
                   jax.ShapeDtypeStruct(kt_state.shape, F32),
                   jax.ShapeDtypeStruct(vt_state.shape, F32)],
        input_output_aliases=aliases,
        compiler_params=_compiler_params(("arbitrary", "arbitrary")),
        name="mixer_sample",
    )(*args)
    return oa, oc, cs, (okt, ovt)


def _outproj_kernel(oa_ref, oc_ref, ga_ref, gc_ref, w_ref, x_ref, o_ref, z_scr):
    n = pl.program_id(1)
    d_attn = oa_ref.shape[1]
    tn = o_ref.shape[1]

    @pl.when(n == 0)
    def _():
        z_scr[:, 0:d_attn] = _rms_rows(oa_ref[...], ga_ref[...]).astype(BF16)
        z_scr[:, d_attn:] = _rms_rows(oc_ref[...], gc_ref[...]).astype(BF16)

    w = w_ref[:, pl.ds(pl.multiple_of(n * tn, tn), tn)]
    o_ref[...] = x_ref[...] + jnp.dot(z_scr[...], w, preferred_element_type=F32)


def _outproj(oa, oc, attn_gain, conv_gain, w_out, x, *, tm, tn):
    m_rows, d_attn = oa.shape
    d_conv = oc.shape[1]
    d_model = x.shape[1]
    d_mix = d_attn + d_conv
    const = lambda m, n: (0, 0)
    return pl.pallas_call(
        _outproj_kernel,
        grid=(m_rows // tm, d_model // tn),
        in_specs=[pl.BlockSpec((tm, d_attn), lambda m, n: (m, 0)),
                  pl.BlockSpec((tm, d_conv), lambda m, n: (m, 0)),
                  pl.BlockSpec((1, d_attn), const),
                  pl.BlockSpec((1, d_conv), const),
                  pl.BlockSpec((d_mix, d_model), const, pipeline_mode=pl.Buffered(1)),
                  pl.BlockSpec((tm, tn), lambda m, n: (m, n))],
        out_specs=pl.BlockSpec((tm, tn), lambda m, n: (m, n)),
        out_shape=jax.ShapeDtypeStruct((m_rows, d_model), F32),
        scratch_shapes=[pltpu.VMEM((tm, d_mix), BF16)],
        compiler_params=_compiler_params(("arbitrary", "arbitrary")),
        name="outproj",
    )(oa, oc, attn_gain.reshape(1, d_attn), conv_gain.reshape(1, d_conv), w_out, x)


def _mlp_kernel(x_hbm, g_ref, wu_ref, wd_ref, o_ref, x_buf, h_scr, sem, *, tm):
    m = pl.program_id(0)
    f = pl.program_id(1)

    def x_copy(block):
        rows = pl.ds(pl.multiple_of(block * tm, tm), tm)
        return pltpu.make_async_copy(x_hbm.at[rows], x_buf, sem)

    @pl.when(f == 0)
    def _():
        @pl.when(m == 0)
        def _():
            x_copy(0).start()

        x_copy(m).wait()
        x = x_buf[...]
        h_scr[...] = _rms_rows(x, g_ref[...]).astype(BF16)
        o_ref[...] = x

    @pl.when((f == 1) & (m + 1 < pl.num_programs(0)))
    def _():
        x_copy(m + 1).start()

    a = jnp.dot(h_scr[...], wu_ref[...], preferred_element_type=F32)
    a = jnp.square(jnp.maximum(a, 0.0)).astype(BF16)
    o_ref[...] += jnp.dot(a, wd_ref[...], preferred_element_type=F32)


def _mlp(x, gain, w_up, w_down, *, tm, tf):
    m_rows, d_model = x.shape
    d_ff = w_up.shape[1]
    assert d_ff // tf >= 2, "the next row block's x is fetched on a block's second step"
    return pl.pallas_call(
        functools.partial(_mlp_kernel, tm=tm),
        grid=(m_rows // tm, d_ff // tf),
        in_specs=[pl.BlockSpec(memory_space=pl.ANY),
                  pl.BlockSpec((1, d_model), lambda m, f: (0, 0)),
                  pl.BlockSpec((d_model, tf), lambda m, f: (0, f)),
                  pl.BlockSpec((tf, d_model), lambda m, f: (f, 0))],
        out_specs=pl.BlockSpec((tm, d_model), lambda m, f: (m, 0)),
        out_shape=jax.ShapeDtypeStruct((m_rows, d_model), F32),
        scratch_shapes=[pltpu.VMEM((tm, d_model), F32),
                        pltpu.VMEM((tm, d_model), BF16),
                        pltpu.SemaphoreType.DMA(())],
        compiler_params=_compiler_params(("arbitrary", "arbitrary")),
        name="mlp",
    )(x, gain.reshape(1, d_model), w_up, w_down)


PROMPT_TILES = dict(tm=1024, tn_in=2048, tn_out=1024, tf=1024)
SAMPLE_TILES = dict(tn=1024, tf=1024)


def kernel(x_prompt, x_sample, state_attn_k, state_attn_v, state_conv, rel_bias, norm_mix, w_in,
           q_norm, k_norm, conv_w, attn_out_norm, conv_out_norm, w_out, norm_mlp, w_up, w_down):
    batch, seq, d_model = x_prompt.shape
    dec_batch, dec_seq, _ = x_sample.shape
    depth, _, wb, n_heads, head_dim = state_attn_k.shape
    assert dec_seq == 1 and head_dim == HEAD_DIM and wb == max(w for w, _ in DILATED_CONFIGS)
    assert seq == wb

    bias_p = _prompt_bias_tables(rel_bias)
    bias_s, bias0 = _sample_bias_tables(rel_bias, wb)
    kt_state = state_attn_k.transpose(0, 1, 3, 4, 2)
    vt_state = state_attn_v.transpose(0, 1, 3, 4, 2)
    rolled = None
    new_windows = None

    xp = x_prompt.reshape(batch * seq, d_model)
    xs = x_sample.reshape(dec_batch, d_model)
    pt, st = PROMPT_TILES, SAMPLE_TILES
    cp_new, cs_new = [], []
    w_in_l = w_in
    for l in range(depth):
        proj_s = _inproj(xs, norm_mix[l], w_in_l, l, q_norm[l], k_norm[l],
                         tm=dec_batch, tn=st["tn"])
        if l == 0:
            proj_s, w_in_l = proj_s
        width = proj_s.shape[-1]
        oa_s, oc_s, cs, rolled = _mixer_sample(l, proj_s, kt_state, vt_state, rolled, bias_s, bias0,
                                               conv_w[l], state_conv)
        cs_new.append(cs)

        if l + 1 < depth:
            proj, w_in_l = _inproj(xp, norm_mix[l], w_in_l, l, q_norm[l], k_norm[l],
                                   tm=pt["tm"], tn=pt["tn_in"], w_next=(w_in, l + 1))
        else:
            proj = _inproj(xp, norm_mix[l], w_in_l, l, q_norm[l], k_norm[l],
                           tm=pt["tm"], tn=pt["tn_in"])
        oa, oc, cs, new_windows, (w_out_l, w_up_l, w_down_l) = _mixer_prompt(
            l, depth, proj.reshape(N_SEGMENTS, batch, seq, width), conv_w[l], bias_p, new_windows,
            [(w_out, l), (w_up, l), (w_down, l)])
        cp_new.append(cs)

        xs = _outproj(oa_s.reshape(dec_batch, width), oc_s.reshape(dec_batch, width),
                      attn_out_norm[l], conv_out_norm[l], w_out_l, xs, tm=dec_batch, tn=st["tn"])
        xs = _mlp(xs, norm_mlp[l], w_up_l, w_down_l, tm=dec_batch, tf=st["tf"])
        xp = _outproj(oa.reshape(batch * seq, width), oc.reshape(batch * seq, width),
                      attn_out_norm[l], conv_out_norm[l], w_out_l, xp, tm=pt["tm"], tn=pt["tn_out"])
        xp = _mlp(xp, norm_mlp[l], w_up_l, w_down_l, tm=pt["tm"], tf=pt["tf"])

    ks_new, vs_new = (r.transpose(0, 1, 4, 2, 3) for r in rolled)
    kp_new, vp_new = (w.transpose(0, 1, 4, 2, 3) for w in new_windows)
    return (xp.reshape(batch, seq, d_model), xs.reshape(dec_batch, dec_seq, d_model),
            kp_new, vp_new, jnp.stack(cp_new), ks_new, vs_new, jnp.stack(cs_new))
```

```python
import functools

import jax
import jax.numpy as jnp
import numpy as np
from jax import lax
from jax.experimental import pallas as pl
from jax.experimental.pallas import tpu as pltpu

HEAD_DIM = 64
CONV_WIDTH = 3
DILATED_CONFIGS = ((128, 1), (512, 4), (2048, 16))
N_BUCKETS = 32
MAX_DISTANCE = 2048
EPS = 1e-6
ATTN_SCALE = HEAD_DIM ** -0.5
MASKED = -1e30

LANES = 128
MXU_WIDTH = 256
Q_BLOCK = 128
MERGE_ROWS = 256
HEADS_PER_BLOCK = LANES // HEAD_DIM
VMEM_LIMIT_BYTES = 56 * 1024 * 1024

F32 = jnp.float32
BF16 = jnp.bfloat16


def _compiler_params(semantics, flags=None):
    return pltpu.CompilerParams(dimension_semantics=semantics,
                                vmem_limit_bytes=VMEM_LIMIT_BYTES, flags=flags)


def _rms_rows(x, gain):
    ms = jnp.mean(x * x, axis=-1, keepdims=True)
    return (x * lax.rsqrt(ms + EPS)) * gain


N_SEGMENTS = 6
SEG_Q, SEG_K, SEG_V, SEG_VC, SEG_B, SEG_C = range(N_SEGMENTS)


def _bf16_weight(w_ref, copy_ref):
    w = w_ref[...]
    if w.dtype != BF16:
        w = w.astype(BF16)
    if copy_ref is not None:
        copy_ref[...] = w
    return w


def _inproj_kernel(x_hbm, g_ref, w_ref, hg_ref, grp_ref, *rest, tm, emit_copy, round_next):
    rest = list(rest)
    wnext_ref = rest.pop(0) if round_next else None
    o_ref = rest.pop(0)
    wcopy_ref = rest.pop(0) if emit_copy else None
    wnext_copy_ref = rest.pop(0) if round_next else None
    x_buf, h_scr, sem = rest
    m = pl.program_id(0)
    n = pl.program_id(1)

    def x_copy(block):
        rows = pl.ds(pl.multiple_of(block * tm, tm), tm)
        return pltpu.make_async_copy(x_hbm.at[rows], x_buf, sem)

    if round_next:
        wnext_copy_ref[...] = wnext_ref[...].astype(BF16)

    @pl.when(n == 0)
    def _():
        @pl.when(m == 0)
        def _():
            x_copy(0).start()

        x_copy(m).wait()
        h_scr[...] = _rms_rows(x_buf[...], g_ref[...]).astype(BF16)

    @pl.when((n == 1) & (m + 1 < pl.num_programs(0)))
    def _():
        x_copy(m + 1).start(priority=1)

    segs, _, seg_w = o_ref.shape
    w = _bf16_weight(w_ref, wcopy_ref)
    for s in range(segs):
        o_ref[s] = jnp.dot(h_scr[...], w[:, s * seg_w:(s + 1) * seg_w], preferred_element_type=F32)

    for s in range(segs):
        seg = n * segs + s

        @pl.when(seg <= SEG_K)
        def _(s=s, seg=seg):
            gain = hg_ref[pl.ds(seg, 1), :]
            chunk = grp_ref.shape[0]
            for j in range(seg_w // chunk):
                cols = slice(j * chunk, (j + 1) * chunk)
                yj = o_ref[s, :, cols]
                ms = jnp.dot((yj * yj).astype(BF16), grp_ref[...], preferred_element_type=F32)
                o_ref[s, :, cols] = (yj * lax.rsqrt(ms + EPS)) * gain[:, cols]


def _weight_spec(w, layer, block, index_map):
    if w.ndim == 3:
        return pl.BlockSpec((None,) + block, lambda *ids: (layer,) + index_map(*ids))
    return pl.BlockSpec(block, index_map)


def _inproj(x, gain, w_in, layer, q_gain, k_gain, *, tm, tn, w_next=None):
    m_rows, d_model = x.shape
    d_in = w_in.shape[-1]
    emit_copy = w_in.ndim == 3
    seg_w = d_in // N_SEGMENTS
    segs = tn // seg_w
    assert segs * seg_w == tn and N_SEGMENTS % segs == 0
    grp = jnp.asarray(
        (np.arange(MXU_WIDTH)[:, None] // HEAD_DIM == np.arange(MXU_WIDTH)[None, :] // HEAD_DIM)
        .astype(np.float32) / HEAD_DIM, BF16)
    head_gains = jnp.stack([jnp.tile(q_gain, seg_w // HEAD_DIM),
                            jnp.tile(k_gain, seg_w // HEAD_DIM)])
    assert (SEG_Q, SEG_K) == (0, 1)
    const = lambda m, n: (0, 0)
    out_specs = [pl.BlockSpec((segs, tm, seg_w), lambda m, n: (n, m, 0))]
    out_shape = [jax.ShapeDtypeStruct((N_SEGMENTS, m_rows, seg_w), F32)]
    if emit_copy:
        assert m_rows == tm, "the bf16 copy is written once per column tile"
        out_specs.append(pl.BlockSpec((d_model, tn), lambda m, n: (0, n)))
        out_shape.append(jax.ShapeDtypeStruct((d_model, d_in), BF16))
    assert d_in // tn >= 2, "the next row block's x is fetched on a block's second step"
    in_specs = [pl.BlockSpec(memory_space=pl.ANY),
                pl.BlockSpec((1, d_model), const),
                _weight_spec(w_in, layer, (d_model, tn), lambda m, n: (0, n)),
                pl.BlockSpec((2, seg_w), const),
                pl.BlockSpec((MXU_WIDTH, MXU_WIDTH), const)]
    args = [x, gain.reshape(1, d_model), w_in, head_gains, grp]
    if w_next is not None:
        w_stack, next_layer = w_next
        rows = d_model // (m_rows // tm)
        in_specs.append(pl.BlockSpec((None, rows, tn), lambda m, n: (next_layer, m, n)))
        args.append(w_stack)
        out_specs.append(pl.BlockSpec((rows, tn), lambda m, n: (m, n)))
        out_shape.append(jax.ShapeDtypeStruct((d_model, d_in), BF16))
    outs = pl.pallas_call(
        functools.partial(_inproj_kernel, tm=tm, emit_copy=emit_copy,
                          round_next=w_next is not None),
        grid=(m_rows // tm, d_in // tn),
        in_specs=in_specs,
        out_specs=out_specs,
        out_shape=out_shape,
        scratch_shapes=[pltpu.VMEM((tm, d_model), F32),
                        pltpu.VMEM((tm, d_model), BF16),
                        pltpu.SemaphoreType.DMA(())],
        compiler_params=_compiler_params(("arbitrary", "arbitrary")),
        name="inproj",
    )(*args)
    return outs if len(outs) > 1 else outs[0]


def _t5_bucket(dist):
    n_exact = N_BUCKETS // 2
    large = n_exact + (np.log(np.maximum(dist, 1) / n_exact) / np.log(MAX_DISTANCE / n_exact)
                       * (N_BUCKETS - n_exact)).astype(np.int32)
    large = np.minimum(large, N_BUCKETS - 1)
    return np.where(dist < n_exact, dist, large).astype(np.int32)


def _bias_lookup(rel_bias, buckets):
    onehot = buckets[..., None] == np.arange(rel_bias.shape[0])
    return jnp.sum(jnp.where(onehot[..., None], rel_bias, 0.0), axis=-2)


def _prompt_bias_tables(rel_bias):
    n_heads = rel_bias.shape[1]
    tables = []
    for window, dil in DILATED_CONFIGS:
        nw = window // dil
        assert nw == Q_BLOCK
        bias_k = _bias_lookup(rel_bias, _t5_bucket(np.arange(nw + 1) * dil))
        pad = jnp.full((Q_BLOCK - 1, n_heads), MASKED, F32)
        u = jnp.concatenate([pad, bias_k[::-1], pad, pad[:1]])
        row = 3 * Q_BLOCK - 1
        skew = jnp.tile(u, (Q_BLOCK, 1))[:Q_BLOCK * row].reshape(Q_BLOCK, row, n_heads)
        t = skew[:, Q_BLOCK - 1:3 * Q_BLOCK - 1].transpose(2, 0, 1)
        tables.append(t.reshape(n_heads // HEADS_PER_BLOCK, HEADS_PER_BLOCK * Q_BLOCK,
                                2 * Q_BLOCK))
    return jnp.stack(tables)


def _mixer_prompt_kernel(qkv_ref, conv_ref, cw_ref, bias_ref, *rest, seq, n_aliased, n_round):
    q_ref, k_ref, v_ref = (qkv_ref.at[i] for i in range(3))
    vc_ref, gb_ref, gc_ref = (conv_ref.at[i] for i in range(3))
    w_refs = rest[:n_round]
    rest = rest[n_round + n_aliased:]
    oa_ref, oc_ref, cs_ref, okt_ref, ovt_ref = rest[:5]
    wcopy_refs = rest[5:5 + n_round]
    qs_scr, kd_scr, v0_scr, v1_scr, o_scr, m_scr, d_scr, u_scr = rest[5 + n_round:]

    for w_ref, wcopy_ref in zip(w_refs, wcopy_refs):
        wcopy_ref[...] = w_ref[...].astype(BF16)

    for src, dst in ((k_ref, okt_ref), (v_ref, ovt_ref)):
        t = src[0].T
        for h in range(HEADS_PER_BLOCK):
            dst[h] = t[h * HEAD_DIM:(h + 1) * HEAD_DIM]

    nblk_total = seq // Q_BLOCK
    lane = lax.broadcasted_iota(jnp.int32, (Q_BLOCK, LANES), 1)
    head0 = lane < HEAD_DIM

    for c, (window, dil) in enumerate(DILATED_CONFIGS):
        sub_len = seq // dil
        blocks_per_sub = sub_len // Q_BLOCK

        for r in range(dil):
            rows = pl.ds(r, sub_len, stride=dil) if dil > 1 else pl.ds(0, seq)
            head0_rows = lax.broadcasted_iota(jnp.int32, (sub_len, LANES), 1) < HEAD_DIM
            qv = (q_ref[0, rows, :] * ATTN_SCALE).astype(BF16)
            kv = k_ref[0, rows, :].astype(BF16)
            vv = v_ref[0, rows, :].astype(BF16)
            zero = jnp.zeros_like(qv)
            q0 = jnp.where(head0_rows, qv, zero)
            q1 = jnp.where(head0_rows, zero, qv)
            for gb in range(blocks_per_sub):
                g = r * blocks_per_sub + gb
                src = slice(gb * Q_BLOCK, (gb + 1) * Q_BLOCK)
                qs_scr[c, 2 * g * Q_BLOCK:(2 * g + 1) * Q_BLOCK, :] = q0[src]
                qs_scr[c, (2 * g + 1) * Q_BLOCK:(2 * g + 2) * Q_BLOCK, :] = q1[src]
            dst = slice(r * sub_len, (r + 1) * sub_len)
            kd_scr[c, dst, :] = kv
            v0_scr[c, dst, :] = jnp.where(head0_rows, vv, zero)
            v1_scr[c, dst, :] = jnp.where(head0_rows, zero, vv)

        for g in range(nblk_total):
            r, gb = divmod(g, blocks_per_sub)
            lhs = qs_scr[c, 2 * g * Q_BLOCK:(2 * g + 2) * Q_BLOCK, :]
            if gb == 0:
                win = slice(g * Q_BLOCK, (g + 1) * Q_BLOCK)
                bias = bias_ref[c, 0, :, Q_BLOCK:]
            else:
                win = slice((g - 1) * Q_BLOCK, (g + 1) * Q_BLOCK)
                bias = bias_ref[c, 0]
            logits = lax.dot_general(lhs, kd_scr[c, win, :], (((1,), (1,)), ((), ())),
                                     preferred_element_type=F32) + bias
            m = jnp.max(logits, axis=-1, keepdims=True)
            p = jnp.exp(logits - m)
            den = jnp.sum(p, axis=-1, keepdims=True)
            pb = p.astype(BF16)
            o = (jnp.dot(pb[0:Q_BLOCK], v0_scr[c, win, :], preferred_element_type=F32)
                 + jnp.dot(pb[Q_BLOCK:], v1_scr[c, win, :], preferred_element_type=F32))
            m2 = jnp.where(head0, m[0:Q_BLOCK], m[Q_BLOCK:])
            d2 = jnp.where(head0, den[0:Q_BLOCK], den[Q_BLOCK:])
            if dil > 1:
                out_rows = pl.ds(gb * Q_BLOCK * dil + r, Q_BLOCK, stride=dil)
            else:
                out_rows = pl.ds(g * Q_BLOCK, Q_BLOCK)
            o_scr[c, out_rows, :] = o
            m_scr[c, out_rows, :] = m2
            d_scr[c, out_rows, :] = d2

    n_br = len(DILATED_CONFIGS)

    def merge_rows(i, carry):
        rows = pl.ds(pl.multiple_of(i * MERGE_ROWS, MERGE_ROWS), MERGE_ROWS)
        mmax = m_scr[0, rows, :]
        for c in range(1, n_br):
            mmax = jnp.maximum(mmax, m_scr[c, rows, :])
        num = jnp.zeros((MERGE_ROWS, LANES), F32)
        den = jnp.zeros((MERGE_ROWS, LANES), F32)
        for c in range(n_br):
            w = jnp.exp(m_scr[c, rows, :] - mmax)
            num = num + w * o_scr[c, rows, :]
            den = den + w * d_scr[c, rows, :]
        oa_ref[0, rows, :] = num / den
        return carry

    lax.fori_loop(0, seq // MERGE_ROWS, merge_rows, 0)

    u = gc_ref[0] * vc_ref[0]
    u_scr[0:8, :] = jnp.zeros((8, LANES), F32)
    u_scr[8:8 + seq, :] = u
    y = cw_ref[0:1, :] * u_scr[pl.ds(8 - (CONV_WIDTH - 1), seq), :]
    for i in range(1, CONV_WIDTH):
        y = y + cw_ref[i:i + 1, :] * u_scr[pl.ds(8 - (CONV_WIDTH - 1) + i, seq), :]
    oc_ref[0] = gb_ref[0] * y
    cs_ref[0] = u_scr[pl.ds(8 + seq - (CONV_WIDTH - 1), CONV_WIDTH - 1), :]


def _mixer_prompt(layer, depth, proj, conv_w_l, bias_tables, new_windows, to_round):
    _, batch, seq, width = proj.shape
    n_pairs = width // LANES
    n_heads = width // HEAD_DIM
    n_br = len(DILATED_CONFIGS)
    tok = pl.BlockSpec((1, seq, LANES), lambda b, hp: (b, 0, hp))
    assert (SEG_Q, SEG_K, SEG_V, SEG_VC, SEG_B, SEG_C) == tuple(range(6))
    seg_specs = [pl.BlockSpec((3, 1, seq, LANES), lambda b, hp, s=s: (s, b, 0, hp))
                 for s in range(N_SEGMENTS // 3)]
    win_spec = pl.BlockSpec((None, None, HEADS_PER_BLOCK, HEAD_DIM, seq),
                            lambda b, hp: (layer, b, hp, 0, 0))
    win_shape = jax.ShapeDtypeStruct((depth, batch, n_heads, HEAD_DIM, seq), F32)
    in_specs = seg_specs + [
        pl.BlockSpec((CONV_WIDTH, LANES), lambda b, hp: (0, hp)),
        pl.BlockSpec((n_br, 1, 2 * Q_BLOCK, 2 * Q_BLOCK), lambda b, hp: (0, hp, 0, 0))]
    args = [proj] * (N_SEGMENTS // 3) + [conv_w_l, bias_tables]
    n_steps = batch * n_pairs
    copy_specs, copy_shapes = [], []
    for w, w_layer in to_round:
        _, w_rows, w_cols = w.shape
        rows = w_rows // n_steps
        in_specs.append(pl.BlockSpec((None, rows, w_cols),
                                     lambda b, hp, w_layer=w_layer: (w_layer, b * n_pairs + hp, 0)))
        args.append(w)
        copy_specs.append(pl.BlockSpec((rows, w_cols), lambda b, hp: (b * n_pairs + hp, 0)))
        copy_shapes.append(jax.ShapeDtypeStruct((w_rows, w_cols), BF16))
    aliases = {}
    if new_windows is not None:
        aliases = {len(args): 3, len(args) + 1: 4}
        in_specs = in_specs + [pl.BlockSpec(memory_space=pl.ANY)] * 2
        args = args + list(new_windows)
    oa, oc, cs, okt, ovt, *copies = pl.pallas_call(
        functools.partial(_mixer_prompt_kernel, seq=seq, n_aliased=len(aliases),
                          n_round=len(to_round)),
        grid=(batch, n_pairs),
        in_specs=in_specs,
        out_specs=[tok, tok,
                   pl.BlockSpec((1, CONV_WIDTH - 1, LANES), lambda b, hp: (b, 0, hp)),
                   win_spec, win_spec] + copy_specs,
        out_shape=[jax.ShapeDtypeStruct((batch, seq, width), F32),
                   jax.ShapeDtypeStruct((batch, seq, width), F32),
                   jax.ShapeDtypeStruct((batch, CONV_WIDTH - 1, width), F32),
                   win_shape, win_shape] + copy_shapes,
        input_output_aliases=aliases,
        scratch_shapes=[pltpu.VMEM((n_br, 2 * seq, LANES), BF16),
                        pltpu.VMEM((n_br, seq, LANES), BF16),
                        pltpu.VMEM((n_br, seq, LANES), BF16),
                        pltpu.VMEM((n_br, seq, LANES), BF16),
                        pltpu.VMEM((n_br, seq, LANES), F32),
                        pltpu.VMEM((n_br, seq, LANES), F32),
                        pltpu.VMEM((n_br, seq, LANES), F32),
                        pltpu.VMEM((seq + 8, LANES), F32)],
        compiler_params=_compiler_params(("arbitrary", "arbitrary")),
        name="mixer_prompt",
    )(*args)
    return oa, oc, cs, (okt, ovt), copies


SAMPLE_HEADS_PER_STEP = 8


def _sample_bias_tables(rel_bias, wb):
    dist = wb - np.arange(wb)
    tables = []
    for window, dil in DILATED_CONFIGS:
        in_branch = (dist % dil == 0) & (dist // dil <= window // dil)
        bias = jnp.where(in_branch[:, None], _bias_lookup(rel_bias, _t5_bucket(dist)), MASKED)
        tables.append(bias.T)
    own = rel_bias[_t5_bucket(np.zeros((1,), np.int64))][0]
    return jnp.stack(tables), own[:, None]


def _mixer_sample_kernel(q_ref, kn_ref, vn_ref, kt_ref, vt_ref, bias_ref, bias0_ref,
                         vc_ref, gb_ref, gc_ref, cw_ref, sc_ref, *rest):
    oa_ref, oc_ref, cs_ref, okt_ref, ovt_ref = rest[-5:]
    heads, head_dim, wb = kt_ref.shape
    newest = lax.broadcasted_iota(jnp.int32, (head_dim, wb), 1) == wb - 1
    n_br = len(DILATED_CONFIGS)

    s = jnp.concatenate([jnp.sum(kt_ref[h] * q_ref[h], axis=0, keepdims=True)
                         for h in range(heads)], axis=0) * ATTN_SCALE
    l_new = jnp.concatenate([jnp.sum(q_ref[h] * kn_ref[h], axis=0, keepdims=True)
                             for h in range(heads)], axis=0) * ATTN_SCALE + bias0_ref[...]
    branches = []
    for c in range(n_br):
        logits = s + bias_ref[c]
        m = jnp.maximum(jnp.max(logits, axis=-1, keepdims=True), l_new)
        p = jnp.exp(logits - m)
        p_new = jnp.exp(l_new - m)
        den = jnp.sum(p, axis=-1, keepdims=True) + p_new
        branches.append((p, p_new, m, den))
    mmax = branches[0][2]
    for br in branches[1:]:
        mmax = jnp.maximum(mmax, br[2])
    p_all = jnp.zeros_like(s)
    p_new_all = jnp.zeros_like(mmax)
    den_all = jnp.zeros_like(mmax)
    for p, p_new, m, den in branches:
        w = jnp.exp(m - mmax)
        p_all = p_all + w * p
        p_new_all = p_new_all + w * p_new
        den_all = den_all + w * den

    for h in range(heads):
        kt = kt_ref[h]
        vt = vt_ref[h]
        k_new, v_new = kn_ref[h], vn_ref[h]
        num = (jnp.sum(vt * p_all[h:h + 1, :], axis=-1, keepdims=True)
               + p_new_all[h:h + 1, :] * v_new)
        oa_ref[h] = num / den_all[h:h + 1, :]
        okt_ref[h] = jnp.where(newest, k_new, pltpu.roll(kt, wb - 1, axis=1))
        ovt_ref[h] = jnp.where(newest, v_new, pltpu.roll(vt, wb - 1, axis=1))

    @pl.when(pl.program_id(1) == 0)
    def _():
        u = gc_ref[0] * vc_ref[0]
        state = sc_ref[0]
        y = cw_ref[0:1, :] * state[0:1]
        for i in range(1, CONV_WIDTH - 1):
            y = y + cw_ref[i:i + 1, :] * state[i:i + 1]
        y = y + cw_ref[CONV_WIDTH - 1:CONV_WIDTH, :] * u
        oc_ref[0] = gb_ref[0] * y
        cs_ref[0] = jnp.concatenate([state[1:], u], axis=0)


def _mixer_sample(layer, proj, kt_state, vt_state, rolled, bias_s, bias0, conv_w_l, state_conv):
    depth, batch, n_heads, head_dim, wb = kt_state.shape
    width = proj.shape[-1]
    hb = SAMPLE_HEADS_PER_STEP
    n_br = len(DILATED_CONFIGS)
    qkv = proj[SEG_Q:SEG_V + 1].reshape(3, batch, n_heads, head_dim, 1)
    rows = proj.reshape(N_SEGMENTS, batch, 1, width)
    col_specs = [pl.BlockSpec((None, None, hb, head_dim, 1), lambda b, hc, s=s: (s, b, hc, 0, 0))
                 for s in range(3)]
    row_specs = [pl.BlockSpec((None, 1, 1, width), lambda b, hc, s=s: (s, b, 0, 0))
                 for s in (SEG_VC, SEG_B, SEG_C)]
    buf_spec = pl.BlockSpec((None, None, hb, head_dim, wb), lambda b, hc: (layer, b, hc, 0, 0))
    in_specs = col_specs + [
        buf_spec, buf_spec,
        pl.BlockSpec((n_br, hb, wb), lambda b, hc: (0, hc, 0)),
        pl.BlockSpec((hb, 1), lambda b, hc: (hc, 0))] + row_specs + [
        pl.BlockSpec((CONV_WIDTH, width), lambda b, hc: (0, 0)),
        pl.BlockSpec((1, CONV_WIDTH - 1, width), lambda b, hc: (layer * batch + b, 0, 0))]
    args = [qkv, qkv, qkv, kt_state, vt_state, bias_s, bias0, rows, rows, rows, conv_w_l,
            state_conv.reshape(depth * batch, CONV_WIDTH - 1, width)]
    aliases = {}
    if rolled is not None:
        aliases = {len(args): 3, len(args) + 1: 4}
        in_specs = in_specs + [pl.BlockSpec(memory_space=pl.ANY)] * 2
        args = args + list(rolled)
    oa, oc, cs, okt, ovt = pl.pallas_call(
        _mixer_sample_kernel,
        grid=(batch, n_heads // hb),
        in_specs=in_specs,
        out_specs=[pl.BlockSpec((None, hb, head_dim, 1), lambda b, hc: (b, hc, 0, 0)),
                   pl.BlockSpec((1, 1, width), lambda b, hc: (b, 0, 0)),
                   pl.BlockSpec((1, CONV_WIDTH - 1, width), lambda b, hc: (b, 0, 0)),
                   buf_spec, buf_spec],
        out_shape=[jax.ShapeDtypeStruct((batch, n_heads, head_dim, 1), F32),
                   jax.ShapeDtypeStruct((batch, 1, width), F32),
                   jax.ShapeDtypeStruct((batch, CONV_WIDTH - 1, width), F32),
                   jax.ShapeDtypeStruct(kt_state.shape, F32),
                   jax.ShapeDtypeStruct(vt_state.shape, F32)],
        input_output_aliases=aliases,
        compiler_params=_compiler_params(("arbitrary", "arbitrary")),
        name="mixer_sample",
    )(*args)
    return oa, oc, cs, (okt, ovt)


def _outproj_kernel(oa_ref, oc_ref, ga_ref, gc_ref, w_ref, x_ref, o_ref, z_scr):
    n = pl.program_id(1)
    d_attn = oa_ref.shape[1]
    tn = o_ref.shape[1]

    @pl.when(n == 0)
    def _():
        z_scr[:, 0:d_attn] = _rms_rows(oa_ref[...], ga_ref[...]).astype(BF16)
        z_scr[:, d_attn:] = _rms_rows(oc_ref[...], gc_ref[...]).astype(BF16)

    w = w_ref[:, pl.ds(pl.multiple_of(n * tn, tn), tn)]
    o_ref[...] = x_ref[...] + jnp.dot(z_scr[...], w, preferred_element_type=F32)


def _outproj(oa, oc, attn_gain, conv_gain, w_out, x, *, tm, tn):
    m_rows, d_attn = oa.shape
    d_conv = oc.shape[1]
    d_model = x.shape[1]
    d_mix = d_attn + d_conv
    const = lambda m, n: (0, 0)
    return pl.pallas_call(
        _outproj_kernel,
        grid=(m_rows // tm, d_model // tn),
        in_specs=[pl.BlockSpec((tm, d_attn), lambda m, n: (m, 0)),
                  pl.BlockSpec((tm, d_conv), lambda m, n: (m, 0)),
                  pl.BlockSpec((1, d_attn), const),
                  pl.BlockSpec((1, d_conv), const),
                  pl.BlockSpec((d_mix, d_model), const, pipeline_mode=pl.Buffered(1)),
                  pl.BlockSpec((tm, tn), lambda m, n: (m, n))],
        out_specs=pl.BlockSpec((tm, tn), lambda m, n: (m, n)),
        out_shape=jax.ShapeDtypeStruct((m_rows, d_model), F32),
        scratch_shapes=[pltpu.VMEM((tm, d_mix), BF16)],
        compiler_params=_compiler_params(("arbitrary", "arbitrary")),
        name="outproj",
    )(oa, oc, attn_gain.reshape(1, d_attn), conv_gain.reshape(1, d_conv), w_out, x)


def _mlp_kernel(x_hbm, g_ref, wu_ref, wd_ref, o_ref, x_buf, h_scr, sem, *, tm):
    m = pl.program_id(0)
    f = pl.program_id(1)

    def x_copy(block):
        rows = pl.ds(pl.multiple_of(block * tm, tm), tm)
        return pltpu.make_async_copy(x_hbm.at[rows], x_buf, sem)

    @pl.when(f == 0)
    def _():
        @pl.when(m == 0)
        def _():
            x_copy(0).start()

        x_copy(m).wait()
        x = x_buf[...]
        h_scr[...] = _rms_rows(x, g_ref[...]).astype(BF16)
        o_ref[...] = x

    @pl.when((f == 1) & (m + 1 < pl.num_programs(0)))
    def _():
        x_copy(m + 1).start(priority=1)

    a = jnp.dot(h_scr[...], wu_ref[...], preferred_element_type=F32)
    a = jnp.square(jnp.maximum(a, 0.0)).astype(BF16)
    o_ref[...] += jnp.dot(a, wd_ref[...], preferred_element_type=F32)


def _mlp(x, gain, w_up, w_down, *, tm, tf):
    m_rows, d_model = x.shape
    d_ff = w_up.shape[1]
    assert d_ff // tf >= 2, "the next row block's x is fetched on a block's second step"
    return pl.pallas_call(
        functools.partial(_mlp_kernel, tm=tm),
        grid=(m_rows // tm, d_ff // tf),
        in_specs=[pl.BlockSpec(memory_space=pl.ANY),
                  pl.BlockSpec((1, d_model), lambda m, f: (0, 0)),
                  pl.BlockSpec((d_model, tf), lambda m, f: (0, f)),
                  pl.BlockSpec((tf, d_model), lambda m, f: (f, 0))],
        out_specs=pl.BlockSpec((tm, d_model), lambda m, f: (m, 0)),
        out_shape=jax.ShapeDtypeStruct((m_rows, d_model), F32),
        scratch_shapes=[pltpu.VMEM((tm, d_model), F32),
                        pltpu.VMEM((tm, d_model), BF16),
                        pltpu.SemaphoreType.DMA(())],
        compiler_params=_compiler_params(("arbitrary", "arbitrary")),
        name="mlp",
    )(x, gain.reshape(1, d_model), w_up, w_down)


PROMPT_TILES = dict(tm=1024, tn_in=2048, tn_out=1024, tf=1024)
SAMPLE_TILES = dict(tn=1024, tf=1024)


def kernel(x_prompt, x_sample, state_attn_k, state_attn_v, state_conv, rel_bias, norm_mix, w_in,
           q_norm, k_norm, conv_w, attn_out_norm, conv_out_norm, w_out, norm_mlp, w_up, w_down):
    batch, seq, d_model = x_prompt.shape
    dec_batch, dec_seq, _ = x_sample.shape
    depth, _, wb, n_heads, head_dim = state_attn_k.shape
    assert dec_seq == 1 and head_dim == HEAD_DIM and wb == max(w for w, _ in DILATED_CONFIGS)
    assert seq == wb

    bias_p = _prompt_bias_tables(rel_bias)
    bias_s, bias0 = _sample_bias_tables(rel_bias, wb)
    kt_state = state_attn_k.transpose(0, 1, 3, 4, 2)
    vt_state = state_attn_v.transpose(0, 1, 3, 4, 2)
    rolled = None
    new_windows = None

    xp = x_prompt.reshape(batch * seq, d_model)
    xs = x_sample.reshape(dec_batch, d_model)
    pt, st = PROMPT_TILES, SAMPLE_TILES
    cp_new, cs_new = [], []
    w_in_l = w_in
    for l in range(depth):
        proj_s = _inproj(xs, norm_mix[l], w_in_l, l, q_norm[l], k_norm[l],
                         tm=dec_batch, tn=st["tn"])
        if l == 0:
            proj_s, w_in_l = proj_s
        width = proj_s.shape[-1]
        oa_s, oc_s, cs, rolled = _mixer_sample(l, proj_s, kt_state, vt_state, rolled, bias_s, bias0,
                                               conv_w[l], state_conv)
        cs_new.append(cs)

        if l + 1 < depth:
            proj, w_in_l = _inproj(xp, norm_mix[l], w_in_l, l, q_norm[l], k_norm[l],
                                   tm=pt["tm"], tn=pt["tn_in"], w_next=(w_in, l + 1))
        else:
            proj = _inproj(xp, norm_mix[l], w_in_l, l, q_norm[l], k_norm[l],
                           tm=pt["tm"], tn=pt["tn_in"])
        oa, oc, cs, new_windows, (w_out_l, w_up_l, w_down_l) = _mixer_prompt(
            l, depth, proj.reshape(N_SEGMENTS, batch, seq, width), conv_w[l], bias_p, new_windows,
            [(w_out, l), (w_up, l), (w_down, l)])
        cp_new.append(cs)

        xs = _outproj(oa_s.reshape(dec_batch, width), oc_s.reshape(dec_batch, width),
                      attn_out_norm[l], conv_out_norm[l], w_out_l, xs, tm=dec_batch, tn=st["tn"])
        xs = _mlp(xs, norm_mlp[l], w_up_l, w_down_l, tm=dec_batch, tf=st["tf"])
        xp = _outproj(oa.reshape(batch * seq, width), oc.reshape(batch * seq, width),
                      attn_out_norm[l], conv_out_norm[l], w_out_l, xp, tm=pt["tm"], tn=pt["tn_out"])
        xp = _mlp(xp, norm_mlp[l], w_up_l, w_down_l, tm=pt["tm"], tf=pt["tf"])

    ks_new, vs_new = (r.transpose(0, 1, 4, 2, 3) for r in rolled)
    kp_new, vp_new = (w.transpose(0, 1, 4, 2, 3) for w in new_windows)
    return (xp.reshape(batch, seq, d_model), xs.reshape(dec_batch, dec_seq, d_model),
            kp_new, vp_new, jnp.stack(cp_new), ks_new, vs_new, jnp.stack(cs_new))
```
